```python
import math
import jax, jax.numpy as jnp
from jax import lax
import numpy as np

D_MODEL = 2048
BATCH = 8
SEQ = 2048
DEPTH = 1

D_MIX = D_MODEL
D_ATTN = D_MIX // 2
D_SGU = D_MIX - D_ATTN
HEAD_DIM = 128
N_ATTN_HEADS = D_ATTN // HEAD_DIM
DILATED_GROUPS = ((128, 1), (512, 4), (2048, 16))
NUM_REL_BUCKETS = 32
REL_MAX_DISTANCE = 2048
SGU_CHUNK = 128
SGU_GROUP_DIM = 128
N_SGU_GROUPS = D_SGU // SGU_GROUP_DIM
D_IN_PROJ = 3 * D_ATTN + 2 * D_SGU
N_EXPERTS = 256
TOP_K = 8
N_EXPERT_GROUPS = 8
TOPK_GROUPS = 4
D_EXPERT = D_MODEL // 4
D_SHARED = D_EXPERT
ROUTED_SCALE = 2.5
EXPERT_BLOCK = 128
NORM_EPS = 1e-6

kernel_name = "hybrid_dilated_sgu_moe_block"


def rms_norm(x, g):
    xf = x.astype(jnp.float32)
    r = lax.rsqrt(jnp.mean(xf * xf, axis=-1, keepdims=True) + NORM_EPS)
    return (xf * r).astype(x.dtype) * g


def layer_norm(x, g, b):
    xf = x.astype(jnp.float32)
    mu = jnp.mean(xf, axis=-1, keepdims=True)
    var = jnp.mean(jnp.square(xf - mu), axis=-1, keepdims=True)
    return ((xf - mu) * lax.rsqrt(var + NORM_EPS)).astype(x.dtype) * g + b


def t5_bucket(distance):
    max_exact = NUM_REL_BUCKETS // 2
    nf = jnp.maximum(distance, 1).astype(jnp.float32)
    large = max_exact + (jnp.log(nf / max_exact) / math.log(REL_MAX_DISTANCE / max_exact)
                         * (NUM_REL_BUCKETS - max_exact)).astype(jnp.int32)
    large = jnp.minimum(large, NUM_REL_BUCKETS - 1)
    return jnp.where(distance < max_exact, distance, large)


def dilated_group_attention(q, k, v, rel_bias, window, dilation):
    B, S, H, Dh = q.shape
    n = window // dilation
    L = S // dilation
    nb = -(-L // n)
    Lp = nb * n

    def to_blocks(t):
        t = t.reshape(B, L, dilation, H, Dh).transpose(0, 2, 1, 3, 4)
        t = jnp.pad(t, ((0, 0), (0, 0), (0, Lp - L), (0, 0), (0, 0)))
        return t.reshape(B, dilation, nb, n, H, Dh)

    def band(t):
        prev = jnp.pad(t, ((0, 0), (0, 0), (1, 0), (0, 0), (0, 0), (0, 0)))[:, :, :-1]
        return jnp.concatenate([prev, t], axis=3)

    qb = to_blocks(q)
    kband = band(to_blocks(k))
    vband = band(to_blocks(v))
    logits = jnp.einsum('brnqhe,brnkhe->brnhqk', qb, kband,
                        preferred_element_type=jnp.float32)
    qi = jnp.arange(n)[:, None]
    ki = jnp.arange(2 * n)[None, :]
    steps = n + qi - ki
    in_band = (steps >= 0) & (steps <= n)
    blk = jnp.arange(nb)[:, None, None]
    valid = in_band[None] & ((blk > 0) | (ki[None] >= n))
    bucket = t5_bucket(jnp.clip(steps, 0, n) * dilation)
    bias = rel_bias[bucket].astype(jnp.float32).transpose(2, 0, 1)
    logits = jnp.where(valid[:, None], logits + bias, -jnp.inf)
    m = jnp.max(logits, axis=-1, keepdims=True)
    p = jnp.exp(logits - m)
    s = jnp.sum(p, axis=-1, keepdims=True)
    o = jnp.einsum('brnhqk,brnkhe->brnqhe', p, vband,
                   preferred_element_type=jnp.float32) / s.transpose(0, 1, 2, 4, 3, 5)
    lse = (m + jnp.log(s))[..., 0].transpose(0, 1, 2, 4, 3)
    o = o.reshape(B, dilation, Lp, H, Dh)[:, :, :L].transpose(0, 2, 1, 3, 4).reshape(B, S, H, Dh)
    lse = lse.reshape(B, dilation, Lp, H)[:, :, :L].transpose(0, 2, 1, 3).reshape(B, S, H)
    return o, lse


def dilated_mixture_attention(q, k, v, rel_bias):
    outs, lses = [], []
    for window, dilation in DILATED_GROUPS:
        o, lse = dilated_group_attention(q, k, v, rel_bias, window, dilation)
        outs.append(o)
        lses.append(lse)
    w = jax.nn.softmax(jnp.stack(lses), axis=0)
    return jnp.einsum('gbsh,gbshe->bshe', w, jnp.stack(outs))


def spatial_gating(u, z, ln_g, ln_b, w_s, b_s):
    B, S, _ = u.shape
    u = jax.nn.gelu(u, approximate=False)
    z = jax.nn.gelu(z, approximate=False)
    z = z.reshape(B, S // SGU_CHUNK, SGU_CHUNK, N_SGU_GROUPS, SGU_GROUP_DIM)
    z = layer_norm(z, ln_g.reshape(N_SGU_GROUPS, SGU_GROUP_DIM), ln_b.reshape(N_SGU_GROUPS, SGU_GROUP_DIM))
    causal = jnp.tril(jnp.ones((SGU_CHUNK, SGU_CHUNK), dtype=w_s.dtype))
    mixed = jnp.einsum('gts,bnsgc->bntgc', w_s * causal, z) + b_s.T[None, None, :, :, None]
    return u * mixed.reshape(B, S, D_SGU)


def swiglu(x, wg, wu, wd):
    return (jax.nn.silu(x @ wg) * (x @ wu)) @ wd


def route(h, router_w, router_bias):
    T = h.shape[0]
    scores = jax.nn.sigmoid(jnp.dot(h, router_w, preferred_element_type=jnp.float32))
    sel = scores + router_bias.astype(jnp.float32)
    grp = sel.reshape(T, N_EXPERT_GROUPS, N_EXPERTS // N_EXPERT_GROUPS)
    grp_score = jnp.sum(lax.top_k(grp, 2)[0], axis=-1)
    _, top_grp = lax.top_k(grp_score, TOPK_GROUPS)
    grp_mask = jnp.any(top_grp[..., None] == jnp.arange(N_EXPERT_GROUPS), axis=1)
    exp_mask = jnp.repeat(grp_mask, N_EXPERTS // N_EXPERT_GROUPS, axis=1)
    _, idx = lax.top_k(jnp.where(exp_mask, sel, -jnp.inf), TOP_K)
    w = jnp.take_along_axis(scores, idx, axis=-1)
    w = w / jnp.sum(w, axis=-1, keepdims=True) * ROUTED_SCALE
    return idx, w


def routed_experts(h, idx, wts, w_gate, w_up, w_down):
    T, D = h.shape
    M = T * TOP_K
    e_flat = idx.reshape(M)
    tok_flat = jnp.arange(M, dtype=jnp.int32) // TOP_K
    w_flat = wts.reshape(M)
    order = jnp.argsort(e_flat)
    se, stok, sw = e_flat[order], tok_flat[order], w_flat[order]
    counts = jnp.zeros((N_EXPERTS,), jnp.int32).at[e_flat].add(1)
    starts = jnp.cumsum(counts) - counts
    padded = (counts + EXPERT_BLOCK - 1) // EXPERT_BLOCK * EXPERT_BLOCK
    pends = jnp.cumsum(padded)
    pstarts = pends - padded
    dest = pstarts[se] + (jnp.arange(M, dtype=jnp.int32) - starts[se])
    n_blocks = -(-M // EXPERT_BLOCK) + N_EXPERTS
    rows = n_blocks * EXPERT_BLOCK
    row_tok = jnp.full((rows,), T, jnp.int32).at[dest].set(stok)
    row_w = jnp.zeros((rows,), h.dtype).at[dest].set(sw.astype(h.dtype))
    block_start = jnp.arange(n_blocks, dtype=jnp.int32) * EXPERT_BLOCK
    block_exp = jnp.minimum(jnp.searchsorted(pends, block_start, side='right'), N_EXPERTS - 1)
    h_pad = jnp.concatenate([h, jnp.zeros((1, D), h.dtype)], axis=0)

    def body(acc, blk):
        tok, w, e = blk
        xb = h_pad[tok]
        yb = swiglu(xb, w_gate[e], w_up[e], w_down[e]) * w[:, None]
        return acc.at[tok].add(yb), None

    acc, _ = lax.scan(body, jnp.zeros((T + 1, D), h.dtype),
                      (row_tok.reshape(n_blocks, EXPERT_BLOCK), row_w.reshape(n_blocks, EXPERT_BLOCK), block_exp))
    return acc[:T]


def setup_inputs(seed: int = 0) -> dict:
    key = jax.random.key(seed)
    ks = jax.random.split(key, 24)
    f32 = jnp.float32
    nrm = lambda k, shape, s: jax.random.normal(k, shape, f32) * s
    D, E, F = D_MODEL, N_EXPERTS, D_EXPERT
    return {
        "x": nrm(ks[0], (BATCH, SEQ, D), 1.0),
        "c": nrm(ks[1], (BATCH, D), 1.0),
        "ada_w": nrm(ks[2], (DEPTH, D, 6 * D), 0.5 * D ** -0.5),
        "ada_b": nrm(ks[3], (DEPTH, 6 * D), 0.02),
        "mix_norm_g": 1.0 + nrm(ks[4], (DEPTH, D), 0.05),
        "ffn_norm_g": 1.0 + nrm(ks[5], (DEPTH, D), 0.05),
        "w_in": nrm(ks[6], (DEPTH, D, D_IN_PROJ), D ** -0.5),
        "q_norm_g": 1.0 + nrm(ks[7], (DEPTH, HEAD_DIM), 0.05),
        "k_norm_g": 1.0 + nrm(ks[8], (DEPTH, HEAD_DIM), 0.05),
        "rel_bias": nrm(ks[9], (NUM_REL_BUCKETS, N_ATTN_HEADS), 0.5),
        "sgu_ln_g": 1.0 + nrm(ks[10], (DEPTH, D_SGU), 0.05),
        "sgu_ln_b": nrm(ks[11], (DEPTH, D_SGU), 0.02),
        "sgu_w": nrm(ks[12], (DEPTH, N_SGU_GROUPS, SGU_CHUNK, SGU_CHUNK), SGU_CHUNK ** -0.5),
        "sgu_b": 1.0 + nrm(ks[13], (DEPTH, N_SGU_GROUPS, SGU_CHUNK), 0.1),
        "w_out": nrm(ks[14], (DEPTH, D_MIX, D), D_MIX ** -0.5),
        "router_w": nrm(ks[15], (DEPTH, D, E), D ** -0.5),
        "router_bias": nrm(ks[16], (DEPTH, E), 0.01),
        "shared_w_gate": nrm(ks[17], (DEPTH, D, D_SHARED), D ** -0.5),
        "shared_w_up": nrm(ks[18], (DEPTH, D, D_SHARED), D ** -0.5),
        "shared_w_down": nrm(ks[19], (DEPTH, D_SHARED, D), D_SHARED ** -0.5),
        "expert_w_gate": nrm(ks[20], (DEPTH, E, D, F), D ** -0.5),
        "expert_w_up": nrm(ks[21], (DEPTH, E, D, F), D ** -0.5),
        "expert_w_down": nrm(ks[22], (DEPTH, E, F, D), F ** -0.5),
    }


def reference(x, c, ada_w, ada_b, mix_norm_g, ffn_norm_g, w_in, q_norm_g, k_norm_g, rel_bias,
              sgu_ln_g, sgu_ln_b, sgu_w, sgu_b, w_out, router_w, router_bias,
              shared_w_gate, shared_w_up, shared_w_down, expert_w_gate, expert_w_up, expert_w_down):
    B, S, D = x.shape
    T = B * S
    c_act = jax.nn.silu(c)
    for l in range(DEPTH):
        mod = c_act @ ada_w[l] + ada_b[l]
        shift_a, scale_a, gate_a, shift_f, scale_f, gate_f = [m[:, None, :] for m in jnp.split(mod, 6, axis=-1)]

        h = rms_norm(x, mix_norm_g[l]) * (1.0 + scale_a) + shift_a
        proj = h @ w_in[l]
        q, k, v, u, z = jnp.split(proj, [D_ATTN, 2 * D_ATTN, 3 * D_ATTN, 3 * D_ATTN + D_SGU], axis=-1)
        q = rms_norm(q.reshape(B, S, N_ATTN_HEADS, HEAD_DIM), q_norm_g[l]) * (HEAD_DIM ** -0.5)
        k = rms_norm(k.reshape(B, S, N_ATTN_HEADS, HEAD_DIM), k_norm_g[l])
        v = v.reshape(B, S, N_ATTN_HEADS, HEAD_DIM)
        attn = dilated_mixture_attention(q, k, v, rel_bias).reshape(B, S, D_ATTN).astype(x.dtype)
        sgu = spatial_gating(u, z, sgu_ln_g[l], sgu_ln_b[l], sgu_w[l], sgu_b[l])
        mixed = jnp.concatenate([attn, sgu], axis=-1) @ w_out[l]
        x = x + gate_a * mixed

        h2 = (rms_norm(x, ffn_norm_g[l]) * (1.0 + scale_f) + shift_f).reshape(T, D)
        idx, wts = route(h2, router_w[l], router_bias[l])
        ffn = swiglu(h2, shared_w_gate[l], shared_w_up[l], shared_w_down[l]) + \
            routed_experts(h2, idx, wts, expert_w_gate[l], expert_w_up[l], expert_w_down[l])
        x = x + gate_f * ffn.reshape(B, S, D)
    return x
```

```python
import functools
import math

import numpy as np
import jax
import jax.numpy as jnp
from jax import lax
from jax.experimental import pallas as pl
from jax.experimental.pallas import tpu as pltpu

F32 = jnp.float32
BF16 = jnp.bfloat16
I32 = jnp.int32

D_MODEL = 2048
SEQ = 2048
HEAD_DIM = 128
N_HEADS = 8
D_ATTN = N_HEADS * HEAD_DIM
D_SGU = 1024
N_SGU_GROUPS = 8
SGU_CHUNK = 128
D_IN_PROJ = 3 * D_ATTN + 2 * D_SGU
DILATED_GROUPS = ((128, 1), (512, 4), (2048, 16))
BAND = 128
NUM_REL_BUCKETS = 32
REL_MAX_DISTANCE = 2048
N_EXPERTS = 256
TOP_K = 8
N_EXPERT_GROUPS = 8
GROUP_SIZE = N_EXPERTS // N_EXPERT_GROUPS
TOPK_GROUPS = 4
D_EXPERT = 512
ROUTED_SCALE = 2.5
NORM_EPS = 1e-6
MASK_VALUE = -1e30

LANES = 128
VMEM_LIMIT = 56 * 1024 * 1024

TM_PROJ = 512
TM_SGU = 512
TM_OUT = 256
TM_ROUTE = 512
TM_DISPATCH = 256
TM_COMBINE = 128
EXPERT_BLOCK = 128


def _cparams(sem):
    return pltpu.CompilerParams(dimension_semantics=sem, vmem_limit_bytes=VMEM_LIMIT)


def _silu(v):
    return v * jax.nn.sigmoid(v)


def _gelu(v):
    return 0.5 * v * (1.0 + lax.erf(v * (1.0 / math.sqrt(2.0))))


def _mod_kernel(c_ref, w_ref, b_ref, o_ref):
    ca = _silu(c_ref[...]).astype(BF16)
    o_ref[...] = jnp.dot(ca, w_ref[...].astype(BF16), preferred_element_type=F32) + b_ref[...]


def _modulation(c, ada_w, ada_b):
    B, D = c.shape
    N = ada_w.shape[1]
    tn = 1024
    return pl.pallas_call(
        _mod_kernel,
        grid=(N // tn,),
        in_specs=[
            pl.BlockSpec((B, D), lambda j: (0, 0)),
            pl.BlockSpec((D, tn), lambda j: (0, j)),
            pl.BlockSpec((1, tn), lambda j: (0, j)),
        ],
        out_specs=pl.BlockSpec((B, tn), lambda j: (0, j)),
        out_shape=jax.ShapeDtypeStruct((B, N), F32),
        compiler_params=_cparams(("arbitrary",)),
        name="modulation",
    )(c, ada_w, ada_b.reshape(1, N))


def _inproj_kernel(x_ref, mod_ref, g_ref, w_ref, qg_ref, kg_ref, o_ref, h_ref):
    j = pl.program_id(1)

    @pl.when(j == 0)
    def _():
        x = x_ref[...]
        r = lax.rsqrt(jnp.mean(x * x, axis=-1, keepdims=True) + NORM_EPS)
        h = (x * r) * g_ref[...] * (1.0 + mod_ref[0, 1:2, :]) + mod_ref[0, 0:1, :]
        h_ref[...] = h.astype(BF16)

    acc = jnp.dot(h_ref[...], w_ref[...], preferred_element_type=F32)

    def head_norm(gain_ref, scale):
        for hd in range(N_HEADS):
            a = acc[:, hd * HEAD_DIM:(hd + 1) * HEAD_DIM]
            r = lax.rsqrt(jnp.mean(a * a, axis=-1, keepdims=True) + NORM_EPS)
            o_ref[:, hd * HEAD_DIM:(hd + 1) * HEAD_DIM] = ((a * r) * gain_ref[...] * scale).astype(BF16)

    @pl.when(j == 0)
    def _():
        head_norm(qg_ref, HEAD_DIM ** -0.5)

    @pl.when(j == 1)
    def _():
        head_norm(kg_ref, 1.0)

    @pl.when(j >= 2)
    def _():
        o_ref[...] = acc.astype(BF16)


def _in_projection(x2, mod3, norm_g, w_in_b, q_g, k_g):
    T, D = x2.shape
    tm, tn = TM_PROJ, D_ATTN
    per_batch = SEQ // tm
    return pl.pallas_call(
        _inproj_kernel,
        grid=(T // tm, D_IN_PROJ // tn),
        in_specs=[
            pl.BlockSpec((tm, D), lambda i, j: (i, 0)),
            pl.BlockSpec((1, 6, D), lambda i, j: (i // per_batch, 0, 0)),
            pl.BlockSpec((1, D), lambda i, j: (0, 0)),
            pl.BlockSpec((D, tn), lambda i, j: (0, j)),
            pl.BlockSpec((1, HEAD_DIM), lambda i, j: (0, 0)),
            pl.BlockSpec((1, HEAD_DIM), lambda i, j: (0, 0)),
        ],
        out_specs=pl.BlockSpec((tm, tn), lambda i, j: (i, j)),
        out_shape=jax.ShapeDtypeStruct((T, D_IN_PROJ), BF16),
        scratch_shapes=[pltpu.VMEM((tm, D), BF16)],
        compiler_params=_cparams(("arbitrary", "arbitrary")),
        name="in_projection",
    )(x2, mod3, norm_g, w_in_b, q_g, k_g)


def _bias_tables(rel_bias):
    n = BAND
    qi = jnp.arange(n)[:, None]
    ki = jnp.arange(2 * n)[None, :]
    steps = n + qi - ki
    in_band = (steps >= 0) & (steps <= n)
    max_exact = NUM_REL_BUCKETS // 2
    tabs = []
    for _, dilation in DILATED_GROUPS:
        dist = jnp.clip(steps, 0, n) * dilation
        nf = jnp.maximum(dist, 1).astype(F32)
        large = max_exact + (jnp.log(nf / max_exact) / math.log(REL_MAX_DISTANCE / max_exact)
                             * (NUM_REL_BUCKETS - max_exact)).astype(I32)
        large = jnp.minimum(large, NUM_REL_BUCKETS - 1)
        bucket = jnp.where(dist < max_exact, dist, large)
        b = rel_bias[bucket].astype(F32).transpose(2, 0, 1)
        tabs.append(jnp.where(in_band[None], b, MASK_VALUE))
    return jnp.stack(tabs)


def _attn_kernel(q_ref, k_ref, v_ref, bias_ref, o_ref, qf, kf, vf, m_ref, s_ref, acc_ref):
    n = BAND
    qf[...] = q_ref[...].astype(F32)
    kf[...] = k_ref[...].astype(F32)
    vf[...] = v_ref[...].astype(F32)

    def piece(qb, kb, vb, bias):
        l = lax.dot_general(qb, kb, (((1,), (1,)), ((), ())), preferred_element_type=F32) + bias
        m = jnp.max(l, axis=-1, keepdims=True)
        p = jnp.exp(l - m)
        s = jnp.sum(p, axis=-1, keepdims=True)
        o = jnp.dot(p.astype(BF16), vb, preferred_element_type=F32)
        return m, s, o

    def merge(rows, m, s, o):
        m_old = m_ref[rows, :]
        m_new = jnp.maximum(m_old, m)
        a = jnp.exp(m_old - m_new)
        b = jnp.exp(m - m_new)
        m_ref[rows, :] = m_new
        s_ref[rows, :] = a * s_ref[rows, :] + b * s
        acc_ref[rows, :] = a * acc_ref[rows, :] + b * o

    def store0(rows, m, s, o):
        m_ref[rows, :] = jnp.broadcast_to(m, (n, LANES))
        s_ref[rows, :] = jnp.broadcast_to(s, (n, LANES))
        acc_ref[rows, :] = o

    m, s, o = piece(q_ref[0:n, :], k_ref[0:n, :], v_ref[0:n, :], bias_ref[0, :, n:])
    store0(pl.ds(0, n), m, s, o)

    def body0(i, carry):
        q0 = pl.multiple_of(i * n, n)
        k0 = pl.multiple_of(i * n - n, n)
        m, s, o = piece(q_ref[pl.ds(q0, n), :], k_ref[pl.ds(k0, 2 * n), :],
                        v_ref[pl.ds(k0, 2 * n), :], bias_ref[0])
        store0(pl.ds(q0, n), m, s, o)
        return carry

    lax.fori_loop(1, SEQ // n, body0, 0)

    for g in (1, 2):
        d = DILATED_GROUPS[g][1]
        nb = SEQ // d // n
        for r in range(d):
            for blk in range(nb):
                rows = pl.ds(blk * n * d + r, n, stride=d)
                qb = qf[rows, :].astype(BF16)
                if blk == 0:
                    kb = kf[rows, :].astype(BF16)
                    vb = vf[rows, :].astype(BF16)
                    bias = bias_ref[g, :, n:]
                else:
                    krows = pl.ds((blk - 1) * n * d + r, 2 * n, stride=d)
                    kb = kf[krows, :].astype(BF16)
                    vb = vf[krows, :].astype(BF16)
                    bias = bias_ref[g]
                m, s, o = piece(qb, kb, vb, bias)
                merge(rows, m, s, o)

    o_ref[...] = (acc_ref[...] / s_ref[...]).astype(BF16)


def _attention(proj3, bias_tabs):
    B = proj3.shape[0]
    blk = lambda off: pl.BlockSpec((None, SEQ, HEAD_DIM), lambda b, h: (b, 0, off + h))
    return pl.pallas_call(
        _attn_kernel,
        grid=(B, N_HEADS),
        in_specs=[
            blk(0), blk(N_HEADS), blk(2 * N_HEADS),
            pl.BlockSpec((3, None, BAND, 2 * BAND), lambda b, h: (0, h, 0, 0)),
        ],
        out_specs=pl.BlockSpec((None, SEQ, HEAD_DIM), lambda b, h: (b, 0, h)),
        out_shape=jax.ShapeDtypeStruct((B, SEQ, D_ATTN), BF16),
        scratch_shapes=[pltpu.VMEM((SEQ, HEAD_DIM), F32) for _ in range(6)],
        compiler_params=_cparams(("arbitrary", "arbitrary")),
        name="dilated_attention",
    )(proj3, proj3, proj3, bias_tabs)


def _sgu_kernel(u_ref, z_ref, w_ref, g_ref, b_ref, bs_ref, o_ref):
    n = SGU_CHUNK
    for c in range(TM_SGU // n):
        rs = slice(c * n, (c + 1) * n)
        for g in range(N_SGU_GROUPS):
            cs = slice(g * n, (g + 1) * n)
            z = _gelu(z_ref[rs, cs].astype(F32))
            mu = jnp.mean(z, axis=-1, keepdims=True)
            zc = z - mu
            var = jnp.mean(zc * zc, axis=-1, keepdims=True)
            zn = (zc * lax.rsqrt(var + NORM_EPS)) * g_ref[:, cs] + b_ref[:, cs]
            mixed = jnp.dot(w_ref[g], zn.astype(BF16), preferred_element_type=F32) + bs_ref[:, g:g + 1]
            u = _gelu(u_ref[rs, cs].astype(F32))
            o_ref[rs, cs] = (u * mixed).astype(BF16)


def _spatial_gating(proj, w_causal_b, ln_g, ln_b, bs_t):
    T = proj.shape[0]
    tm = TM_SGU
    ucol = 3 * D_ATTN // D_SGU
    return pl.pallas_call(
        _sgu_kernel,
        grid=(T // tm,),
        in_specs=[
            pl.BlockSpec((tm, D_SGU), lambda i: (i, ucol)),
            pl.BlockSpec((tm, D_SGU), lambda i: (i, ucol + 1)),
            pl.BlockSpec((N_SGU_GROUPS, SGU_CHUNK, SGU_CHUNK), lambda i: (0, 0, 0)),
            pl.BlockSpec((1, D_SGU), lambda i: (0, 0)),
            pl.BlockSpec((1, D_SGU), lambda i: (0, 0)),
            pl.BlockSpec((SGU_CHUNK, N_SGU_GROUPS), lambda i: (0, 0)),
        ],
        out_specs=pl.BlockSpec((tm, D_SGU), lambda i: (i, 0)),
        out_shape=jax.ShapeDtypeStruct((T, D_SGU), BF16),
        compiler_params=_cparams(("arbitrary",)),
        name="spatial_gating",
    )(proj, proj, w_causal_b, ln_g, ln_b, bs_t)


def _outproj_kernel(a_ref, s_ref, x_ref, mod_ref, g_ref, w_ref, x1_ref, h2_ref):
    mixed = jnp.dot(a_ref[...], w_ref[0:D_ATTN, :], preferred_element_type=F32)
    mixed += jnp.dot(s_ref[...], w_ref[D_ATTN:, :], preferred_element_type=F32)
    x1 = x_ref[...] + mod_ref[0, 2:3, :] * mixed
    x1_ref[...] = x1
    r = lax.rsqrt(jnp.mean(x1 * x1, axis=-1, keepdims=True) + NORM_EPS)
    h2_ref[...] = (x1 * r) * g_ref[...] * (1.0 + mod_ref[0, 4:5, :]) + mod_ref[0, 3:4, :]


def _out_projection(attn, sgu, x2, mod3, norm_g, w_out_b):
    T, D = x2.shape
    tm = TM_OUT
    per_batch = SEQ // tm
    return pl.pallas_call(
        _outproj_kernel,
        grid=(T // tm,),
        in_specs=[
            pl.BlockSpec((tm, D_ATTN), lambda i: (i, 0)),
            pl.BlockSpec((tm, D_SGU), lambda i: (i, 0)),
            pl.BlockSpec((tm, D), lambda i: (i, 0)),
            pl.BlockSpec((1, 6, D), lambda i: (i // per_batch, 0, 0)),
            pl.BlockSpec((1, D), lambda i: (0, 0)),
            pl.BlockSpec((D_ATTN + D_SGU, D), lambda i: (0, 0)),
        ],
        out_specs=[pl.BlockSpec((tm, D), lambda i: (i, 0)), pl.BlockSpec((tm, D), lambda i: (i, 0))],
        out_shape=[jax.ShapeDtypeStruct((T, D), F32), jax.ShapeDtypeStruct((T, D), F32)],
        compiler_params=_cparams(("arbitrary",)),
        name="out_projection",
    )(attn, sgu, x2, mod3, norm_g, w_out_b)


def _router_kernel(h_ref, w_ref, b_ref, idx_ref, wt_ref, rank_ref, cnt_ref, run_ref):
    i = pl.program_id(0)
    tr = TM_ROUTE
    E = N_EXPERTS

    @pl.when(i == 0)
    def _():
        run_ref[...] = jnp.zeros_like(run_ref)

    hb = h_ref[...].astype(BF16)
    logits = lax.dot_general(w_ref[...], hb, (((1,), (1,)), ((), ())), preferred_element_type=F32)
    scores = jax.nn.sigmoid(logits)
    sel = scores + b_ref[...]

    slabs = [sel[g * GROUP_SIZE:(g + 1) * GROUP_SIZE, :] for g in range(N_EXPERT_GROUPS)]
    si = lax.broadcasted_iota(I32, (GROUP_SIZE, tr), 0).astype(F32)
    gs = []
    for slab in slabs:
        m1 = jnp.max(slab, axis=0, keepdims=True)
        first = jnp.min(jnp.where(slab == m1, si, float(GROUP_SIZE)), axis=0, keepdims=True)
        m2 = jnp.max(jnp.where(si == first, -jnp.inf, slab), axis=0, keepdims=True)
        gs.append(m1 + m2)

    kept = []
    for g in range(N_EXPERT_GROUPS):
        beaten = jnp.zeros((1, tr), F32)
        for o in range(N_EXPERT_GROUPS):
            if o < g:
                beaten += (gs[o] >= gs[g]).astype(F32)
            elif o > g:
                beaten += (gs[o] > gs[g]).astype(F32)
        kept.append(jnp.where(beaten < TOPK_GROUPS, slabs[g], -jnp.inf))
    masked = jnp.concatenate(kept, axis=0)

    ei = lax.broadcasted_iota(I32, (E, tr), 0).astype(F32)
    picks, pick_scores = [], []
    onehot_sum = jnp.zeros((E, tr), F32)
    for k in range(TOP_K):
        m = jnp.max(masked, axis=0, keepdims=True)
        ik = jnp.min(jnp.where(masked == m, ei, float(E)), axis=0, keepdims=True)
        oh = ei == ik
        pick_scores.append(jnp.sum(jnp.where(oh, scores, 0.0), axis=0, keepdims=True))
        masked = jnp.where(oh, -jnp.inf, masked)
        onehot_sum += oh.astype(F32)
        picks.append(ik)

    ti = lax.broadcasted_iota(I32, (tr, tr), 0)
    tj = lax.broadcasted_iota(I32, (tr, tr), 1)
    before = (ti < tj).astype(BF16)
    prior = jnp.dot(onehot_sum.astype(BF16), before, preferred_element_type=F32) + run_ref[...]

    total = pick_scores[0]
    for k in range(1, TOP_K):
        total += pick_scores[k]
    for k in range(TOP_K):
        idx_ref[k:k + 1, :] = picks[k].astype(I32)
        wt_ref[k:k + 1, :] = pick_scores[k] / total * ROUTED_SCALE
        rk = jnp.sum(jnp.where(ei == picks[k], prior, 0.0), axis=0, keepdims=True)
        rank_ref[k:k + 1, :] = rk.astype(I32)

    run_ref[...] += jnp.sum(onehot_sum, axis=1, keepdims=True)
    cnt_ref[...] = run_ref[...].astype(I32)


def _route(h2, router_wt_b, router_bias):
    T, D = h2.shape
    tr = TM_ROUTE
    E = N_EXPERTS
    row_spec = pl.BlockSpec((TOP_K, tr), lambda i: (0, i))
    return pl.pallas_call(
        _router_kernel,
        grid=(T // tr,),
        in_specs=[
            pl.BlockSpec((tr, D), lambda i: (i, 0)),
            pl.BlockSpec((E, D), lambda i: (0, 0)),
            pl.BlockSpec((E, 1), lambda i: (0, 0)),
        ],
        out_specs=[row_spec, row_spec, row_spec, pl.BlockSpec((E, 1), lambda i: (0, 0))],
        out_shape=[
            jax.ShapeDtypeStruct((TOP_K, T), I32),
            jax.ShapeDtypeStruct((TOP_K, T), F32),
            jax.ShapeDtypeStruct((TOP_K, T), I32),
            jax.ShapeDtypeStruct((E, 1), I32),
        ],
        scratch_shapes=[pltpu.VMEM((E, 1), F32)],
        compiler_params=_cparams(("arbitrary",)),
        name="router",
    )(h2, router_wt_b, router_bias.reshape(E, 1))


def _dispatch_kernel(pend_ref, h_ref, dest_ref, xg_ref, dest_s, zero_ref, sem_s, sem):
    i = pl.program_id(0)
    td = TM_DISPATCH
    blk = EXPERT_BLOCK

    def zero_copy(e):
        start = pl.multiple_of(pend_ref[e + 1] - blk, blk)
        return pltpu.make_async_copy(zero_ref, xg_ref.at[pl.ds(start, blk), :], sem)

    @pl.when(i == 0)
    def _():
        zero_ref[...] = jnp.zeros_like(zero_ref)

        def start_zero(e, c):
            @pl.when(pend_ref[e + 1] > pend_ref[e])
            def _():
                zero_copy(e).start()
            return c

        def wait_zero(e, c):
            @pl.when(pend_ref[e + 1] > pend_ref[e])
            def _():
                zero_copy(e).wait()
            return c

        lax.fori_loop(0, N_EXPERTS, start_zero, 0)
        lax.fori_loop(0, N_EXPERTS, wait_zero, 0)

    cp = pltpu.make_async_copy(dest_ref, dest_s, sem_s)
    cp.start()
    cp.wait()

    def row_copy(t, k):
        return pltpu.make_async_copy(h_ref.at[pl.ds(t, 1), :], xg_ref.at[pl.ds(dest_s[k, t], 1), :], sem)

    def start_rows(t, c):
        for k in range(TOP_K):
            row_copy(t, k).start()
        return c

    def wait_rows(t, c):
        for k in range(TOP_K):
            row_copy(t, k).wait()
        return c

    lax.fori_loop(0, td, start_rows, 0)
    lax.fori_loop(0, td, wait_rows, 0)


def _dispatch(h2, dest, pend0, rows):
    T, D = h2.shape
    td = TM_DISPATCH
    return pl.pallas_call(
        _dispatch_kernel,
        grid_spec=pltpu.PrefetchScalarGridSpec(
            num_scalar_prefetch=1,
            grid=(T // td,),
            in_specs=[
                pl.BlockSpec((td, D), lambda i, p: (i, 0)),
                pl.BlockSpec((TOP_K, td), lambda i, p: (0, i)),
            ],
            out_specs=pl.BlockSpec(memory_space=pl.ANY),
            scratch_shapes=[
                pltpu.SMEM((TOP_K, td), I32),
                pltpu.VMEM((EXPERT_BLOCK, D), F32),
                pltpu.SemaphoreType.DMA,
                pltpu.SemaphoreType.DMA,
            ],
        ),
        out_shape=jax.ShapeDtypeStruct((rows, D), F32),
        compiler_params=_cparams(("arbitrary",)),
        name="dispatch",
    )(pend0, h2, dest)


def _experts_kernel(bexp_ref, nused_ref, x_ref, wg_ref, wu_ref, wd_ref, y_ref, wg_b, wu_b, wd_b):
    i = pl.program_id(0)

    @pl.when(i < nused_ref[0])
    def _():
        prev = bexp_ref[jnp.maximum(i - 1, 0)]

        @pl.when((i == 0) | (bexp_ref[i] != prev))
        def _():
            wg_b[...] = wg_ref[...].astype(BF16)
            wu_b[...] = wu_ref[...].astype(BF16)
            wd_b[...] = wd_ref[...].astype(BF16)

        x = x_ref[...].astype(BF16)
        g = jnp.dot(x, wg_b[...], preferred_element_type=F32)
        u = jnp.dot(x, wu_b[...], preferred_element_type=F32)
        a = (_silu(g) * u).astype(BF16)
        y_ref[...] = jnp.dot(a, wd_b[...], preferred_element_type=F32)


def _experts(xg, block_exp, n_used, w_gate, w_up, w_down):
    rows, D = xg.shape
    bm = EXPERT_BLOCK
    F = D_EXPERT
    nb = rows // bm
    row_map = lambda i, be, nu: (jnp.minimum(i, nu[0] - 1), 0)
    w_map = lambda i, be, nu: (be[i], 0, 0)
    return pl.pallas_call(
        _experts_kernel,
        grid_spec=pltpu.PrefetchScalarGridSpec(
            num_scalar_prefetch=2,
            grid=(nb,),
            in_specs=[
                pl.BlockSpec((bm, D), row_map),
                pl.BlockSpec((None, D, F), w_map),
                pl.BlockSpec((None, D, F), w_map),
                pl.BlockSpec((None, F, D), w_map),
            ],
            out_specs=pl.BlockSpec((bm, D), row_map),
            scratch_shapes=[pltpu.VMEM((D, F), BF16), pltpu.VMEM((D, F), BF16), pltpu.VMEM((F, D), BF16)],
        ),
        out_shape=jax.ShapeDtypeStruct((rows, D), F32),
        compiler_params=_cparams(("arbitrary",)),
        name="experts",
    )(block_exp, n_used, xg, w_gate, w_up, w_down)


def _combine_kernel(dest_ref, wt_ref, x1_ref, h_ref, mod_ref, sg_ref, su_ref, sd_ref, yg_ref, o_ref,
                    dest_s, buf, sem_s, sem):
    tc = TM_COMBINE
    cp = pltpu.make_async_copy(dest_ref, dest_s, sem_s)
    cp.start()
    cp.wait()

    def row_copy(t, k):
        return pltpu.make_async_copy(yg_ref.at[pl.ds(dest_s[k, t], 1), :], buf.at[k, pl.ds(t, 1), :], sem)

    def start_rows(t, c):
        for k in range(TOP_K):
            row_copy(t, k).start()
        return c

    def wait_rows(t, c):
        for k in range(TOP_K):
            row_copy(t, k).wait()
        return c

    lax.fori_loop(0, tc, start_rows, 0)

    hb = h_ref[...].astype(BF16)
    g = jnp.dot(hb, sg_ref[...], preferred_element_type=F32)
    u = jnp.dot(hb, su_ref[...], preferred_element_type=F32)
    ffn = jnp.dot((_silu(g) * u).astype(BF16), sd_ref[...], preferred_element_type=F32)

    lax.fori_loop(0, tc, wait_rows, 0)
    for k in range(TOP_K):
        ffn += buf[k] * wt_ref[:, k:k + 1]
    o_ref[...] = x1_ref[...] + mod_ref[0, 5:6, :] * ffn


def _combine(dest, wt_t, x1, h2, mod3, sg_b, su_b, sd_b, yg):
    T, D = x1.shape
    tc = TM_COMBINE
    per_batch = SEQ // tc
    F = D_EXPERT
    return pl.pallas_call(
        _combine_kernel,
        grid=(T // tc,),
        in_specs=[
            pl.BlockSpec((TOP_K, tc), lambda i: (0, i)),
            pl.BlockSpec((tc, TOP_K), lambda i: (i, 0)),
            pl.BlockSpec((tc, D), lambda i: (i, 0)),
            pl.BlockSpec((tc, D), lambda i: (i, 0)),
            pl.BlockSpec((1, 6, D), lambda i: (i // per_batch, 0, 0)),
            pl.BlockSpec((D, F), lambda i: (0, 0)),
            pl.BlockSpec((D, F), lambda i: (0, 0)),
            pl.BlockSpec((F, D), lambda i: (0, 0)),
            pl.BlockSpec(memory_space=pl.ANY),
        ],
        out_specs=pl.BlockSpec((tc, D), lambda i: (i, 0)),
        out_shape=jax.ShapeDtypeStruct((T, D), F32),
        scratch_shapes=[
            pltpu.SMEM((TOP_K, tc), I32),
            pltpu.VMEM((TOP_K, tc, D), F32),
            pltpu.SemaphoreType.DMA,
            pltpu.SemaphoreType.DMA,
        ],
        compiler_params=_cparams(("arbitrary",)),
        name="combine",
    )(dest, wt_t, x1, h2, mod3, sg_b, su_b, sd_b, yg)


def kernel(x, c, ada_w, ada_b, mix_norm_g, ffn_norm_g, w_in, q_norm_g, k_norm_g, rel_bias, sgu_ln_g, sgu_ln_b, sgu_w, sgu_b, w_out, router_w, router_bias, shared_w_gate, shared_w_up, shared_w_down, expert_w_gate, expert_w_up, expert_w_down):
    B, S, D = x.shape
    assert S == SEQ and D == D_MODEL and ada_w.shape[0] == 1
    T = B * S
    x2 = x.reshape(T, D)

    mod3 = _modulation(c, ada_w[0], ada_b[0]).reshape(B, 6, D)

    proj = _in_projection(x2, mod3, mix_norm_g, w_in[0].astype(BF16), q_norm_g, k_norm_g)
    attn = _attention(proj.reshape(B, S, D_IN_PROJ), _bias_tables(rel_bias)).reshape(T, D_ATTN)
    causal = jnp.tril(jnp.ones((SGU_CHUNK, SGU_CHUNK), F32))
    sgu = _spatial_gating(proj, (sgu_w[0] * causal).astype(BF16), sgu_ln_g, sgu_ln_b, sgu_b[0].T)
    x1, h2 = _out_projection(attn, sgu, x2, mod3, ffn_norm_g, w_out[0].astype(BF16))

    idx, wts, rank, counts = _route(h2, router_w[0].T.astype(BF16), router_bias[0])

    bm = EXPERT_BLOCK
    counts = counts.reshape(N_EXPERTS)
    padded = (counts + bm - 1) // bm * bm
    pends = jnp.cumsum(padded)
    pstarts = pends - padded
    n_blocks = T * TOP_K // bm + N_EXPERTS
    block_start = jnp.arange(n_blocks, dtype=I32) * bm
    block_exp = jnp.minimum(jnp.searchsorted(pends, block_start, side="right"), N_EXPERTS - 1).astype(I32)
    n_used = (pends[-1] // bm).astype(I32).reshape(1)
    dest = jnp.take(pstarts, idx, axis=0).astype(I32) + rank
    pend0 = jnp.concatenate([jnp.zeros((1,), I32), pends.astype(I32)])

    xg = _dispatch(h2, dest, pend0, n_blocks * bm)
    yg = _experts(xg, block_exp, n_used, expert_w_gate[0], expert_w_up[0], expert_w_down[0])
    out = _combine(dest, wts.T, x1, h2, mod3, shared_w_gate[0].astype(BF16), shared_w_up[0].astype(BF16),
                   shared_w_down[0].astype(BF16), yg)
    return out.reshape(B, S, D)
```

```python
import functools
import math

import numpy as np
import jax
import jax.numpy as jnp
from jax import lax
from jax.experimental import pallas as pl
from jax.experimental.pallas import tpu as pltpu

F32 = jnp.float32
BF16 = jnp.bfloat16
I32 = jnp.int32

D_MODEL = 2048
SEQ = 2048
HEAD_DIM = 128
N_HEADS = 8
D_ATTN = N_HEADS * HEAD_DIM
D_SGU = 1024
N_SGU_GROUPS = 8
SGU_CHUNK = 128
D_IN_PROJ = 3 * D_ATTN + 2 * D_SGU
DILATED_GROUPS = ((128, 1), (512, 4), (2048, 16))
BAND = 128
NUM_REL_BUCKETS = 32
REL_MAX_DISTANCE = 2048
N_EXPERTS = 256
TOP_K = 8
N_EXPERT_GROUPS = 8
GROUP_SIZE = N_EXPERTS // N_EXPERT_GROUPS
TOPK_GROUPS = 4
D_EXPERT = 512
ROUTED_SCALE = 2.5
NORM_EPS = 1e-6
MASK_VALUE = -1e30

LANES = 128
VMEM_LIMIT = 56 * 1024 * 1024

TM_PROJ = 512
TM_SGU = 512
TM_OUT = 256
TM_ROUTE = 512
TM_DISPATCH = 256
TM_COMBINE = 128
EXPERT_BLOCK = 128


def _cparams(sem):
    return pltpu.CompilerParams(dimension_semantics=sem, vmem_limit_bytes=VMEM_LIMIT)


def _silu(v):
    return v * jax.nn.sigmoid(v)


def _gelu(v):
    return 0.5 * v * (1.0 + lax.erf(v * (1.0 / math.sqrt(2.0))))


def _mod_kernel(c_ref, w_ref, b_ref, o_ref):
    ca = _silu(c_ref[...]).astype(BF16)
    o_ref[...] = jnp.dot(ca, w_ref[...].astype(BF16), preferred_element_type=F32) + b_ref[...]


def _modulation(c, ada_w, ada_b):
    B, D = c.shape
    N = ada_w.shape[1]
    tn = 1024
    return pl.pallas_call(
        _mod_kernel,
        grid=(N // tn,),
        in_specs=[
            pl.BlockSpec((B, D), lambda j: (0, 0)),
            pl.BlockSpec((D, tn), lambda j: (0, j)),
            pl.BlockSpec((1, tn), lambda j: (0, j)),
        ],
        out_specs=pl.BlockSpec((B, tn), lambda j: (0, j)),
        out_shape=jax.ShapeDtypeStruct((B, N), F32),
        compiler_params=_cparams(("arbitrary",)),
        name="modulation",
    )(c, ada_w, ada_b.reshape(1, N))


def _inproj_kernel(x_ref, mod_ref, g_ref, w_ref, qg_ref, kg_ref, o_ref, h_ref):
    j = pl.program_id(1)

    @pl.when(j == 0)
    def _():
        x = x_ref[...]
        r = lax.rsqrt(jnp.mean(x * x, axis=-1, keepdims=True) + NORM_EPS)
        h = (x * r) * g_ref[...] * (1.0 + mod_ref[0, 1:2, :]) + mod_ref[0, 0:1, :]
        h_ref[...] = h.astype(BF16)

    acc = jnp.dot(h_ref[...], w_ref[...], preferred_element_type=F32)

    def head_norm(gain_ref, scale):
        for hd in range(N_HEADS):
            a = acc[:, hd * HEAD_DIM:(hd + 1) * HEAD_DIM]
            r = lax.rsqrt(jnp.mean(a * a, axis=-1, keepdims=True) + NORM_EPS)
            o_ref[:, hd * HEAD_DIM:(hd + 1) * HEAD_DIM] = ((a * r) * gain_ref[...] * scale).astype(BF16)

    @pl.when(j == 0)
    def _():
        head_norm(qg_ref, HEAD_DIM ** -0.5)

    @pl.when(j == 1)
    def _():
        head_norm(kg_ref, 1.0)

    @pl.when(j >= 2)
    def _():
        o_ref[...] = acc.astype(BF16)


def _in_projection(x2, mod3, norm_g, w_in_b, q_g, k_g):
    T, D = x2.shape
    tm, tn = TM_PROJ, D_ATTN
    per_batch = SEQ // tm
    return pl.pallas_call(
        _inproj_kernel,
        grid=(T // tm, D_IN_PROJ // tn),
        in_specs=[
            pl.BlockSpec((tm, D), lambda i, j: (i, 0)),
            pl.BlockSpec((1, 6, D), lambda i, j: (i // per_batch, 0, 0)),
            pl.BlockSpec((1, D), lambda i, j: (0, 0)),
            pl.BlockSpec((D, tn), lambda i, j: (0, j)),
            pl.BlockSpec((1, HEAD_DIM), lambda i, j: (0, 0)),
            pl.BlockSpec((1, HEAD_DIM), lambda i, j: (0, 0)),
        ],
        out_specs=pl.BlockSpec((tm, tn), lambda i, j: (i, j)),
        out_shape=jax.ShapeDtypeStruct((T, D_IN_PROJ), BF16),
        scratch_shapes=[pltpu.VMEM((tm, D), BF16)],
        compiler_params=_cparams(("arbitrary", "arbitrary")),
        name="in_projection",
    )(x2, mod3, norm_g, w_in_b, q_g, k_g)


def _bias_tables(rel_bias):
    n = BAND
    qi = jnp.arange(n)[:, None]
    ki = jnp.arange(2 * n)[None, :]
    steps = n + qi - ki
    in_band = (steps >= 0) & (steps <= n)
    max_exact = NUM_REL_BUCKETS // 2
    tabs = []
    for _, dilation in DILATED_GROUPS:
        dist = jnp.clip(steps, 0, n) * dilation
        nf = jnp.maximum(dist, 1).astype(F32)
        large = max_exact + (jnp.log(nf / max_exact) / math.log(REL_MAX_DISTANCE / max_exact)
                             * (NUM_REL_BUCKETS - max_exact)).astype(I32)
        large = jnp.minimum(large, NUM_REL_BUCKETS - 1)
        bucket = jnp.where(dist < max_exact, dist, large)
        onehot = jax.nn.one_hot(bucket, NUM_REL_BUCKETS, dtype=F32)
        b = jnp.einsum("qkb,bh->hqk", onehot, rel_bias.astype(F32), precision=lax.Precision.HIGHEST)
        tabs.append(jnp.where(in_band[None], b, MASK_VALUE))
    return jnp.stack(tabs)


def _attn_kernel(q_ref, k_ref, v_ref, bias_ref, o_ref, qf, kf, vf, q4, k4, v4,
                 o0, l0, o1, l1, o2, l2, stage):
    n = BAND
    quarter = SEQ // 4

    for src, nat, res in ((q_ref, qf, q4), (k_ref, kf, k4), (v_ref, vf, v4)):
        nat[...] = src[...].astype(F32)
        for r in range(4):
            res[r * quarter:(r + 1) * quarter, :] = nat[pl.ds(r, quarter, stride=4), :]

    def piece(qb, kb, vb, bias, o_out, l_out, rows):
        l = lax.dot_general(qb, kb, (((1,), (1,)), ((), ())), preferred_element_type=F32) + bias
        m = jnp.max(l, axis=-1, keepdims=True)
        p = jnp.exp(l - m)
        s = jnp.sum(p, axis=-1, keepdims=True)
        o = jnp.dot(p.astype(BF16), vb, preferred_element_type=F32)
        o_out[rows, :] = o / s
        l_out[rows, :] = jnp.broadcast_to(m + jnp.log(s), (n, LANES))

    for i in range(SEQ // n):
        rows = slice(i * n, (i + 1) * n)
        if i == 0:
            piece(q_ref[rows, :], k_ref[rows, :], v_ref[rows, :], bias_ref[0, :, n:], o0, l0, rows)
        else:
            krows = slice((i - 1) * n, (i + 1) * n)
            piece(q_ref[rows, :], k_ref[krows, :], v_ref[krows, :], bias_ref[0], o0, l0, rows)

    for r in range(4):
        for blk in range(quarter // n):
            base = r * quarter + blk * n
            rows = slice(base, base + n)
            krows = rows if blk == 0 else slice(base - n, base + n)
            bias = bias_ref[1, :, n:] if blk == 0 else bias_ref[1]
            piece(q4[rows, :].astype(BF16), k4[krows, :].astype(BF16), v4[krows, :].astype(BF16),
                  bias, o1, l1, rows)

    for r in range(4):
        for a in range(4):
            rows = pl.ds(r * quarter + a, n, stride=4)
            piece(q4[rows, :].astype(BF16), k4[rows, :].astype(BF16), v4[rows, :].astype(BF16),
                  bias_ref[2, :, n:], o2, l2, rows)

    for r in range(4):
        for c in range(quarter // n):
            rows = slice(r * quarter + c * n, r * quarter + (c + 1) * n)
            nat = pl.ds(r + 4 * c * n, n, stride=4)
            a0, a1, a2 = l0[nat, :], l1[rows, :], l2[rows, :]
            m = jnp.maximum(jnp.maximum(a0, a1), a2)
            e0, e1, e2 = jnp.exp(a0 - m), jnp.exp(a1 - m), jnp.exp(a2 - m)
            mix = (e0 * o0[nat, :] + e1 * o1[rows, :] + e2 * o2[rows, :]) / (e0 + e1 + e2)
            stage[nat, :] = mix
    o_ref[...] = stage[...].astype(BF16)


def _attention(proj3, bias_tabs):
    B = proj3.shape[0]
    blk = lambda off: pl.BlockSpec((None, SEQ, HEAD_DIM), lambda b, h: (b, 0, off + h))
    return pl.pallas_call(
        _attn_kernel,
        grid=(B, N_HEADS),
        in_specs=[
            blk(0), blk(N_HEADS), blk(2 * N_HEADS),
            pl.BlockSpec((3, None, BAND, 2 * BAND), lambda b, h: (0, h, 0, 0)),
        ],
        out_specs=pl.BlockSpec((None, SEQ, HEAD_DIM), lambda b, h: (b, 0, h)),
        out_shape=jax.ShapeDtypeStruct((B, SEQ, D_ATTN), BF16),
        scratch_shapes=[pltpu.VMEM((SEQ, HEAD_DIM), F32) for _ in range(13)],
        compiler_params=_cparams(("arbitrary", "arbitrary")),
        name="dilated_attention",
    )(proj3, proj3, proj3, bias_tabs)


def _sgu_kernel(u_ref, z_ref, w_ref, g_ref, b_ref, bs_ref, o_ref):
    n = SGU_CHUNK
    for c in range(TM_SGU // n):
        rs = slice(c * n, (c + 1) * n)
        for g in range(N_SGU_GROUPS):
            cs = slice(g * n, (g + 1) * n)
            z = _gelu(z_ref[rs, cs].astype(F32))
            mu = jnp.mean(z, axis=-1, keepdims=True)
            zc = z - mu
            var = jnp.mean(zc * zc, axis=-1, keepdims=True)
            zn = (zc * lax.rsqrt(var + NORM_EPS)) * g_ref[:, cs] + b_ref[:, cs]
            mixed = jnp.dot(w_ref[g], zn.astype(BF16), preferred_element_type=F32) + bs_ref[:, g:g + 1]
            u = _gelu(u_ref[rs, cs].astype(F32))
            o_ref[rs, cs] = (u * mixed).astype(BF16)


def _spatial_gating(proj, w_causal_b, ln_g, ln_b, bs_t):
    T = proj.shape[0]
    tm = TM_SGU
    ucol = 3 * D_ATTN // D_SGU
    return pl.pallas_call(
        _sgu_kernel,
        grid=(T // tm,),
        in_specs=[
            pl.BlockSpec((tm, D_SGU), lambda i: (i, ucol)),
            pl.BlockSpec((tm, D_SGU), lambda i: (i, ucol + 1)),
            pl.BlockSpec((N_SGU_GROUPS, SGU_CHUNK, SGU_CHUNK), lambda i: (0, 0, 0)),
            pl.BlockSpec((1, D_SGU), lambda i: (0, 0)),
            pl.BlockSpec((1, D_SGU), lambda i: (0, 0)),
            pl.BlockSpec((SGU_CHUNK, N_SGU_GROUPS), lambda i: (0, 0)),
        ],
        out_specs=pl.BlockSpec((tm, D_SGU), lambda i: (i, 0)),
        out_shape=jax.ShapeDtypeStruct((T, D_SGU), BF16),
        compiler_params=_cparams(("arbitrary",)),
        name="spatial_gating",
    )(proj, proj, w_causal_b, ln_g, ln_b, bs_t)


def _outproj_kernel(a_ref, s_ref, x_ref, mod_ref, g_ref, w_ref, x1_ref, h2_ref):
    mixed = jnp.dot(a_ref[...], w_ref[0:D_ATTN, :], preferred_element_type=F32)
    mixed += jnp.dot(s_ref[...], w_ref[D_ATTN:, :], preferred_element_type=F32)
    x1 = x_ref[...] + mod_ref[0, 2:3, :] * mixed
    x1_ref[...] = x1
    r = lax.rsqrt(jnp.mean(x1 * x1, axis=-1, keepdims=True) + NORM_EPS)
    h2_ref[...] = (x1 * r) * g_ref[...] * (1.0 + mod_ref[0, 4:5, :]) + mod_ref[0, 3:4, :]


def _out_projection(attn, sgu, x2, mod3, norm_g, w_out_b):
    T, D = x2.shape
    tm = TM_OUT
    per_batch = SEQ // tm
    return pl.pallas_call(
        _outproj_kernel,
        grid=(T // tm,),
        in_specs=[
            pl.BlockSpec((tm, D_ATTN), lambda i: (i, 0)),
            pl.BlockSpec((tm, D_SGU), lambda i: (i, 0)),
            pl.BlockSpec((tm, D), lambda i: (i, 0)),
            pl.BlockSpec((1, 6, D), lambda i: (i // per_batch, 0, 0)),
            pl.BlockSpec((1, D), lambda i: (0, 0)),
            pl.BlockSpec((D_ATTN + D_SGU, D), lambda i: (0, 0)),
        ],
        out_specs=[pl.BlockSpec((tm, D), lambda i: (i, 0)), pl.BlockSpec((tm, D), lambda i: (i, 0))],
        out_shape=[jax.ShapeDtypeStruct((T, D), F32), jax.ShapeDtypeStruct((T, D), F32)],
        compiler_params=_cparams(("arbitrary",)),
        name="out_projection",
    )(attn, sgu, x2, mod3, norm_g, w_out_b)


def _router_kernel(h_ref, w_ref, b_ref, idx_ref, wt_ref, rank_ref, cnt_ref, run_ref):
    i = pl.program_id(0)
    tr = TM_ROUTE
    E = N_EXPERTS

    @pl.when(i == 0)
    def _():
        run_ref[...] = jnp.zeros_like(run_ref)

    hb = h_ref[...].astype(BF16)
    logits = lax.dot_general(w_ref[...], hb, (((1,), (1,)), ((), ())), preferred_element_type=F32)
    scores = jax.nn.sigmoid(logits)
    sel = scores + b_ref[...]

    slabs = [sel[g * GROUP_SIZE:(g + 1) * GROUP_SIZE, :] for g in range(N_EXPERT_GROUPS)]
    si = lax.broadcasted_iota(I32, (GROUP_SIZE, tr), 0).astype(F32)
    gs = []
    for slab in slabs:
        m1 = jnp.max(slab, axis=0, keepdims=True)
        first = jnp.min(jnp.where(slab == m1, si, float(GROUP_SIZE)), axis=0, keepdims=True)
        m2 = jnp.max(jnp.where(si == first, -jnp.inf, slab), axis=0, keepdims=True)
        gs.append(m1 + m2)

    kept = []
    for g in range(N_EXPERT_GROUPS):
        beaten = jnp.zeros((1, tr), F32)
        for o in range(N_EXPERT_GROUPS):
            if o < g:
                beaten += (gs[o] >= gs[g]).astype(F32)
            elif o > g:
                beaten += (gs[o] > gs[g]).astype(F32)
        kept.append(jnp.where(beaten < TOPK_GROUPS, slabs[g], -jnp.inf))
    masked = jnp.concatenate(kept, axis=0)

    ei = lax.broadcasted_iota(I32, (E, tr), 0).astype(F32)
    picks, pick_scores = [], []
    onehot_sum = jnp.zeros((E, tr), F32)
    for k in range(TOP_K):
        m = jnp.max(masked, axis=0, keepdims=True)
        ik = jnp.min(jnp.where(masked == m, ei, float(E)), axis=0, keepdims=True)
        oh = ei == ik
        pick_scores.append(jnp.sum(jnp.where(oh, scores, 0.0), axis=0, keepdims=True))
        masked = jnp.where(oh, -jnp.inf, masked)
        onehot_sum += oh.astype(F32)
        picks.append(ik)

    ti = lax.broadcasted_iota(I32, (tr, tr), 0)
    tj = lax.broadcasted_iota(I32, (tr, tr), 1)
    before = (ti < tj).astype(BF16)
    prior = jnp.dot(onehot_sum.astype(BF16), before, preferred_element_type=F32) + run_ref[...]

    total = pick_scores[0]
    for k in range(1, TOP_K):
        total += pick_scores[k]
    for k in range(TOP_K):
        idx_ref[k:k + 1, :] = picks[k].astype(I32)
        wt_ref[k:k + 1, :] = pick_scores[k] / total * ROUTED_SCALE
        rk = jnp.sum(jnp.where(ei == picks[k], prior, 0.0), axis=0, keepdims=True)
        rank_ref[k:k + 1, :] = rk.astype(I32)

    run_ref[...] += jnp.sum(onehot_sum, axis=1, keepdims=True)
    cnt_ref[...] = run_ref[...].astype(I32)


def _route(h2, router_wt_b, router_bias):
    T, D = h2.shape
    tr = TM_ROUTE
    E = N_EXPERTS
    row_spec = pl.BlockSpec((TOP_K, tr), lambda i: (0, i))
    return pl.pallas_call(
        _router_kernel,
        grid=(T // tr,),
        in_specs=[
            pl.BlockSpec((tr, D), lambda i: (i, 0)),
            pl.BlockSpec((E, D), lambda i: (0, 0)),
            pl.BlockSpec((E, 1), lambda i: (0, 0)),
        ],
        out_specs=[row_spec, row_spec, row_spec, pl.BlockSpec((E, 1), lambda i: (0, 0))],
        out_shape=[
            jax.ShapeDtypeStruct((TOP_K, T), I32),
            jax.ShapeDtypeStruct((TOP_K, T), F32),
            jax.ShapeDtypeStruct((TOP_K, T), I32),
            jax.ShapeDtypeStruct((E, 1), I32),
        ],
        scratch_shapes=[pltpu.VMEM((E, 1), F32)],
        compiler_params=_cparams(("arbitrary",)),
        name="router",
    )(h2, router_wt_b, router_bias.reshape(E, 1))


def _dispatch_kernel(pend_ref, h_ref, dest_ref, xg_ref, dest_s, zero_ref, sem_s, sem):
    i = pl.program_id(0)
    td = TM_DISPATCH
    blk = EXPERT_BLOCK

    def zero_copy(e):
        start = pl.multiple_of(pend_ref[e + 1] - blk, blk)
        return pltpu.make_async_copy(zero_ref, xg_ref.at[pl.ds(start, blk), :], sem)

    @pl.when(i == 0)
    def _():
        zero_ref[...] = jnp.zeros_like(zero_ref)

        def start_zero(e, c):
            @pl.when(pend_ref[e + 1] > pend_ref[e])
            def _():
                zero_copy(e).start()
            return c

        def wait_zero(e, c):
            @pl.when(pend_ref[e + 1] > pend_ref[e])
            def _():
                zero_copy(e).wait()
            return c

        lax.fori_loop(0, N_EXPERTS, start_zero, 0)
        lax.fori_loop(0, N_EXPERTS, wait_zero, 0)

    cp = pltpu.make_async_copy(dest_ref, dest_s, sem_s)
    cp.start()
    cp.wait()

    def row_copy(t, k):
        return pltpu.make_async_copy(h_ref.at[pl.ds(t, 1), :], xg_ref.at[pl.ds(dest_s[k, t], 1), :], sem)

    def start_rows(t8, c):
        base = pl.multiple_of(t8 * 8, 8)
        for j in range(8):
            for k in range(TOP_K):
                row_copy(base + j, k).start()
        return c

    def wait_rows(t, c):
        for k in range(TOP_K):
            row_copy(t, k).wait()
        return c

    lax.fori_loop(0, td // 8, start_rows, 0)
    lax.fori_loop(0, td, wait_rows, 0)


def _dispatch(h2, dest, pend0, rows):
    T, D = h2.shape
    td = TM_DISPATCH
    return pl.pallas_call(
        _dispatch_kernel,
        grid_spec=pltpu.PrefetchScalarGridSpec(
            num_scalar_prefetch=1,
            grid=(T // td,),
            in_specs=[
                pl.BlockSpec((td, D), lambda i, p: (i, 0)),
                pl.BlockSpec((TOP_K, td), lambda i, p: (0, i)),
            ],
            out_specs=pl.BlockSpec(memory_space=pl.ANY),
            scratch_shapes=[
                pltpu.SMEM((TOP_K, td), I32),
                pltpu.VMEM((EXPERT_BLOCK, D), F32),
                pltpu.SemaphoreType.DMA,
                pltpu.SemaphoreType.DMA,
            ],
        ),
        out_shape=jax.ShapeDtypeStruct((rows, D), F32),
        compiler_params=_cparams(("arbitrary",)),
        name="dispatch",
    )(pend0, h2, dest)


def _experts_kernel(bexp_ref, first_ref, slot_ref, next_ref, nused_ref,
                    x_ref, wg_hbm, wu_hbm, wd_hbm, y_ref,
                    wg_f, wu_f, wd_f, wg_b, wu_b, wd_b, sem):
    i = pl.program_id(0)

    def weight_copies(e, s):
        return (pltpu.make_async_copy(wg_hbm.at[e], wg_f.at[s], sem.at[s, 0]),
                pltpu.make_async_copy(wu_hbm.at[e], wu_f.at[s], sem.at[s, 1]),
                pltpu.make_async_copy(wd_hbm.at[e], wd_f.at[s], sem.at[s, 2]))

    @pl.when(i == 0)
    def _():
        for cp in weight_copies(bexp_ref[0], 0):
            cp.start()

    @pl.when(i < nused_ref[0])
    def _():
        @pl.when(first_ref[i] == 1)
        def _():
            s = slot_ref[i]
            for cp in weight_copies(bexp_ref[i], s):
                cp.wait()

            @pl.when(next_ref[i] >= 0)
            def _():
                for cp in weight_copies(next_ref[i], 1 - s):
                    cp.start()

            wg_b[...] = wg_f[s].astype(BF16)
            wu_b[...] = wu_f[s].astype(BF16)
            wd_b[...] = wd_f[s].astype(BF16)

        x = x_ref[...].astype(BF16)
        g = jnp.dot(x, wg_b[...], preferred_element_type=F32)
        u = jnp.dot(x, wu_b[...], preferred_element_type=F32)
        a = (_silu(g) * u).astype(BF16)
        y_ref[...] = jnp.dot(a, wd_b[...], preferred_element_type=F32)


def _experts(xg, block_exp, first, slot, next_exp, n_used, w_gate, w_up, w_down):
    rows, D = xg.shape
    bm = EXPERT_BLOCK
    F = D_EXPERT
    nb = rows // bm
    row_map = lambda i, be, fi, sl, nx, nu: (jnp.minimum(i, nu[0] - 1), 0)
    return pl.pallas_call(
        _experts_kernel,
        grid_spec=pltpu.PrefetchScalarGridSpec(
            num_scalar_prefetch=5,
            grid=(nb,),
            in_specs=[
                pl.BlockSpec((bm, D), row_map),
                pl.BlockSpec(memory_space=pl.ANY),
                pl.BlockSpec(memory_space=pl.ANY),
                pl.BlockSpec(memory_space=pl.ANY),
            ],
            out_specs=pl.BlockSpec((bm, D), row_map),
            scratch_shapes=[
                pltpu.VMEM((2, D, F), F32), pltpu.VMEM((2, D, F), F32), pltpu.VMEM((2, F, D), F32),
                pltpu.VMEM((D, F), BF16), pltpu.VMEM((D, F), BF16), pltpu.VMEM((F, D), BF16),
                pltpu.SemaphoreType.DMA((2, 3)),
            ],
        ),
        out_shape=jax.ShapeDtypeStruct((rows, D), F32),
        compiler_params=_cparams(("arbitrary",)),
        name="experts",
    )(block_exp, first, slot, next_exp, n_used, xg, w_gate, w_up, w_down)


def _combine_kernel(dest_ref, wt_ref, x1_ref, h_ref, mod_ref, sg_ref, su_ref, sd_ref, yg_ref, o_ref,
                    dest_s, buf, sem_s, sem):
    tc = TM_COMBINE
    cp = pltpu.make_async_copy(dest_ref, dest_s, sem_s)
    cp.start()
    cp.wait()

    def row_copy(t, k):
        return pltpu.make_async_copy(yg_ref.at[pl.ds(dest_s[k, t], 1), :], buf.at[k, pl.ds(t, 1), :], sem)

    def start_rows(t8, c):
        base = pl.multiple_of(t8 * 8, 8)
        for j in range(8):
            for k in range(TOP_K):
                row_copy(base + j, k).start()
        return c

    def wait_rows(t, c):
        for k in range(TOP_K):
            row_copy(t, k).wait()
        return c

    lax.fori_loop(0, tc // 8, start_rows, 0)

    hb = h_ref[...].astype(BF16)
    g = jnp.dot(hb, sg_ref[...], preferred_element_type=F32)
    u = jnp.dot(hb, su_ref[...], preferred_element_type=F32)
    ffn = jnp.dot((_silu(g) * u).astype(BF16), sd_ref[...], preferred_element_type=F32)

    lax.fori_loop(0, tc, wait_rows, 0)
    for k in range(TOP_K):
        ffn += buf[k] * wt_ref[:, k:k + 1]
    o_ref[...] = x1_ref[...] + mod_ref[0, 5:6, :] * ffn


def _combine(dest, wt_t, x1, h2, mod3, sg_b, su_b, sd_b, yg):
    T, D = x1.shape
    tc = TM_COMBINE
    per_batch = SEQ // tc
    F = D_EXPERT
    return pl.pallas_call(
        _combine_kernel,
        grid=(T // tc,),
        in_specs=[
            pl.BlockSpec((TOP_K, tc), lambda i: (0, i)),
            pl.BlockSpec((tc, TOP_K), lambda i: (i, 0)),
            pl.BlockSpec((tc, D), lambda i: (i, 0)),
            pl.BlockSpec((tc, D), lambda i: (i, 0)),
            pl.BlockSpec((1, 6, D), lambda i: (i // per_batch, 0, 0)),
            pl.BlockSpec((D, F), lambda i: (0, 0)),
            pl.BlockSpec((D, F), lambda i: (0, 0)),
            pl.BlockSpec((F, D), lambda i: (0, 0)),
            pl.BlockSpec(memory_space=pl.ANY),
        ],
        out_specs=pl.BlockSpec((tc, D), lambda i: (i, 0)),
        out_shape=jax.ShapeDtypeStruct((T, D), F32),
        scratch_shapes=[
            pltpu.SMEM((TOP_K, tc), I32),
            pltpu.VMEM((TOP_K, tc, D), F32),
            pltpu.SemaphoreType.DMA,
            pltpu.SemaphoreType.DMA,
        ],
        compiler_params=_cparams(("arbitrary",)),
        name="combine",
    )(dest, wt_t, x1, h2, mod3, sg_b, su_b, sd_b, yg)


def kernel(x, c, ada_w, ada_b, mix_norm_g, ffn_norm_g, w_in, q_norm_g, k_norm_g, rel_bias, sgu_ln_g, sgu_ln_b, sgu_w, sgu_b, w_out, router_w, router_bias, shared_w_gate, shared_w_up, shared_w_down, expert_w_gate, expert_w_up, expert_w_down):
    B, S, D = x.shape
    assert S == SEQ and D == D_MODEL and ada_w.shape[0] == 1
    T = B * S
    x2 = x.reshape(T, D)

    mod3 = _modulation(c, ada_w[0], ada_b[0]).reshape(B, 6, D)

    proj = _in_projection(x2, mod3, mix_norm_g, w_in[0].astype(BF16), q_norm_g, k_norm_g)
    attn = _attention(proj.reshape(B, S, D_IN_PROJ), _bias_tables(rel_bias)).reshape(T, D_ATTN)
    causal = jnp.tril(jnp.ones((SGU_CHUNK, SGU_CHUNK), F32))
    sgu = _spatial_gating(proj, (sgu_w[0] * causal).astype(BF16), sgu_ln_g, sgu_ln_b, sgu_b[0].T)
    x1, h2 = _out_projection(attn, sgu, x2, mod3, ffn_norm_g, w_out[0].astype(BF16))

    idx, wts, rank, counts = _route(h2, router_w[0].T.astype(BF16), router_bias[0])

    bm = EXPERT_BLOCK
    counts = counts.reshape(N_EXPERTS)
    padded = (counts + bm - 1) // bm * bm
    pends = jnp.cumsum(padded)
    pstarts = pends - padded
    n_blocks = T * TOP_K // bm + N_EXPERTS
    block_start = jnp.arange(n_blocks, dtype=I32) * bm
    block_exp = jnp.minimum(jnp.sum((pends[None, :] <= block_start[:, None]).astype(I32), axis=1), N_EXPERTS - 1)
    n_used = (pends[-1] // bm).astype(I32).reshape(1)
    eids = jnp.arange(N_EXPERTS, dtype=I32)
    dest = jnp.sum(jnp.where(idx[:, :, None] == eids, pstarts.astype(I32), 0), axis=-1) + rank
    pend0 = jnp.concatenate([jnp.zeros((1,), I32), pends.astype(I32)])
    first = jnp.concatenate([jnp.ones((1,), I32), (block_exp[1:] != block_exp[:-1]).astype(I32)])
    slot = (jnp.cumsum(first) - 1) % 2
    later_used = jnp.where(counts > 0, eids, N_EXPERTS)
    next_used = lax.cummin(jnp.concatenate([later_used[1:], jnp.full((1,), N_EXPERTS, I32)]), reverse=True)
    next_used = jnp.where(next_used >= N_EXPERTS, -1, next_used)
    next_exp = jnp.sum(jnp.where(block_exp[:, None] == eids, next_used, 0), axis=-1)

    xg = _dispatch(h2, dest, pend0, n_blocks * bm)
    yg = _experts(xg, block_exp, first, slot.astype(I32), next_exp.astype(I32), n_used,
                  expert_w_gate[0], expert_w_up[0], expert_w_down[0])
    out = _combine(dest, wts.T, x1, h2, mod3, shared_w_gate[0].astype(BF16), shared_w_up[0].astype(BF16),
                   shared_w_down[0].astype(BF16), yg)
    return out.reshape(B, S, D)
```

```python
import functools
import math

import numpy as np
import jax
import jax.numpy as jnp
from jax import lax
from jax.experimental import pallas as pl
from jax.experimental.pallas import tpu as pltpu

F32 = jnp.float32
BF16 = jnp.bfloat16
I32 = jnp.int32
U32 = jnp.uint32

D_MODEL = 2048
SEQ = 2048
HEAD_DIM = 128
N_HEADS = 8
D_ATTN = N_HEADS * HEAD_DIM
D_SGU = 1024
N_SGU_GROUPS = 8
SGU_CHUNK = 128
D_IN_PROJ = 3 * D_ATTN + 2 * D_SGU
DILATED_GROUPS = ((128, 1), (512, 4), (2048, 16))
BAND = 128
NUM_REL_BUCKETS = 32
REL_MAX_DISTANCE = 2048
N_EXPERTS = 256
TOP_K = 8
N_EXPERT_GROUPS = 8
GROUP_SIZE = N_EXPERTS // N_EXPERT_GROUPS
TOPK_GROUPS = 4
D_EXPERT = 512
ROUTED_SCALE = 2.5
NORM_EPS = 1e-6
MASK_VALUE = -1e30

LANES = 128
VMEM_LIMIT = 56 * 1024 * 1024

TM_PROJ = 512
TM_SGU = 512
TM_OUT = 256
TM_ROUTE = 512
TM_DISPATCH = 256
TM_COMBINE = 128
EXPERT_BLOCK = 128


def _cparams(sem):
    return pltpu.CompilerParams(dimension_semantics=sem, vmem_limit_bytes=VMEM_LIMIT)


def _silu(v):
    return v * jax.nn.sigmoid(v)


def _gelu(v):
    return 0.5 * v * (1.0 + lax.erf(v * (1.0 / math.sqrt(2.0))))


def _pack_halves(v):
    h = v.shape[1] // 2
    lo = lax.bitcast_convert_type(v[:, :h].astype(BF16).astype(F32), U32)
    hi = lax.bitcast_convert_type(v[:, h:].astype(BF16).astype(F32), U32)
    return (hi & jnp.uint32(0xFFFF0000)) | (lo >> 16)


def _unpack_halves(w):
    lo = lax.bitcast_convert_type(w << 16, F32)
    hi = lax.bitcast_convert_type(w & jnp.uint32(0xFFFF0000), F32)
    return lo, hi


def _mod_kernel(c_ref, w_ref, b_ref, o_ref):
    ca = _silu(c_ref[...]).astype(BF16)
    o_ref[...] = jnp.dot(ca, w_ref[...].astype(BF16), preferred_element_type=F32) + b_ref[...]


def _modulation(c, ada_w, ada_b):
    B, D = c.shape
    N = ada_w.shape[1]
    tn = 1024
    return pl.pallas_call(
        _mod_kernel,
        grid=(N // tn,),
        in_specs=[
            pl.BlockSpec((B, D), lambda j: (0, 0)),
            pl.BlockSpec((D, tn), lambda j: (0, j)),
            pl.BlockSpec((1, tn), lambda j: (0, j)),
        ],
        out_specs=pl.BlockSpec((B, tn), lambda j: (0, j)),
        out_shape=jax.ShapeDtypeStruct((B, N), F32),
        compiler_params=_cparams(("arbitrary",)),
        name="modulation",
    )(c, ada_w, ada_b.reshape(1, N))


def _inproj_kernel(x_ref, mod_ref, g_ref, w_ref, qg_ref, kg_ref, o_ref, h_ref):
    j = pl.program_id(1)

    @pl.when(j == 0)
    def _():
        x = x_ref[...]
        r = lax.rsqrt(jnp.mean(x * x, axis=-1, keepdims=True) + NORM_EPS)
        h = (x * r) * g_ref[...] * (1.0 + mod_ref[0, 1:2, :]) + mod_ref[0, 0:1, :]
        h_ref[...] = h.astype(BF16)

    acc = jnp.dot(h_ref[...], w_ref[...], preferred_element_type=F32)

    def head_norm(gain_ref, scale):
        for hd in range(N_HEADS):
            a = acc[:, hd * HEAD_DIM:(hd + 1) * HEAD_DIM]
            r = lax.rsqrt(jnp.mean(a * a, axis=-1, keepdims=True) + NORM_EPS)
            o_ref[:, hd * HEAD_DIM:(hd + 1) * HEAD_DIM] = ((a * r) * gain_ref[...] * scale).astype(BF16)

    @pl.when(j == 0)
    def _():
        head_norm(qg_ref, HEAD_DIM ** -0.5)

    @pl.when(j == 1)
    def _():
        head_norm(kg_ref, 1.0)

    @pl.when(j >= 2)
    def _():
        o_ref[...] = acc.astype(BF16)


def _in_projection(x2, mod3, norm_g, w_in_b, q_g, k_g):
    T, D = x2.shape
    tm, tn = TM_PROJ, D_ATTN
    per_batch = SEQ // tm
    return pl.pallas_call(
        _inproj_kernel,
        grid=(T // tm, D_IN_PROJ // tn),
        in_specs=[
            pl.BlockSpec((tm, D), lambda i, j: (i, 0)),
            pl.BlockSpec((1, 6, D), lambda i, j: (i // per_batch, 0, 0)),
            pl.BlockSpec((1, D), lambda i, j: (0, 0)),
            pl.BlockSpec((D, tn), lambda i, j: (0, j)),
            pl.BlockSpec((1, HEAD_DIM), lambda i, j: (0, 0)),
            pl.BlockSpec((1, HEAD_DIM), lambda i, j: (0, 0)),
        ],
        out_specs=pl.BlockSpec((tm, tn), lambda i, j: (i, j)),
        out_shape=jax.ShapeDtypeStruct((T, D_IN_PROJ), BF16),
        scratch_shapes=[pltpu.VMEM((tm, D), BF16)],
        compiler_params=_cparams(("arbitrary", "arbitrary")),
        name="in_projection",
    )(x2, mod3, norm_g, w_in_b, q_g, k_g)


def _bias_tables(rel_bias):
    n = BAND
    qi = jnp.arange(n)[:, None]
    ki = jnp.arange(2 * n)[None, :]
    steps = n + qi - ki
    in_band = (steps >= 0) & (steps <= n)
    max_exact = NUM_REL_BUCKETS // 2
    tabs = []
    for _, dilation in DILATED_GROUPS:
        dist = jnp.clip(steps, 0, n) * dilation
        nf = jnp.maximum(dist, 1).astype(F32)
        large = max_exact + (jnp.log(nf / max_exact) / math.log(REL_MAX_DISTANCE / max_exact)
                             * (NUM_REL_BUCKETS - max_exact)).astype(I32)
        large = jnp.minimum(large, NUM_REL_BUCKETS - 1)
        bucket = jnp.where(dist < max_exact, dist, large)
        onehot = jax.nn.one_hot(bucket, NUM_REL_BUCKETS, dtype=F32)
        b = jnp.einsum("qkb,bh->hqk", onehot, rel_bias.astype(F32), precision=lax.Precision.HIGHEST)
        tabs.append(jnp.where(in_band[None], b, MASK_VALUE))
    return jnp.stack(tabs)


def _attn_kernel(q_ref, k_ref, v_ref, bias_ref, o_ref, qf, kf, vf, q4, k4, v4,
                 o0, l0, o1, l1, o2, l2, stage):
    n = BAND
    quarter = SEQ // 4

    for src, nat, res in ((q_ref, qf, q4), (k_ref, kf, k4), (v_ref, vf, v4)):
        nat[...] = src[...].astype(F32)
        for r in range(4):
            res[r * quarter:(r + 1) * quarter, :] = nat[pl.ds(r, quarter, stride=4), :]

    def piece(qb, kb, vb, bias, o_out, l_out, rows):
        l = lax.dot_general(qb, kb, (((1,), (1,)), ((), ())), preferred_element_type=F32) + bias
        m = jnp.max(l, axis=-1, keepdims=True)
        p = jnp.exp(l - m)
        s = jnp.sum(p, axis=-1, keepdims=True)
        o = jnp.dot(p.astype(BF16), vb, preferred_element_type=F32)
        o_out[rows, :] = o / s
        l_out[rows, :] = jnp.broadcast_to(m + jnp.log(s), (n, LANES))

    for i in range(SEQ // n):
        rows = slice(i * n, (i + 1) * n)
        if i == 0:
            piece(q_ref[rows, :], k_ref[rows, :], v_ref[rows, :], bias_ref[0, :, n:], o0, l0, rows)
        else:
            krows = slice((i - 1) * n, (i + 1) * n)
            piece(q_ref[rows, :], k_ref[krows, :], v_ref[krows, :], bias_ref[0], o0, l0, rows)

    for r in range(4):
        for blk in range(quarter // n):
            base = r * quarter + blk * n
            rows = slice(base, base + n)
            krows = rows if blk == 0 else slice(base - n, base + n)
            bias = bias_ref[1, :, n:] if blk == 0 else bias_ref[1]
            piece(q4[rows, :].astype(BF16), k4[krows, :].astype(BF16), v4[krows, :].astype(BF16),
                  bias, o1, l1, rows)

    for r in range(4):
        for a in range(4):
            rows = pl.ds(r * quarter + a, n, stride=4)
            piece(q4[rows, :].astype(BF16), k4[rows, :].astype(BF16), v4[rows, :].astype(BF16),
                  bias_ref[2, :, n:], o2, l2, rows)

    for r in range(4):
        for c in range(quarter // n):
            rows = slice(r * quarter + c * n, r * quarter + (c + 1) * n)
            nat = pl.ds(r + 4 * c * n, n, stride=4)
            a0, a1, a2 = l0[nat, :], l1[rows, :], l2[rows, :]
            m = jnp.maximum(jnp.maximum(a0, a1), a2)
            e0, e1, e2 = jnp.exp(a0 - m), jnp.exp(a1 - m), jnp.exp(a2 - m)
            mix = (e0 * o0[nat, :] + e1 * o1[rows, :] + e2 * o2[rows, :]) / (e0 + e1 + e2)
            stage[nat, :] = mix
    o_ref[...] = stage[...].astype(BF16)


def _attention(proj3, bias_tabs):
    B = proj3.shape[0]
    blk = lambda off: pl.BlockSpec((None, SEQ, HEAD_DIM), lambda b, h: (b, 0, off + h))
    return pl.pallas_call(
        _attn_kernel,
        grid=(B, N_HEADS),
        in_specs=[
            blk(0), blk(N_HEADS), blk(2 * N_HEADS),
            pl.BlockSpec((3, None, BAND, 2 * BAND), lambda b, h: (0, h, 0, 0)),
        ],
        out_specs=pl.BlockSpec((None, SEQ, HEAD_DIM), lambda b, h: (b, 0, h)),
        out_shape=jax.ShapeDtypeStruct((B, SEQ, D_ATTN), BF16),
        scratch_shapes=[pltpu.VMEM((SEQ, HEAD_DIM), F32) for _ in range(13)],
        compiler_params=_cparams(("arbitrary", "arbitrary")),
        name="dilated_attention",
    )(proj3, proj3, proj3, bias_tabs)


def _sgu_kernel(u_ref, z_ref, w_ref, g_ref, b_ref, bs_ref, o_ref):
    n = SGU_CHUNK
    for c in range(TM_SGU // n):
        rs = slice(c * n, (c + 1) * n)
        for g in range(N_SGU_GROUPS):
            cs = slice(g * n, (g + 1) * n)
            z = _gelu(z_ref[rs, cs].astype(F32))
            mu = jnp.mean(z, axis=-1, keepdims=True)
            zc = z - mu
            var = jnp.mean(zc * zc, axis=-1, keepdims=True)
            zn = (zc * lax.rsqrt(var + NORM_EPS)) * g_ref[:, cs] + b_ref[:, cs]
            mixed = jnp.dot(w_ref[g], zn.astype(BF16), preferred_element_type=F32) + bs_ref[:, g:g + 1]
            u = _gelu(u_ref[rs, cs].astype(F32))
            o_ref[rs, cs] = (u * mixed).astype(BF16)


def _spatial_gating(proj, w_causal_b, ln_g, ln_b, bs_t):
    T = proj.shape[0]
    tm = TM_SGU
    ucol = 3 * D_ATTN // D_SGU
    return pl.pallas_call(
        _sgu_kernel,
        grid=(T // tm,),
        in_specs=[
            pl.BlockSpec((tm, D_SGU), lambda i: (i, ucol)),
            pl.BlockSpec((tm, D_SGU), lambda i: (i, ucol + 1)),
            pl.BlockSpec((N_SGU_GROUPS, SGU_CHUNK, SGU_CHUNK), lambda i: (0, 0, 0)),
            pl.BlockSpec((1, D_SGU), lambda i: (0, 0)),
            pl.BlockSpec((1, D_SGU), lambda i: (0, 0)),
            pl.BlockSpec((SGU_CHUNK, N_SGU_GROUPS), lambda i: (0, 0)),
        ],
        out_specs=pl.BlockSpec((tm, D_SGU), lambda i: (i, 0)),
        out_shape=jax.ShapeDtypeStruct((T, D_SGU), BF16),
        compiler_params=_cparams(("arbitrary",)),
        name="spatial_gating",
    )(proj, proj, w_causal_b, ln_g, ln_b, bs_t)


def _outproj_kernel(a_ref, s_ref, x_ref, mod_ref, g_ref, w_ref, x1_ref, h2_ref, h2p_ref):
    mixed = jnp.dot(a_ref[...], w_ref[0:D_ATTN, :], preferred_element_type=F32)
    mixed += jnp.dot(s_ref[...], w_ref[D_ATTN:, :], preferred_element_type=F32)
    x1 = x_ref[...] + mod_ref[0, 2:3, :] * mixed
    x1_ref[...] = x1
    r = lax.rsqrt(jnp.mean(x1 * x1, axis=-1, keepdims=True) + NORM_EPS)
    h2 = (x1 * r) * g_ref[...] * (1.0 + mod_ref[0, 4:5, :]) + mod_ref[0, 3:4, :]
    h2_ref[...] = h2.astype(BF16)
    h2p_ref[...] = _pack_halves(h2)


def _out_projection(attn, sgu, x2, mod3, norm_g, w_out_b):
    T, D = x2.shape
    tm = TM_OUT
    per_batch = SEQ // tm
    return pl.pallas_call(
        _outproj_kernel,
        grid=(T // tm,),
        in_specs=[
            pl.BlockSpec((tm, D_ATTN), lambda i: (i, 0)),
            pl.BlockSpec((tm, D_SGU), lambda i: (i, 0)),
            pl.BlockSpec((tm, D), lambda i: (i, 0)),
            pl.BlockSpec((1, 6, D), lambda i: (i // per_batch, 0, 0)),
            pl.BlockSpec((1, D), lambda i: (0, 0)),
            pl.BlockSpec((D_ATTN + D_SGU, D), lambda i: (0, 0)),
        ],
        out_specs=[pl.BlockSpec((tm, D), lambda i: (i, 0)), pl.BlockSpec((tm, D), lambda i: (i, 0)),
                   pl.BlockSpec((tm, D // 2), lambda i: (i, 0))],
        out_shape=[jax.ShapeDtypeStruct((T, D), F32), jax.ShapeDtypeStruct((T, D), BF16),
                   jax.ShapeDtypeStruct((T, D // 2), U32)],
        compiler_params=_cparams(("arbitrary",)),
        name="out_projection",
    )(attn, sgu, x2, mod3, norm_g, w_out_b)


def _router_kernel(h_ref, w_ref, b_ref, idx_ref, wt_ref, rank_ref, cnt_ref, run_ref):
    i = pl.program_id(0)
    tr = TM_ROUTE
    E = N_EXPERTS

    @pl.when(i == 0)
    def _():
        run_ref[...] = jnp.zeros_like(run_ref)

    logits = lax.dot_general(w_ref[...], h_ref[...], (((1,), (1,)), ((), ())), preferred_element_type=F32)
    scores = jax.nn.sigmoid(logits)
    sel = scores + b_ref[...]

    slabs = [sel[g * GROUP_SIZE:(g + 1) * GROUP_SIZE, :] for g in range(N_EXPERT_GROUPS)]
    si = lax.broadcasted_iota(I32, (GROUP_SIZE, tr), 0).astype(F32)
    gs = []
    for slab in slabs:
        m1 = jnp.max(slab, axis=0, keepdims=True)
        first = jnp.min(jnp.where(slab == m1, si, float(GROUP_SIZE)), axis=0, keepdims=True)
        m2 = jnp.max(jnp.where(si == first, -jnp.inf, slab), axis=0, keepdims=True)
        gs.append(m1 + m2)

    kept = []
    for g in range(N_EXPERT_GROUPS):
        beaten = jnp.zeros((1, tr), F32)
        for o in range(N_EXPERT_GROUPS):
            if o < g:
                beaten += (gs[o] >= gs[g]).astype(F32)
            elif o > g:
                beaten += (gs[o] > gs[g]).astype(F32)
        kept.append(jnp.where(beaten < TOPK_GROUPS, slabs[g], -jnp.inf))
    masked = jnp.concatenate(kept, axis=0)

    ei = lax.broadcasted_iota(I32, (E, tr), 0).astype(F32)
    picks, pick_scores = [], []
    onehot_sum = jnp.zeros((E, tr), F32)
    for k in range(TOP_K):
        m = jnp.max(masked, axis=0, keepdims=True)
        ik = jnp.min(jnp.where(masked == m, ei, float(E)), axis=0, keepdims=True)
        oh = ei == ik
        pick_scores.append(jnp.sum(jnp.where(oh, scores, 0.0), axis=0, keepdims=True))
        masked = jnp.where(oh, -jnp.inf, masked)
        onehot_sum += oh.astype(F32)
        picks.append(ik)

    ti = lax.broadcasted_iota(I32, (tr, tr), 0)
    tj = lax.broadcasted_iota(I32, (tr, tr), 1)
    before = (ti < tj).astype(BF16)
    prior = jnp.dot(onehot_sum.astype(BF16), before, preferred_element_type=F32) + run_ref[...]

    total = pick_scores[0]
    for k in range(1, TOP_K):
        total += pick_scores[k]
    for k in range(TOP_K):
        idx_ref[k:k + 1, :] = picks[k].astype(I32)
        wt_ref[k:k + 1, :] = pick_scores[k] / total * ROUTED_SCALE
        rk = jnp.sum(jnp.where(ei == picks[k], prior, 0.0), axis=0, keepdims=True)
        rank_ref[k:k + 1, :] = rk.astype(I32)

    run_ref[...] += jnp.sum(onehot_sum, axis=1, keepdims=True)
    cnt_ref[...] = run_ref[...].astype(I32)


def _route(h2, router_wt_b, router_bias):
    T, D = h2.shape
    tr = TM_ROUTE
    E = N_EXPERTS
    row_spec = pl.BlockSpec((TOP_K, tr), lambda i: (0, i))
    return pl.pallas_call(
        _router_kernel,
        grid=(T // tr,),
        in_specs=[
            pl.BlockSpec((tr, D), lambda i: (i, 0)),
            pl.BlockSpec((E, D), lambda i: (0, 0)),
            pl.BlockSpec((E, 1), lambda i: (0, 0)),
        ],
        out_specs=[row_spec, row_spec, row_spec, pl.BlockSpec((E, 1), lambda i: (0, 0))],
        out_shape=[
            jax.ShapeDtypeStruct((TOP_K, T), I32),
            jax.ShapeDtypeStruct((TOP_K, T), F32),
            jax.ShapeDtypeStruct((TOP_K, T), I32),
            jax.ShapeDtypeStruct((E, 1), I32),
        ],
        scratch_shapes=[pltpu.VMEM((E, 1), F32)],
        compiler_params=_cparams(("arbitrary",)),
        name="router",
    )(h2, router_wt_b, router_bias.reshape(E, 1))


def _dispatch_kernel(pend_ref, h_ref, dest_ref, xg_ref, dest_s, zero_ref, sem_s, sem):
    i = pl.program_id(0)
    td = TM_DISPATCH
    blk = EXPERT_BLOCK

    def zero_copy(e):
        start = pl.multiple_of(pend_ref[e + 1] - blk, blk)
        return pltpu.make_async_copy(zero_ref, xg_ref.at[pl.ds(start, blk), :], sem)

    @pl.when(i == 0)
    def _():
        zero_ref[...] = jnp.zeros_like(zero_ref)

        def start_zero(e, c):
            @pl.when(pend_ref[e + 1] > pend_ref[e])
            def _():
                zero_copy(e).start()
            return c

        def wait_zero(e, c):
            @pl.when(pend_ref[e + 1] > pend_ref[e])
            def _():
                zero_copy(e).wait()
            return c

        lax.fori_loop(0, N_EXPERTS, start_zero, 0)
        lax.fori_loop(0, N_EXPERTS, wait_zero, 0)

    cp = pltpu.make_async_copy(dest_ref, dest_s, sem_s)
    cp.start()
    cp.wait()

    def row_copy(t, k):
        return pltpu.make_async_copy(h_ref.at[pl.ds(t, 1), :], xg_ref.at[pl.ds(dest_s[k, t], 1), :], sem)

    def start_rows(t8, c):
        base = pl.multiple_of(t8 * 8, 8)
        for j in range(8):
            for k in range(TOP_K):
                row_copy(base + j, k).start()
        return c

    def wait_rows(t, c):
        for k in range(TOP_K):
            row_copy(t, k).wait()
        return c

    lax.fori_loop(0, td // 8, start_rows, 0)
    lax.fori_loop(0, td, wait_rows, 0)


def _dispatch(h2p, dest, pend0, rows):
    T, D = h2p.shape
    td = TM_DISPATCH
    return pl.pallas_call(
        _dispatch_kernel,
        grid_spec=pltpu.PrefetchScalarGridSpec(
            num_scalar_prefetch=1,
            grid=(T // td,),
            in_specs=[
                pl.BlockSpec((td, D), lambda i, p: (i, 0)),
                pl.BlockSpec((TOP_K, td), lambda i, p: (0, i)),
            ],
            out_specs=pl.BlockSpec(memory_space=pl.ANY),
            scratch_shapes=[
                pltpu.SMEM((TOP_K, td), I32),
                pltpu.VMEM((EXPERT_BLOCK, D), U32),
                pltpu.SemaphoreType.DMA,
                pltpu.SemaphoreType.DMA,
            ],
        ),
        out_shape=jax.ShapeDtypeStruct((rows, D), U32),
        compiler_params=_cparams(("arbitrary",)),
        name="dispatch",
    )(pend0, h2p, dest)


def _experts_kernel(bexp_ref, first_ref, slot_ref, next_ref, nused_ref,
                    x_ref, wg_hbm, wu_hbm, wd_hbm, y_ref,
                    wg_f, wu_f, wd_f, wg_b, wu_b, wd_b, sem):
    i = pl.program_id(0)

    def weight_copies(e, s):
        return (pltpu.make_async_copy(wg_hbm.at[e], wg_f.at[s], sem.at[s, 0]),
                pltpu.make_async_copy(wu_hbm.at[e], wu_f.at[s], sem.at[s, 1]),
                pltpu.make_async_copy(wd_hbm.at[e], wd_f.at[s], sem.at[s, 2]))

    @pl.when(i == 0)
    def _():
        for cp in weight_copies(bexp_ref[0], 0):
            cp.start()

    @pl.when(i < nused_ref[0])
    def _():
        @pl.when(first_ref[i] == 1)
        def _():
            s = slot_ref[i]
            for cp in weight_copies(bexp_ref[i], s):
                cp.wait()

            @pl.when(next_ref[i] >= 0)
            def _():
                for cp in weight_copies(next_ref[i], 1 - s):
                    cp.start()

            wg_b[...] = wg_f[s].astype(BF16)
            wu_b[...] = wu_f[s].astype(BF16)
            wd_b[...] = wd_f[s].astype(BF16)

        lo, hi = _unpack_halves(x_ref[...])
        x = jnp.concatenate([lo.astype(BF16), hi.astype(BF16)], axis=1)
        g = jnp.dot(x, wg_b[...], preferred_element_type=F32)
        u = jnp.dot(x, wu_b[...], preferred_element_type=F32)
        a = (_silu(g) * u).astype(BF16)
        y_ref[...] = _pack_halves(jnp.dot(a, wd_b[...], preferred_element_type=F32))


def _experts(xg, block_exp, first, slot, next_exp, n_used, w_gate, w_up, w_down):
    rows = xg.shape[0]
    D = D_MODEL
    bm = EXPERT_BLOCK
    F = D_EXPERT
    nb = rows // bm
    row_map = lambda i, be, fi, sl, nx, nu: (jnp.minimum(i, nu[0] - 1), 0)
    return pl.pallas_call(
        _experts_kernel,
        grid_spec=pltpu.PrefetchScalarGridSpec(
            num_scalar_prefetch=5,
            grid=(nb,),
            in_specs=[
                pl.BlockSpec((bm, D // 2), row_map),
                pl.BlockSpec(memory_space=pl.ANY),
                pl.BlockSpec(memory_space=pl.ANY),
                pl.BlockSpec(memory_space=pl.ANY),
            ],
            out_specs=pl.BlockSpec((bm, D // 2), row_map),
            scratch_shapes=[
                pltpu.VMEM((2, D, F), F32), pltpu.VMEM((2, D, F), F32), pltpu.VMEM((2, F, D), F32),
                pltpu.VMEM((D, F), BF16), pltpu.VMEM((D, F), BF16), pltpu.VMEM((F, D), BF16),
                pltpu.SemaphoreType.DMA((2, 3)),
            ],
        ),
        out_shape=jax.ShapeDtypeStruct((rows, D // 2), U32),
        compiler_params=_cparams(("arbitrary",)),
        name="experts",
    )(block_exp, first, slot, next_exp, n_used, xg, w_gate, w_up, w_down)


def _combine_kernel(dest_ref, wt_ref, x1_ref, h_ref, mod_ref, sg_ref, su_ref, sd_ref, yg_ref, o_ref,
                    dest_s, buf, sem_s, sem):
    tc = TM_COMBINE
    cp = pltpu.make_async_copy(dest_ref, dest_s, sem_s)
    cp.start()
    cp.wait()

    def row_copy(t, k):
        return pltpu.make_async_copy(yg_ref.at[pl.ds(dest_s[k, t], 1), :], buf.at[k, pl.ds(t, 1), :], sem)

    def start_rows(t8, c):
        base = pl.multiple_of(t8 * 8, 8)
        for j in range(8):
            for k in range(TOP_K):
                row_copy(base + j, k).start()
        return c

    def wait_rows(t, c):
        for k in range(TOP_K):
            row_copy(t, k).wait()
        return c

    lax.fori_loop(0, tc // 8, start_rows, 0)

    hb = h_ref[...]
    g = jnp.dot(hb, sg_ref[...], preferred_element_type=F32)
    u = jnp.dot(hb, su_ref[...], preferred_element_type=F32)
    ffn = jnp.dot((_silu(g) * u).astype(BF16), sd_ref[...], preferred_element_type=F32)

    lax.fori_loop(0, tc, wait_rows, 0)
    half = D_MODEL // 2
    ffn_lo, ffn_hi = ffn[:, :half], ffn[:, half:]
    for k in range(TOP_K):
        lo, hi = _unpack_halves(buf[k])
        w = wt_ref[:, k:k + 1]
        ffn_lo += lo * w
        ffn_hi += hi * w
    o_ref[:, :half] = x1_ref[:, :half] + mod_ref[0, 5:6, :half] * ffn_lo
    o_ref[:, half:] = x1_ref[:, half:] + mod_ref[0, 5:6, half:] * ffn_hi


def _combine(dest, wt_t, x1, h2, mod3, sg_b, su_b, sd_b, yg):
    T, D = x1.shape
    tc = TM_COMBINE
    per_batch = SEQ // tc
    F = D_EXPERT
    return pl.pallas_call(
        _combine_kernel,
        grid=(T // tc,),
        in_specs=[
            pl.BlockSpec((TOP_K, tc), lambda i: (0, i)),
            pl.BlockSpec((tc, TOP_K), lambda i: (i, 0)),
            pl.BlockSpec((tc, D), lambda i: (i, 0)),
            pl.BlockSpec((tc, D), lambda i: (i, 0)),
            pl.BlockSpec((1, 6, D), lambda i: (i // per_batch, 0, 0)),
            pl.BlockSpec((D, F), lambda i: (0, 0)),
            pl.BlockSpec((D, F), lambda i: (0, 0)),
            pl.BlockSpec((F, D), lambda i: (0, 0)),
            pl.BlockSpec(memory_space=pl.ANY),
        ],
        out_specs=pl.BlockSpec((tc, D), lambda i: (i, 0)),
        out_shape=jax.ShapeDtypeStruct((T, D), F32),
        scratch_shapes=[
            pltpu.SMEM((TOP_K, tc), I32),
            pltpu.VMEM((TOP_K, tc, D // 2), U32),
            pltpu.SemaphoreType.DMA,
            pltpu.SemaphoreType.DMA,
        ],
        compiler_params=_cparams(("arbitrary",)),
        name="combine",
    )(dest, wt_t, x1, h2, mod3, sg_b, su_b, sd_b, yg)


def kernel(x, c, ada_w, ada_b, mix_norm_g, ffn_norm_g, w_in, q_norm_g, k_norm_g, rel_bias, sgu_ln_g, sgu_ln_b, sgu_w, sgu_b, w_out, router_w, router_bias, shared_w_gate, shared_w_up, shared_w_down, expert_w_gate, expert_w_up, expert_w_down):
    B, S, D = x.shape
    assert S == SEQ and D == D_MODEL and ada_w.shape[0] == 1
    T = B * S
    x2 = x.reshape(T, D)

    mod3 = _modulation(c, ada_w[0], ada_b[0]).reshape(B, 6, D)

    proj = _in_projection(x2, mod3, mix_norm_g, w_in[0].astype(BF16), q_norm_g, k_norm_g)
    attn = _attention(proj.reshape(B, S, D_IN_PROJ), _bias_tables(rel_bias)).reshape(T, D_ATTN)
    causal = jnp.tril(jnp.ones((SGU_CHUNK, SGU_CHUNK), F32))
    sgu = _spatial_gating(proj, (sgu_w[0] * causal).astype(BF16), sgu_ln_g, sgu_ln_b, sgu_b[0].T)
    x1, h2, h2p = _out_projection(attn, sgu, x2, mod3, ffn_norm_g, w_out[0].astype(BF16))

    idx, wts, rank, counts = _route(h2, router_w[0].T.astype(BF16), router_bias[0])

    bm = EXPERT_BLOCK
    counts = counts.reshape(N_EXPERTS)
    padded = (counts + bm - 1) // bm * bm
    pends = jnp.cumsum(padded)
    pstarts = pends - padded
    n_blocks = T * TOP_K // bm + N_EXPERTS
    block_start = jnp.arange(n_blocks, dtype=I32) * bm
    block_exp = jnp.minimum(jnp.sum((pends[None, :] <= block_start[:, None]).astype(I32), axis=1), N_EXPERTS - 1)
    n_used = (pends[-1] // bm).astype(I32).reshape(1)
    eids = jnp.arange(N_EXPERTS, dtype=I32)
    dest = jnp.sum(jnp.where(idx[:, :, None] == eids, pstarts.astype(I32), 0), axis=-1) + rank
    pend0 = jnp.concatenate([jnp.zeros((1,), I32), pends.astype(I32)])
    first = jnp.concatenate([jnp.ones((1,), I32), (block_exp[1:] != block_exp[:-1]).astype(I32)])
    slot = (jnp.cumsum(first) - 1) % 2
    later_used = jnp.where(counts > 0, eids, N_EXPERTS)
    next_used = lax.cummin(jnp.concatenate([later_used[1:], jnp.full((1,), N_EXPERTS, I32)]), reverse=True)
    next_used = jnp.where(next_used >= N_EXPERTS, -1, next_used)
    next_exp = jnp.sum(jnp.where(block_exp[:, None] == eids, next_used, 0), axis=-1)

    xg = _dispatch(h2p, dest, pend0, n_blocks * bm)
    yg = _experts(xg, block_exp, first, slot.astype(I32), next_exp.astype(I32), n_used,
                  expert_w_gate[0], expert_w_up[0], expert_w_down[0])
    out = _combine(dest, wts.T, x1, h2, mod3, shared_w_gate[0].astype(BF16), shared_w_up[0].astype(BF16),
                   shared_w_down[0].astype(BF16), yg)
    return out.reshape(B, S, D)
```

```python
import functools
import math

import numpy as np
import jax
import jax.numpy as jnp
from jax import lax
from jax.experimental import pallas as pl
from jax.experimental.pallas import tpu as pltpu

F32 = jnp.float32
BF16 = jnp.bfloat16
I32 = jnp.int32
U32 = jnp.uint32

D_MODEL = 2048
SEQ = 2048
HEAD_DIM = 128
N_HEADS = 8
D_ATTN = N_HEADS * HEAD_DIM
D_SGU = 1024
N_SGU_GROUPS = 8
SGU_CHUNK = 128
D_IN_PROJ = 3 * D_ATTN + 2 * D_SGU
DILATED_GROUPS = ((128, 1), (512, 4), (2048, 16))
BAND = 128
NUM_REL_BUCKETS = 32
REL_MAX_DISTANCE = 2048
N_EXPERTS = 256
TOP_K = 8
N_EXPERT_GROUPS = 8
GROUP_SIZE = N_EXPERTS // N_EXPERT_GROUPS
TOPK_GROUPS = 4
D_EXPERT = 512
ROUTED_SCALE = 2.5
NORM_EPS = 1e-6
MASK_VALUE = -1e30

LANES = 128
VMEM_LIMIT = 56 * 1024 * 1024

TM_PROJ = 512
TM_SGU = 512
TM_OUT = 256
TM_ROUTE = 512
TM_DISPATCH = 256
TM_COMBINE = 128
EXPERT_BLOCK = 128
WEIGHT_CHUNKS = 4


def _cparams(sem):
    return pltpu.CompilerParams(dimension_semantics=sem, vmem_limit_bytes=VMEM_LIMIT)


def _silu(v):
    return v * jax.nn.sigmoid(v)


def _gelu(v):
    return 0.5 * v * (1.0 + lax.erf(v * (1.0 / math.sqrt(2.0))))


def _pack_halves(v):
    h = v.shape[1] // 2
    lo = lax.bitcast_convert_type(v[:, :h].astype(BF16).astype(F32), U32)
    hi = lax.bitcast_convert_type(v[:, h:].astype(BF16).astype(F32), U32)
    return (hi & jnp.uint32(0xFFFF0000)) | (lo >> 16)


def _unpack_halves(w):
    lo = lax.bitcast_convert_type(w << 16, F32)
    hi = lax.bitcast_convert_type(w & jnp.uint32(0xFFFF0000), F32)
    return lo, hi


def _mod_kernel(c_ref, w_ref, b_ref, o_ref):
    ca = _silu(c_ref[...]).astype(BF16)
    o_ref[...] = jnp.dot(ca, w_ref[...].astype(BF16), preferred_element_type=F32) + b_ref[...]


def _modulation(c, ada_w, ada_b):
    B, D = c.shape
    N = ada_w.shape[1]
    tn = 1024
    return pl.pallas_call(
        _mod_kernel,
        grid=(N // tn,),
        in_specs=[
            pl.BlockSpec((B, D), lambda j: (0, 0)),
            pl.BlockSpec((D, tn), lambda j: (0, j)),
            pl.BlockSpec((1, tn), lambda j: (0, j)),
        ],
        out_specs=pl.BlockSpec((B, tn), lambda j: (0, j)),
        out_shape=jax.ShapeDtypeStruct((B, N), F32),
        compiler_params=_cparams(("arbitrary",)),
        name="modulation",
    )(c, ada_w, ada_b.reshape(1, N))


def _inproj_kernel(x_ref, mod_ref, g_ref, w_ref, qg_ref, kg_ref, o_ref, h_ref):
    j = pl.program_id(1)

    @pl.when(j == 0)
    def _():
        x = x_ref[...]
        r = lax.rsqrt(jnp.mean(x * x, axis=-1, keepdims=True) + NORM_EPS)
        h = (x * r) * g_ref[...] * (1.0 + mod_ref[0, 1:2, :]) + mod_ref[0, 0:1, :]
        h_ref[...] = h.astype(BF16)

    acc = jnp.dot(h_ref[...], w_ref[...], preferred_element_type=F32)

    def head_norm(gain_ref, scale):
        for hd in range(N_HEADS):
            a = acc[:, hd * HEAD_DIM:(hd + 1) * HEAD_DIM]
            r = lax.rsqrt(jnp.mean(a * a, axis=-1, keepdims=True) + NORM_EPS)
            o_ref[:, hd * HEAD_DIM:(hd + 1) * HEAD_DIM] = ((a * r) * gain_ref[...] * scale).astype(BF16)

    @pl.when(j == 0)
    def _():
        head_norm(qg_ref, HEAD_DIM ** -0.5)

    @pl.when(j == 1)
    def _():
        head_norm(kg_ref, 1.0)

    @pl.when(j >= 2)
    def _():
        o_ref[...] = acc.astype(BF16)


def _in_projection(x2, mod3, norm_g, w_in_b, q_g, k_g):
    T, D = x2.shape
    tm, tn = TM_PROJ, D_ATTN
    per_batch = SEQ // tm
    return pl.pallas_call(
        _inproj_kernel,
        grid=(T // tm, D_IN_PROJ // tn),
        in_specs=[
            pl.BlockSpec((tm, D), lambda i, j: (i, 0)),
            pl.BlockSpec((1, 6, D), lambda i, j: (i // per_batch, 0, 0)),
            pl.BlockSpec((1, D), lambda i, j: (0, 0)),
            pl.BlockSpec((D, tn), lambda i, j: (0, j)),
            pl.BlockSpec((1, HEAD_DIM), lambda i, j: (0, 0)),
            pl.BlockSpec((1, HEAD_DIM), lambda i, j: (0, 0)),
        ],
        out_specs=pl.BlockSpec((tm, tn), lambda i, j: (i, j)),
        out_shape=jax.ShapeDtypeStruct((T, D_IN_PROJ), BF16),
        scratch_shapes=[pltpu.VMEM((tm, D), BF16)],
        compiler_params=_cparams(("arbitrary", "arbitrary")),
        name="in_projection",
    )(x2, mod3, norm_g, w_in_b, q_g, k_g)


def _bias_tables(rel_bias):
    n = BAND
    qi = jnp.arange(n)[:, None]
    ki = jnp.arange(2 * n)[None, :]
    steps = n + qi - ki
    in_band = (steps >= 0) & (steps <= n)
    max_exact = NUM_REL_BUCKETS // 2
    tabs = []
    for _, dilation in DILATED_GROUPS:
        dist = jnp.clip(steps, 0, n) * dilation
        nf = jnp.maximum(dist, 1).astype(F32)
        large = max_exact + (jnp.log(nf / max_exact) / math.log(REL_MAX_DISTANCE / max_exact)
                             * (NUM_REL_BUCKETS - max_exact)).astype(I32)
        large = jnp.minimum(large, NUM_REL_BUCKETS - 1)
        bucket = jnp.where(dist < max_exact, dist, large)
        onehot = jax.nn.one_hot(bucket, NUM_REL_BUCKETS, dtype=F32)
        b = jnp.einsum("qkb,bh->hqk", onehot, rel_bias.astype(F32), precision=lax.Precision.HIGHEST)
        tabs.append(jnp.where(in_band[None], b, MASK_VALUE))
    return jnp.stack(tabs)


def _attn_kernel(q_ref, k_ref, v_ref, bias_ref, o_ref, qf, kf, vf, q4, k4, v4,
                 o0, l0, o1, l1, o2, l2, stage):
    n = BAND
    quarter = SEQ // 4

    for src, nat, res in ((q_ref, qf, q4), (k_ref, kf, k4), (v_ref, vf, v4)):
        nat[...] = src[...].astype(F32)
        for r in range(4):
            res[r * quarter:(r + 1) * quarter, :] = nat[pl.ds(r, quarter, stride=4), :]

    def piece(qb, kb, vb, bias, o_out, l_out, rows):
        l = lax.dot_general(qb, kb, (((1,), (1,)), ((), ())), preferred_element_type=F32) + bias
        m = jnp.max(l, axis=-1, keepdims=True)
        p = jnp.exp(l - m)
        s = jnp.sum(p, axis=-1, keepdims=True)
        o = jnp.dot(p.astype(BF16), vb, preferred_element_type=F32)
        o_out[rows, :] = o / s
        l_out[rows, :] = jnp.broadcast_to(m + jnp.log(s), (n, LANES))

    for i in range(SEQ // n):
        rows = slice(i * n, (i + 1) * n)
        if i == 0:
            piece(q_ref[rows, :], k_ref[rows, :], v_ref[rows, :], bias_ref[0, :, n:], o0, l0, rows)
        else:
            krows = slice((i - 1) * n, (i + 1) * n)
            piece(q_ref[rows, :], k_ref[krows, :], v_ref[krows, :], bias_ref[0], o0, l0, rows)

    for r in range(4):
        for blk in range(quarter // n):
            base = r * quarter + blk * n
            rows = slice(base, base + n)
            krows = rows if blk == 0 else slice(base - n, base + n)
            bias = bias_ref[1, :, n:] if blk == 0 else bias_ref[1]
            piece(q4[rows, :].astype(BF16), k4[krows, :].astype(BF16), v4[krows, :].astype(BF16),
                  bias, o1, l1, rows)

    for r in range(4):
        for a in range(4):
            rows = pl.ds(r * quarter + a, n, stride=4)
            piece(q4[rows, :].astype(BF16), k4[rows, :].astype(BF16), v4[rows, :].astype(BF16),
                  bias_ref[2, :, n:], o2, l2, rows)

    for r in range(4):
        for c in range(quarter // n):
            rows = slice(r * quarter + c * n, r * quarter + (c + 1) * n)
            nat = pl.ds(r + 4 * c * n, n, stride=4)
            a0, a1, a2 = l0[nat, :], l1[rows, :], l2[rows, :]
            m = jnp.maximum(jnp.maximum(a0, a1), a2)
            e0, e1, e2 = jnp.exp(a0 - m), jnp.exp(a1 - m), jnp.exp(a2 - m)
            mix = (e0 * o0[nat, :] + e1 * o1[rows, :] + e2 * o2[rows, :]) / (e0 + e1 + e2)
            stage[nat, :] = mix
    o_ref[...] = stage[...].astype(BF16)


def _attention(proj3, bias_tabs):
    B = proj3.shape[0]
    blk = lambda off: pl.BlockSpec((None, SEQ, HEAD_DIM), lambda b, h: (b, 0, off + h))
    return pl.pallas_call(
        _attn_kernel,
        grid=(B, N_HEADS),
        in_specs=[
            blk(0), blk(N_HEADS), blk(2 * N_HEADS),
            pl.BlockSpec((3, None, BAND, 2 * BAND), lambda b, h: (0, h, 0, 0)),
        ],
        out_specs=pl.BlockSpec((None, SEQ, HEAD_DIM), lambda b, h: (b, 0, h)),
        out_shape=jax.ShapeDtypeStruct((B, SEQ, D_ATTN), BF16),
        scratch_shapes=[pltpu.VMEM((SEQ, HEAD_DIM), F32) for _ in range(13)],
        compiler_params=_cparams(("arbitrary", "arbitrary")),
        name="dilated_attention",
    )(proj3, proj3, proj3, bias_tabs)


def _sgu_kernel(u_ref, z_ref, w_ref, g_ref, b_ref, bs_ref, o_ref):
    n = SGU_CHUNK
    for c in range(TM_SGU // n):
        rs = slice(c * n, (c + 1) * n)
        for g in range(N_SGU_GROUPS):
            cs = slice(g * n, (g + 1) * n)
            z = _gelu(z_ref[rs, cs].astype(F32))
            mu = jnp.mean(z, axis=-1, keepdims=True)
            zc = z - mu
            var = jnp.mean(zc * zc, axis=-1, keepdims=True)
            zn = (zc * lax.rsqrt(var + NORM_EPS)) * g_ref[:, cs] + b_ref[:, cs]
            mixed = jnp.dot(w_ref[g], zn.astype(BF16), preferred_element_type=F32) + bs_ref[:, g:g + 1]
            u = _gelu(u_ref[rs, cs].astype(F32))
            o_ref[rs, cs] = (u * mixed).astype(BF16)


def _spatial_gating(proj, w_causal_b, ln_g, ln_b, bs_t):
    T = proj.shape[0]
    tm = TM_SGU
    ucol = 3 * D_ATTN // D_SGU
    return pl.pallas_call(
        _sgu_kernel,
        grid=(T // tm,),
        in_specs=[
            pl.BlockSpec((tm, D_SGU), lambda i: (i, ucol)),
            pl.BlockSpec((tm, D_SGU), lambda i: (i, ucol + 1)),
            pl.BlockSpec((N_SGU_GROUPS, SGU_CHUNK, SGU_CHUNK), lambda i: (0, 0, 0)),
            pl.BlockSpec((1, D_SGU), lambda i: (0, 0)),
            pl.BlockSpec((1, D_SGU), lambda i: (0, 0)),
            pl.BlockSpec((SGU_CHUNK, N_SGU_GROUPS), lambda i: (0, 0)),
        ],
        out_specs=pl.BlockSpec((tm, D_SGU), lambda i: (i, 0)),
        out_shape=jax.ShapeDtypeStruct((T, D_SGU), BF16),
        compiler_params=_cparams(("arbitrary",)),
        name="spatial_gating",
    )(proj, proj, w_causal_b, ln_g, ln_b, bs_t)


def _outproj_kernel(a_ref, s_ref, x_ref, mod_ref, g_ref, w_ref, x1_ref, h2_ref, h2p_ref):
    mixed = jnp.dot(a_ref[...], w_ref[0:D_ATTN, :], preferred_element_type=F32)
    mixed += jnp.dot(s_ref[...], w_ref[D_ATTN:, :], preferred_element_type=F32)
    x1 = x_ref[...] + mod_ref[0, 2:3, :] * mixed
    x1_ref[...] = x1
    r = lax.rsqrt(jnp.mean(x1 * x1, axis=-1, keepdims=True) + NORM_EPS)
    h2 = (x1 * r) * g_ref[...] * (1.0 + mod_ref[0, 4:5, :]) + mod_ref[0, 3:4, :]
    h2_ref[...] = h2.astype(BF16)
    h2p_ref[...] = _pack_halves(h2)


def _out_projection(attn, sgu, x2, mod3, norm_g, w_out_b):
    T, D = x2.shape
    tm = TM_OUT
    per_batch = SEQ // tm
    return pl.pallas_call(
        _outproj_kernel,
        grid=(T // tm,),
        in_specs=[
            pl.BlockSpec((tm, D_ATTN), lambda i: (i, 0)),
            pl.BlockSpec((tm, D_SGU), lambda i: (i, 0)),
            pl.BlockSpec((tm, D), lambda i: (i, 0)),
            pl.BlockSpec((1, 6, D), lambda i: (i // per_batch, 0, 0)),
            pl.BlockSpec((1, D), lambda i: (0, 0)),
            pl.BlockSpec((D_ATTN + D_SGU, D), lambda i: (0, 0)),
        ],
        out_specs=[pl.BlockSpec((tm, D), lambda i: (i, 0)), pl.BlockSpec((tm, D), lambda i: (i, 0)),
                   pl.BlockSpec((tm, D // 2), lambda i: (i, 0))],
        out_shape=[jax.ShapeDtypeStruct((T, D), F32), jax.ShapeDtypeStruct((T, D), BF16),
                   jax.ShapeDtypeStruct((T, D // 2), U32)],
        compiler_params=_cparams(("arbitrary",)),
        name="out_projection",
    )(attn, sgu, x2, mod3, norm_g, w_out_b)


def _router_kernel(h_ref, w_ref, b_ref, idx_ref, wt_ref, rank_ref, cnt_ref, run_ref):
    i = pl.program_id(0)
    tr = TM_ROUTE
    E = N_EXPERTS

    @pl.when(i == 0)
    def _():
        run_ref[...] = jnp.zeros_like(run_ref)

    logits = lax.dot_general(w_ref[...], h_ref[...], (((1,), (1,)), ((), ())), preferred_element_type=F32)
    scores = jax.nn.sigmoid(logits)
    sel = scores + b_ref[...]

    slabs = [sel[g * GROUP_SIZE:(g + 1) * GROUP_SIZE, :] for g in range(N_EXPERT_GROUPS)]
    si = lax.broadcasted_iota(I32, (GROUP_SIZE, tr), 0).astype(F32)
    gs = []
    for slab in slabs:
        m1 = jnp.max(slab, axis=0, keepdims=True)
        first = jnp.min(jnp.where(slab == m1, si, float(GROUP_SIZE)), axis=0, keepdims=True)
        m2 = jnp.max(jnp.where(si == first, -jnp.inf, slab), axis=0, keepdims=True)
        gs.append(m1 + m2)

    kept = []
    for g in range(N_EXPERT_GROUPS):
        beaten = jnp.zeros((1, tr), F32)
        for o in range(N_EXPERT_GROUPS):
            if o < g:
                beaten += (gs[o] >= gs[g]).astype(F32)
            elif o > g:
                beaten += (gs[o] > gs[g]).astype(F32)
        kept.append(jnp.where(beaten < TOPK_GROUPS, slabs[g], -jnp.inf))
    masked = jnp.concatenate(kept, axis=0)

    ei = lax.broadcasted_iota(I32, (E, tr), 0).astype(F32)
    picks, pick_scores = [], []
    onehot_sum = jnp.zeros((E, tr), F32)
    for k in range(TOP_K):
        m = jnp.max(masked, axis=0, keepdims=True)
        ik = jnp.min(jnp.where(masked == m, ei, float(E)), axis=0, keepdims=True)
        oh = ei == ik
        pick_scores.append(jnp.sum(jnp.where(oh, scores, 0.0), axis=0, keepdims=True))
        masked = jnp.where(oh, -jnp.inf, masked)
        onehot_sum += oh.astype(F32)
        picks.append(ik)

    ti = lax.broadcasted_iota(I32, (tr, tr), 0)
    tj = lax.broadcasted_iota(I32, (tr, tr), 1)
    before = (ti < tj).astype(BF16)
    prior = jnp.dot(onehot_sum.astype(BF16), before, preferred_element_type=F32) + run_ref[...]

    total = pick_scores[0]
    for k in range(1, TOP_K):
        total += pick_scores[k]
    for k in range(TOP_K):
        idx_ref[k:k + 1, :] = picks[k].astype(I32)
        wt_ref[k:k + 1, :] = pick_scores[k] / total * ROUTED_SCALE
        rk = jnp.sum(jnp.where(ei == picks[k], prior, 0.0), axis=0, keepdims=True)
        rank_ref[k:k + 1, :] = rk.astype(I32)

    run_ref[...] += jnp.sum(onehot_sum, axis=1, keepdims=True)
    cnt_ref[...] = run_ref[...].astype(I32)


def _route(h2, router_wt_b, router_bias):
    T, D = h2.shape
    tr = TM_ROUTE
    E = N_EXPERTS
    row_spec = pl.BlockSpec((TOP_K, tr), lambda i: (0, i))
    return pl.pallas_call(
        _router_kernel,
        grid=(T // tr,),
        in_specs=[
            pl.BlockSpec((tr, D), lambda i: (i, 0)),
            pl.BlockSpec((E, D), lambda i: (0, 0)),
            pl.BlockSpec((E, 1), lambda i: (0, 0)),
        ],
        out_specs=[row_spec, row_spec, row_spec, pl.BlockSpec((E, 1), lambda i: (0, 0))],
        out_shape=[
            jax.ShapeDtypeStruct((TOP_K, T), I32),
            jax.ShapeDtypeStruct((TOP_K, T), F32),
            jax.ShapeDtypeStruct((TOP_K, T), I32),
            jax.ShapeDtypeStruct((E, 1), I32),
        ],
        scratch_shapes=[pltpu.VMEM((E, 1), F32)],
        compiler_params=_cparams(("arbitrary",)),
        name="router",
    )(h2, router_wt_b, router_bias.reshape(E, 1))


def _dispatch_kernel(pend_ref, h_ref, dest_ref, xg_ref, dest_s, zero_ref, sem_s, sem):
    i = pl.program_id(0)
    td = TM_DISPATCH
    blk = EXPERT_BLOCK

    def zero_copy(e):
        start = pl.multiple_of(pend_ref[e + 1] - blk, blk)
        return pltpu.make_async_copy(zero_ref, xg_ref.at[pl.ds(start, blk), :], sem)

    @pl.when(i == 0)
    def _():
        zero_ref[...] = jnp.zeros_like(zero_ref)

        def start_zero(e, c):
            @pl.when(pend_ref[e + 1] > pend_ref[e])
            def _():
                zero_copy(e).start()
            return c

        def wait_zero(e, c):
            @pl.when(pend_ref[e + 1] > pend_ref[e])
            def _():
                zero_copy(e).wait()
            return c

        lax.fori_loop(0, N_EXPERTS, start_zero, 0)
        lax.fori_loop(0, N_EXPERTS, wait_zero, 0)

    cp = pltpu.make_async_copy(dest_ref, dest_s, sem_s)
    cp.start()
    cp.wait()

    def row_copy(t, k):
        return pltpu.make_async_copy(h_ref.at[pl.ds(t, 1), :], xg_ref.at[pl.ds(dest_s[k, t], 1), :], sem)

    def start_rows(t8, c):
        base = pl.multiple_of(t8 * 8, 8)
        for j in range(8):
            for k in range(TOP_K):
                row_copy(base + j, k).start()
        return c

    def wait_rows(t, c):
        for k in range(TOP_K):
            row_copy(t, k).wait()
        return c

    lax.fori_loop(0, td // 8, start_rows, 0)
    lax.fori_loop(0, td, wait_rows, 0)


def _dispatch(h2p, dest, pend0, rows):
    T, D = h2p.shape
    td = TM_DISPATCH
    return pl.pallas_call(
        _dispatch_kernel,
        grid_spec=pltpu.PrefetchScalarGridSpec(
            num_scalar_prefetch=1,
            grid=(T // td,),
            in_specs=[
                pl.BlockSpec((td, D), lambda i, p: (i, 0)),
                pl.BlockSpec((TOP_K, td), lambda i, p: (0, i)),
            ],
            out_specs=pl.BlockSpec(memory_space=pl.ANY),
            scratch_shapes=[
                pltpu.SMEM((TOP_K, td), I32),
                pltpu.VMEM((EXPERT_BLOCK, D), U32),
                pltpu.SemaphoreType.DMA,
                pltpu.SemaphoreType.DMA,
            ],
        ),
        out_shape=jax.ShapeDtypeStruct((rows, D), U32),
        compiler_params=_cparams(("arbitrary",)),
        name="dispatch",
    )(pend0, h2p, dest)


def _experts_kernel(bexp_ref, first_ref, slot_ref, next_ref, pos_ref, nprev_ref, nused_ref,
                    x_ref, wg_hbm, wu_hbm, wd_hbm, y_ref,
                    wg_f, wu_f, wd_f, wg_b, wu_b, wd_b, sem):
    i = pl.program_id(0)
    rows_in = D_MODEL // WEIGHT_CHUNKS
    rows_out = D_EXPERT // WEIGHT_CHUNKS

    def chunk_copies(e, s, c):
        gi = pl.ds(c * rows_in, rows_in)
        di = pl.ds(c * rows_out, rows_out)
        return (pltpu.make_async_copy(wg_hbm.at[e, gi, :], wg_f.at[s, gi, :], sem.at[s, 0, c]),
                pltpu.make_async_copy(wu_hbm.at[e, gi, :], wu_f.at[s, gi, :], sem.at[s, 1, c]),
                pltpu.make_async_copy(wd_hbm.at[e, di, :], wd_f.at[s, di, :], sem.at[s, 2, c]))

    @pl.when(i < nused_ref[0])
    def _():
        s = slot_ref[i]

        @pl.when(first_ref[i] == 1)
        def _():
            for c in range(WEIGHT_CHUNKS):
                @pl.when(nprev_ref[i] <= c)
                def _():
                    for cp in chunk_copies(bexp_ref[i], s, c):
                        cp.start()
            for c in range(WEIGHT_CHUNKS):
                for cp in chunk_copies(bexp_ref[i], s, c):
                    cp.wait()
            wg_b[...] = wg_f[s].astype(BF16)
            wu_b[...] = wu_f[s].astype(BF16)
            wd_b[...] = wd_f[s].astype(BF16)

        for c in range(WEIGHT_CHUNKS):
            @pl.when((pos_ref[i] == c) & (next_ref[i] >= 0))
            def _():
                for cp in chunk_copies(next_ref[i], 1 - s, c):
                    cp.start()

        lo, hi = _unpack_halves(x_ref[...])
        x = jnp.concatenate([lo.astype(BF16), hi.astype(BF16)], axis=1)
        g = jnp.dot(x, wg_b[...], preferred_element_type=F32)
        u = jnp.dot(x, wu_b[...], preferred_element_type=F32)
        a = (_silu(g) * u).astype(BF16)
        y_ref[...] = _pack_halves(jnp.dot(a, wd_b[...], preferred_element_type=F32))


def _experts(xg, block_exp, first, slot, next_exp, pos, nprev, n_used, w_gate, w_up, w_down):
    rows = xg.shape[0]
    D = D_MODEL
    bm = EXPERT_BLOCK
    F = D_EXPERT
    nb = rows // bm
    row_map = lambda i, be, fi, sl, nx, po, npv, nu: (jnp.minimum(i, nu[0] - 1), 0)
    return pl.pallas_call(
        _experts_kernel,
        grid_spec=pltpu.PrefetchScalarGridSpec(
            num_scalar_prefetch=7,
            grid=(nb,),
            in_specs=[
                pl.BlockSpec((bm, D // 2), row_map),
                pl.BlockSpec(memory_space=pl.ANY),
                pl.BlockSpec(memory_space=pl.ANY),
                pl.BlockSpec(memory_space=pl.ANY),
            ],
            out_specs=pl.BlockSpec((bm, D // 2), row_map),
            scratch_shapes=[
                pltpu.VMEM((2, D, F), F32), pltpu.VMEM((2, D, F), F32), pltpu.VMEM((2, F, D), F32),
                pltpu.VMEM((D, F), BF16), pltpu.VMEM((D, F), BF16), pltpu.VMEM((F, D), BF16),
                pltpu.SemaphoreType.DMA((2, 3, WEIGHT_CHUNKS)),
            ],
        ),
        out_shape=jax.ShapeDtypeStruct((rows, D // 2), U32),
        compiler_params=_cparams(("arbitrary",)),
        name="experts",
    )(block_exp, first, slot, next_exp, pos, nprev, n_used, xg, w_gate, w_up, w_down)


def _combine_kernel(dest_ref, wt_ref, x1_ref, h_ref, mod_ref, sg_ref, su_ref, sd_ref, yg_ref, o_ref,
                    dest_s, buf, sem_s, sem):
    tc = TM_COMBINE
    cp = pltpu.make_async_copy(dest_ref, dest_s, sem_s)
    cp.start()
    cp.wait()

    def row_copy(t, k):
        return pltpu.make_async_copy(yg_ref.at[pl.ds(dest_s[k, t], 1), :], buf.at[k, pl.ds(t, 1), :], sem)

    def start_rows(t8, c):
        base = pl.multiple_of(t8 * 8, 8)
        for j in range(8):
            for k in range(TOP_K):
                row_copy(base + j, k).start()
        return c

    def wait_rows(t, c):
        for k in range(TOP_K):
            row_copy(t, k).wait()
        return c

    lax.fori_loop(0, tc // 8, start_rows, 0)

    hb = h_ref[...]
    g = jnp.dot(hb, sg_ref[...], preferred_element_type=F32)
    u = jnp.dot(hb, su_ref[...], preferred_element_type=F32)
    ffn = jnp.dot((_silu(g) * u).astype(BF16), sd_ref[...], preferred_element_type=F32)

    lax.fori_loop(0, tc, wait_rows, 0)
    half = D_MODEL // 2
    ffn_lo, ffn_hi = ffn[:, :half], ffn[:, half:]
    for k in range(TOP_K):
        lo, hi = _unpack_halves(buf[k])
        w = wt_ref[:, k:k + 1]
        ffn_lo += lo * w
        ffn_hi += hi * w
    o_ref[:, :half] = x1_ref[:, :half] + mod_ref[0, 5:6, :half] * ffn_lo
    o_ref[:, half:] = x1_ref[:, half:] + mod_ref[0, 5:6, half:] * ffn_hi


def _combine(dest, wt_t, x1, h2, mod3, sg_b, su_b, sd_b, yg):
    T, D = x1.shape
    tc = TM_COMBINE
    per_batch = SEQ // tc
    F = D_EXPERT
    return pl.pallas_call(
        _combine_kernel,
        grid=(T // tc,),
        in_specs=[
            pl.BlockSpec((TOP_K, tc), lambda i: (0, i)),
            pl.BlockSpec((tc, TOP_K), lambda i: (i, 0)),
            pl.BlockSpec((tc, D), lambda i: (i, 0)),
            pl.BlockSpec((tc, D), lambda i: (i, 0)),
            pl.BlockSpec((1, 6, D), lambda i: (i // per_batch, 0, 0)),
            pl.BlockSpec((D, F), lambda i: (0, 0)),
            pl.BlockSpec((D, F), lambda i: (0, 0)),
            pl.BlockSpec((F, D), lambda i: (0, 0)),
            pl.BlockSpec(memory_space=pl.ANY),
        ],
        out_specs=pl.BlockSpec((tc, D), lambda i: (i, 0)),
        out_shape=jax.ShapeDtypeStruct((T, D), F32),
        scratch_shapes=[
            pltpu.SMEM((TOP_K, tc), I32),
            pltpu.VMEM((TOP_K, tc, D // 2), U32),
            pltpu.SemaphoreType.DMA,
            pltpu.SemaphoreType.DMA,
        ],
        compiler_params=_cparams(("arbitrary",)),
        name="combine",
    )(dest, wt_t, x1, h2, mod3, sg_b, su_b, sd_b, yg)


def kernel(x, c, ada_w, ada_b, mix_norm_g, ffn_norm_g, w_in, q_norm_g, k_norm_g, rel_bias, sgu_ln_g, sgu_ln_b, sgu_w, sgu_b, w_out, router_w, router_bias, shared_w_gate, shared_w_up, shared_w_down, expert_w_gate, expert_w_up, expert_w_down):
    B, S, D = x.shape
    assert S == SEQ and D == D_MODEL and ada_w.shape[0] == 1
    T = B * S
    x2 = x.reshape(T, D)

    mod3 = _modulation(c, ada_w[0], ada_b[0]).reshape(B, 6, D)

    proj = _in_projection(x2, mod3, mix_norm_g, w_in[0].astype(BF16), q_norm_g, k_norm_g)
    attn = _attention(proj.reshape(B, S, D_IN_PROJ), _bias_tables(rel_bias)).reshape(T, D_ATTN)
    causal = jnp.tril(jnp.ones((SGU_CHUNK, SGU_CHUNK), F32))
    sgu = _spatial_gating(proj, (sgu_w[0] * causal).astype(BF16), sgu_ln_g, sgu_ln_b, sgu_b[0].T)
    x1, h2, h2p = _out_projection(attn, sgu, x2, mod3, ffn_norm_g, w_out[0].astype(BF16))

    idx, wts, rank, counts = _route(h2, router_w[0].T.astype(BF16), router_bias[0])

    bm = EXPERT_BLOCK
    counts = counts.reshape(N_EXPERTS)
    padded = (counts + bm - 1) // bm * bm
    pends = jnp.cumsum(padded)
    pstarts = pends - padded
    n_blocks = T * TOP_K // bm + N_EXPERTS
    block_start = jnp.arange(n_blocks, dtype=I32) * bm
    block_exp = jnp.minimum(jnp.sum((pends[None, :] <= block_start[:, None]).astype(I32), axis=1), N_EXPERTS - 1)
    n_used = (pends[-1] // bm).astype(I32).reshape(1)
    eids = jnp.arange(N_EXPERTS, dtype=I32)
    dest = jnp.sum(jnp.where(idx[:, :, None] == eids, pstarts.astype(I32), 0), axis=-1) + rank
    pend0 = jnp.concatenate([jnp.zeros((1,), I32), pends.astype(I32)])
    first = jnp.concatenate([jnp.ones((1,), I32), (block_exp[1:] != block_exp[:-1]).astype(I32)])
    slot = (jnp.cumsum(first) - 1) % 2
    later_used = jnp.where(counts > 0, eids, N_EXPERTS)
    next_used = lax.cummin(jnp.concatenate([later_used[1:], jnp.full((1,), N_EXPERTS, I32)]), reverse=True)
    next_used = jnp.where(next_used >= N_EXPERTS, -1, next_used)
    of_block = block_exp[:, None] == eids
    next_exp = jnp.sum(jnp.where(of_block, next_used, 0), axis=-1)
    pos = jnp.arange(n_blocks, dtype=I32) - jnp.sum(jnp.where(of_block, pstarts // bm, 0), axis=-1)
    earlier_used = jnp.where(counts > 0, eids, -1)
    prev_used = lax.cummax(jnp.concatenate([jnp.full((1,), -1, I32), earlier_used[:-1]]))
    prev_blocks = jnp.sum(jnp.where(prev_used[:, None] == eids, padded // bm, 0), axis=-1)
    nprev = jnp.minimum(jnp.sum(jnp.where(of_block, prev_blocks, 0), axis=-1), WEIGHT_CHUNKS)

    xg = _dispatch(h2p, dest, pend0, n_blocks * bm)
    yg = _experts(xg, block_exp, first, slot.astype(I32), next_exp.astype(I32), pos.astype(I32),
                  nprev.astype(I32), n_used, expert_w_gate[0], expert_w_up[0], expert_w_down[0])
    out = _combine(dest, wts.T, x1, h2, mod3, shared_w_gate[0].astype(BF16), shared_w_up[0].astype(BF16),
                   shared_w_down[0].astype(BF16), yg)
    return out.reshape(B, S, D)
```

```python
import functools
import math

import numpy as np
import jax
import jax.numpy as jnp
from jax import lax
from jax.experimental import pallas as pl
from jax.experimental.pallas import tpu as pltpu

F32 = jnp.float32
BF16 = jnp.bfloat16
I32 = jnp.int32
U32 = jnp.uint32

D_MODEL = 2048
SEQ = 2048
HEAD_DIM = 128
N_HEADS = 8
D_ATTN = N_HEADS * HEAD_DIM
D_SGU = 1024
N_SGU_GROUPS = 8
SGU_CHUNK = 128
D_IN_PROJ = 3 * D_ATTN + 2 * D_SGU
DILATED_GROUPS = ((128, 1), (512, 4), (2048, 16))
BAND = 128
NUM_REL_BUCKETS = 32
REL_MAX_DISTANCE = 2048
N_EXPERTS = 256
TOP_K = 8
N_EXPERT_GROUPS = 8
GROUP_SIZE = N_EXPERTS // N_EXPERT_GROUPS
TOPK_GROUPS = 4
D_EXPERT = 512
ROUTED_SCALE = 2.5
NORM_EPS = 1e-6
MASK_VALUE = -1e30

LANES = 128
VMEM_LIMIT = 56 * 1024 * 1024

TM_PROJ = 512
TM_SGU = 512
TM_OUT = 256
TM_ROUTE = 512
TM_DISPATCH = 256
TM_COMBINE = 128
EXPERT_BLOCK = 256
WEIGHT_CHUNKS = 4


def _cparams(sem):
    return pltpu.CompilerParams(dimension_semantics=sem, vmem_limit_bytes=VMEM_LIMIT)


def _silu(v):
    return v * jax.nn.sigmoid(v)


def _gelu(v):
    return 0.5 * v * (1.0 + lax.erf(v * (1.0 / math.sqrt(2.0))))


def _pack_halves(v):
    h = v.shape[1] // 2
    lo = lax.bitcast_convert_type(v[:, :h].astype(BF16).astype(F32), U32)
    hi = lax.bitcast_convert_type(v[:, h:].astype(BF16).astype(F32), U32)
    return (hi & jnp.uint32(0xFFFF0000)) | (lo >> 16)


def _unpack_halves(w):
    lo = lax.bitcast_convert_type(w << 16, F32)
    hi = lax.bitcast_convert_type(w & jnp.uint32(0xFFFF0000), F32)
    return lo, hi


def _mod_kernel(c_ref, w_ref, b_ref, o_ref):
    ca = _silu(c_ref[...]).astype(BF16)
    o_ref[...] = jnp.dot(ca, w_ref[...].astype(BF16), preferred_element_type=F32) + b_ref[...]


def _modulation(c, ada_w, ada_b):
    B, D = c.shape
    N = ada_w.shape[1]
    tn = 1024
    return pl.pallas_call(
        _mod_kernel,
        grid=(N // tn,),
        in_specs=[
            pl.BlockSpec((B, D), lambda j: (0, 0)),
            pl.BlockSpec((D, tn), lambda j: (0, j)),
            pl.BlockSpec((1, tn), lambda j: (0, j)),
        ],
        out_specs=pl.BlockSpec((B, tn), lambda j: (0, j)),
        out_shape=jax.ShapeDtypeStruct((B, N), F32),
        compiler_params=_cparams(("arbitrary",)),
        name="modulation",
    )(c, ada_w, ada_b.reshape(1, N))


def _inproj_kernel(x_ref, mod_ref, g_ref, w_ref, qg_ref, kg_ref, o_ref, h_ref):
    j = pl.program_id(1)

    @pl.when(j == 0)
    def _():
        x = x_ref[...]
        r = lax.rsqrt(jnp.mean(x * x, axis=-1, keepdims=True) + NORM_EPS)
        h = (x * r) * g_ref[...] * (1.0 + mod_ref[0, 1:2, :]) + mod_ref[0, 0:1, :]
        h_ref[...] = h.astype(BF16)

    acc = jnp.dot(h_ref[...], w_ref[...], preferred_element_type=F32)

    def head_norm(gain_ref, scale):
        for hd in range(N_HEADS):
            a = acc[:, hd * HEAD_DIM:(hd + 1) * HEAD_DIM]
            r = lax.rsqrt(jnp.mean(a * a, axis=-1, keepdims=True) + NORM_EPS)
            o_ref[:, hd * HEAD_DIM:(hd + 1) * HEAD_DIM] = ((a * r) * gain_ref[...] * scale).astype(BF16)

    @pl.when(j == 0)
    def _():
        head_norm(qg_ref, HEAD_DIM ** -0.5)

    @pl.when(j == 1)
    def _():
        head_norm(kg_ref, 1.0)

    @pl.when(j >= 2)
    def _():
        o_ref[...] = acc.astype(BF16)


def _in_projection(x2, mod3, norm_g, w_in_b, q_g, k_g):
    T, D = x2.shape
    tm, tn = TM_PROJ, D_ATTN
    per_batch = SEQ // tm
    return pl.pallas_call(
        _inproj_kernel,
        grid=(T // tm, D_IN_PROJ // tn),
        in_specs=[
            pl.BlockSpec((tm, D), lambda i, j: (i, 0)),
            pl.BlockSpec((1, 6, D), lambda i, j: (i // per_batch, 0, 0)),
            pl.BlockSpec((1, D), lambda i, j: (0, 0)),
            pl.BlockSpec((D, tn), lambda i, j: (0, j)),
            pl.BlockSpec((1, HEAD_DIM), lambda i, j: (0, 0)),
            pl.BlockSpec((1, HEAD_DIM), lambda i, j: (0, 0)),
        ],
        out_specs=pl.BlockSpec((tm, tn), lambda i, j: (i, j)),
        out_shape=jax.ShapeDtypeStruct((T, D_IN_PROJ), BF16),
        scratch_shapes=[pltpu.VMEM((tm, D), BF16)],
        compiler_params=_cparams(("arbitrary", "arbitrary")),
        name="in_projection",
    )(x2, mod3, norm_g, w_in_b, q_g, k_g)


def _bias_tables(rel_bias):
    n = BAND
    qi = jnp.arange(n)[:, None]
    ki = jnp.arange(2 * n)[None, :]
    steps = n + qi - ki
    in_band = (steps >= 0) & (steps <= n)
    max_exact = NUM_REL_BUCKETS // 2
    tabs = []
    for _, dilation in DILATED_GROUPS:
        dist = jnp.clip(steps, 0, n) * dilation
        nf = jnp.maximum(dist, 1).astype(F32)
        large = max_exact + (jnp.log(nf / max_exact) / math.log(REL_MAX_DISTANCE / max_exact)
                             * (NUM_REL_BUCKETS - max_exact)).astype(I32)
        large = jnp.minimum(large, NUM_REL_BUCKETS - 1)
        bucket = jnp.where(dist < max_exact, dist, large)
        onehot = jax.nn.one_hot(bucket, NUM_REL_BUCKETS, dtype=F32)
        b = jnp.einsum("qkb,bh->hqk", onehot, rel_bias.astype(F32), precision=lax.Precision.HIGHEST)
        tabs.append(jnp.where(in_band[None], b, MASK_VALUE))
    return jnp.stack(tabs)


def _attn_kernel(q_ref, k_ref, v_ref, bias_ref, o_ref, qf, kf, vf, q4, k4, v4,
                 o0, l0, o1, l1, o2, l2, stage):
    n = BAND
    quarter = SEQ // 4

    for src, nat, res in ((q_ref, qf, q4), (k_ref, kf, k4), (v_ref, vf, v4)):
        nat[...] = src[...].astype(F32)
        for r in range(4):
            res[r * quarter:(r + 1) * quarter, :] = nat[pl.ds(r, quarter, stride=4), :]

    def piece(qb, kb, vb, bias, o_out, l_out, rows):
        l = lax.dot_general(qb, kb, (((1,), (1,)), ((), ())), preferred_element_type=F32) + bias
        m = jnp.max(l, axis=-1, keepdims=True)
        p = jnp.exp(l - m)
        s = jnp.sum(p, axis=-1, keepdims=True)
        o = jnp.dot(p.astype(BF16), vb, preferred_element_type=F32)
        o_out[rows, :] = o / s
        l_out[rows, :] = jnp.broadcast_to(m + jnp.log(s), (n, LANES))

    for i in range(SEQ // n):
        rows = slice(i * n, (i + 1) * n)
        if i == 0:
            piece(q_ref[rows, :], k_ref[rows, :], v_ref[rows, :], bias_ref[0, :, n:], o0, l0, rows)
        else:
            krows = slice((i - 1) * n, (i + 1) * n)
            piece(q_ref[rows, :], k_ref[krows, :], v_ref[krows, :], bias_ref[0], o0, l0, rows)

    for r in range(4):
        for blk in range(quarter // n):
            base = r * quarter + blk * n
            rows = slice(base, base + n)
            krows = rows if blk == 0 else slice(base - n, base + n)
            bias = bias_ref[1, :, n:] if blk == 0 else bias_ref[1]
            piece(q4[rows, :].astype(BF16), k4[krows, :].astype(BF16), v4[krows, :].astype(BF16),
                  bias, o1, l1, rows)

    for r in range(4):
        for a in range(4):
            rows = pl.ds(r * quarter + a, n, stride=4)
            piece(q4[rows, :].astype(BF16), k4[rows, :].astype(BF16), v4[rows, :].astype(BF16),
                  bias_ref[2, :, n:], o2, l2, rows)

    for r in range(4):
        for c in range(quarter // n):
            rows = slice(r * quarter + c * n, r * quarter + (c + 1) * n)
            nat = pl.ds(r + 4 * c * n, n, stride=4)
            a0, a1, a2 = l0[nat, :], l1[rows, :], l2[rows, :]
            m = jnp.maximum(jnp.maximum(a0, a1), a2)
            e0, e1, e2 = jnp.exp(a0 - m), jnp.exp(a1 - m), jnp.exp(a2 - m)
            mix = (e0 * o0[nat, :] + e1 * o1[rows, :] + e2 * o2[rows, :]) / (e0 + e1 + e2)
            stage[nat, :] = mix
    o_ref[...] = stage[...].astype(BF16)


def _attention(proj3, bias_tabs):
    B = proj3.shape[0]
    blk = lambda off: pl.BlockSpec((None, SEQ, HEAD_DIM), lambda b, h: (b, 0, off + h))
    return pl.pallas_call(
        _attn_kernel,
        grid=(B, N_HEADS),
        in_specs=[
            blk(0), blk(N_HEADS), blk(2 * N_HEADS),
            pl.BlockSpec((3, None, BAND, 2 * BAND), lambda b, h: (0, h, 0, 0)),
        ],
        out_specs=pl.BlockSpec((None, SEQ, HEAD_DIM), lambda b, h: (b, 0, h)),
        out_shape=jax.ShapeDtypeStruct((B, SEQ, D_ATTN), BF16),
        scratch_shapes=[pltpu.VMEM((SEQ, HEAD_DIM), F32) for _ in range(13)],
        compiler_params=_cparams(("arbitrary", "arbitrary")),
        name="dilated_attention",
    )(proj3, proj3, proj3, bias_tabs)


def _sgu_kernel(u_ref, z_ref, w_ref, g_ref, b_ref, bs_ref, o_ref):
    n = SGU_CHUNK
    for c in range(TM_SGU // n):
        rs = slice(c * n, (c + 1) * n)
        for g in range(N_SGU_GROUPS):
            cs = slice(g * n, (g + 1) * n)
            z = _gelu(z_ref[rs, cs].astype(F32))
            mu = jnp.mean(z, axis=-1, keepdims=True)
            zc = z - mu
            var = jnp.mean(zc * zc, axis=-1, keepdims=True)
            zn = (zc * lax.rsqrt(var + NORM_EPS)) * g_ref[:, cs] + b_ref[:, cs]
            mixed = jnp.dot(w_ref[g], zn.astype(BF16), preferred_element_type=F32) + bs_ref[:, g:g + 1]
            u = _gelu(u_ref[rs, cs].astype(F32))
            o_ref[rs, cs] = (u * mixed).astype(BF16)


def _spatial_gating(proj, w_causal_b, ln_g, ln_b, bs_t):
    T = proj.shape[0]
    tm = TM_SGU
    ucol = 3 * D_ATTN // D_SGU
    return pl.pallas_call(
        _sgu_kernel,
        grid=(T // tm,),
        in_specs=[
            pl.BlockSpec((tm, D_SGU), lambda i: (i, ucol)),
            pl.BlockSpec((tm, D_SGU), lambda i: (i, ucol + 1)),
            pl.BlockSpec((N_SGU_GROUPS, SGU_CHUNK, SGU_CHUNK), lambda i: (0, 0, 0)),
            pl.BlockSpec((1, D_SGU), lambda i: (0, 0)),
            pl.BlockSpec((1, D_SGU), lambda i: (0, 0)),
            pl.BlockSpec((SGU_CHUNK, N_SGU_GROUPS), lambda i: (0, 0)),
        ],
        out_specs=pl.BlockSpec((tm, D_SGU), lambda i: (i, 0)),
        out_shape=jax.ShapeDtypeStruct((T, D_SGU), BF16),
        compiler_params=_cparams(("arbitrary",)),
        name="spatial_gating",
    )(proj, proj, w_causal_b, ln_g, ln_b, bs_t)


def _outproj_kernel(a_ref, s_ref, x_ref, mod_ref, g_ref, w_ref, x1_ref, h2_ref, h2p_ref):
    mixed = jnp.dot(a_ref[...], w_ref[0:D_ATTN, :], preferred_element_type=F32)
    mixed += jnp.dot(s_ref[...], w_ref[D_ATTN:, :], preferred_element_type=F32)
    x1 = x_ref[...] + mod_ref[0, 2:3, :] * mixed
    x1_ref[...] = x1
    r = lax.rsqrt(jnp.mean(x1 * x1, axis=-1, keepdims=True) + NORM_EPS)
    h2 = (x1 * r) * g_ref[...] * (1.0 + mod_ref[0, 4:5, :]) + mod_ref[0, 3:4, :]
    h2_ref[...] = h2.astype(BF16)
    h2p_ref[...] = _pack_halves(h2)


def _out_projection(attn, sgu, x2, mod3, norm_g, w_out_b):
    T, D = x2.shape
    tm = TM_OUT
    per_batch = SEQ // tm
    return pl.pallas_call(
        _outproj_kernel,
        grid=(T // tm,),
        in_specs=[
            pl.BlockSpec((tm, D_ATTN), lambda i: (i, 0)),
            pl.BlockSpec((tm, D_SGU), lambda i: (i, 0)),
            pl.BlockSpec((tm, D), lambda i: (i, 0)),
            pl.BlockSpec((1, 6, D), lambda i: (i // per_batch, 0, 0)),
            pl.BlockSpec((1, D), lambda i: (0, 0)),
            pl.BlockSpec((D_ATTN + D_SGU, D), lambda i: (0, 0)),
        ],
        out_specs=[pl.BlockSpec((tm, D), lambda i: (i, 0)), pl.BlockSpec((tm, D), lambda i: (i, 0)),
                   pl.BlockSpec((tm, D // 2), lambda i: (i, 0))],
        out_shape=[jax.ShapeDtypeStruct((T, D), F32), jax.ShapeDtypeStruct((T, D), BF16),
                   jax.ShapeDtypeStruct((T, D // 2), U32)],
        compiler_params=_cparams(("arbitrary",)),
        name="out_projection",
    )(attn, sgu, x2, mod3, norm_g, w_out_b)


def _router_kernel(h_ref, w_ref, b_ref, idx_ref, wt_ref, rank_ref, cnt_ref, run_ref):
    i = pl.program_id(0)
    tr = TM_ROUTE
    E = N_EXPERTS

    @pl.when(i == 0)
    def _():
        run_ref[...] = jnp.zeros_like(run_ref)

    logits = lax.dot_general(w_ref[...], h_ref[...], (((1,), (1,)), ((), ())), preferred_element_type=F32)
    scores = jax.nn.sigmoid(logits)
    sel = scores + b_ref[...]

    slabs = [sel[g * GROUP_SIZE:(g + 1) * GROUP_SIZE, :] for g in range(N_EXPERT_GROUPS)]
    si = lax.broadcasted_iota(I32, (GROUP_SIZE, tr), 0).astype(F32)
    gs = []
    for slab in slabs:
        m1 = jnp.max(slab, axis=0, keepdims=True)
        first = jnp.min(jnp.where(slab == m1, si, float(GROUP_SIZE)), axis=0, keepdims=True)
        m2 = jnp.max(jnp.where(si == first, -jnp.inf, slab), axis=0, keepdims=True)
        gs.append(m1 + m2)

    kept = []
    for g in range(N_EXPERT_GROUPS):
        beaten = jnp.zeros((1, tr), F32)
        for o in range(N_EXPERT_GROUPS):
            if o < g:
                beaten += (gs[o] >= gs[g]).astype(F32)
            elif o > g:
                beaten += (gs[o] > gs[g]).astype(F32)
        kept.append(jnp.where(beaten < TOPK_GROUPS, slabs[g], -jnp.inf))
    masked = jnp.concatenate(kept, axis=0)

    ei = lax.broadcasted_iota(I32, (E, tr), 0).astype(F32)
    picks, pick_scores = [], []
    onehot_sum = jnp.zeros((E, tr), F32)
    for k in range(TOP_K):
        m = jnp.max(masked, axis=0, keepdims=True)
        ik = jnp.min(jnp.where(masked == m, ei, float(E)), axis=0, keepdims=True)
        oh = ei == ik
        pick_scores.append(jnp.sum(jnp.where(oh, scores, 0.0), axis=0, keepdims=True))
        masked = jnp.where(oh, -jnp.inf, masked)
        onehot_sum += oh.astype(F32)
        picks.append(ik)

    ti = lax.broadcasted_iota(I32, (tr, tr), 0)
    tj = lax.broadcasted_iota(I32, (tr, tr), 1)
    before = (ti < tj).astype(BF16)
    prior = jnp.dot(onehot_sum.astype(BF16), before, preferred_element_type=F32) + run_ref[...]

    total = pick_scores[0]
    for k in range(1, TOP_K):
        total += pick_scores[k]
    for k in range(TOP_K):
        idx_ref[k:k + 1, :] = picks[k].astype(I32)
        wt_ref[k:k + 1, :] = pick_scores[k] / total * ROUTED_SCALE
        rk = jnp.sum(jnp.where(ei == picks[k], prior, 0.0), axis=0, keepdims=True)
        rank_ref[k:k + 1, :] = rk.astype(I32)

    run_ref[...] += jnp.sum(onehot_sum, axis=1, keepdims=True)
    cnt_ref[...] = run_ref[...].astype(I32)


def _route(h2, router_wt_b, router_bias):
    T, D = h2.shape
    tr = TM_ROUTE
    E = N_EXPERTS
    row_spec = pl.BlockSpec((TOP_K, tr), lambda i: (0, i))
    return pl.pallas_call(
        _router_kernel,
        grid=(T // tr,),
        in_specs=[
            pl.BlockSpec((tr, D), lambda i: (i, 0)),
            pl.BlockSpec((E, D), lambda i: (0, 0)),
            pl.BlockSpec((E, 1), lambda i: (0, 0)),
        ],
        out_specs=[row_spec, row_spec, row_spec, pl.BlockSpec((E, 1), lambda i: (0, 0))],
        out_shape=[
            jax.ShapeDtypeStruct((TOP_K, T), I32),
            jax.ShapeDtypeStruct((TOP_K, T), F32),
            jax.ShapeDtypeStruct((TOP_K, T), I32),
            jax.ShapeDtypeStruct((E, 1), I32),
        ],
        scratch_shapes=[pltpu.VMEM((E, 1), F32)],
        compiler_params=_cparams(("arbitrary",)),
        name="router",
    )(h2, router_wt_b, router_bias.reshape(E, 1))


def _dispatch_kernel(pend_ref, h_ref, dest_ref, xg_ref, dest_s, zero_ref, sem_s, sem):
    i = pl.program_id(0)
    td = TM_DISPATCH
    blk = EXPERT_BLOCK

    def zero_copy(e):
        start = pl.multiple_of(pend_ref[e + 1] - blk, blk)
        return pltpu.make_async_copy(zero_ref, xg_ref.at[pl.ds(start, blk), :], sem)

    @pl.when(i == 0)
    def _():
        zero_ref[...] = jnp.zeros_like(zero_ref)

        def start_zero(e, c):
            @pl.when(pend_ref[e + 1] > pend_ref[e])
            def _():
                zero_copy(e).start()
            return c

        def wait_zero(e, c):
            @pl.when(pend_ref[e + 1] > pend_ref[e])
            def _():
                zero_copy(e).wait()
            return c

        lax.fori_loop(0, N_EXPERTS, start_zero, 0)
        lax.fori_loop(0, N_EXPERTS, wait_zero, 0)

    cp = pltpu.make_async_copy(dest_ref, dest_s, sem_s)
    cp.start()
    cp.wait()

    def row_copy(t, k):
        return pltpu.make_async_copy(h_ref.at[pl.ds(t, 1), :], xg_ref.at[pl.ds(dest_s[k, t], 1), :], sem)

    def start_rows(t8, c):
        base = pl.multiple_of(t8 * 8, 8)
        for j in range(8):
            for k in range(TOP_K):
                row_copy(base + j, k).start()
        return c

    def wait_rows(t, c):
        for k in range(TOP_K):
            row_copy(t, k).wait()
        return c

    lax.fori_loop(0, td // 8, start_rows, 0)
    lax.fori_loop(0, td, wait_rows, 0)


def _dispatch(h2p, dest, pend0, rows):
    T, D = h2p.shape
    td = TM_DISPATCH
    return pl.pallas_call(
        _dispatch_kernel,
        grid_spec=pltpu.PrefetchScalarGridSpec(
            num_scalar_prefetch=1,
            grid=(T // td,),
            in_specs=[
                pl.BlockSpec((td, D), lambda i, p: (i, 0)),
                pl.BlockSpec((TOP_K, td), lambda i, p: (0, i)),
            ],
            out_specs=pl.BlockSpec(memory_space=pl.ANY),
            scratch_shapes=[
                pltpu.SMEM((TOP_K, td), I32),
                pltpu.VMEM((EXPERT_BLOCK, D), U32),
                pltpu.SemaphoreType.DMA,
                pltpu.SemaphoreType.DMA,
            ],
        ),
        out_shape=jax.ShapeDtypeStruct((rows, D), U32),
        compiler_params=_cparams(("arbitrary",)),
        name="dispatch",
    )(pend0, h2p, dest)


def _experts_kernel(bexp_ref, first_ref, slot_ref, next_ref, pos_ref, nprev_ref, nused_ref,
                    x_ref, wg_hbm, wu_hbm, wd_hbm, y_ref,
                    wg_f, wu_f, wd_f, wg_b, wu_b, wd_b, sem):
    i = pl.program_id(0)
    rows_in = D_MODEL // WEIGHT_CHUNKS
    rows_out = D_EXPERT // WEIGHT_CHUNKS

    def chunk_copies(e, s, c):
        gi = pl.ds(c * rows_in, rows_in)
        di = pl.ds(c * rows_out, rows_out)
        return (pltpu.make_async_copy(wg_hbm.at[e, gi, :], wg_f.at[s, gi, :], sem.at[s, 0, c]),
                pltpu.make_async_copy(wu_hbm.at[e, gi, :], wu_f.at[s, gi, :], sem.at[s, 1, c]),
                pltpu.make_async_copy(wd_hbm.at[e, di, :], wd_f.at[s, di, :], sem.at[s, 2, c]))

    @pl.when(i < nused_ref[0])
    def _():
        s = slot_ref[i]

        @pl.when(first_ref[i] == 1)
        def _():
            for c in range(WEIGHT_CHUNKS):
                @pl.when(nprev_ref[i] <= c)
                def _():
                    for cp in chunk_copies(bexp_ref[i], s, c):
                        cp.start()
            for c in range(WEIGHT_CHUNKS):
                for cp in chunk_copies(bexp_ref[i], s, c):
                    cp.wait()
            wg_b[...] = wg_f[s].astype(BF16)
            wu_b[...] = wu_f[s].astype(BF16)
            wd_b[...] = wd_f[s].astype(BF16)

        for c in range(WEIGHT_CHUNKS):
            @pl.when((pos_ref[i] == c) & (next_ref[i] >= 0))
            def _():
                for cp in chunk_copies(next_ref[i], 1 - s, c):
                    cp.start()

        lo, hi = _unpack_halves(x_ref[...])
        x = jnp.concatenate([lo.astype(BF16), hi.astype(BF16)], axis=1)
        g = jnp.dot(x, wg_b[...], preferred_element_type=F32)
        u = jnp.dot(x, wu_b[...], preferred_element_type=F32)
        a = (_silu(g) * u).astype(BF16)
        y_ref[...] = _pack_halves(jnp.dot(a, wd_b[...], preferred_element_type=F32))


def _experts(xg, block_exp, first, slot, next_exp, pos, nprev, n_used, w_gate, w_up, w_down):
    rows = xg.shape[0]
    D = D_MODEL
    bm = EXPERT_BLOCK
    F = D_EXPERT
    nb = rows // bm
    row_map = lambda i, be, fi, sl, nx, po, npv, nu: (jnp.minimum(i, nu[0] - 1), 0)
    return pl.pallas_call(
        _experts_kernel,
        grid_spec=pltpu.PrefetchScalarGridSpec(
            num_scalar_prefetch=7,
            grid=(nb,),
            in_specs=[
                pl.BlockSpec((bm, D // 2), row_map),
                pl.BlockSpec(memory_space=pl.ANY),
                pl.BlockSpec(memory_space=pl.ANY),
                pl.BlockSpec(memory_space=pl.ANY),
            ],
            out_specs=pl.BlockSpec((bm, D // 2), row_map),
            scratch_shapes=[
                pltpu.VMEM((2, D, F), F32), pltpu.VMEM((2, D, F), F32), pltpu.VMEM((2, F, D), F32),
                pltpu.VMEM((D, F), BF16), pltpu.VMEM((D, F), BF16), pltpu.VMEM((F, D), BF16),
                pltpu.SemaphoreType.DMA((2, 3, WEIGHT_CHUNKS)),
            ],
        ),
        out_shape=jax.ShapeDtypeStruct((rows, D // 2), U32),
        compiler_params=_cparams(("arbitrary",)),
        name="experts",
    )(block_exp, first, slot, next_exp, pos, nprev, n_used, xg, w_gate, w_up, w_down)


def _combine_kernel(dest_ref, wt_ref, x1_ref, h_ref, mod_ref, sg_ref, su_ref, sd_ref, yg_ref, o_ref,
                    dest_s, buf, sem_s, sem):
    tc = TM_COMBINE
    cp = pltpu.make_async_copy(dest_ref, dest_s, sem_s)
    cp.start()
    cp.wait()

    def row_copy(t, k):
        return pltpu.make_async_copy(yg_ref.at[pl.ds(dest_s[k, t], 1), :], buf.at[k, pl.ds(t, 1), :], sem)

    def start_rows(t8, c):
        base = pl.multiple_of(t8 * 8, 8)
        for j in range(8):
            for k in range(TOP_K):
                row_copy(base + j, k).start()
        return c

    def wait_rows(t, c):
        for k in range(TOP_K):
            row_copy(t, k).wait()
        return c

    lax.fori_loop(0, tc // 8, start_rows, 0)

    hb = h_ref[...]
    g = jnp.dot(hb, sg_ref[...], preferred_element_type=F32)
    u = jnp.dot(hb, su_ref[...], preferred_element_type=F32)
    ffn = jnp.dot((_silu(g) * u).astype(BF16), sd_ref[...], preferred_element_type=F32)

    lax.fori_loop(0, tc, wait_rows, 0)
    half = D_MODEL // 2
    ffn_lo, ffn_hi = ffn[:, :half], ffn[:, half:]
    for k in range(TOP_K):
        lo, hi = _unpack_halves(buf[k])
        w = wt_ref[:, k:k + 1]
        ffn_lo += lo * w
        ffn_hi += hi * w
    o_ref[:, :half] = x1_ref[:, :half] + mod_ref[0, 5:6, :half] * ffn_lo
    o_ref[:, half:] = x1_ref[:, half:] + mod_ref[0, 5:6, half:] * ffn_hi


def _combine(dest, wt_t, x1, h2, mod3, sg_b, su_b, sd_b, yg):
    T, D = x1.shape
    tc = TM_COMBINE
    per_batch = SEQ // tc
    F = D_EXPERT
    return pl.pallas_call(
        _combine_kernel,
        grid=(T // tc,),
        in_specs=[
            pl.BlockSpec((TOP_K, tc), lambda i: (0, i)),
            pl.BlockSpec((tc, TOP_K), lambda i: (i, 0)),
            pl.BlockSpec((tc, D), lambda i: (i, 0)),
            pl.BlockSpec((tc, D), lambda i: (i, 0)),
            pl.BlockSpec((1, 6, D), lambda i: (i // per_batch, 0, 0)),
            pl.BlockSpec((D, F), lambda i: (0, 0)),
            pl.BlockSpec((D, F), lambda i: (0, 0)),
            pl.BlockSpec((F, D), lambda i: (0, 0)),
            pl.BlockSpec(memory_space=pl.ANY),
        ],
        out_specs=pl.BlockSpec((tc, D), lambda i: (i, 0)),
        out_shape=jax.ShapeDtypeStruct((T, D), F32),
        scratch_shapes=[
            pltpu.SMEM((TOP_K, tc), I32),
            pltpu.VMEM((TOP_K, tc, D // 2), U32),
            pltpu.SemaphoreType.DMA,
            pltpu.SemaphoreType.DMA,
        ],
        compiler_params=_cparams(("arbitrary",)),
        name="combine",
    )(dest, wt_t, x1, h2, mod3, sg_b, su_b, sd_b, yg)


def kernel(x, c, ada_w, ada_b, mix_norm_g, ffn_norm_g, w_in, q_norm_g, k_norm_g, rel_bias, sgu_ln_g, sgu_ln_b, sgu_w, sgu_b, w_out, router_w, router_bias, shared_w_gate, shared_w_up, shared_w_down, expert_w_gate, expert_w_up, expert_w_down):
    B, S, D = x.shape
    assert S == SEQ and D == D_MODEL and ada_w.shape[0] == 1
    T = B * S
    x2 = x.reshape(T, D)

    mod3 = _modulation(c, ada_w[0], ada_b[0]).reshape(B, 6, D)

    proj = _in_projection(x2, mod3, mix_norm_g, w_in[0].astype(BF16), q_norm_g, k_norm_g)
    attn = _attention(proj.reshape(B, S, D_IN_PROJ), _bias_tables(rel_bias)).reshape(T, D_ATTN)
    causal = jnp.tril(jnp.ones((SGU_CHUNK, SGU_CHUNK), F32))
    sgu = _spatial_gating(proj, (sgu_w[0] * causal).astype(BF16), sgu_ln_g, sgu_ln_b, sgu_b[0].T)
    x1, h2, h2p = _out_projection(attn, sgu, x2, mod3, ffn_norm_g, w_out[0].astype(BF16))

    idx, wts, rank, counts = _route(h2, router_w[0].T.astype(BF16), router_bias[0])

    bm = EXPERT_BLOCK
    counts = counts.reshape(N_EXPERTS)
    padded = (counts + bm - 1) // bm * bm
    pends = jnp.cumsum(padded)
    pstarts = pends - padded
    n_blocks = T * TOP_K // bm + N_EXPERTS
    block_start = jnp.arange(n_blocks, dtype=I32) * bm
    block_exp = jnp.minimum(jnp.sum((pends[None, :] <= block_start[:, None]).astype(I32), axis=1), N_EXPERTS - 1)
    n_used = (pends[-1] // bm).astype(I32).reshape(1)
    eids = jnp.arange(N_EXPERTS, dtype=I32)
    dest = jnp.sum(jnp.where(idx[:, :, None] == eids, pstarts.astype(I32), 0), axis=-1) + rank
    pend0 = jnp.concatenate([jnp.zeros((1,), I32), pends.astype(I32)])
    first = jnp.concatenate([jnp.ones((1,), I32), (block_exp[1:] != block_exp[:-1]).astype(I32)])
    slot = (jnp.cumsum(first) - 1) % 2
    later_used = jnp.where(counts > 0, eids, N_EXPERTS)
    next_used = lax.cummin(jnp.concatenate([later_used[1:], jnp.full((1,), N_EXPERTS, I32)]), reverse=True)
    next_used = jnp.where(next_used >= N_EXPERTS, -1, next_used)
    of_block = block_exp[:, None] == eids
    next_exp = jnp.sum(jnp.where(of_block, next_used, 0), axis=-1)
    pos = jnp.arange(n_blocks, dtype=I32) - jnp.sum(jnp.where(of_block, pstarts // bm, 0), axis=-1)
    earlier_used = jnp.where(counts > 0, eids, -1)
    prev_used = lax.cummax(jnp.concatenate([jnp.full((1,), -1, I32), earlier_used[:-1]]))
    prev_blocks = jnp.sum(jnp.where(prev_used[:, None] == eids, padded // bm, 0), axis=-1)
    nprev = jnp.minimum(jnp.sum(jnp.where(of_block, prev_blocks, 0), axis=-1), WEIGHT_CHUNKS)

    xg = _dispatch(h2p, dest, pend0, n_blocks * bm)
    yg = _experts(xg, block_exp, first, slot.astype(I32), next_exp.astype(I32), pos.astype(I32),
                  nprev.astype(I32), n_used, expert_w_gate[0], expert_w_up[0], expert_w_down[0])
    out = _combine(dest, wts.T, x1, h2, mod3, shared_w_gate[0].astype(BF16), shared_w_up[0].astype(BF16),
                   shared_w_down[0].astype(BF16), yg)
    return out.reshape(B, S, D)
```

```python
import functools
import math

import numpy as np
import jax
import jax.numpy as jnp
from jax import lax
from jax.experimental import pallas as pl
from jax.experimental.pallas import tpu as pltpu

F32 = jnp.float32
BF16 = jnp.bfloat16
I32 = jnp.int32
U32 = jnp.uint32

D_MODEL = 2048
SEQ = 2048
HEAD_DIM = 128
N_HEADS = 8
D_ATTN = N_HEADS * HEAD_DIM
D_SGU = 1024
N_SGU_GROUPS = 8
SGU_CHUNK = 128
D_IN_PROJ = 3 * D_ATTN + 2 * D_SGU
DILATED_GROUPS = ((128, 1), (512, 4), (2048, 16))
BAND = 128
NUM_REL_BUCKETS = 32
REL_MAX_DISTANCE = 2048
N_EXPERTS = 256
TOP_K = 8
N_EXPERT_GROUPS = 8
GROUP_SIZE = N_EXPERTS // N_EXPERT_GROUPS
TOPK_GROUPS = 4
D_EXPERT = 512
ROUTED_SCALE = 2.5
NORM_EPS = 1e-6
MASK_VALUE = -1e30

LANES = 128
VMEM_LIMIT = 56 * 1024 * 1024

TM_PROJ = 512
TM_SGU = 512
TM_OUT = 256
TM_ROUTE = 512
TM_DISPATCH = 256
TM_COMBINE = 128
EXPERT_BLOCK = 128
WEIGHT_CHUNKS = 4
WEIGHT_SETS_PER_STEP = 2


def _cparams(sem):
    return pltpu.CompilerParams(dimension_semantics=sem, vmem_limit_bytes=VMEM_LIMIT)


def _silu(v):
    return v * jax.nn.sigmoid(v)


def _gelu(v):
    return 0.5 * v * (1.0 + lax.erf(v * (1.0 / math.sqrt(2.0))))


def _pack_halves(v):
    h = v.shape[1] // 2
    lo = lax.bitcast_convert_type(v[:, :h].astype(BF16).astype(F32), U32)
    hi = lax.bitcast_convert_type(v[:, h:].astype(BF16).astype(F32), U32)
    return (hi & jnp.uint32(0xFFFF0000)) | (lo >> 16)


def _unpack_halves(w):
    lo = lax.bitcast_convert_type(w << 16, F32)
    hi = lax.bitcast_convert_type(w & jnp.uint32(0xFFFF0000), F32)
    return lo, hi


def _mod_kernel(c_ref, w_ref, b_ref, o_ref):
    ca = _silu(c_ref[...]).astype(BF16)
    o_ref[...] = jnp.dot(ca, w_ref[...].astype(BF16), preferred_element_type=F32) + b_ref[...]


def _modulation(c, ada_w, ada_b):
    B, D = c.shape
    N = ada_w.shape[1]
    tn = 1024
    return pl.pallas_call(
        _mod_kernel,
        grid=(N // tn,),
        in_specs=[
            pl.BlockSpec((B, D), lambda j: (0, 0)),
            pl.BlockSpec((D, tn), lambda j: (0, j)),
            pl.BlockSpec((1, tn), lambda j: (0, j)),
        ],
        out_specs=pl.BlockSpec((B, tn), lambda j: (0, j)),
        out_shape=jax.ShapeDtypeStruct((B, N), F32),
        compiler_params=_cparams(("arbitrary",)),
        name="modulation",
    )(c, ada_w, ada_b.reshape(1, N))


def _inproj_kernel(x_ref, mod_ref, g_ref, w_ref, qg_ref, kg_ref, o_ref, h_ref):
    j = pl.program_id(1)

    @pl.when(j == 0)
    def _():
        x = x_ref[...]
        r = lax.rsqrt(jnp.mean(x * x, axis=-1, keepdims=True) + NORM_EPS)
        h = (x * r) * g_ref[...] * (1.0 + mod_ref[0, 1:2, :]) + mod_ref[0, 0:1, :]
        h_ref[...] = h.astype(BF16)

    acc = jnp.dot(h_ref[...], w_ref[...], preferred_element_type=F32)

    def head_norm(gain_ref, scale):
        for hd in range(N_HEADS):
            a = acc[:, hd * HEAD_DIM:(hd + 1) * HEAD_DIM]
            r = lax.rsqrt(jnp.mean(a * a, axis=-1, keepdims=True) + NORM_EPS)
            o_ref[:, hd * HEAD_DIM:(hd + 1) * HEAD_DIM] = ((a * r) * gain_ref[...] * scale).astype(BF16)

    @pl.when(j == 0)
    def _():
        head_norm(qg_ref, HEAD_DIM ** -0.5)

    @pl.when(j == 1)
    def _():
        head_norm(kg_ref, 1.0)

    @pl.when(j >= 2)
    def _():
        o_ref[...] = acc.astype(BF16)


def _in_projection(x2, mod3, norm_g, w_in_b, q_g, k_g):
    T, D = x2.shape
    tm, tn = TM_PROJ, D_ATTN
    per_batch = SEQ // tm
    return pl.pallas_call(
        _inproj_kernel,
        grid=(T // tm, D_IN_PROJ // tn),
        in_specs=[
            pl.BlockSpec((tm, D), lambda i, j: (i, 0)),
            pl.BlockSpec((1, 6, D), lambda i, j: (i // per_batch, 0, 0)),
            pl.BlockSpec((1, D), lambda i, j: (0, 0)),
            pl.BlockSpec((D, tn), lambda i, j: (0, j)),
            pl.BlockSpec((1, HEAD_DIM), lambda i, j: (0, 0)),
            pl.BlockSpec((1, HEAD_DIM), lambda i, j: (0, 0)),
        ],
        out_specs=pl.BlockSpec((tm, tn), lambda i, j: (i, j)),
        out_shape=jax.ShapeDtypeStruct((T, D_IN_PROJ), BF16),
        scratch_shapes=[pltpu.VMEM((tm, D), BF16)],
        compiler_params=_cparams(("arbitrary", "arbitrary")),
        name="in_projection",
    )(x2, mod3, norm_g, w_in_b, q_g, k_g)


def _bias_tables(rel_bias):
    n = BAND
    qi = jnp.arange(n)[:, None]
    ki = jnp.arange(2 * n)[None, :]
    steps = n + qi - ki
    in_band = (steps >= 0) & (steps <= n)
    max_exact = NUM_REL_BUCKETS // 2
    tabs = []
    for _, dilation in DILATED_GROUPS:
        dist = jnp.clip(steps, 0, n) * dilation
        nf = jnp.maximum(dist, 1).astype(F32)
        large = max_exact + (jnp.log(nf / max_exact) / math.log(REL_MAX_DISTANCE / max_exact)
                             * (NUM_REL_BUCKETS - max_exact)).astype(I32)
        large = jnp.minimum(large, NUM_REL_BUCKETS - 1)
        bucket = jnp.where(dist < max_exact, dist, large)
        onehot = jax.nn.one_hot(bucket, NUM_REL_BUCKETS, dtype=F32)
        b = jnp.einsum("qkb,bh->hqk", onehot, rel_bias.astype(F32), precision=lax.Precision.HIGHEST)
        tabs.append(jnp.where(in_band[None], b, MASK_VALUE))
    return jnp.stack(tabs)


def _attn_kernel(q_ref, k_ref, v_ref, bias_ref, o_ref, qf, kf, vf, q4, k4, v4,
                 o0, l0, o1, l1, o2, l2, stage):
    n = BAND
    quarter = SEQ // 4

    for src, nat, res in ((q_ref, qf, q4), (k_ref, kf, k4), (v_ref, vf, v4)):
        nat[...] = src[...].astype(F32)
        for r in range(4):
            res[r * quarter:(r + 1) * quarter, :] = nat[pl.ds(r, quarter, stride=4), :]

    def piece(qb, kb, vb, bias, o_out, l_out, rows):
        l = lax.dot_general(qb, kb, (((1,), (1,)), ((), ())), preferred_element_type=F32) + bias
        m = jnp.max(l, axis=-1, keepdims=True)
        p = jnp.exp(l - m)
        s = jnp.sum(p, axis=-1, keepdims=True)
        o = jnp.dot(p.astype(BF16), vb, preferred_element_type=F32)
        o_out[rows, :] = o / s
        l_out[rows, :] = jnp.broadcast_to(m + jnp.log(s), (n, LANES))

    for i in range(SEQ // n):
        rows = slice(i * n, (i + 1) * n)
        if i == 0:
            piece(q_ref[rows, :], k_ref[rows, :], v_ref[rows, :], bias_ref[0, :, n:], o0, l0, rows)
        else:
            krows = slice((i - 1) * n, (i + 1) * n)
            piece(q_ref[rows, :], k_ref[krows, :], v_ref[krows, :], bias_ref[0], o0, l0, rows)

    for r in range(4):
        for blk in range(quarter // n):
            base = r * quarter + blk * n
            rows = slice(base, base + n)
            krows = rows if blk == 0 else slice(base - n, base + n)
            bias = bias_ref[1, :, n:] if blk == 0 else bias_ref[1]
            piece(q4[rows, :].astype(BF16), k4[krows, :].astype(BF16), v4[krows, :].astype(BF16),
                  bias, o1, l1, rows)

    for r in range(4):
        for a in range(4):
            rows = pl.ds(r * quarter + a, n, stride=4)
            piece(q4[rows, :].astype(BF16), k4[rows, :].astype(BF16), v4[rows, :].astype(BF16),
                  bias_ref[2, :, n:], o2, l2, rows)

    for r in range(4):
        for c in range(quarter // n):
            rows = slice(r * quarter + c * n, r * quarter + (c + 1) * n)
            nat = pl.ds(r + 4 * c * n, n, stride=4)
            a0, a1, a2 = l0[nat, :], l1[rows, :], l2[rows, :]
            m = jnp.maximum(jnp.maximum(a0, a1), a2)
            e0, e1, e2 = jnp.exp(a0 - m), jnp.exp(a1 - m), jnp.exp(a2 - m)
            mix = (e0 * o0[nat, :] + e1 * o1[rows, :] + e2 * o2[rows, :]) / (e0 + e1 + e2)
            stage[nat, :] = mix
    o_ref[...] = stage[...].astype(BF16)


def _attention(proj3, bias_tabs):
    B = proj3.shape[0]
    blk = lambda off: pl.BlockSpec((None, SEQ, HEAD_DIM), lambda b, h: (b, 0, off + h))
    return pl.pallas_call(
        _attn_kernel,
        grid=(B, N_HEADS),
        in_specs=[
            blk(0), blk(N_HEADS), blk(2 * N_HEADS),
            pl.BlockSpec((3, None, BAND, 2 * BAND), lambda b, h: (0, h, 0, 0)),
        ],
        out_specs=pl.BlockSpec((None, SEQ, HEAD_DIM), lambda b, h: (b, 0, h)),
        out_shape=jax.ShapeDtypeStruct((B, SEQ, D_ATTN), BF16),
        scratch_shapes=[pltpu.VMEM((SEQ, HEAD_DIM), F32) for _ in range(13)],
        compiler_params=_cparams(("arbitrary", "arbitrary")),
        name="dilated_attention",
    )(proj3, proj3, proj3, bias_tabs)


def _sgu_kernel(u_ref, z_ref, w_ref, g_ref, b_ref, bs_ref, o_ref):
    n = SGU_CHUNK
    for c in range(TM_SGU // n):
        rs = slice(c * n, (c + 1) * n)
        for g in range(N_SGU_GROUPS):
            cs = slice(g * n, (g + 1) * n)
            z = _gelu(z_ref[rs, cs].astype(F32))
            mu = jnp.mean(z, axis=-1, keepdims=True)
            zc = z - mu
            var = jnp.mean(zc * zc, axis=-1, keepdims=True)
            zn = (zc * lax.rsqrt(var + NORM_EPS)) * g_ref[:, cs] + b_ref[:, cs]
            mixed = jnp.dot(w_ref[g], zn.astype(BF16), preferred_element_type=F32) + bs_ref[:, g:g + 1]
            u = _gelu(u_ref[rs, cs].astype(F32))
            o_ref[rs, cs] = (u * mixed).astype(BF16)


def _spatial_gating(proj, w_causal_b, ln_g, ln_b, bs_t):
    T = proj.shape[0]
    tm = TM_SGU
    ucol = 3 * D_ATTN // D_SGU
    return pl.pallas_call(
        _sgu_kernel,
        grid=(T // tm,),
        in_specs=[
            pl.BlockSpec((tm, D_SGU), lambda i: (i, ucol)),
            pl.BlockSpec((tm, D_SGU), lambda i: (i, ucol + 1)),
            pl.BlockSpec((N_SGU_GROUPS, SGU_CHUNK, SGU_CHUNK), lambda i: (0, 0, 0)),
            pl.BlockSpec((1, D_SGU), lambda i: (0, 0)),
            pl.BlockSpec((1, D_SGU), lambda i: (0, 0)),
            pl.BlockSpec((SGU_CHUNK, N_SGU_GROUPS), lambda i: (0, 0)),
        ],
        out_specs=pl.BlockSpec((tm, D_SGU), lambda i: (i, 0)),
        out_shape=jax.ShapeDtypeStruct((T, D_SGU), BF16),
        compiler_params=_cparams(("arbitrary",)),
        name="spatial_gating",
    )(proj, proj, w_causal_b, ln_g, ln_b, bs_t)


def _outproj_kernel(a_ref, s_ref, x_ref, mod_ref, g_ref, w_ref, x1_ref, h2_ref, h2p_ref):
    mixed = jnp.dot(a_ref[...], w_ref[0:D_ATTN, :], preferred_element_type=F32)
    mixed += jnp.dot(s_ref[...], w_ref[D_ATTN:, :], preferred_element_type=F32)
    x1 = x_ref[...] + mod_ref[0, 2:3, :] * mixed
    x1_ref[...] = x1
    r = lax.rsqrt(jnp.mean(x1 * x1, axis=-1, keepdims=True) + NORM_EPS)
    h2 = (x1 * r) * g_ref[...] * (1.0 + mod_ref[0, 4:5, :]) + mod_ref[0, 3:4, :]
    h2_ref[...] = h2.astype(BF16)
    h2p_ref[...] = _pack_halves(h2)


def _out_projection(attn, sgu, x2, mod3, norm_g, w_out_b):
    T, D = x2.shape
    tm = TM_OUT
    per_batch = SEQ // tm
    return pl.pallas_call(
        _outproj_kernel,
        grid=(T // tm,),
        in_specs=[
            pl.BlockSpec((tm, D_ATTN), lambda i: (i, 0)),
            pl.BlockSpec((tm, D_SGU), lambda i: (i, 0)),
            pl.BlockSpec((tm, D), lambda i: (i, 0)),
            pl.BlockSpec((1, 6, D), lambda i: (i // per_batch, 0, 0)),
            pl.BlockSpec((1, D), lambda i: (0, 0)),
            pl.BlockSpec((D_ATTN + D_SGU, D), lambda i: (0, 0)),
        ],
        out_specs=[pl.BlockSpec((tm, D), lambda i: (i, 0)), pl.BlockSpec((tm, D), lambda i: (i, 0)),
                   pl.BlockSpec((tm, D // 2), lambda i: (i, 0))],
        out_shape=[jax.ShapeDtypeStruct((T, D), F32), jax.ShapeDtypeStruct((T, D), BF16),
                   jax.ShapeDtypeStruct((T, D // 2), U32)],
        compiler_params=_cparams(("arbitrary",)),
        name="out_projection",
    )(attn, sgu, x2, mod3, norm_g, w_out_b)


def _router_kernel(h_ref, w_ref, b_ref, idx_ref, wt_ref, rank_ref, cnt_ref, run_ref):
    i = pl.program_id(0)
    tr = TM_ROUTE
    E = N_EXPERTS

    @pl.when(i == 0)
    def _():
        run_ref[...] = jnp.zeros_like(run_ref)

    logits = lax.dot_general(w_ref[...], h_ref[...], (((1,), (1,)), ((), ())), preferred_element_type=F32)
    scores = jax.nn.sigmoid(logits)
    sel = scores + b_ref[...]

    slabs = [sel[g * GROUP_SIZE:(g + 1) * GROUP_SIZE, :] for g in range(N_EXPERT_GROUPS)]
    si = lax.broadcasted_iota(I32, (GROUP_SIZE, tr), 0).astype(F32)
    gs = []
    for slab in slabs:
        m1 = jnp.max(slab, axis=0, keepdims=True)
        first = jnp.min(jnp.where(slab == m1, si, float(GROUP_SIZE)), axis=0, keepdims=True)
        m2 = jnp.max(jnp.where(si == first, -jnp.inf, slab), axis=0, keepdims=True)
        gs.append(m1 + m2)

    kept = []
    for g in range(N_EXPERT_GROUPS):
        beaten = jnp.zeros((1, tr), F32)
        for o in range(N_EXPERT_GROUPS):
            if o < g:
                beaten += (gs[o] >= gs[g]).astype(F32)
            elif o > g:
                beaten += (gs[o] > gs[g]).astype(F32)
        kept.append(jnp.where(beaten < TOPK_GROUPS, slabs[g], -jnp.inf))
    masked = jnp.concatenate(kept, axis=0)

    ei = lax.broadcasted_iota(I32, (E, tr), 0).astype(F32)
    picks, pick_scores = [], []
    onehot_sum = jnp.zeros((E, tr), F32)
    for k in range(TOP_K):
        m = jnp.max(masked, axis=0, keepdims=True)
        ik = jnp.min(jnp.where(masked == m, ei, float(E)), axis=0, keepdims=True)
        oh = ei == ik
        pick_scores.append(jnp.sum(jnp.where(oh, scores, 0.0), axis=0, keepdims=True))
        masked = jnp.where(oh, -jnp.inf, masked)
        onehot_sum += oh.astype(F32)
        picks.append(ik)

    ti = lax.broadcasted_iota(I32, (tr, tr), 0)
    tj = lax.broadcasted_iota(I32, (tr, tr), 1)
    before = (ti < tj).astype(BF16)
    prior = jnp.dot(onehot_sum.astype(BF16), before, preferred_element_type=F32) + run_ref[...]

    total = pick_scores[0]
    for k in range(1, TOP_K):
        total += pick_scores[k]
    for k in range(TOP_K):
        idx_ref[k:k + 1, :] = picks[k].astype(I32)
        wt_ref[k:k + 1, :] = pick_scores[k] / total * ROUTED_SCALE
        rk = jnp.sum(jnp.where(ei == picks[k], prior, 0.0), axis=0, keepdims=True)
        rank_ref[k:k + 1, :] = rk.astype(I32)

    run_ref[...] += jnp.sum(onehot_sum, axis=1, keepdims=True)
    cnt_ref[...] = run_ref[...].astype(I32)


def _route(h2, router_wt_b, router_bias):
    T, D = h2.shape
    tr = TM_ROUTE
    E = N_EXPERTS
    row_spec = pl.BlockSpec((TOP_K, tr), lambda i: (0, i))
    return pl.pallas_call(
        _router_kernel,
        grid=(T // tr,),
        in_specs=[
            pl.BlockSpec((tr, D), lambda i: (i, 0)),
            pl.BlockSpec((E, D), lambda i: (0, 0)),
            pl.BlockSpec((E, 1), lambda i: (0, 0)),
        ],
        out_specs=[row_spec, row_spec, row_spec, pl.BlockSpec((E, 1), lambda i: (0, 0))],
        out_shape=[
            jax.ShapeDtypeStruct((TOP_K, T), I32),
            jax.ShapeDtypeStruct((TOP_K, T), F32),
            jax.ShapeDtypeStruct((TOP_K, T), I32),
            jax.ShapeDtypeStruct((E, 1), I32),
        ],
        scratch_shapes=[pltpu.VMEM((E, 1), F32)],
        compiler_params=_cparams(("arbitrary",)),
        name="router",
    )(h2, router_wt_b, router_bias.reshape(E, 1))


def _dispatch_kernel(pend_ref, h_ref, dest_ref, xg_ref, dest_s, zero_ref, sem_s, sem):
    i = pl.program_id(0)
    td = TM_DISPATCH
    blk = EXPERT_BLOCK

    def zero_copy(e):
        start = pl.multiple_of(pend_ref[e + 1] - blk, blk)
        return pltpu.make_async_copy(zero_ref, xg_ref.at[pl.ds(start, blk), :], sem)

    @pl.when(i == 0)
    def _():
        zero_ref[...] = jnp.zeros_like(zero_ref)

        def start_zero(e, c):
            @pl.when(pend_ref[e + 1] > pend_ref[e])
            def _():
                zero_copy(e).start()
            return c

        def wait_zero(e, c):
            @pl.when(pend_ref[e + 1] > pend_ref[e])
            def _():
                zero_copy(e).wait()
            return c

        lax.fori_loop(0, N_EXPERTS, start_zero, 0)
        lax.fori_loop(0, N_EXPERTS, wait_zero, 0)

    cp = pltpu.make_async_copy(dest_ref, dest_s, sem_s)
    cp.start()
    cp.wait()

    def row_copy(t, k):
        return pltpu.make_async_copy(h_ref.at[pl.ds(t, 1), :], xg_ref.at[pl.ds(dest_s[k, t], 1), :], sem)

    def start_rows(t8, c):
        base = pl.multiple_of(t8 * 8, 8)
        for j in range(8):
            for k in range(TOP_K):
                row_copy(base + j, k).start()
        return c

    def wait_rows(t, c):
        for k in range(TOP_K):
            row_copy(t, k).wait()
        return c

    lax.fori_loop(0, td // 8, start_rows, 0)
    lax.fori_loop(0, td, wait_rows, 0)


def _dispatch(h2p, dest, pend0, rows):
    T, D = h2p.shape
    td = TM_DISPATCH
    return pl.pallas_call(
        _dispatch_kernel,
        grid_spec=pltpu.PrefetchScalarGridSpec(
            num_scalar_prefetch=1,
            grid=(T // td,),
            in_specs=[
                pl.BlockSpec((td, D), lambda i, p: (i, 0)),
                pl.BlockSpec((TOP_K, td), lambda i, p: (0, i)),
            ],
            out_specs=pl.BlockSpec(memory_space=pl.ANY),
            scratch_shapes=[
                pltpu.SMEM((TOP_K, td), I32),
                pltpu.VMEM((EXPERT_BLOCK, D), U32),
                pltpu.SemaphoreType.DMA,
                pltpu.SemaphoreType.DMA,
            ],
        ),
        out_shape=jax.ShapeDtypeStruct((rows, D), U32),
        compiler_params=_cparams(("arbitrary",)),
        name="dispatch",
    )(pend0, h2p, dest)


def _experts_kernel(first_ref, gidx_ref, used_ref, nused_ref, ngroups_ref,
                    x_ref, wg_hbm, wu_hbm, wd_hbm, y_ref,
                    wg_f, wu_f, wd_f, wg_b, wu_b, wd_b, issued, sem):
    i = pl.program_id(0)
    rows_in = D_MODEL // WEIGHT_CHUNKS
    rows_out = D_EXPERT // WEIGHT_CHUNKS

    def set_copies(q):
        h = q // WEIGHT_CHUNKS
        c = q % WEIGHT_CHUNKS
        e = used_ref[h]
        s = h % 2
        gi = pl.ds(pl.multiple_of(c * rows_in, rows_in), rows_in)
        di = pl.ds(pl.multiple_of(c * rows_out, rows_out), rows_out)
        return (pltpu.make_async_copy(wg_hbm.at[e, gi, :], wg_f.at[s, gi, :], sem.at[s, 0, c]),
                pltpu.make_async_copy(wu_hbm.at[e, gi, :], wu_f.at[s, gi, :], sem.at[s, 1, c]),
                pltpu.make_async_copy(wd_hbm.at[e, di, :], wd_f.at[s, di, :], sem.at[s, 2, c]))

    def issue_until(n):
        def body(q, carry):
            for cp in set_copies(q):
                cp.start()
            return carry

        lax.fori_loop(issued[0], n, body, 0)
        issued[0] = jnp.maximum(issued[0], n)

    @pl.when(i == 0)
    def _():
        issued[0] = 0

    @pl.when(i < nused_ref[0])
    def _():
        g = gidx_ref[i]

        @pl.when(first_ref[i] == 1)
        def _():
            issue_until(WEIGHT_CHUNKS * (g + 1))
            for c in range(WEIGHT_CHUNKS):
                for cp in set_copies(WEIGHT_CHUNKS * g + c):
                    cp.wait()
            s = g % 2
            wg_b[...] = wg_f[s].astype(BF16)
            wu_b[...] = wu_f[s].astype(BF16)
            wd_b[...] = wd_f[s].astype(BF16)

        cap = WEIGHT_CHUNKS * jnp.minimum(g + 3, ngroups_ref[0])
        issue_until(jnp.minimum(issued[0] + WEIGHT_SETS_PER_STEP, cap))

        lo, hi = _unpack_halves(x_ref[...])
        x = jnp.concatenate([lo.astype(BF16), hi.astype(BF16)], axis=1)
        g = jnp.dot(x, wg_b[...], preferred_element_type=F32)
        u = jnp.dot(x, wu_b[...], preferred_element_type=F32)
        a = (_silu(g) * u).astype(BF16)
        y_ref[...] = _pack_halves(jnp.dot(a, wd_b[...], preferred_element_type=F32))


def _experts(xg, first, gidx, used_list, n_used, n_groups, w_gate, w_up, w_down):
    rows = xg.shape[0]
    D = D_MODEL
    bm = EXPERT_BLOCK
    F = D_EXPERT
    nb = rows // bm
    row_map = lambda i, fi, gi, ul, nu, ng: (jnp.minimum(i, nu[0] - 1), 0)
    return pl.pallas_call(
        _experts_kernel,
        grid_spec=pltpu.PrefetchScalarGridSpec(
            num_scalar_prefetch=5,
            grid=(nb,),
            in_specs=[
                pl.BlockSpec((bm, D // 2), row_map),
                pl.BlockSpec(memory_space=pl.ANY),
                pl.BlockSpec(memory_space=pl.ANY),
                pl.BlockSpec(memory_space=pl.ANY),
            ],
            out_specs=pl.BlockSpec((bm, D // 2), row_map),
            scratch_shapes=[
                pltpu.VMEM((2, D, F), F32), pltpu.VMEM((2, D, F), F32), pltpu.VMEM((2, F, D), F32),
                pltpu.VMEM((D, F), BF16), pltpu.VMEM((D, F), BF16), pltpu.VMEM((F, D), BF16),
                pltpu.SMEM((1,), I32),
                pltpu.SemaphoreType.DMA((2, 3, WEIGHT_CHUNKS)),
            ],
        ),
        out_shape=jax.ShapeDtypeStruct((rows, D // 2), U32),
        compiler_params=_cparams(("arbitrary",)),
        name="experts",
    )(first, gidx, used_list, n_used, n_groups, xg, w_gate, w_up, w_down)


def _combine_kernel(dest_ref, wt_ref, x1_ref, h_ref, mod_ref, sg_ref, su_ref, sd_ref, yg_ref, o_ref,
                    dest_s, buf, sem_s, sem):
    tc = TM_COMBINE
    cp = pltpu.make_async_copy(dest_ref, dest_s, sem_s)
    cp.start()
    cp.wait()

    def row_copy(t, k):
        return pltpu.make_async_copy(yg_ref.at[pl.ds(dest_s[k, t], 1), :], buf.at[k, pl.ds(t, 1), :], sem)

    def start_rows(t8, c):
        base = pl.multiple_of(t8 * 8, 8)
        for j in range(8):
            for k in range(TOP_K):
                row_copy(base + j, k).start()
        return c

    def wait_rows(t, c):
        for k in range(TOP_K):
            row_copy(t, k).wait()
        return c

    lax.fori_loop(0, tc // 8, start_rows, 0)

    hb = h_ref[...]
    g = jnp.dot(hb, sg_ref[...], preferred_element_type=F32)
    u = jnp.dot(hb, su_ref[...], preferred_element_type=F32)
    ffn = jnp.dot((_silu(g) * u).astype(BF16), sd_ref[...], preferred_element_type=F32)

    lax.fori_loop(0, tc, wait_rows, 0)
    half = D_MODEL // 2
    ffn_lo, ffn_hi = ffn[:, :half], ffn[:, half:]
    for k in range(TOP_K):
        lo, hi = _unpack_halves(buf[k])
        w = wt_ref[:, k:k + 1]
        ffn_lo += lo * w
        ffn_hi += hi * w
    o_ref[:, :half] = x1_ref[:, :half] + mod_ref[0, 5:6, :half] * ffn_lo
    o_ref[:, half:] = x1_ref[:, half:] + mod_ref[0, 5:6, half:] * ffn_hi


def _combine(dest, wt_t, x1, h2, mod3, sg_b, su_b, sd_b, yg):
    T, D = x1.shape
    tc = TM_COMBINE
    per_batch = SEQ // tc
    F = D_EXPERT
    return pl.pallas_call(
        _combine_kernel,
        grid=(T // tc,),
        in_specs=[
            pl.BlockSpec((TOP_K, tc), lambda i: (0, i)),
            pl.BlockSpec((tc, TOP_K), lambda i: (i, 0)),
            pl.BlockSpec((tc, D), lambda i: (i, 0)),
            pl.BlockSpec((tc, D), lambda i: (i, 0)),
            pl.BlockSpec((1, 6, D), lambda i: (i // per_batch, 0, 0)),
            pl.BlockSpec((D, F), lambda i: (0, 0)),
            pl.BlockSpec((D, F), lambda i: (0, 0)),
            pl.BlockSpec((F, D), lambda i: (0, 0)),
            pl.BlockSpec(memory_space=pl.ANY),
        ],
        out_specs=pl.BlockSpec((tc, D), lambda i: (i, 0)),
        out_shape=jax.ShapeDtypeStruct((T, D), F32),
        scratch_shapes=[
            pltpu.SMEM((TOP_K, tc), I32),
            pltpu.VMEM((TOP_K, tc, D // 2), U32),
            pltpu.SemaphoreType.DMA,
            pltpu.SemaphoreType.DMA,
        ],
        compiler_params=_cparams(("arbitrary",)),
        name="combine",
    )(dest, wt_t, x1, h2, mod3, sg_b, su_b, sd_b, yg)


def kernel(x, c, ada_w, ada_b, mix_norm_g, ffn_norm_g, w_in, q_norm_g, k_norm_g, rel_bias, sgu_ln_g, sgu_ln_b, sgu_w, sgu_b, w_out, router_w, router_bias, shared_w_gate, shared_w_up, shared_w_down, expert_w_gate, expert_w_up, expert_w_down):
    B, S, D = x.shape
    assert S == SEQ and D == D_MODEL and ada_w.shape[0] == 1
    T = B * S
    x2 = x.reshape(T, D)

    mod3 = _modulation(c, ada_w[0], ada_b[0]).reshape(B, 6, D)

    proj = _in_projection(x2, mod3, mix_norm_g, w_in[0].astype(BF16), q_norm_g, k_norm_g)
    attn = _attention(proj.reshape(B, S, D_IN_PROJ), _bias_tables(rel_bias)).reshape(T, D_ATTN)
    causal = jnp.tril(jnp.ones((SGU_CHUNK, SGU_CHUNK), F32))
    sgu = _spatial_gating(proj, (sgu_w[0] * causal).astype(BF16), sgu_ln_g, sgu_ln_b, sgu_b[0].T)
    x1, h2, h2p = _out_projection(attn, sgu, x2, mod3, ffn_norm_g, w_out[0].astype(BF16))

    idx, wts, rank, counts = _route(h2, router_w[0].T.astype(BF16), router_bias[0])

    bm = EXPERT_BLOCK
    counts = counts.reshape(N_EXPERTS)
    padded = (counts + bm - 1) // bm * bm
    pends = jnp.cumsum(padded)
    pstarts = pends - padded
    n_blocks = T * TOP_K // bm + N_EXPERTS
    block_start = jnp.arange(n_blocks, dtype=I32) * bm
    block_exp = jnp.minimum(jnp.sum((pends[None, :] <= block_start[:, None]).astype(I32), axis=1), N_EXPERTS - 1)
    n_used = (pends[-1] // bm).astype(I32).reshape(1)
    eids = jnp.arange(N_EXPERTS, dtype=I32)
    dest = jnp.sum(jnp.where(idx[:, :, None] == eids, pstarts.astype(I32), 0), axis=-1) + rank
    pend0 = jnp.concatenate([jnp.zeros((1,), I32), pends.astype(I32)])
    first = jnp.concatenate([jnp.ones((1,), I32), (block_exp[1:] != block_exp[:-1]).astype(I32)])
    used = counts > 0
    ordinal = jnp.cumsum(used.astype(I32)) - 1
    n_groups = jnp.sum(used.astype(I32)).reshape(1)
    used_list = jnp.sum(jnp.where(used[None, :] & (ordinal[None, :] == eids[:, None]), eids[None, :], 0), axis=-1)
    gidx = jnp.sum(jnp.where(block_exp[:, None] == eids, ordinal, 0), axis=-1)

    xg = _dispatch(h2p, dest, pend0, n_blocks * bm)
    yg = _experts(xg, first, gidx.astype(I32), used_list.astype(I32), n_used, n_groups,
                  expert_w_gate[0], expert_w_up[0], expert_w_down[0])
    out = _combine(dest, wts.T, x1, h2, mod3, shared_w_gate[0].astype(BF16), shared_w_up[0].astype(BF16),
                   shared_w_down[0].astype(BF16), yg)
    return out.reshape(B, S, D)
```

```python
import functools
import math

import numpy as np
import jax
import jax.numpy as jnp
from jax import lax
from jax.experimental import pallas as pl
from jax.experimental.pallas import tpu as pltpu

F32 = jnp.float32
BF16 = jnp.bfloat16
I32 = jnp.int32
U32 = jnp.uint32

D_MODEL = 2048
SEQ = 2048
HEAD_DIM = 128
N_HEADS = 8
D_ATTN = N_HEADS * HEAD_DIM
D_SGU = 1024
N_SGU_GROUPS = 8
SGU_CHUNK = 128
D_IN_PROJ = 3 * D_ATTN + 2 * D_SGU
DILATED_GROUPS = ((128, 1), (512, 4), (2048, 16))
BAND = 128
NUM_REL_BUCKETS = 32
REL_MAX_DISTANCE = 2048
N_EXPERTS = 256
TOP_K = 8
N_EXPERT_GROUPS = 8
GROUP_SIZE = N_EXPERTS // N_EXPERT_GROUPS
TOPK_GROUPS = 4
D_EXPERT = 512
ROUTED_SCALE = 2.5
NORM_EPS = 1e-6
MASK_VALUE = -1e30

LANES = 128
VMEM_LIMIT = 56 * 1024 * 1024

TM_PROJ = 512
TM_SGU = 512
TM_OUT = 256
TM_ROUTE = 512
TM_DISPATCH = 256
TM_COMBINE = 128
EXPERT_BLOCK = 128


def _cparams(sem):
    return pltpu.CompilerParams(dimension_semantics=sem, vmem_limit_bytes=VMEM_LIMIT)


def _silu(v):
    return v * jax.nn.sigmoid(v)


def _gelu(v):
    return 0.5 * v * (1.0 + lax.erf(v * (1.0 / math.sqrt(2.0))))


def _pack_halves(v):
    h = v.shape[1] // 2
    lo = lax.bitcast_convert_type(v[:, :h].astype(BF16).astype(F32), U32)
    hi = lax.bitcast_convert_type(v[:, h:].astype(BF16).astype(F32), U32)
    return (hi & jnp.uint32(0xFFFF0000)) | (lo >> 16)


def _unpack_halves(w):
    lo = lax.bitcast_convert_type(w << 16, F32)
    hi = lax.bitcast_convert_type(w & jnp.uint32(0xFFFF0000), F32)
    return lo, hi


def _mod_kernel(c_ref, w_ref, b_ref, o_ref):
    ca = _silu(c_ref[...]).astype(BF16)
    o_ref[...] = jnp.dot(ca, w_ref[...].astype(BF16), preferred_element_type=F32) + b_ref[...]


def _modulation(c, ada_w, ada_b):
    B, D = c.shape
    N = ada_w.shape[1]
    tn = 1024
    return pl.pallas_call(
        _mod_kernel,
        grid=(N // tn,),
        in_specs=[
            pl.BlockSpec((B, D), lambda j: (0, 0)),
            pl.BlockSpec((D, tn), lambda j: (0, j)),
            pl.BlockSpec((1, tn), lambda j: (0, j)),
        ],
        out_specs=pl.BlockSpec((B, tn), lambda j: (0, j)),
        out_shape=jax.ShapeDtypeStruct((B, N), F32),
        compiler_params=_cparams(("arbitrary",)),
        name="modulation",
    )(c, ada_w, ada_b.reshape(1, N))


def _inproj_kernel(x_ref, mod_ref, g_ref, w_ref, qg_ref, kg_ref, o_ref, h_ref):
    j = pl.program_id(1)

    @pl.when(j == 0)
    def _():
        x = x_ref[...]
        r = lax.rsqrt(jnp.mean(x * x, axis=-1, keepdims=True) + NORM_EPS)
        h = (x * r) * g_ref[...] * (1.0 + mod_ref[0, 1:2, :]) + mod_ref[0, 0:1, :]
        h_ref[...] = h.astype(BF16)

    acc = jnp.dot(h_ref[...], w_ref[...], preferred_element_type=F32)

    def head_norm(gain_ref, scale):
        for hd in range(N_HEADS):
            a = acc[:, hd * HEAD_DIM:(hd + 1) * HEAD_DIM]
            r = lax.rsqrt(jnp.mean(a * a, axis=-1, keepdims=True) + NORM_EPS)
            o_ref[:, hd * HEAD_DIM:(hd + 1) * HEAD_DIM] = ((a * r) * gain_ref[...] * scale).astype(BF16)

    @pl.when(j == 0)
    def _():
        head_norm(qg_ref, HEAD_DIM ** -0.5)

    @pl.when(j == 1)
    def _():
        head_norm(kg_ref, 1.0)

    @pl.when(j >= 2)
    def _():
        o_ref[...] = acc.astype(BF16)


def _in_projection(x2, mod3, norm_g, w_in_b, q_g, k_g):
    T, D = x2.shape
    tm, tn = TM_PROJ, D_ATTN
    per_batch = SEQ // tm
    return pl.pallas_call(
        _inproj_kernel,
        grid=(T // tm, D_IN_PROJ // tn),
        in_specs=[
            pl.BlockSpec((tm, D), lambda i, j: (i, 0)),
            pl.BlockSpec((1, 6, D), lambda i, j: (i // per_batch, 0, 0)),
            pl.BlockSpec((1, D), lambda i, j: (0, 0)),
            pl.BlockSpec((D, tn), lambda i, j: (0, j)),
            pl.BlockSpec((1, HEAD_DIM), lambda i, j: (0, 0)),
            pl.BlockSpec((1, HEAD_DIM), lambda i, j: (0, 0)),
        ],
        out_specs=pl.BlockSpec((tm, tn), lambda i, j: (i, j)),
        out_shape=jax.ShapeDtypeStruct((T, D_IN_PROJ), BF16),
        scratch_shapes=[pltpu.VMEM((tm, D), BF16)],
        compiler_params=_cparams(("arbitrary", "arbitrary")),
        name="in_projection",
    )(x2, mod3, norm_g, w_in_b, q_g, k_g)


def _bias_tables(rel_bias):
    n = BAND
    qi = jnp.arange(n)[:, None]
    ki = jnp.arange(2 * n)[None, :]
    steps = n + qi - ki
    in_band = (steps >= 0) & (steps <= n)
    max_exact = NUM_REL_BUCKETS // 2
    tabs = []
    for _, dilation in DILATED_GROUPS:
        dist = jnp.clip(steps, 0, n) * dilation
        nf = jnp.maximum(dist, 1).astype(F32)
        large = max_exact + (jnp.log(nf / max_exact) / math.log(REL_MAX_DISTANCE / max_exact)
                             * (NUM_REL_BUCKETS - max_exact)).astype(I32)
        large = jnp.minimum(large, NUM_REL_BUCKETS - 1)
        bucket = jnp.where(dist < max_exact, dist, large)
        onehot = jax.nn.one_hot(bucket, NUM_REL_BUCKETS, dtype=F32)
        b = jnp.einsum("qkb,bh->hqk", onehot, rel_bias.astype(F32), precision=lax.Precision.HIGHEST)
        tabs.append(jnp.where(in_band[None], b, MASK_VALUE))
    return jnp.stack(tabs)


def _attn_kernel(q_ref, k_ref, v_ref, bias_ref, o_ref, qf, kf, vf, q4, k4, v4,
                 o0, l0, o1, l1, o2, l2, stage):
    n = BAND
    quarter = SEQ // 4

    for src, nat, res in ((q_ref, qf, q4), (k_ref, kf, k4), (v_ref, vf, v4)):
        nat[...] = src[...].astype(F32)
        for r in range(4):
            res[r * quarter:(r + 1) * quarter, :] = nat[pl.ds(r, quarter, stride=4), :]

    def piece(qb, kb, vb, bias, o_out, l_out, rows):
        l = lax.dot_general(qb, kb, (((1,), (1,)), ((), ())), preferred_element_type=F32) + bias
        m = jnp.max(l, axis=-1, keepdims=True)
        p = jnp.exp(l - m)
        s = jnp.sum(p, axis=-1, keepdims=True)
        o = jnp.dot(p.astype(BF16), vb, preferred_element_type=F32)
        o_out[rows, :] = o / s
        l_out[rows, :] = jnp.broadcast_to(m + jnp.log(s), (n, LANES))

    for i in range(SEQ // n):
        rows = slice(i * n, (i + 1) * n)
        if i == 0:
            piece(q_ref[rows, :], k_ref[rows, :], v_ref[rows, :], bias_ref[0, :, n:], o0, l0, rows)
        else:
            krows = slice((i - 1) * n, (i + 1) * n)
            piece(q_ref[rows, :], k_ref[krows, :], v_ref[krows, :], bias_ref[0], o0, l0, rows)

    for r in range(4):
        for blk in range(quarter // n):
            base = r * quarter + blk * n
            rows = slice(base, base + n)
            krows = rows if blk == 0 else slice(base - n, base + n)
            bias = bias_ref[1, :, n:] if blk == 0 else bias_ref[1]
            piece(q4[rows, :].astype(BF16), k4[krows, :].astype(BF16), v4[krows, :].astype(BF16),
                  bias, o1, l1, rows)

    for r in range(4):
        for a in range(4):
            rows = pl.ds(r * quarter + a, n, stride=4)
            piece(q4[rows, :].astype(BF16), k4[rows, :].astype(BF16), v4[rows, :].astype(BF16),
                  bias_ref[2, :, n:], o2, l2, rows)

    for r in range(4):
        for c in range(quarter // n):
            rows = slice(r * quarter + c * n, r * quarter + (c + 1) * n)
            nat = pl.ds(r + 4 * c * n, n, stride=4)
            a0, a1, a2 = l0[nat, :], l1[rows, :], l2[rows, :]
            m = jnp.maximum(jnp.maximum(a0, a1), a2)
            e0, e1, e2 = jnp.exp(a0 - m), jnp.exp(a1 - m), jnp.exp(a2 - m)
            mix = (e0 * o0[nat, :] + e1 * o1[rows, :] + e2 * o2[rows, :]) / (e0 + e1 + e2)
            stage[nat, :] = mix
    o_ref[...] = stage[...].astype(BF16)


def _attention(proj3, bias_tabs):
    B = proj3.shape[0]
    blk = lambda off: pl.BlockSpec((None, SEQ, HEAD_DIM), lambda b, h: (b, 0, off + h))
    return pl.pallas_call(
        _attn_kernel,
        grid=(B, N_HEADS),
        in_specs=[
            blk(0), blk(N_HEADS), blk(2 * N_HEADS),
            pl.BlockSpec((3, None, BAND, 2 * BAND), lambda b, h: (0, h, 0, 0)),
        ],
        out_specs=pl.BlockSpec((None, SEQ, HEAD_DIM), lambda b, h: (b, 0, h)),
        out_shape=jax.ShapeDtypeStruct((B, SEQ, D_ATTN), BF16),
        scratch_shapes=[pltpu.VMEM((SEQ, HEAD_DIM), F32) for _ in range(13)],
        compiler_params=_cparams(("arbitrary", "arbitrary")),
        name="dilated_attention",
    )(proj3, proj3, proj3, bias_tabs)


def _sgu_kernel(u_ref, z_ref, w_ref, g_ref, b_ref, bs_ref, o_ref):
    n = SGU_CHUNK
    for c in range(TM_SGU // n):
        rs = slice(c * n, (c + 1) * n)
        for g in range(N_SGU_GROUPS):
            cs = slice(g * n, (g + 1) * n)
            z = _gelu(z_ref[rs, cs].astype(F32))
            mu = jnp.mean(z, axis=-1, keepdims=True)
            zc = z - mu
            var = jnp.mean(zc * zc, axis=-1, keepdims=True)
            zn = (zc * lax.rsqrt(var + NORM_EPS)) * g_ref[:, cs] + b_ref[:, cs]
            mixed = jnp.dot(w_ref[g], zn.astype(BF16), preferred_element_type=F32) + bs_ref[:, g:g + 1]
            u = _gelu(u_ref[rs, cs].astype(F32))
            o_ref[rs, cs] = (u * mixed).astype(BF16)


def _spatial_gating(proj, w_causal_b, ln_g, ln_b, bs_t):
    T = proj.shape[0]
    tm = TM_SGU
    ucol = 3 * D_ATTN // D_SGU
    return pl.pallas_call(
        _sgu_kernel,
        grid=(T // tm,),
        in_specs=[
            pl.BlockSpec((tm, D_SGU), lambda i: (i, ucol)),
            pl.BlockSpec((tm, D_SGU), lambda i: (i, ucol + 1)),
            pl.BlockSpec((N_SGU_GROUPS, SGU_CHUNK, SGU_CHUNK), lambda i: (0, 0, 0)),
            pl.BlockSpec((1, D_SGU), lambda i: (0, 0)),
            pl.BlockSpec((1, D_SGU), lambda i: (0, 0)),
            pl.BlockSpec((SGU_CHUNK, N_SGU_GROUPS), lambda i: (0, 0)),
        ],
        out_specs=pl.BlockSpec((tm, D_SGU), lambda i: (i, 0)),
        out_shape=jax.ShapeDtypeStruct((T, D_SGU), BF16),
        compiler_params=_cparams(("arbitrary",)),
        name="spatial_gating",
    )(proj, proj, w_causal_b, ln_g, ln_b, bs_t)


def _outproj_kernel(a_ref, s_ref, x_ref, mod_ref, g_ref, w_ref, x1_ref, h2_ref, h2p_ref):
    mixed = jnp.dot(a_ref[...], w_ref[0:D_ATTN, :], preferred_element_type=F32)
    mixed += jnp.dot(s_ref[...], w_ref[D_ATTN:, :], preferred_element_type=F32)
    x1 = x_ref[...] + mod_ref[0, 2:3, :] * mixed
    x1_ref[...] = x1
    r = lax.rsqrt(jnp.mean(x1 * x1, axis=-1, keepdims=True) + NORM_EPS)
    h2 = (x1 * r) * g_ref[...] * (1.0 + mod_ref[0, 4:5, :]) + mod_ref[0, 3:4, :]
    h2_ref[...] = h2.astype(BF16)
    h2p_ref[...] = _pack_halves(h2)


def _out_projection(attn, sgu, x2, mod3, norm_g, w_out_b):
    T, D = x2.shape
    tm = TM_OUT
    per_batch = SEQ // tm
    return pl.pallas_call(
        _outproj_kernel,
        grid=(T // tm,),
        in_specs=[
            pl.BlockSpec((tm, D_ATTN), lambda i: (i, 0)),
            pl.BlockSpec((tm, D_SGU), lambda i: (i, 0)),
            pl.BlockSpec((tm, D), lambda i: (i, 0)),
            pl.BlockSpec((1, 6, D), lambda i: (i // per_batch, 0, 0)),
            pl.BlockSpec((1, D), lambda i: (0, 0)),
            pl.BlockSpec((D_ATTN + D_SGU, D), lambda i: (0, 0)),
        ],
        out_specs=[pl.BlockSpec((tm, D), lambda i: (i, 0)), pl.BlockSpec((tm, D), lambda i: (i, 0)),
                   pl.BlockSpec((tm, D // 2), lambda i: (i, 0))],
        out_shape=[jax.ShapeDtypeStruct((T, D), F32), jax.ShapeDtypeStruct((T, D), BF16),
                   jax.ShapeDtypeStruct((T, D // 2), U32)],
        compiler_params=_cparams(("arbitrary",)),
        name="out_projection",
    )(attn, sgu, x2, mod3, norm_g, w_out_b)


def _router_kernel(h_ref, w_ref, b_ref, idx_ref, wt_ref, rank_ref, cnt_ref, run_ref):
    i = pl.program_id(0)
    tr = TM_ROUTE
    E = N_EXPERTS

    @pl.when(i == 0)
    def _():
        run_ref[...] = jnp.zeros_like(run_ref)

    logits = lax.dot_general(w_ref[...], h_ref[...], (((1,), (1,)), ((), ())), preferred_element_type=F32)
    scores = jax.nn.sigmoid(logits)
    sel = scores + b_ref[...]

    slabs = [sel[g * GROUP_SIZE:(g + 1) * GROUP_SIZE, :] for g in range(N_EXPERT_GROUPS)]
    si = lax.broadcasted_iota(I32, (GROUP_SIZE, tr), 0).astype(F32)
    gs = []
    for slab in slabs:
        m1 = jnp.max(slab, axis=0, keepdims=True)
        first = jnp.min(jnp.where(slab == m1, si, float(GROUP_SIZE)), axis=0, keepdims=True)
        m2 = jnp.max(jnp.where(si == first, -jnp.inf, slab), axis=0, keepdims=True)
        gs.append(m1 + m2)

    kept = []
    for g in range(N_EXPERT_GROUPS):
        beaten = jnp.zeros((1, tr), F32)
        for o in range(N_EXPERT_GROUPS):
            if o < g:
                beaten += (gs[o] >= gs[g]).astype(F32)
            elif o > g:
                beaten += (gs[o] > gs[g]).astype(F32)
        kept.append(jnp.where(beaten < TOPK_GROUPS, slabs[g], -jnp.inf))
    masked = jnp.concatenate(kept, axis=0)

    ei = lax.broadcasted_iota(I32, (E, tr), 0).astype(F32)
    picks, pick_scores = [], []
    onehot_sum = jnp.zeros((E, tr), F32)
    for k in range(TOP_K):
        m = jnp.max(masked, axis=0, keepdims=True)
        ik = jnp.min(jnp.where(masked == m, ei, float(E)), axis=0, keepdims=True)
        oh = ei == ik
        pick_scores.append(jnp.sum(jnp.where(oh, scores, 0.0), axis=0, keepdims=True))
        masked = jnp.where(oh, -jnp.inf, masked)
        onehot_sum += oh.astype(F32)
        picks.append(ik)

    ti = lax.broadcasted_iota(I32, (tr, tr), 0)
    tj = lax.broadcasted_iota(I32, (tr, tr), 1)
    before = (ti < tj).astype(BF16)
    prior = jnp.dot(onehot_sum.astype(BF16), before, preferred_element_type=F32) + run_ref[...]

    total = pick_scores[0]
    for k in range(1, TOP_K):
        total += pick_scores[k]
    for k in range(TOP_K):
        idx_ref[k:k + 1, :] = picks[k].astype(I32)
        wt_ref[k:k + 1, :] = pick_scores[k] / total * ROUTED_SCALE
        rk = jnp.sum(jnp.where(ei == picks[k], prior, 0.0), axis=0, keepdims=True)
        rank_ref[k:k + 1, :] = rk.astype(I32)

    run_ref[...] += jnp.sum(onehot_sum, axis=1, keepdims=True)
    cnt_ref[...] = run_ref[...].astype(I32)


def _route(h2, router_wt_b, router_bias):
    T, D = h2.shape
    tr = TM_ROUTE
    E = N_EXPERTS
    row_spec = pl.BlockSpec((TOP_K, tr), lambda i: (0, i))
    return pl.pallas_call(
        _router_kernel,
        grid=(T // tr,),
        in_specs=[
            pl.BlockSpec((tr, D), lambda i: (i, 0)),
            pl.BlockSpec((E, D), lambda i: (0, 0)),
            pl.BlockSpec((E, 1), lambda i: (0, 0)),
        ],
        out_specs=[row_spec, row_spec, row_spec, pl.BlockSpec((E, 1), lambda i: (0, 0))],
        out_shape=[
            jax.ShapeDtypeStruct((TOP_K, T), I32),
            jax.ShapeDtypeStruct((TOP_K, T), F32),
            jax.ShapeDtypeStruct((TOP_K, T), I32),
            jax.ShapeDtypeStruct((E, 1), I32),
        ],
        scratch_shapes=[pltpu.VMEM((E, 1), F32)],
        compiler_params=_cparams(("arbitrary",)),
        name="router",
    )(h2, router_wt_b, router_bias.reshape(E, 1))


def _dispatch_kernel(pend_ref, h_ref, dest_ref, xg_ref, dest_s, zero_ref, sem_s, sem):
    i = pl.program_id(0)
    td = TM_DISPATCH
    blk = EXPERT_BLOCK

    def zero_copy(e):
        start = pl.multiple_of(pend_ref[e + 1] - blk, blk)
        return pltpu.make_async_copy(zero_ref, xg_ref.at[pl.ds(start, blk), :], sem)

    @pl.when(i == 0)
    def _():
        zero_ref[...] = jnp.zeros_like(zero_ref)

        def start_zero(e, c):
            @pl.when(pend_ref[e + 1] > pend_ref[e])
            def _():
                zero_copy(e).start()
            return c

        def wait_zero(e, c):
            @pl.when(pend_ref[e + 1] > pend_ref[e])
            def _():
                zero_copy(e).wait()
            return c

        lax.fori_loop(0, N_EXPERTS, start_zero, 0)
        lax.fori_loop(0, N_EXPERTS, wait_zero, 0)

    cp = pltpu.make_async_copy(dest_ref, dest_s, sem_s)
    cp.start()
    cp.wait()

    def row_copy(t, k):
        return pltpu.make_async_copy(h_ref.at[pl.ds(t, 1), :], xg_ref.at[pl.ds(dest_s[k, t], 1), :], sem)

    def start_rows(t8, c):
        base = pl.multiple_of(t8 * 8, 8)
        for j in range(8):
            for k in range(TOP_K):
                row_copy(base + j, k).start()
        return c

    lax.fori_loop(0, td // 8, start_rows, 0)
    for k in range(TOP_K):
        pltpu.make_async_copy(h_ref, h_ref, sem).wait()


def _dispatch(h2p, dest, pend0, rows):
    T, D = h2p.shape
    td = TM_DISPATCH
    return pl.pallas_call(
        _dispatch_kernel,
        grid_spec=pltpu.PrefetchScalarGridSpec(
            num_scalar_prefetch=1,
            grid=(T // td,),
            in_specs=[
                pl.BlockSpec((td, D), lambda i, p: (i, 0)),
                pl.BlockSpec((TOP_K, td), lambda i, p: (0, i)),
            ],
            out_specs=pl.BlockSpec(memory_space=pl.ANY),
            scratch_shapes=[
                pltpu.SMEM((TOP_K, td), I32),
                pltpu.VMEM((EXPERT_BLOCK, D), U32),
                pltpu.SemaphoreType.DMA,
                pltpu.SemaphoreType.DMA,
            ],
        ),
        out_shape=jax.ShapeDtypeStruct((rows, D), U32),
        compiler_params=_cparams(("arbitrary",)),
        name="dispatch",
    )(pend0, h2p, dest)


def _experts_kernel(first_ref, gidx_ref, used_ref, nused_ref, ngroups_ref,
                    x_ref, wg_hbm, wu_hbm, wd_hbm, y_ref,
                    wgu_f, wd_f, wgu_b, wd_b, issued, sem):
    i = pl.program_id(0)
    F = D_EXPERT

    def tensor_copy(e, s, t):
        if t == 0:
            return pltpu.make_async_copy(wg_hbm.at[e], wgu_f.at[s, 0], sem.at[s, 0])
        if t == 1:
            return pltpu.make_async_copy(wu_hbm.at[e], wgu_f.at[s, 1], sem.at[s, 1])
        return pltpu.make_async_copy(wd_hbm.at[e], wd_f.at[s], sem.at[s, 2])

    def issue_until(n):
        def body(q, carry):
            h = q // 3
            for t in range(3):
                @pl.when(q % 3 == t)
                def _():
                    tensor_copy(used_ref[h], h % 2, t).start()
            return carry

        lax.fori_loop(issued[0], n, body, 0)
        issued[0] = jnp.maximum(issued[0], n)

    @pl.when(i == 0)
    def _():
        issued[0] = 0

    @pl.when(i < nused_ref[0])
    def _():
        g = gidx_ref[i]

        @pl.when(first_ref[i] == 1)
        def _():
            issue_until(3 * (g + 1))
            s = g % 2
            for t in range(3):
                tensor_copy(used_ref[g], s, t).wait()
            wgu_b[:, :F] = wgu_f[s, 0].astype(BF16)
            wgu_b[:, F:] = wgu_f[s, 1].astype(BF16)
            wd_b[...] = wd_f[s].astype(BF16)

        cap = 3 * jnp.minimum(g + 3, ngroups_ref[0])
        issue_until(jnp.minimum(issued[0] + 1, cap))

        lo, hi = _unpack_halves(x_ref[...])
        x = jnp.concatenate([lo.astype(BF16), hi.astype(BF16)], axis=1)
        gu = jnp.dot(x, wgu_b[...], preferred_element_type=F32)
        a = (_silu(gu[:, :F]) * gu[:, F:]).astype(BF16)
        y_ref[...] = _pack_halves(jnp.dot(a, wd_b[...], preferred_element_type=F32))


def _experts(xg, first, gidx, used_list, n_used, n_groups, w_gate, w_up, w_down):
    rows = xg.shape[0]
    D = D_MODEL
    bm = EXPERT_BLOCK
    F = D_EXPERT
    nb = rows // bm
    row_map = lambda i, fi, gi, ul, nu, ng: (jnp.minimum(i, nu[0] - 1), 0)
    return pl.pallas_call(
        _experts_kernel,
        grid_spec=pltpu.PrefetchScalarGridSpec(
            num_scalar_prefetch=5,
            grid=(nb,),
            in_specs=[
                pl.BlockSpec((bm, D // 2), row_map),
                pl.BlockSpec(memory_space=pl.ANY),
                pl.BlockSpec(memory_space=pl.ANY),
                pl.BlockSpec(memory_space=pl.ANY),
            ],
            out_specs=pl.BlockSpec((bm, D // 2), row_map),
            scratch_shapes=[
                pltpu.VMEM((2, 2, D, F), F32), pltpu.VMEM((2, F, D), F32),
                pltpu.VMEM((D, 2 * F), BF16), pltpu.VMEM((F, D), BF16),
                pltpu.SMEM((1,), I32),
                pltpu.SemaphoreType.DMA((2, 3)),
            ],
        ),
        out_shape=jax.ShapeDtypeStruct((rows, D // 2), U32),
        compiler_params=_cparams(("arbitrary",)),
        name="experts",
    )(first, gidx, used_list, n_used, n_groups, xg, w_gate, w_up, w_down)


def _combine_kernel(dest_ref, dnext_ref, wt_ref, x1_ref, h_ref, mod_ref, sg_ref, su_ref, sd_ref, yg_ref, o_ref,
                    dest_s, buf, sem_s, sem):
    i = pl.program_id(0)
    last = pl.num_programs(0) - 1
    tc = TM_COMBINE
    cur = i % 2
    nxt = 1 - cur

    def row_copy(slot, t, k):
        return pltpu.make_async_copy(yg_ref.at[pl.ds(dest_s[slot, k, t], 1), :],
                                     buf.at[slot, k, pl.ds(t, 1), :], sem.at[slot])

    def load_dest(src_ref, slot):
        cp = pltpu.make_async_copy(src_ref, dest_s.at[slot], sem_s)
        cp.start()
        cp.wait()

    def wait_rows(slot):
        pltpu.make_async_copy(buf.at[slot], buf.at[slot], sem.at[slot]).wait()

    @pl.when(i == 0)
    def _():
        load_dest(dest_ref, 0)

        def body(t, c):
            for k in range(TOP_K):
                row_copy(0, t, k).start()
            return c

        lax.fori_loop(0, tc, body, 0)

    load_dest(dnext_ref, nxt)
    for t in range(tc):
        for k in range(TOP_K):
            row_copy(nxt, t, k).start()

    hb = h_ref[...]
    g = jnp.dot(hb, sg_ref[...], preferred_element_type=F32)
    u = jnp.dot(hb, su_ref[...], preferred_element_type=F32)
    ffn = jnp.dot((_silu(g) * u).astype(BF16), sd_ref[...], preferred_element_type=F32)

    wait_rows(cur)
    half = D_MODEL // 2
    ffn_lo, ffn_hi = ffn[:, :half], ffn[:, half:]
    for k in range(TOP_K):
        lo, hi = _unpack_halves(buf[cur, k])
        w = wt_ref[:, k:k + 1]
        ffn_lo += lo * w
        ffn_hi += hi * w
    o_ref[:, :half] = x1_ref[:, :half] + mod_ref[0, 5:6, :half] * ffn_lo
    o_ref[:, half:] = x1_ref[:, half:] + mod_ref[0, 5:6, half:] * ffn_hi

    @pl.when(i == last)
    def _():
        wait_rows(nxt)


def _combine(dest, wt_t, x1, h2, mod3, sg_b, su_b, sd_b, yg):
    T, D = x1.shape
    tc = TM_COMBINE
    per_batch = SEQ // tc
    F = D_EXPERT
    return pl.pallas_call(
        _combine_kernel,
        grid=(T // tc,),
        in_specs=[
            pl.BlockSpec((TOP_K, tc), lambda i: (0, i)),
            pl.BlockSpec((TOP_K, tc), lambda i: (0, jnp.minimum(i + 1, T // tc - 1))),
            pl.BlockSpec((tc, TOP_K), lambda i: (i, 0)),
            pl.BlockSpec((tc, D), lambda i: (i, 0)),
            pl.BlockSpec((tc, D), lambda i: (i, 0)),
            pl.BlockSpec((1, 6, D), lambda i: (i // per_batch, 0, 0)),
            pl.BlockSpec((D, F), lambda i: (0, 0)),
            pl.BlockSpec((D, F), lambda i: (0, 0)),
            pl.BlockSpec((F, D), lambda i: (0, 0)),
            pl.BlockSpec(memory_space=pl.ANY),
        ],
        out_specs=pl.BlockSpec((tc, D), lambda i: (i, 0)),
        out_shape=jax.ShapeDtypeStruct((T, D), F32),
        scratch_shapes=[
            pltpu.SMEM((2, TOP_K, tc), I32),
            pltpu.VMEM((2, TOP_K, tc, D // 2), U32),
            pltpu.SemaphoreType.DMA,
            pltpu.SemaphoreType.DMA((2,)),
        ],
        compiler_params=_cparams(("arbitrary",)),
        name="combine",
    )(dest, dest, wt_t, x1, h2, mod3, sg_b, su_b, sd_b, yg)


def kernel(x, c, ada_w, ada_b, mix_norm_g, ffn_norm_g, w_in, q_norm_g, k_norm_g, rel_bias, sgu_ln_g, sgu_ln_b, sgu_w, sgu_b, w_out, router_w, router_bias, shared_w_gate, shared_w_up, shared_w_down, expert_w_gate, expert_w_up, expert_w_down):
    B, S, D = x.shape
    assert S == SEQ and D == D_MODEL and ada_w.shape[0] == 1
    T = B * S
    x2 = x.reshape(T, D)

    mod3 = _modulation(c, ada_w[0], ada_b[0]).reshape(B, 6, D)

    proj = _in_projection(x2, mod3, mix_norm_g, w_in[0].astype(BF16), q_norm_g, k_norm_g)
    attn = _attention(proj.reshape(B, S, D_IN_PROJ), _bias_tables(rel_bias)).reshape(T, D_ATTN)
    causal = jnp.tril(jnp.ones((SGU_CHUNK, SGU_CHUNK), F32))
    sgu = _spatial_gating(proj, (sgu_w[0] * causal).astype(BF16), sgu_ln_g, sgu_ln_b, sgu_b[0].T)
    x1, h2, h2p = _out_projection(attn, sgu, x2, mod3, ffn_norm_g, w_out[0].astype(BF16))

    idx, wts, rank, counts = _route(h2, router_w[0].T.astype(BF16), router_bias[0])

    bm = EXPERT_BLOCK
    counts = counts.reshape(N_EXPERTS)
    padded = (counts + bm - 1) // bm * bm
    pends = jnp.cumsum(padded)
    pstarts = pends - padded
    n_blocks = T * TOP_K // bm + N_EXPERTS
    block_start = jnp.arange(n_blocks, dtype=I32) * bm
    block_exp = jnp.minimum(jnp.sum((pends[None, :] <= block_start[:, None]).astype(I32), axis=1), N_EXPERTS - 1)
    n_used = (pends[-1] // bm).astype(I32).reshape(1)
    eids = jnp.arange(N_EXPERTS, dtype=I32)
    dest = jnp.sum(jnp.where(idx[:, :, None] == eids, pstarts.astype(I32), 0), axis=-1) + rank
    pend0 = jnp.concatenate([jnp.zeros((1,), I32), pends.astype(I32)])
    first = jnp.concatenate([jnp.ones((1,), I32), (block_exp[1:] != block_exp[:-1]).astype(I32)])
    used = counts > 0
    ordinal = jnp.cumsum(used.astype(I32)) - 1
    n_groups = jnp.sum(used.astype(I32)).reshape(1)
    used_list = jnp.sum(jnp.where(used[None, :] & (ordinal[None, :] == eids[:, None]), eids[None, :], 0), axis=-1)
    gidx = jnp.sum(jnp.where(block_exp[:, None] == eids, ordinal, 0), axis=-1)

    xg = _dispatch(h2p, dest, pend0, n_blocks * bm)
    yg = _experts(xg, first, gidx.astype(I32), used_list.astype(I32), n_used, n_groups,
                  expert_w_gate[0], expert_w_up[0], expert_w_down[0])
    out = _combine(dest, wts.T, x1, h2, mod3, shared_w_gate[0].astype(BF16), shared_w_up[0].astype(BF16),
                   shared_w_down[0].astype(BF16), yg)
    return out.reshape(B, S, D)
```

```python
import functools
import math

import numpy as np
import jax
import jax.numpy as jnp
from jax import lax
from jax.experimental import pallas as pl
from jax.experimental.pallas import tpu as pltpu

F32 = jnp.float32
BF16 = jnp.bfloat16
I32 = jnp.int32
U32 = jnp.uint32

D_MODEL = 2048
SEQ = 2048
HEAD_DIM = 128
N_HEADS = 8
D_ATTN = N_HEADS * HEAD_DIM
D_SGU = 1024
N_SGU_GROUPS = 8
SGU_CHUNK = 128
D_IN_PROJ = 3 * D_ATTN + 2 * D_SGU
DILATED_GROUPS = ((128, 1), (512, 4), (2048, 16))
BAND = 128
NUM_REL_BUCKETS = 32
REL_MAX_DISTANCE = 2048
N_EXPERTS = 256
TOP_K = 8
N_EXPERT_GROUPS = 8
GROUP_SIZE = N_EXPERTS // N_EXPERT_GROUPS
TOPK_GROUPS = 4
D_EXPERT = 512
ROUTED_SCALE = 2.5
NORM_EPS = 1e-6
MASK_VALUE = -1e30

LANES = 128
VMEM_LIMIT = 56 * 1024 * 1024
DMA_QUEUES = 2

TM_PROJ = 512
TM_SGU = 512
TM_OUT = 256
TM_ROUTE = 512
TM_DISPATCH = 256
TM_COMBINE = 128
EXPERT_BLOCK = 128


def _cparams(sem):
    return pltpu.CompilerParams(dimension_semantics=sem, vmem_limit_bytes=VMEM_LIMIT)


def _silu(v):
    return v * jax.nn.sigmoid(v)


def _gelu(v):
    return 0.5 * v * (1.0 + lax.erf(v * (1.0 / math.sqrt(2.0))))


def _pack_halves(v):
    h = v.shape[1] // 2
    lo = lax.bitcast_convert_type(v[:, :h].astype(BF16).astype(F32), U32)
    hi = lax.bitcast_convert_type(v[:, h:].astype(BF16).astype(F32), U32)
    return (hi & jnp.uint32(0xFFFF0000)) | (lo >> 16)


def _unpack_halves(w):
    lo = lax.bitcast_convert_type(w << 16, F32)
    hi = lax.bitcast_convert_type(w & jnp.uint32(0xFFFF0000), F32)
    return lo, hi


def _mod_kernel(c_ref, w_ref, b_ref, o_ref):
    ca = _silu(c_ref[...]).astype(BF16)
    o_ref[...] = jnp.dot(ca, w_ref[...].astype(BF16), preferred_element_type=F32) + b_ref[...]


def _modulation(c, ada_w, ada_b):
    B, D = c.shape
    N = ada_w.shape[1]
    tn = 1024
    return pl.pallas_call(
        _mod_kernel,
        grid=(N // tn,),
        in_specs=[
            pl.BlockSpec((B, D), lambda j: (0, 0)),
            pl.BlockSpec((D, tn), lambda j: (0, j)),
            pl.BlockSpec((1, tn), lambda j: (0, j)),
        ],
        out_specs=pl.BlockSpec((B, tn), lambda j: (0, j)),
        out_shape=jax.ShapeDtypeStruct((B, N), F32),
        compiler_params=_cparams(("arbitrary",)),
        name="modulation",
    )(c, ada_w, ada_b.reshape(1, N))


def _inproj_kernel(x_ref, mod_ref, g_ref, w_ref, qg_ref, kg_ref, o_ref, h_ref):
    j = pl.program_id(1)

    @pl.when(j == 0)
    def _():
        x = x_ref[...]
        r = lax.rsqrt(jnp.mean(x * x, axis=-1, keepdims=True) + NORM_EPS)
        h = (x * r) * g_ref[...] * (1.0 + mod_ref[0, 1:2, :]) + mod_ref[0, 0:1, :]
        h_ref[...] = h.astype(BF16)

    acc = jnp.dot(h_ref[...], w_ref[...], preferred_element_type=F32)

    def head_norm(gain_ref, scale):
        for hd in range(N_HEADS):
            a = acc[:, hd * HEAD_DIM:(hd + 1) * HEAD_DIM]
            r = lax.rsqrt(jnp.mean(a * a, axis=-1, keepdims=True) + NORM_EPS)
            o_ref[:, hd * HEAD_DIM:(hd + 1) * HEAD_DIM] = ((a * r) * gain_ref[...] * scale).astype(BF16)

    @pl.when(j == 0)
    def _():
        head_norm(qg_ref, HEAD_DIM ** -0.5)

    @pl.when(j == 1)
    def _():
        head_norm(kg_ref, 1.0)

    @pl.when(j >= 2)
    def _():
        o_ref[...] = acc.astype(BF16)


def _in_projection(x2, mod3, norm_g, w_in_b, q_g, k_g):
    T, D = x2.shape
    tm, tn = TM_PROJ, D_ATTN
    per_batch = SEQ // tm
    return pl.pallas_call(
        _inproj_kernel,
        grid=(T // tm, D_IN_PROJ // tn),
        in_specs=[
            pl.BlockSpec((tm, D), lambda i, j: (i, 0)),
            pl.BlockSpec((1, 6, D), lambda i, j: (i // per_batch, 0, 0)),
            pl.BlockSpec((1, D), lambda i, j: (0, 0)),
            pl.BlockSpec((D, tn), lambda i, j: (0, j)),
            pl.BlockSpec((1, HEAD_DIM), lambda i, j: (0, 0)),
            pl.BlockSpec((1, HEAD_DIM), lambda i, j: (0, 0)),
        ],
        out_specs=pl.BlockSpec((tm, tn), lambda i, j: (i, j)),
        out_shape=jax.ShapeDtypeStruct((T, D_IN_PROJ), BF16),
        scratch_shapes=[pltpu.VMEM((tm, D), BF16)],
        compiler_params=_cparams(("arbitrary", "arbitrary")),
        name="in_projection",
    )(x2, mod3, norm_g, w_in_b, q_g, k_g)


def _bias_tables(rel_bias):
    n = BAND
    qi = jnp.arange(n)[:, None]
    ki = jnp.arange(2 * n)[None, :]
    steps = n + qi - ki
    in_band = (steps >= 0) & (steps <= n)
    max_exact = NUM_REL_BUCKETS // 2
    tabs = []
    for _, dilation in DILATED_GROUPS:
        dist = jnp.clip(steps, 0, n) * dilation
        nf = jnp.maximum(dist, 1).astype(F32)
        large = max_exact + (jnp.log(nf / max_exact) / math.log(REL_MAX_DISTANCE / max_exact)
                             * (NUM_REL_BUCKETS - max_exact)).astype(I32)
        large = jnp.minimum(large, NUM_REL_BUCKETS - 1)
        bucket = jnp.where(dist < max_exact, dist, large)
        onehot = jax.nn.one_hot(bucket, NUM_REL_BUCKETS, dtype=F32)
        b = jnp.einsum("qkb,bh->hqk", onehot, rel_bias.astype(F32), precision=lax.Precision.HIGHEST)
        tabs.append(jnp.where(in_band[None], b, MASK_VALUE))
    return jnp.stack(tabs)


def _attn_kernel(q_ref, k_ref, v_ref, bias_ref, o_ref, qf, kf, vf, q4, k4, v4,
                 o0, l0, o1, l1, o2, l2, stage):
    n = BAND
    quarter = SEQ // 4

    for src, nat, res in ((q_ref, qf, q4), (k_ref, kf, k4), (v_ref, vf, v4)):
        nat[...] = src[...].astype(F32)
        for r in range(4):
            res[r * quarter:(r + 1) * quarter, :] = nat[pl.ds(r, quarter, stride=4), :]

    def piece(qb, kb, vb, bias, o_out, l_out, rows):
        l = lax.dot_general(qb, kb, (((1,), (1,)), ((), ())), preferred_element_type=F32) + bias
        m = jnp.max(l, axis=-1, keepdims=True)
        p = jnp.exp(l - m)
        s = jnp.sum(p, axis=-1, keepdims=True)
        o = jnp.dot(p.astype(BF16), vb, preferred_element_type=F32)
        o_out[rows, :] = o / s
        l_out[rows, :] = jnp.broadcast_to(m + jnp.log(s), (n, LANES))

    for i in range(SEQ // n):
        rows = slice(i * n, (i + 1) * n)
        if i == 0:
            piece(q_ref[rows, :], k_ref[rows, :], v_ref[rows, :], bias_ref[0, :, n:], o0, l0, rows)
        else:
            krows = slice((i - 1) * n, (i + 1) * n)
            piece(q_ref[rows, :], k_ref[krows, :], v_ref[krows, :], bias_ref[0], o0, l0, rows)

    for r in range(4):
        for blk in range(quarter // n):
            base = r * quarter + blk * n
            rows = slice(base, base + n)
            krows = rows if blk == 0 else slice(base - n, base + n)
            bias = bias_ref[1, :, n:] if blk == 0 else bias_ref[1]
            piece(q4[rows, :].astype(BF16), k4[krows, :].astype(BF16), v4[krows, :].astype(BF16),
                  bias, o1, l1, rows)

    for r in range(4):
        for a in range(4):
            rows = pl.ds(r * quarter + a, n, stride=4)
            piece(q4[rows, :].astype(BF16), k4[rows, :].astype(BF16), v4[rows, :].astype(BF16),
                  bias_ref[2, :, n:], o2, l2, rows)

    for r in range(4):
        for c in range(quarter // n):
            rows = slice(r * quarter + c * n, r * quarter + (c + 1) * n)
            nat = pl.ds(r + 4 * c * n, n, stride=4)
            a0, a1, a2 = l0[nat, :], l1[rows, :], l2[rows, :]
            m = jnp.maximum(jnp.maximum(a0, a1), a2)
            e0, e1, e2 = jnp.exp(a0 - m), jnp.exp(a1 - m), jnp.exp(a2 - m)
            mix = (e0 * o0[nat, :] + e1 * o1[rows, :] + e2 * o2[rows, :]) / (e0 + e1 + e2)
            stage[nat, :] = mix
    o_ref[...] = stage[...].astype(BF16)


def _attention(proj3, bias_tabs):
    B = proj3.shape[0]
    blk = lambda off: pl.BlockSpec((None, SEQ, HEAD_DIM), lambda b, h: (b, 0, off + h))
    return pl.pallas_call(
        _attn_kernel,
        grid=(B, N_HEADS),
        in_specs=[
            blk(0), blk(N_HEADS), blk(2 * N_HEADS),
            pl.BlockSpec((3, None, BAND, 2 * BAND), lambda b, h: (0, h, 0, 0)),
        ],
        out_specs=pl.BlockSpec((None, SEQ, HEAD_DIM), lambda b, h: (b, 0, h)),
        out_shape=jax.ShapeDtypeStruct((B, SEQ, D_ATTN), BF16),
        scratch_shapes=[pltpu.VMEM((SEQ, HEAD_DIM), F32) for _ in range(13)],
        compiler_params=_cparams(("arbitrary", "arbitrary")),
        name="dilated_attention",
    )(proj3, proj3, proj3, bias_tabs)


def _sgu_kernel(u_ref, z_ref, w_ref, g_ref, b_ref, bs_ref, o_ref):
    n = SGU_CHUNK
    for c in range(TM_SGU // n):
        rs = slice(c * n, (c + 1) * n)
        for g in range(N_SGU_GROUPS):
            cs = slice(g * n, (g + 1) * n)
            z = _gelu(z_ref[rs, cs].astype(F32))
            mu = jnp.mean(z, axis=-1, keepdims=True)
            zc = z - mu
            var = jnp.mean(zc * zc, axis=-1, keepdims=True)
            zn = (zc * lax.rsqrt(var + NORM_EPS)) * g_ref[:, cs] + b_ref[:, cs]
            mixed = jnp.dot(w_ref[g], zn.astype(BF16), preferred_element_type=F32) + bs_ref[:, g:g + 1]
            u = _gelu(u_ref[rs, cs].astype(F32))
            o_ref[rs, cs] = (u * mixed).astype(BF16)


def _spatial_gating(proj, w_causal_b, ln_g, ln_b, bs_t):
    T = proj.shape[0]
    tm = TM_SGU
    ucol = 3 * D_ATTN // D_SGU
    return pl.pallas_call(
        _sgu_kernel,
        grid=(T // tm,),
        in_specs=[
            pl.BlockSpec((tm, D_SGU), lambda i: (i, ucol)),
            pl.BlockSpec((tm, D_SGU), lambda i: (i, ucol + 1)),
            pl.BlockSpec((N_SGU_GROUPS, SGU_CHUNK, SGU_CHUNK), lambda i: (0, 0, 0)),
            pl.BlockSpec((1, D_SGU), lambda i: (0, 0)),
            pl.BlockSpec((1, D_SGU), lambda i: (0, 0)),
            pl.BlockSpec((SGU_CHUNK, N_SGU_GROUPS), lambda i: (0, 0)),
        ],
        out_specs=pl.BlockSpec((tm, D_SGU), lambda i: (i, 0)),
        out_shape=jax.ShapeDtypeStruct((T, D_SGU), BF16),
        compiler_params=_cparams(("arbitrary",)),
        name="spatial_gating",
    )(proj, proj, w_causal_b, ln_g, ln_b, bs_t)


def _outproj_kernel(a_ref, s_ref, x_ref, mod_ref, g_ref, w_ref, x1_ref, h2_ref, h2p_ref):
    mixed = jnp.dot(a_ref[...], w_ref[0:D_ATTN, :], preferred_element_type=F32)
    mixed += jnp.dot(s_ref[...], w_ref[D_ATTN:, :], preferred_element_type=F32)
    x1 = x_ref[...] + mod_ref[0, 2:3, :] * mixed
    x1_ref[...] = x1
    r = lax.rsqrt(jnp.mean(x1 * x1, axis=-1, keepdims=True) + NORM_EPS)
    h2 = (x1 * r) * g_ref[...] * (1.0 + mod_ref[0, 4:5, :]) + mod_ref[0, 3:4, :]
    h2_ref[...] = h2.astype(BF16)
    h2p_ref[...] = _pack_halves(h2)


def _out_projection(attn, sgu, x2, mod3, norm_g, w_out_b):
    T, D = x2.shape
    tm = TM_OUT
    per_batch = SEQ // tm
    return pl.pallas_call(
        _outproj_kernel,
        grid=(T // tm,),
        in_specs=[
            pl.BlockSpec((tm, D_ATTN), lambda i: (i, 0)),
            pl.BlockSpec((tm, D_SGU), lambda i: (i, 0)),
            pl.BlockSpec((tm, D), lambda i: (i, 0)),
            pl.BlockSpec((1, 6, D), lambda i: (i // per_batch, 0, 0)),
            pl.BlockSpec((1, D), lambda i: (0, 0)),
            pl.BlockSpec((D_ATTN + D_SGU, D), lambda i: (0, 0)),
        ],
        out_specs=[pl.BlockSpec((tm, D), lambda i: (i, 0)), pl.BlockSpec((tm, D), lambda i: (i, 0)),
                   pl.BlockSpec((tm, D // 2), lambda i: (i, 0))],
        out_shape=[jax.ShapeDtypeStruct((T, D), F32), jax.ShapeDtypeStruct((T, D), BF16),
                   jax.ShapeDtypeStruct((T, D // 2), U32)],
        compiler_params=_cparams(("arbitrary",)),
        name="out_projection",
    )(attn, sgu, x2, mod3, norm_g, w_out_b)


def _router_kernel(h_ref, w_ref, b_ref, idx_ref, wt_ref, rank_ref, cnt_ref, run_ref):
    i = pl.program_id(0)
    tr = TM_ROUTE
    E = N_EXPERTS

    @pl.when(i == 0)
    def _():
        run_ref[...] = jnp.zeros_like(run_ref)

    logits = lax.dot_general(w_ref[...], h_ref[...], (((1,), (1,)), ((), ())), preferred_element_type=F32)
    scores = jax.nn.sigmoid(logits)
    sel = scores + b_ref[...]

    slabs = [sel[g * GROUP_SIZE:(g + 1) * GROUP_SIZE, :] for g in range(N_EXPERT_GROUPS)]
    si = lax.broadcasted_iota(I32, (GROUP_SIZE, tr), 0).astype(F32)
    gs = []
    for slab in slabs:
        m1 = jnp.max(slab, axis=0, keepdims=True)
        first = jnp.min(jnp.where(slab == m1, si, float(GROUP_SIZE)), axis=0, keepdims=True)
        m2 = jnp.max(jnp.where(si == first, -jnp.inf, slab), axis=0, keepdims=True)
        gs.append(m1 + m2)

    kept = []
    for g in range(N_EXPERT_GROUPS):
        beaten = jnp.zeros((1, tr), F32)
        for o in range(N_EXPERT_GROUPS):
            if o < g:
                beaten += (gs[o] >= gs[g]).astype(F32)
            elif o > g:
                beaten += (gs[o] > gs[g]).astype(F32)
        kept.append(jnp.where(beaten < TOPK_GROUPS, slabs[g], -jnp.inf))
    masked = jnp.concatenate(kept, axis=0)

    ei = lax.broadcasted_iota(I32, (E, tr), 0).astype(F32)
    picks, pick_scores = [], []
    onehot_sum = jnp.zeros((E, tr), F32)
    for k in range(TOP_K):
        m = jnp.max(masked, axis=0, keepdims=True)
        ik = jnp.min(jnp.where(masked == m, ei, float(E)), axis=0, keepdims=True)
        oh = ei == ik
        pick_scores.append(jnp.sum(jnp.where(oh, scores, 0.0), axis=0, keepdims=True))
        masked = jnp.where(oh, -jnp.inf, masked)
        onehot_sum += oh.astype(F32)
        picks.append(ik)

    ti = lax.broadcasted_iota(I32, (tr, tr), 0)
    tj = lax.broadcasted_iota(I32, (tr, tr), 1)
    before = (ti < tj).astype(BF16)
    prior = jnp.dot(onehot_sum.astype(BF16), before, preferred_element_type=F32) + run_ref[...]

    total = pick_scores[0]
    for k in range(1, TOP_K):
        total += pick_scores[k]
    for k in range(TOP_K):
        idx_ref[k:k + 1, :] = picks[k].astype(I32)
        wt_ref[k:k + 1, :] = pick_scores[k] / total * ROUTED_SCALE
        rk = jnp.sum(jnp.where(ei == picks[k], prior, 0.0), axis=0, keepdims=True)
        rank_ref[k:k + 1, :] = rk.astype(I32)

    run_ref[...] += jnp.sum(onehot_sum, axis=1, keepdims=True)
    cnt_ref[...] = run_ref[...].astype(I32)


def _route(h2, router_wt_b, router_bias):
    T, D = h2.shape
    tr = TM_ROUTE
    E = N_EXPERTS
    row_spec = pl.BlockSpec((TOP_K, tr), lambda i: (0, i))
    return pl.pallas_call(
        _router_kernel,
        grid=(T // tr,),
        in_specs=[
            pl.BlockSpec((tr, D), lambda i: (i, 0)),
            pl.BlockSpec((E, D), lambda i: (0, 0)),
            pl.BlockSpec((E, 1), lambda i: (0, 0)),
        ],
        out_specs=[row_spec, row_spec, row_spec, pl.BlockSpec((E, 1), lambda i: (0, 0))],
        out_shape=[
            jax.ShapeDtypeStruct((TOP_K, T), I32),
            jax.ShapeDtypeStruct((TOP_K, T), F32),
            jax.ShapeDtypeStruct((TOP_K, T), I32),
            jax.ShapeDtypeStruct((E, 1), I32),
        ],
        scratch_shapes=[pltpu.VMEM((E, 1), F32)],
        compiler_params=_cparams(("arbitrary",)),
        name="router",
    )(h2, router_wt_b, router_bias.reshape(E, 1))


def _dispatch_kernel(pend_ref, h_ref, dest_ref, xg_ref, dest_s, zero_ref, sem_s, sem):
    i = pl.program_id(0)
    td = TM_DISPATCH
    blk = EXPERT_BLOCK

    def zero_copy(e):
        start = pl.multiple_of(pend_ref[e + 1] - blk, blk)
        return pltpu.make_async_copy(zero_ref, xg_ref.at[pl.ds(start, blk), :], sem)

    @pl.when(i == 0)
    def _():
        zero_ref[...] = jnp.zeros_like(zero_ref)

        def start_zero(e, c):
            @pl.when(pend_ref[e + 1] > pend_ref[e])
            def _():
                zero_copy(e).start()
            return c

        def wait_zero(e, c):
            @pl.when(pend_ref[e + 1] > pend_ref[e])
            def _():
                zero_copy(e).wait()
            return c

        lax.fori_loop(0, N_EXPERTS, start_zero, 0)
        lax.fori_loop(0, N_EXPERTS, wait_zero, 0)

    cp = pltpu.make_async_copy(dest_ref, dest_s, sem_s)
    cp.start()
    cp.wait()

    def row_copy(t, k):
        return pltpu.make_async_copy(h_ref.at[pl.ds(t, 1), :], xg_ref.at[pl.ds(dest_s[k, t], 1), :], sem)

    def start_rows(t8, c):
        base = pl.multiple_of(t8 * 8, 8)
        for j in range(8):
            for k in range(TOP_K):
                row_copy(base + j, k).start(priority=k % DMA_QUEUES)
        return c

    lax.fori_loop(0, td // 8, start_rows, 0)
    for k in range(TOP_K):
        pltpu.make_async_copy(h_ref, h_ref, sem).wait()


def _dispatch(h2p, dest, pend0, rows):
    T, D = h2p.shape
    td = TM_DISPATCH
    return pl.pallas_call(
        _dispatch_kernel,
        grid_spec=pltpu.PrefetchScalarGridSpec(
            num_scalar_prefetch=1,
            grid=(T // td,),
            in_specs=[
                pl.BlockSpec((td, D), lambda i, p: (i, 0)),
                pl.BlockSpec((TOP_K, td), lambda i, p: (0, i)),
            ],
            out_specs=pl.BlockSpec(memory_space=pl.ANY),
            scratch_shapes=[
                pltpu.SMEM((TOP_K, td), I32),
                pltpu.VMEM((EXPERT_BLOCK, D), U32),
                pltpu.SemaphoreType.DMA,
                pltpu.SemaphoreType.DMA,
            ],
        ),
        out_shape=jax.ShapeDtypeStruct((rows, D), U32),
        compiler_params=_cparams(("arbitrary",)),
        name="dispatch",
    )(pend0, h2p, dest)


def _experts_kernel(first_ref, gidx_ref, used_ref, nused_ref, ngroups_ref,
                    x_ref, wg_hbm, wu_hbm, wd_hbm, y_ref,
                    wgu_f, wd_f, wgu_b, wd_b, issued, sem):
    i = pl.program_id(0)
    F = D_EXPERT

    def tensor_copy(e, s, t):
        if t == 0:
            return pltpu.make_async_copy(wg_hbm.at[e], wgu_f.at[s, 0], sem.at[s, 0])
        if t == 1:
            return pltpu.make_async_copy(wu_hbm.at[e], wgu_f.at[s, 1], sem.at[s, 1])
        return pltpu.make_async_copy(wd_hbm.at[e], wd_f.at[s], sem.at[s, 2])

    def issue_until(n):
        def body(q, carry):
            h = q // 3
            for t in range(3):
                @pl.when(q % 3 == t)
                def _():
                    tensor_copy(used_ref[h], h % 2, t).start(priority=DMA_QUEUES - 1)
            return carry

        lax.fori_loop(issued[0], n, body, 0)
        issued[0] = jnp.maximum(issued[0], n)

    @pl.when(i == 0)
    def _():
        issued[0] = 0

    @pl.when(i < nused_ref[0])
    def _():
        g = gidx_ref[i]

        @pl.when(first_ref[i] == 1)
        def _():
            issue_until(3 * (g + 1))
            s = g % 2
            for t in range(3):
                tensor_copy(used_ref[g], s, t).wait()
            wgu_b[:, :F] = wgu_f[s, 0].astype(BF16)
            wgu_b[:, F:] = wgu_f[s, 1].astype(BF16)
            wd_b[...] = wd_f[s].astype(BF16)

        cap = 3 * jnp.minimum(g + 3, ngroups_ref[0])
        issue_until(jnp.minimum(issued[0] + 1, cap))

        lo, hi = _unpack_halves(x_ref[...])
        x = jnp.concatenate([lo.astype(BF16), hi.astype(BF16)], axis=1)
        gu = jnp.dot(x, wgu_b[...], preferred_element_type=F32)
        a = (_silu(gu[:, :F]) * gu[:, F:]).astype(BF16)
        y_ref[...] = _pack_halves(jnp.dot(a, wd_b[...], preferred_element_type=F32))


def _experts(xg, first, gidx, used_list, n_used, n_groups, w_gate, w_up, w_down):
    rows = xg.shape[0]
    D = D_MODEL
    bm = EXPERT_BLOCK
    F = D_EXPERT
    nb = rows // bm
    row_map = lambda i, fi, gi, ul, nu, ng: (jnp.minimum(i, nu[0] - 1), 0)
    return pl.pallas_call(
        _experts_kernel,
        grid_spec=pltpu.PrefetchScalarGridSpec(
            num_scalar_prefetch=5,
            grid=(nb,),
            in_specs=[
                pl.BlockSpec((bm, D // 2), row_map),
                pl.BlockSpec(memory_space=pl.ANY),
                pl.BlockSpec(memory_space=pl.ANY),
                pl.BlockSpec(memory_space=pl.ANY),
            ],
            out_specs=pl.BlockSpec((bm, D // 2), row_map),
            scratch_shapes=[
                pltpu.VMEM((2, 2, D, F), F32), pltpu.VMEM((2, F, D), F32),
                pltpu.VMEM((D, 2 * F), BF16), pltpu.VMEM((F, D), BF16),
                pltpu.SMEM((1,), I32),
                pltpu.SemaphoreType.DMA((2, 3)),
            ],
        ),
        out_shape=jax.ShapeDtypeStruct((rows, D // 2), U32),
        compiler_params=_cparams(("arbitrary",)),
        name="experts",
    )(first, gidx, used_list, n_used, n_groups, xg, w_gate, w_up, w_down)


def _combine_kernel(dest_ref, dnext_ref, wt_ref, x1_ref, h_ref, mod_ref, sg_ref, su_ref, sd_ref, yg_ref, o_ref,
                    dest_s, buf, sem_s, sem):
    i = pl.program_id(0)
    last = pl.num_programs(0) - 1
    tc = TM_COMBINE
    cur = i % 2
    nxt = 1 - cur

    def row_copy(slot, t, k):
        return pltpu.make_async_copy(yg_ref.at[pl.ds(dest_s[slot, k, t], 1), :],
                                     buf.at[slot, k, pl.ds(t, 1), :], sem.at[slot])

    def load_dest(src_ref, slot):
        cp = pltpu.make_async_copy(src_ref, dest_s.at[slot], sem_s)
        cp.start()
        cp.wait()

    def wait_rows(slot):
        pltpu.make_async_copy(buf.at[slot], buf.at[slot], sem.at[slot]).wait()

    @pl.when(i == 0)
    def _():
        load_dest(dest_ref, 0)

        def body(t, c):
            for k in range(TOP_K):
                row_copy(0, t, k).start(priority=k % DMA_QUEUES)
            return c

        lax.fori_loop(0, tc, body, 0)

    load_dest(dnext_ref, nxt)
    for t in range(tc):
        for k in range(TOP_K):
            row_copy(nxt, t, k).start(priority=k % DMA_QUEUES)

    hb = h_ref[...]
    g = jnp.dot(hb, sg_ref[...], preferred_element_type=F32)
    u = jnp.dot(hb, su_ref[...], preferred_element_type=F32)
    ffn = jnp.dot((_silu(g) * u).astype(BF16), sd_ref[...], preferred_element_type=F32)

    wait_rows(cur)
    half = D_MODEL // 2
    ffn_lo, ffn_hi = ffn[:, :half], ffn[:, half:]
    for k in range(TOP_K):
        lo, hi = _unpack_halves(buf[cur, k])
        w = wt_ref[:, k:k + 1]
        ffn_lo += lo * w
        ffn_hi += hi * w
    o_ref[:, :half] = x1_ref[:, :half] + mod_ref[0, 5:6, :half] * ffn_lo
    o_ref[:, half:] = x1_ref[:, half:] + mod_ref[0, 5:6, half:] * ffn_hi

    @pl.when(i == last)
    def _():
        wait_rows(nxt)


def _combine(dest, wt_t, x1, h2, mod3, sg_b, su_b, sd_b, yg):
    T, D = x1.shape
    tc = TM_COMBINE
    per_batch = SEQ // tc
    F = D_EXPERT
    return pl.pallas_call(
        _combine_kernel,
        grid=(T // tc,),
        in_specs=[
            pl.BlockSpec((TOP_K, tc), lambda i: (0, i)),
            pl.BlockSpec((TOP_K, tc), lambda i: (0, jnp.minimum(i + 1, T // tc - 1))),
            pl.BlockSpec((tc, TOP_K), lambda i: (i, 0)),
            pl.BlockSpec((tc, D), lambda i: (i, 0)),
            pl.BlockSpec((tc, D), lambda i: (i, 0)),
            pl.BlockSpec((1, 6, D), lambda i: (i // per_batch, 0, 0)),
            pl.BlockSpec((D, F), lambda i: (0, 0)),
            pl.BlockSpec((D, F), lambda i: (0, 0)),
            pl.BlockSpec((F, D), lambda i: (0, 0)),
            pl.BlockSpec(memory_space=pl.ANY),
        ],
        out_specs=pl.BlockSpec((tc, D), lambda i: (i, 0)),
        out_shape=jax.ShapeDtypeStruct((T, D), F32),
        scratch_shapes=[
            pltpu.SMEM((2, TOP_K, tc), I32),
            pltpu.VMEM((2, TOP_K, tc, D // 2), U32),
            pltpu.SemaphoreType.DMA,
            pltpu.SemaphoreType.DMA((2,)),
        ],
        compiler_params=_cparams(("arbitrary",)),
        name="combine",
    )(dest, dest, wt_t, x1, h2, mod3, sg_b, su_b, sd_b, yg)


def kernel(x, c, ada_w, ada_b, mix_norm_g, ffn_norm_g, w_in, q_norm_g, k_norm_g, rel_bias, sgu_ln_g, sgu_ln_b, sgu_w, sgu_b, w_out, router_w, router_bias, shared_w_gate, shared_w_up, shared_w_down, expert_w_gate, expert_w_up, expert_w_down):
    B, S, D = x.shape
    assert S == SEQ and D == D_MODEL and ada_w.shape[0] == 1
    T = B * S
    x2 = x.reshape(T, D)

    mod3 = _modulation(c, ada_w[0], ada_b[0]).reshape(B, 6, D)

    proj = _in_projection(x2, mod3, mix_norm_g, w_in[0].astype(BF16), q_norm_g, k_norm_g)
    attn = _attention(proj.reshape(B, S, D_IN_PROJ), _bias_tables(rel_bias)).reshape(T, D_ATTN)
    causal = jnp.tril(jnp.ones((SGU_CHUNK, SGU_CHUNK), F32))
    sgu = _spatial_gating(proj, (sgu_w[0] * causal).astype(BF16), sgu_ln_g, sgu_ln_b, sgu_b[0].T)
    x1, h2, h2p = _out_projection(attn, sgu, x2, mod3, ffn_norm_g, w_out[0].astype(BF16))

    idx, wts, rank, counts = _route(h2, router_w[0].T.astype(BF16), router_bias[0])

    bm = EXPERT_BLOCK
    counts = counts.reshape(N_EXPERTS)
    padded = (counts + bm - 1) // bm * bm
    pends = jnp.cumsum(padded)
    pstarts = pends - padded
    n_blocks = T * TOP_K // bm + N_EXPERTS
    block_start = jnp.arange(n_blocks, dtype=I32) * bm
    block_exp = jnp.minimum(jnp.sum((pends[None, :] <= block_start[:, None]).astype(I32), axis=1), N_EXPERTS - 1)
    n_used = (pends[-1] // bm).astype(I32).reshape(1)
    eids = jnp.arange(N_EXPERTS, dtype=I32)
    dest = jnp.sum(jnp.where(idx[:, :, None] == eids, pstarts.astype(I32), 0), axis=-1) + rank
    pend0 = jnp.concatenate([jnp.zeros((1,), I32), pends.astype(I32)])
    first = jnp.concatenate([jnp.ones((1,), I32), (block_exp[1:] != block_exp[:-1]).astype(I32)])
    used = counts > 0
    ordinal = jnp.cumsum(used.astype(I32)) - 1
    n_groups = jnp.sum(used.astype(I32)).reshape(1)
    used_list = jnp.sum(jnp.where(used[None, :] & (ordinal[None, :] == eids[:, None]), eids[None, :], 0), axis=-1)
    gidx = jnp.sum(jnp.where(block_exp[:, None] == eids, ordinal, 0), axis=-1)

    xg = _dispatch(h2p, dest, pend0, n_blocks * bm)
    yg = _experts(xg, first, gidx.astype(I32), used_list.astype(I32), n_used, n_groups,
                  expert_w_gate[0], expert_w_up[0], expert_w_down[0])
    out = _combine(dest, wts.T, x1, h2, mod3, shared_w_gate[0].astype(BF16), shared_w_up[0].astype(BF16),
                   shared_w_down[0].astype(BF16), yg)
    return out.reshape(B, S, D)
```

```python
import functools
import math

import numpy as np
import jax
import jax.numpy as jnp
from jax import lax
from jax.experimental import pallas as pl
from jax.experimental.pallas import tpu as pltpu

F32 = jnp.float32
BF16 = jnp.bfloat16
I32 = jnp.int32
U32 = jnp.uint32

D_MODEL = 2048
SEQ = 2048
HEAD_DIM = 128
N_HEADS = 8
D_ATTN = N_HEADS * HEAD_DIM
D_SGU = 1024
N_SGU_GROUPS = 8
SGU_CHUNK = 128
D_IN_PROJ = 3 * D_ATTN + 2 * D_SGU
DILATED_GROUPS = ((128, 1), (512, 4), (2048, 16))
BAND = 128
NUM_REL_BUCKETS = 32
REL_MAX_DISTANCE = 2048
N_EXPERTS = 256
TOP_K = 8
N_EXPERT_GROUPS = 8
GROUP_SIZE = N_EXPERTS // N_EXPERT_GROUPS
TOPK_GROUPS = 4
D_EXPERT = 512
ROUTED_SCALE = 2.5
NORM_EPS = 1e-6
MASK_VALUE = -1e30

LANES = 128
ROW_TILE = D_MODEL // 2 // LANES
VMEM_LIMIT = 56 * 1024 * 1024
DMA_QUEUES = 2

TM_PROJ = 512
TM_SGU = 512
TM_OUT = 256
TM_ROUTE = 512
TM_DISPATCH = 256
TM_COMBINE = 128
EXPERT_BLOCK = 128


def _cparams(sem):
    return pltpu.CompilerParams(dimension_semantics=sem, vmem_limit_bytes=VMEM_LIMIT)


def _silu(v):
    return v * jax.nn.sigmoid(v)


def _gelu(v):
    return 0.5 * v * (1.0 + lax.erf(v * (1.0 / math.sqrt(2.0))))


def _pack_halves(v):
    h = v.shape[1] // 2
    lo = lax.bitcast_convert_type(v[:, :h].astype(BF16).astype(F32), U32)
    hi = lax.bitcast_convert_type(v[:, h:].astype(BF16).astype(F32), U32)
    return (hi & jnp.uint32(0xFFFF0000)) | (lo >> 16)


def _unpack_halves(w):
    lo = lax.bitcast_convert_type(w << 16, F32)
    hi = lax.bitcast_convert_type(w & jnp.uint32(0xFFFF0000), F32)
    return lo, hi


def _store_row_tiles(ref, base, rows, packed):
    for s in range(ROW_TILE):
        ref[pl.ds(base + s, rows, stride=ROW_TILE), :] = packed[:, s * LANES:(s + 1) * LANES]


def _load_row_tiles(ref, base, rows):
    return jnp.concatenate([ref[pl.ds(base + s, rows, stride=ROW_TILE), :] for s in range(ROW_TILE)], axis=1)


def _mod_kernel(c_ref, w_ref, b_ref, o_ref):
    ca = _silu(c_ref[...]).astype(BF16)
    o_ref[...] = jnp.dot(ca, w_ref[...].astype(BF16), preferred_element_type=F32) + b_ref[...]


def _modulation(c, ada_w, ada_b):
    B, D = c.shape
    N = ada_w.shape[1]
    tn = 1024
    return pl.pallas_call(
        _mod_kernel,
        grid=(N // tn,),
        in_specs=[
            pl.BlockSpec((B, D), lambda j: (0, 0)),
            pl.BlockSpec((D, tn), lambda j: (0, j)),
            pl.BlockSpec((1, tn), lambda j: (0, j)),
        ],
        out_specs=pl.BlockSpec((B, tn), lambda j: (0, j)),
        out_shape=jax.ShapeDtypeStruct((B, N), F32),
        compiler_params=_cparams(("arbitrary",)),
        name="modulation",
    )(c, ada_w, ada_b.reshape(1, N))


def _inproj_kernel(x_ref, mod_ref, g_ref, w_ref, qg_ref, kg_ref, o_ref, h_ref):
    j = pl.program_id(1)

    @pl.when(j == 0)
    def _():
        x = x_ref[...]
        r = lax.rsqrt(jnp.mean(x * x, axis=-1, keepdims=True) + NORM_EPS)
        h = (x * r) * g_ref[...] * (1.0 + mod_ref[0, 1:2, :]) + mod_ref[0, 0:1, :]
        h_ref[...] = h.astype(BF16)

    acc = jnp.dot(h_ref[...], w_ref[...], preferred_element_type=F32)

    def head_norm(gain_ref, scale):
        for hd in range(N_HEADS):
            a = acc[:, hd * HEAD_DIM:(hd + 1) * HEAD_DIM]
            r = lax.rsqrt(jnp.mean(a * a, axis=-1, keepdims=True) + NORM_EPS)
            o_ref[:, hd * HEAD_DIM:(hd + 1) * HEAD_DIM] = ((a * r) * gain_ref[...] * scale).astype(BF16)

    @pl.when(j == 0)
    def _():
        head_norm(qg_ref, HEAD_DIM ** -0.5)

    @pl.when(j == 1)
    def _():
        head_norm(kg_ref, 1.0)

    @pl.when(j >= 2)
    def _():
        o_ref[...] = acc.astype(BF16)


def _in_projection(x2, mod3, norm_g, w_in_b, q_g, k_g):
    T, D = x2.shape
    tm, tn = TM_PROJ, D_ATTN
    per_batch = SEQ // tm
    return pl.pallas_call(
        _inproj_kernel,
        grid=(T // tm, D_IN_PROJ // tn),
        in_specs=[
            pl.BlockSpec((tm, D), lambda i, j: (i, 0)),
            pl.BlockSpec((1, 6, D), lambda i, j: (i // per_batch, 0, 0)),
            pl.BlockSpec((1, D), lambda i, j: (0, 0)),
            pl.BlockSpec((D, tn), lambda i, j: (0, j)),
            pl.BlockSpec((1, HEAD_DIM), lambda i, j: (0, 0)),
            pl.BlockSpec((1, HEAD_DIM), lambda i, j: (0, 0)),
        ],
        out_specs=pl.BlockSpec((tm, tn), lambda i, j: (i, j)),
        out_shape=jax.ShapeDtypeStruct((T, D_IN_PROJ), BF16),
        scratch_shapes=[pltpu.VMEM((tm, D), BF16)],
        compiler_params=_cparams(("arbitrary", "arbitrary")),
        name="in_projection",
    )(x2, mod3, norm_g, w_in_b, q_g, k_g)


def _bias_tables(rel_bias):
    n = BAND
    qi = jnp.arange(n)[:, None]
    ki = jnp.arange(2 * n)[None, :]
    steps = n + qi - ki
    in_band = (steps >= 0) & (steps <= n)
    max_exact = NUM_REL_BUCKETS // 2
    tabs = []
    for _, dilation in DILATED_GROUPS:
        dist = jnp.clip(steps, 0, n) * dilation
        nf = jnp.maximum(dist, 1).astype(F32)
        large = max_exact + (jnp.log(nf / max_exact) / math.log(REL_MAX_DISTANCE / max_exact)
                             * (NUM_REL_BUCKETS - max_exact)).astype(I32)
        large = jnp.minimum(large, NUM_REL_BUCKETS - 1)
        bucket = jnp.where(dist < max_exact, dist, large)
        onehot = jax.nn.one_hot(bucket, NUM_REL_BUCKETS, dtype=F32)
        b = jnp.einsum("qkb,bh->hqk", onehot, rel_bias.astype(F32), precision=lax.Precision.HIGHEST)
        tabs.append(jnp.where(in_band[None], b, MASK_VALUE))
    return jnp.stack(tabs)


def _attn_kernel(q_ref, k_ref, v_ref, bias_ref, o_ref, qf, kf, vf, q4, k4, v4,
                 o0, l0, o1, l1, o2, l2, stage):
    n = BAND
    quarter = SEQ // 4

    for src, nat, res in ((q_ref, qf, q4), (k_ref, kf, k4), (v_ref, vf, v4)):
        nat[...] = src[...].astype(F32)
        for r in range(4):
            res[r * quarter:(r + 1) * quarter, :] = nat[pl.ds(r, quarter, stride=4), :]

    def piece(qb, kb, vb, bias, o_out, l_out, rows):
        l = lax.dot_general(qb, kb, (((1,), (1,)), ((), ())), preferred_element_type=F32) + bias
        m = jnp.max(l, axis=-1, keepdims=True)
        p = jnp.exp(l - m)
        s = jnp.sum(p, axis=-1, keepdims=True)
        o = jnp.dot(p.astype(BF16), vb, preferred_element_type=F32)
        o_out[rows, :] = o / s
        l_out[rows, :] = jnp.broadcast_to(m + jnp.log(s), (n, LANES))

    for i in range(SEQ // n):
        rows = slice(i * n, (i + 1) * n)
        if i == 0:
            piece(q_ref[rows, :], k_ref[rows, :], v_ref[rows, :], bias_ref[0, :, n:], o0, l0, rows)
        else:
            krows = slice((i - 1) * n, (i + 1) * n)
            piece(q_ref[rows, :], k_ref[krows, :], v_ref[krows, :], bias_ref[0], o0, l0, rows)

    for r in range(4):
        for blk in range(quarter // n):
            base = r * quarter + blk * n
            rows = slice(base, base + n)
            krows = rows if blk == 0 else slice(base - n, base + n)
            bias = bias_ref[1, :, n:] if blk == 0 else bias_ref[1]
            piece(q4[rows, :].astype(BF16), k4[krows, :].astype(BF16), v4[krows, :].astype(BF16),
                  bias, o1, l1, rows)

    for r in range(4):
        for a in range(4):
            rows = pl.ds(r * quarter + a, n, stride=4)
            piece(q4[rows, :].astype(BF16), k4[rows, :].astype(BF16), v4[rows, :].astype(BF16),
                  bias_ref[2, :, n:], o2, l2, rows)

    for r in range(4):
        for c in range(quarter // n):
            rows = slice(r * quarter + c * n, r * quarter + (c + 1) * n)
            nat = pl.ds(r + 4 * c * n, n, stride=4)
            a0, a1, a2 = l0[nat, :], l1[rows, :], l2[rows, :]
            m = jnp.maximum(jnp.maximum(a0, a1), a2)
            e0, e1, e2 = jnp.exp(a0 - m), jnp.exp(a1 - m), jnp.exp(a2 - m)
            mix = (e0 * o0[nat, :] + e1 * o1[rows, :] + e2 * o2[rows, :]) / (e0 + e1 + e2)
            stage[nat, :] = mix
    o_ref[...] = stage[...].astype(BF16)


def _attention(proj3, bias_tabs):
    B = proj3.shape[0]
    blk = lambda off: pl.BlockSpec((None, SEQ, HEAD_DIM), lambda b, h: (b, 0, off + h))
    return pl.pallas_call(
        _attn_kernel,
        grid=(B, N_HEADS),
        in_specs=[
            blk(0), blk(N_HEADS), blk(2 * N_HEADS),
            pl.BlockSpec((3, None, BAND, 2 * BAND), lambda b, h: (0, h, 0, 0)),
        ],
        out_specs=pl.BlockSpec((None, SEQ, HEAD_DIM), lambda b, h: (b, 0, h)),
        out_shape=jax.ShapeDtypeStruct((B, SEQ, D_ATTN), BF16),
        scratch_shapes=[pltpu.VMEM((SEQ, HEAD_DIM), F32) for _ in range(13)],
        compiler_params=_cparams(("arbitrary", "arbitrary")),
        name="dilated_attention",
    )(proj3, proj3, proj3, bias_tabs)


def _sgu_kernel(u_ref, z_ref, w_ref, g_ref, b_ref, bs_ref, o_ref):
    n = SGU_CHUNK
    for c in range(TM_SGU // n):
        rs = slice(c * n, (c + 1) * n)
        for g in range(N_SGU_GROUPS):
            cs = slice(g * n, (g + 1) * n)
            z = _gelu(z_ref[rs, cs].astype(F32))
            mu = jnp.mean(z, axis=-1, keepdims=True)
            zc = z - mu
            var = jnp.mean(zc * zc, axis=-1, keepdims=True)
            zn = (zc * lax.rsqrt(var + NORM_EPS)) * g_ref[:, cs] + b_ref[:, cs]
            mixed = jnp.dot(w_ref[g], zn.astype(BF16), preferred_element_type=F32) + bs_ref[:, g:g + 1]
            u = _gelu(u_ref[rs, cs].astype(F32))
            o_ref[rs, cs] = (u * mixed).astype(BF16)


def _spatial_gating(proj, w_causal_b, ln_g, ln_b, bs_t):
    T = proj.shape[0]
    tm = TM_SGU
    ucol = 3 * D_ATTN // D_SGU
    return pl.pallas_call(
        _sgu_kernel,
        grid=(T // tm,),
        in_specs=[
            pl.BlockSpec((tm, D_SGU), lambda i: (i, ucol)),
            pl.BlockSpec((tm, D_SGU), lambda i: (i, ucol + 1)),
            pl.BlockSpec((N_SGU_GROUPS, SGU_CHUNK, SGU_CHUNK), lambda i: (0, 0, 0)),
            pl.BlockSpec((1, D_SGU), lambda i: (0, 0)),
            pl.BlockSpec((1, D_SGU), lambda i: (0, 0)),
            pl.BlockSpec((SGU_CHUNK, N_SGU_GROUPS), lambda i: (0, 0)),
        ],
        out_specs=pl.BlockSpec((tm, D_SGU), lambda i: (i, 0)),
        out_shape=jax.ShapeDtypeStruct((T, D_SGU), BF16),
        compiler_params=_cparams(("arbitrary",)),
        name="spatial_gating",
    )(proj, proj, w_causal_b, ln_g, ln_b, bs_t)


def _outproj_kernel(a_ref, s_ref, x_ref, mod_ref, g_ref, w_ref, x1_ref, h2_ref, h2p_ref):
    mixed = jnp.dot(a_ref[...], w_ref[0:D_ATTN, :], preferred_element_type=F32)
    mixed += jnp.dot(s_ref[...], w_ref[D_ATTN:, :], preferred_element_type=F32)
    x1 = x_ref[...] + mod_ref[0, 2:3, :] * mixed
    x1_ref[...] = x1
    r = lax.rsqrt(jnp.mean(x1 * x1, axis=-1, keepdims=True) + NORM_EPS)
    h2 = (x1 * r) * g_ref[...] * (1.0 + mod_ref[0, 4:5, :]) + mod_ref[0, 3:4, :]
    h2_ref[...] = h2.astype(BF16)
    _store_row_tiles(h2p_ref, 0, TM_OUT, _pack_halves(h2))


def _out_projection(attn, sgu, x2, mod3, norm_g, w_out_b):
    T, D = x2.shape
    tm = TM_OUT
    per_batch = SEQ // tm
    return pl.pallas_call(
        _outproj_kernel,
        grid=(T // tm,),
        in_specs=[
            pl.BlockSpec((tm, D_ATTN), lambda i: (i, 0)),
            pl.BlockSpec((tm, D_SGU), lambda i: (i, 0)),
            pl.BlockSpec((tm, D), lambda i: (i, 0)),
            pl.BlockSpec((1, 6, D), lambda i: (i // per_batch, 0, 0)),
            pl.BlockSpec((1, D), lambda i: (0, 0)),
            pl.BlockSpec((D_ATTN + D_SGU, D), lambda i: (0, 0)),
        ],
        out_specs=[pl.BlockSpec((tm, D), lambda i: (i, 0)), pl.BlockSpec((tm, D), lambda i: (i, 0)),
                   pl.BlockSpec((tm * ROW_TILE, LANES), lambda i: (i, 0))],
        out_shape=[jax.ShapeDtypeStruct((T, D), F32), jax.ShapeDtypeStruct((T, D), BF16),
                   jax.ShapeDtypeStruct((T * ROW_TILE, LANES), U32)],
        compiler_params=_cparams(("arbitrary",)),
        name="out_projection",
    )(attn, sgu, x2, mod3, norm_g, w_out_b)


def _router_kernel(h_ref, w_ref, b_ref, idx_ref, wt_ref, rank_ref, cnt_ref, run_ref):
    i = pl.program_id(0)
    tr = TM_ROUTE
    E = N_EXPERTS

    @pl.when(i == 0)
    def _():
        run_ref[...] = jnp.zeros_like(run_ref)

    logits = lax.dot_general(w_ref[...], h_ref[...], (((1,), (1,)), ((), ())), preferred_element_type=F32)
    scores = jax.nn.sigmoid(logits)
    sel = scores + b_ref[...]

    slabs = [sel[g * GROUP_SIZE:(g + 1) * GROUP_SIZE, :] for g in range(N_EXPERT_GROUPS)]
    si = lax.broadcasted_iota(I32, (GROUP_SIZE, tr), 0).astype(F32)
    gs = []
    for slab in slabs:
        m1 = jnp.max(slab, axis=0, keepdims=True)
        first = jnp.min(jnp.where(slab == m1, si, float(GROUP_SIZE)), axis=0, keepdims=True)
        m2 = jnp.max(jnp.where(si == first, -jnp.inf, slab), axis=0, keepdims=True)
        gs.append(m1 + m2)

    kept = []
    for g in range(N_EXPERT_GROUPS):
        beaten = jnp.zeros((1, tr), F32)
        for o in range(N_EXPERT_GROUPS):
            if o < g:
                beaten += (gs[o] >= gs[g]).astype(F32)
            elif o > g:
                beaten += (gs[o] > gs[g]).astype(F32)
        kept.append(jnp.where(beaten < TOPK_GROUPS, slabs[g], -jnp.inf))
    masked = jnp.concatenate(kept, axis=0)

    ei = lax.broadcasted_iota(I32, (E, tr), 0).astype(F32)
    picks, pick_scores = [], []
    onehot_sum = jnp.zeros((E, tr), F32)
    for k in range(TOP_K):
        m = jnp.max(masked, axis=0, keepdims=True)
        ik = jnp.min(jnp.where(masked == m, ei, float(E)), axis=0, keepdims=True)
        oh = ei == ik
        pick_scores.append(jnp.sum(jnp.where(oh, scores, 0.0), axis=0, keepdims=True))
        masked = jnp.where(oh, -jnp.inf, masked)
        onehot_sum += oh.astype(F32)
        picks.append(ik)

    ti = lax.broadcasted_iota(I32, (tr, tr), 0)
    tj = lax.broadcasted_iota(I32, (tr, tr), 1)
    before = (ti < tj).astype(BF16)
    prior = jnp.dot(onehot_sum.astype(BF16), before, preferred_element_type=F32) + run_ref[...]

    total = pick_scores[0]
    for k in range(1, TOP_K):
        total += pick_scores[k]
    for k in range(TOP_K):
        idx_ref[k:k + 1, :] = picks[k].astype(I32)
        wt_ref[k:k + 1, :] = pick_scores[k] / total * ROUTED_SCALE
        rk = jnp.sum(jnp.where(ei == picks[k], prior, 0.0), axis=0, keepdims=True)
        rank_ref[k:k + 1, :] = rk.astype(I32)

    run_ref[...] += jnp.sum(onehot_sum, axis=1, keepdims=True)
    cnt_ref[...] = run_ref[...].astype(I32)


def _route(h2, router_wt_b, router_bias):
    T, D = h2.shape
    tr = TM_ROUTE
    E = N_EXPERTS
    row_spec = pl.BlockSpec((TOP_K, tr), lambda i: (0, i))
    return pl.pallas_call(
        _router_kernel,
        grid=(T // tr,),
        in_specs=[
            pl.BlockSpec((tr, D), lambda i: (i, 0)),
            pl.BlockSpec((E, D), lambda i: (0, 0)),
            pl.BlockSpec((E, 1), lambda i: (0, 0)),
        ],
        out_specs=[row_spec, row_spec, row_spec, pl.BlockSpec((E, 1), lambda i: (0, 0))],
        out_shape=[
            jax.ShapeDtypeStruct((TOP_K, T), I32),
            jax.ShapeDtypeStruct((TOP_K, T), F32),
            jax.ShapeDtypeStruct((TOP_K, T), I32),
            jax.ShapeDtypeStruct((E, 1), I32),
        ],
        scratch_shapes=[pltpu.VMEM((E, 1), F32)],
        compiler_params=_cparams(("arbitrary",)),
        name="router",
    )(h2, router_wt_b, router_bias.reshape(E, 1))


def _dispatch_kernel(pend_ref, h_ref, dest_ref, xg_ref, dest_s, zero_ref, sem_s, sem):
    i = pl.program_id(0)
    td = TM_DISPATCH
    blk = EXPERT_BLOCK

    def zero_copy(e):
        start = pl.multiple_of(pend_ref[e + 1] - blk, blk)
        return pltpu.make_async_copy(zero_ref, xg_ref.at[pl.ds(start, blk)], sem)

    @pl.when(i == 0)
    def _():
        zero_ref[...] = jnp.zeros_like(zero_ref)

        def start_zero(e, c):
            @pl.when(pend_ref[e + 1] > pend_ref[e])
            def _():
                zero_copy(e).start()
            return c

        def wait_zero(e, c):
            @pl.when(pend_ref[e + 1] > pend_ref[e])
            def _():
                zero_copy(e).wait()
            return c

        lax.fori_loop(0, N_EXPERTS, start_zero, 0)
        lax.fori_loop(0, N_EXPERTS, wait_zero, 0)

    cp = pltpu.make_async_copy(dest_ref, dest_s, sem_s)
    cp.start()
    cp.wait()

    def row_copy(t, k):
        return pltpu.make_async_copy(h_ref.at[t], xg_ref.at[dest_s[k, t]], sem)

    def start_rows(t8, c):
        base = pl.multiple_of(t8 * 8, 8)
        for j in range(8):
            for k in range(TOP_K):
                row_copy(base + j, k).start(priority=k % DMA_QUEUES)
        return c

    lax.fori_loop(0, td // 8, start_rows, 0)
    for k in range(TOP_K):
        pltpu.make_async_copy(h_ref, h_ref, sem).wait()


def _dispatch(h2p, dest, pend0, rows):
    T = h2p.shape[0]
    td = TM_DISPATCH
    tile = (ROW_TILE, LANES)
    return pl.pallas_call(
        _dispatch_kernel,
        grid_spec=pltpu.PrefetchScalarGridSpec(
            num_scalar_prefetch=1,
            grid=(T // td,),
            in_specs=[
                pl.BlockSpec((td,) + tile, lambda i, p: (i, 0, 0)),
                pl.BlockSpec((TOP_K, td), lambda i, p: (0, i)),
            ],
            out_specs=pl.BlockSpec(memory_space=pl.ANY),
            scratch_shapes=[
                pltpu.SMEM((TOP_K, td), I32),
                pltpu.VMEM((EXPERT_BLOCK,) + tile, U32),
                pltpu.SemaphoreType.DMA,
                pltpu.SemaphoreType.DMA,
            ],
        ),
        out_shape=jax.ShapeDtypeStruct((rows,) + tile, U32),
        compiler_params=_cparams(("arbitrary",)),
        name="dispatch",
    )(pend0, h2p, dest)


def _experts_kernel(first_ref, gidx_ref, used_ref, nused_ref, ngroups_ref,
                    x_ref, wg_hbm, wu_hbm, wd_hbm, y_ref,
                    wgu_f, wd_f, wgu_b, wd_b, issued, sem):
    i = pl.program_id(0)
    F = D_EXPERT

    def tensor_copy(e, s, t):
        if t == 0:
            return pltpu.make_async_copy(wg_hbm.at[e], wgu_f.at[s, 0], sem.at[s, 0])
        if t == 1:
            return pltpu.make_async_copy(wu_hbm.at[e], wgu_f.at[s, 1], sem.at[s, 1])
        return pltpu.make_async_copy(wd_hbm.at[e], wd_f.at[s], sem.at[s, 2])

    def issue_until(n):
        def body(q, carry):
            h = q // 3
            for t in range(3):
                @pl.when(q % 3 == t)
                def _():
                    tensor_copy(used_ref[h], h % 2, t).start(priority=DMA_QUEUES - 1)
            return carry

        lax.fori_loop(issued[0], n, body, 0)
        issued[0] = jnp.maximum(issued[0], n)

    @pl.when(i == 0)
    def _():
        issued[0] = 0

    @pl.when(i < nused_ref[0])
    def _():
        g = gidx_ref[i]

        @pl.when(first_ref[i] == 1)
        def _():
            issue_until(3 * (g + 1))
            s = g % 2
            for t in range(3):
                tensor_copy(used_ref[g], s, t).wait()
            wgu_b[:, :F] = wgu_f[s, 0].astype(BF16)
            wgu_b[:, F:] = wgu_f[s, 1].astype(BF16)
            wd_b[...] = wd_f[s].astype(BF16)

        cap = 3 * jnp.minimum(g + 3, ngroups_ref[0])
        issue_until(jnp.minimum(issued[0] + 1, cap))

        lo, hi = _unpack_halves(_load_row_tiles(x_ref, 0, EXPERT_BLOCK))
        x = jnp.concatenate([lo.astype(BF16), hi.astype(BF16)], axis=1)
        gu = jnp.dot(x, wgu_b[...], preferred_element_type=F32)
        a = (_silu(gu[:, :F]) * gu[:, F:]).astype(BF16)
        y = jnp.dot(a, wd_b[...], preferred_element_type=F32)
        _store_row_tiles(y_ref, 0, EXPERT_BLOCK, _pack_halves(y))


def _experts(xg, first, gidx, used_list, n_used, n_groups, w_gate, w_up, w_down):
    D = D_MODEL
    bm = EXPERT_BLOCK
    F = D_EXPERT
    nb = xg.shape[0] // (bm * ROW_TILE)
    row_map = lambda i, fi, gi, ul, nu, ng: (jnp.minimum(i, nu[0] - 1), 0)
    return pl.pallas_call(
        _experts_kernel,
        grid_spec=pltpu.PrefetchScalarGridSpec(
            num_scalar_prefetch=5,
            grid=(nb,),
            in_specs=[
                pl.BlockSpec((bm * ROW_TILE, LANES), row_map),
                pl.BlockSpec(memory_space=pl.ANY),
                pl.BlockSpec(memory_space=pl.ANY),
                pl.BlockSpec(memory_space=pl.ANY),
            ],
            out_specs=pl.BlockSpec((bm * ROW_TILE, LANES), row_map),
            scratch_shapes=[
                pltpu.VMEM((2, 2, D, F), F32), pltpu.VMEM((2, F, D), F32),
                pltpu.VMEM((D, 2 * F), BF16), pltpu.VMEM((F, D), BF16),
                pltpu.SMEM((1,), I32),
                pltpu.SemaphoreType.DMA((2, 3)),
            ],
        ),
        out_shape=jax.ShapeDtypeStruct(xg.shape, U32),
        compiler_params=_cparams(("arbitrary",)),
        name="experts",
    )(first, gidx, used_list, n_used, n_groups, xg, w_gate, w_up, w_down)


def _combine_kernel(dest_ref, dnext_ref, wt_ref, x1_ref, h_ref, mod_ref, sg_ref, su_ref, sd_ref, yg_ref, o_ref,
                    dest_s, buf, sem_s, sem):
    i = pl.program_id(0)
    last = pl.num_programs(0) - 1
    tc = TM_COMBINE
    cur = i % 2
    nxt = 1 - cur

    def plane(slot, k):
        off = (slot * TOP_K + k) * (tc * ROW_TILE)
        return off if isinstance(off, int) else pl.multiple_of(off, tc * ROW_TILE)

    def row_copy(slot, t, k):
        return pltpu.make_async_copy(yg_ref.at[dest_s[slot, k, t]],
                                     buf.at[pl.ds(plane(slot, k) + t * ROW_TILE, ROW_TILE), :], sem.at[slot])

    def load_dest(src_ref, slot):
        cp = pltpu.make_async_copy(src_ref, dest_s.at[slot], sem_s)
        cp.start()
        cp.wait()

    def wait_rows(slot):
        whole = buf.at[pl.ds(plane(slot, 0), TOP_K * tc * ROW_TILE), :]
        pltpu.make_async_copy(whole, whole, sem.at[slot]).wait()

    @pl.when(i == 0)
    def _():
        load_dest(dest_ref, 0)

        def body(t, c):
            for k in range(TOP_K):
                row_copy(0, t, k).start(priority=k % DMA_QUEUES)
            return c

        lax.fori_loop(0, tc, body, 0)

    load_dest(dnext_ref, nxt)
    for t in range(tc):
        for k in range(TOP_K):
            row_copy(nxt, t, k).start(priority=k % DMA_QUEUES)

    hb = h_ref[...]
    g = jnp.dot(hb, sg_ref[...], preferred_element_type=F32)
    u = jnp.dot(hb, su_ref[...], preferred_element_type=F32)
    ffn = jnp.dot((_silu(g) * u).astype(BF16), sd_ref[...], preferred_element_type=F32)

    wait_rows(cur)
    half = D_MODEL // 2
    ffn_lo, ffn_hi = ffn[:, :half], ffn[:, half:]
    for k in range(TOP_K):
        lo, hi = _unpack_halves(_load_row_tiles(buf, plane(cur, k), tc))
        w = wt_ref[:, k:k + 1]
        ffn_lo += lo * w
        ffn_hi += hi * w
    o_ref[:, :half] = x1_ref[:, :half] + mod_ref[0, 5:6, :half] * ffn_lo
    o_ref[:, half:] = x1_ref[:, half:] + mod_ref[0, 5:6, half:] * ffn_hi

    @pl.when(i == last)
    def _():
        wait_rows(nxt)


def _combine(dest, wt_t, x1, h2, mod3, sg_b, su_b, sd_b, yg):
    T, D = x1.shape
    tc = TM_COMBINE
    per_batch = SEQ // tc
    F = D_EXPERT
    return pl.pallas_call(
        _combine_kernel,
        grid=(T // tc,),
        in_specs=[
            pl.BlockSpec((TOP_K, tc), lambda i: (0, i)),
            pl.BlockSpec((TOP_K, tc), lambda i: (0, jnp.minimum(i + 1, T // tc - 1))),
            pl.BlockSpec((tc, TOP_K), lambda i: (i, 0)),
            pl.BlockSpec((tc, D), lambda i: (i, 0)),
            pl.BlockSpec((tc, D), lambda i: (i, 0)),
            pl.BlockSpec((1, 6, D), lambda i: (i // per_batch, 0, 0)),
            pl.BlockSpec((D, F), lambda i: (0, 0)),
            pl.BlockSpec((D, F), lambda i: (0, 0)),
            pl.BlockSpec((F, D), lambda i: (0, 0)),
            pl.BlockSpec(memory_space=pl.ANY),
        ],
        out_specs=pl.BlockSpec((tc, D), lambda i: (i, 0)),
        out_shape=jax.ShapeDtypeStruct((T, D), F32),
        scratch_shapes=[
            pltpu.SMEM((2, TOP_K, tc), I32),
            pltpu.VMEM((2 * TOP_K * tc * ROW_TILE, LANES), U32),
            pltpu.SemaphoreType.DMA,
            pltpu.SemaphoreType.DMA((2,)),
        ],
        compiler_params=_cparams(("arbitrary",)),
        name="combine",
    )(dest, dest, wt_t, x1, h2, mod3, sg_b, su_b, sd_b, yg)


def kernel(x, c, ada_w, ada_b, mix_norm_g, ffn_norm_g, w_in, q_norm_g, k_norm_g, rel_bias, sgu_ln_g, sgu_ln_b, sgu_w, sgu_b, w_out, router_w, router_bias, shared_w_gate, shared_w_up, shared_w_down, expert_w_gate, expert_w_up, expert_w_down):
    B, S, D = x.shape
    assert S == SEQ and D == D_MODEL and ada_w.shape[0] == 1
    T = B * S
    x2 = x.reshape(T, D)

    mod3 = _modulation(c, ada_w[0], ada_b[0]).reshape(B, 6, D)

    proj = _in_projection(x2, mod3, mix_norm_g, w_in[0].astype(BF16), q_norm_g, k_norm_g)
    attn = _attention(proj.reshape(B, S, D_IN_PROJ), _bias_tables(rel_bias)).reshape(T, D_ATTN)
    causal = jnp.tril(jnp.ones((SGU_CHUNK, SGU_CHUNK), F32))
    sgu = _spatial_gating(proj, (sgu_w[0] * causal).astype(BF16), sgu_ln_g, sgu_ln_b, sgu_b[0].T)
    x1, h2, h2p = _out_projection(attn, sgu, x2, mod3, ffn_norm_g, w_out[0].astype(BF16))

    idx, wts, rank, counts = _route(h2, router_w[0].T.astype(BF16), router_bias[0])

    bm = EXPERT_BLOCK
    counts = counts.reshape(N_EXPERTS)
    padded = (counts + bm - 1) // bm * bm
    pends = jnp.cumsum(padded)
    pstarts = pends - padded
    n_blocks = T * TOP_K // bm + N_EXPERTS
    block_start = jnp.arange(n_blocks, dtype=I32) * bm
    block_exp = jnp.minimum(jnp.sum((pends[None, :] <= block_start[:, None]).astype(I32), axis=1), N_EXPERTS - 1)
    n_used = (pends[-1] // bm).astype(I32).reshape(1)
    eids = jnp.arange(N_EXPERTS, dtype=I32)
    dest = jnp.sum(jnp.where(idx[:, :, None] == eids, pstarts.astype(I32), 0), axis=-1) + rank
    pend0 = jnp.concatenate([jnp.zeros((1,), I32), pends.astype(I32)])
    first = jnp.concatenate([jnp.ones((1,), I32), (block_exp[1:] != block_exp[:-1]).astype(I32)])
    used = counts > 0
    ordinal = jnp.cumsum(used.astype(I32)) - 1
    n_groups = jnp.sum(used.astype(I32)).reshape(1)
    used_list = jnp.sum(jnp.where(used[None, :] & (ordinal[None, :] == eids[:, None]), eids[None, :], 0), axis=-1)
    gidx = jnp.sum(jnp.where(block_exp[:, None] == eids, ordinal, 0), axis=-1)

    n_rows = n_blocks * bm
    xg = _dispatch(h2p.reshape(T, ROW_TILE, LANES), dest, pend0, n_rows)
    yg = _experts(xg.reshape(n_rows * ROW_TILE, LANES), first, gidx.astype(I32), used_list.astype(I32),
                  n_used, n_groups, expert_w_gate[0], expert_w_up[0], expert_w_down[0])
    out = _combine(dest, wts.T, x1, h2, mod3, shared_w_gate[0].astype(BF16), shared_w_up[0].astype(BF16),
                   shared_w_down[0].astype(BF16), yg.reshape(n_rows, ROW_TILE, LANES))
    return out.reshape(B, S, D)
```

```python
import functools
import math

import numpy as np
import jax
import jax.numpy as jnp
from jax import lax
from jax.experimental import pallas as pl
from jax.experimental.pallas import tpu as pltpu

F32 = jnp.float32
BF16 = jnp.bfloat16
I32 = jnp.int32
U32 = jnp.uint32

D_MODEL = 2048
SEQ = 2048
HEAD_DIM = 128
N_HEADS = 8
D_ATTN = N_HEADS * HEAD_DIM
D_SGU = 1024
N_SGU_GROUPS = 8
SGU_CHUNK = 128
D_IN_PROJ = 3 * D_ATTN + 2 * D_SGU
DILATED_GROUPS = ((128, 1), (512, 4), (2048, 16))
BAND = 128
NUM_REL_BUCKETS = 32
REL_MAX_DISTANCE = 2048
N_EXPERTS = 256
TOP_K = 8
N_EXPERT_GROUPS = 8
GROUP_SIZE = N_EXPERTS // N_EXPERT_GROUPS
TOPK_GROUPS = 4
D_EXPERT = 512
ROUTED_SCALE = 2.5
NORM_EPS = 1e-6
MASK_VALUE = -1e30

LANES = 128
ROW_TILE = D_MODEL // 2 // LANES
VMEM_LIMIT = 56 * 1024 * 1024
DMA_QUEUES = 2

TM_PROJ = 512
TM_SGU = 512
TM_OUT = 256
TM_ROUTE = 512
TM_DISPATCH = 256
TM_COMBINE = 128
EXPERT_BLOCK = 128


def _cparams(sem):
    return pltpu.CompilerParams(dimension_semantics=sem, vmem_limit_bytes=VMEM_LIMIT)


def _silu(v):
    return v * jax.nn.sigmoid(v)


def _gelu(v):
    return 0.5 * v * (1.0 + lax.erf(v * (1.0 / math.sqrt(2.0))))


def _pack_halves(v):
    h = v.shape[1] // 2
    lo = lax.bitcast_convert_type(v[:, :h].astype(BF16).astype(F32), U32)
    hi = lax.bitcast_convert_type(v[:, h:].astype(BF16).astype(F32), U32)
    return (hi & jnp.uint32(0xFFFF0000)) | (lo >> 16)


def _unpack_halves(w):
    lo = lax.bitcast_convert_type(w << 16, F32)
    hi = lax.bitcast_convert_type(w & jnp.uint32(0xFFFF0000), F32)
    return lo, hi


def _store_row_tiles(ref, base, rows, packed):
    for s in range(ROW_TILE):
        ref[pl.ds(base + s, rows, stride=ROW_TILE), :] = packed[:, s * LANES:(s + 1) * LANES]


def _load_row_tiles(ref, base, rows):
    return jnp.concatenate([ref[pl.ds(base + s, rows, stride=ROW_TILE), :] for s in range(ROW_TILE)], axis=1)


def _mod_kernel(c_ref, w_ref, b_ref, o_ref):
    ca = _silu(c_ref[...]).astype(BF16)
    o_ref[...] = jnp.dot(ca, w_ref[...].astype(BF16), preferred_element_type=F32) + b_ref[...]


def _modulation(c, ada_w, ada_b):
    B, D = c.shape
    N = ada_w.shape[1]
    tn = 1024
    return pl.pallas_call(
        _mod_kernel,
        grid=(N // tn,),
        in_specs=[
            pl.BlockSpec((B, D), lambda j: (0, 0)),
            pl.BlockSpec((D, tn), lambda j: (0, j)),
            pl.BlockSpec((1, tn), lambda j: (0, j)),
        ],
        out_specs=pl.BlockSpec((B, tn), lambda j: (0, j)),
        out_shape=jax.ShapeDtypeStruct((B, N), F32),
        compiler_params=_cparams(("arbitrary",)),
        name="modulation",
    )(c, ada_w, ada_b.reshape(1, N))


def _inproj_kernel(x_ref, mod_ref, g_ref, w_ref, qg_ref, kg_ref, o_ref, h_ref):
    j = pl.program_id(1)

    @pl.when(j == 0)
    def _():
        x = x_ref[...]
        r = lax.rsqrt(jnp.mean(x * x, axis=-1, keepdims=True) + NORM_EPS)
        h = (x * r) * g_ref[...] * (1.0 + mod_ref[0, 1:2, :]) + mod_ref[0, 0:1, :]
        h_ref[...] = h.astype(BF16)

    acc = jnp.dot(h_ref[...], w_ref[...], preferred_element_type=F32)

    def head_norm(gain_ref, scale):
        for hd in range(N_HEADS):
            a = acc[:, hd * HEAD_DIM:(hd + 1) * HEAD_DIM]
            r = lax.rsqrt(jnp.mean(a * a, axis=-1, keepdims=True) + NORM_EPS)
            o_ref[:, hd * HEAD_DIM:(hd + 1) * HEAD_DIM] = ((a * r) * gain_ref[...] * scale).astype(BF16)

    @pl.when(j == 0)
    def _():
        head_norm(qg_ref, HEAD_DIM ** -0.5)

    @pl.when(j == 1)
    def _():
        head_norm(kg_ref, 1.0)

    @pl.when(j >= 2)
    def _():
        o_ref[...] = acc.astype(BF16)


def _in_projection(x2, mod3, norm_g, w_in_b, q_g, k_g):
    T, D = x2.shape
    tm, tn = TM_PROJ, D_ATTN
    per_batch = SEQ // tm
    return pl.pallas_call(
        _inproj_kernel,
        grid=(T // tm, D_IN_PROJ // tn),
        in_specs=[
            pl.BlockSpec((tm, D), lambda i, j: (i, 0)),
            pl.BlockSpec((1, 6, D), lambda i, j: (i // per_batch, 0, 0)),
            pl.BlockSpec((1, D), lambda i, j: (0, 0)),
            pl.BlockSpec((D, tn), lambda i, j: (0, j)),
            pl.BlockSpec((1, HEAD_DIM), lambda i, j: (0, 0)),
            pl.BlockSpec((1, HEAD_DIM), lambda i, j: (0, 0)),
        ],
        out_specs=pl.BlockSpec((tm, tn), lambda i, j: (i, j)),
        out_shape=jax.ShapeDtypeStruct((T, D_IN_PROJ), BF16),
        scratch_shapes=[pltpu.VMEM((tm, D), BF16)],
        compiler_params=_cparams(("arbitrary", "arbitrary")),
        name="in_projection",
    )(x2, mod3, norm_g, w_in_b, q_g, k_g)


def _bias_tables(rel_bias):
    n = BAND
    qi = jnp.arange(n)[:, None]
    ki = jnp.arange(2 * n)[None, :]
    steps = n + qi - ki
    in_band = (steps >= 0) & (steps <= n)
    max_exact = NUM_REL_BUCKETS // 2
    tabs = []
    for _, dilation in DILATED_GROUPS:
        dist = jnp.clip(steps, 0, n) * dilation
        nf = jnp.maximum(dist, 1).astype(F32)
        large = max_exact + (jnp.log(nf / max_exact) / math.log(REL_MAX_DISTANCE / max_exact)
                             * (NUM_REL_BUCKETS - max_exact)).astype(I32)
        large = jnp.minimum(large, NUM_REL_BUCKETS - 1)
        bucket = jnp.where(dist < max_exact, dist, large)
        onehot = jax.nn.one_hot(bucket, NUM_REL_BUCKETS, dtype=F32)
        b = jnp.einsum("qkb,bh->hqk", onehot, rel_bias.astype(F32), precision=lax.Precision.HIGHEST)
        tabs.append(jnp.where(in_band[None], b, MASK_VALUE))
    return jnp.stack(tabs)


def _attn_kernel(q_ref, k_ref, v_ref, bias_ref, o_ref, qf, kf, vf, q4, k4, v4,
                 o0, l0, o1, l1, o2, l2, stage):
    n = BAND
    quarter = SEQ // 4

    for src, nat, res in ((q_ref, qf, q4), (k_ref, kf, k4), (v_ref, vf, v4)):
        nat[...] = src[...].astype(F32)
        for r in range(4):
            res[r * quarter:(r + 1) * quarter, :] = nat[pl.ds(r, quarter, stride=4), :]

    def piece(qb, kb, vb, bias, o_out, l_out, rows):
        l = lax.dot_general(qb, kb, (((1,), (1,)), ((), ())), preferred_element_type=F32) + bias
        m = jnp.max(l, axis=-1, keepdims=True)
        p = jnp.exp(l - m)
        s = jnp.sum(p, axis=-1, keepdims=True)
        o = jnp.dot(p.astype(BF16), vb, preferred_element_type=F32)
        o_out[rows, :] = o / s
        l_out[rows, :] = jnp.broadcast_to(m + jnp.log(s), (n, LANES))

    for i in range(SEQ // n):
        rows = slice(i * n, (i + 1) * n)
        if i == 0:
            piece(q_ref[rows, :], k_ref[rows, :], v_ref[rows, :], bias_ref[0, :, n:], o0, l0, rows)
        else:
            krows = slice((i - 1) * n, (i + 1) * n)
            piece(q_ref[rows, :], k_ref[krows, :], v_ref[krows, :], bias_ref[0], o0, l0, rows)

    for r in range(4):
        for blk in range(quarter // n):
            base = r * quarter + blk * n
            rows = slice(base, base + n)
            krows = rows if blk == 0 else slice(base - n, base + n)
            bias = bias_ref[1, :, n:] if blk == 0 else bias_ref[1]
            piece(q4[rows, :].astype(BF16), k4[krows, :].astype(BF16), v4[krows, :].astype(BF16),
                  bias, o1, l1, rows)

    for r in range(4):
        for a in range(4):
            rows = pl.ds(r * quarter + a, n, stride=4)
            piece(q4[rows, :].astype(BF16), k4[rows, :].astype(BF16), v4[rows, :].astype(BF16),
                  bias_ref[2, :, n:], o2, l2, rows)

    for r in range(4):
        for c in range(quarter // n):
            rows = slice(r * quarter + c * n, r * quarter + (c + 1) * n)
            nat = pl.ds(r + 4 * c * n, n, stride=4)
            a0, a1, a2 = l0[nat, :], l1[rows, :], l2[rows, :]
            m = jnp.maximum(jnp.maximum(a0, a1), a2)
            e0, e1, e2 = jnp.exp(a0 - m), jnp.exp(a1 - m), jnp.exp(a2 - m)
            mix = (e0 * o0[nat, :] + e1 * o1[rows, :] + e2 * o2[rows, :]) / (e0 + e1 + e2)
            stage[nat, :] = mix
    o_ref[...] = stage[...].astype(BF16)


def _attention(proj3, bias_tabs):
    B = proj3.shape[0]
    blk = lambda off: pl.BlockSpec((None, SEQ, HEAD_DIM), lambda b, h: (b, 0, off + h))
    return pl.pallas_call(
        _attn_kernel,
        grid=(B, N_HEADS),
        in_specs=[
            blk(0), blk(N_HEADS), blk(2 * N_HEADS),
            pl.BlockSpec((3, None, BAND, 2 * BAND), lambda b, h: (0, h, 0, 0)),
        ],
        out_specs=pl.BlockSpec((None, SEQ, HEAD_DIM), lambda b, h: (b, 0, h)),
        out_shape=jax.ShapeDtypeStruct((B, SEQ, D_ATTN), BF16),
        scratch_shapes=[pltpu.VMEM((SEQ, HEAD_DIM), F32) for _ in range(13)],
        compiler_params=_cparams(("arbitrary", "arbitrary")),
        name="dilated_attention",
    )(proj3, proj3, proj3, bias_tabs)


def _sgu_kernel(u_ref, z_ref, w_ref, g_ref, b_ref, bs_ref, o_ref):
    n = SGU_CHUNK
    for c in range(TM_SGU // n):
        rs = slice(c * n, (c + 1) * n)
        for g in range(N_SGU_GROUPS):
            cs = slice(g * n, (g + 1) * n)
            z = _gelu(z_ref[rs, cs].astype(F32))
            mu = jnp.mean(z, axis=-1, keepdims=True)
            zc = z - mu
            var = jnp.mean(zc * zc, axis=-1, keepdims=True)
            zn = (zc * lax.rsqrt(var + NORM_EPS)) * g_ref[:, cs] + b_ref[:, cs]
            mixed = jnp.dot(w_ref[g], zn.astype(BF16), preferred_element_type=F32) + bs_ref[:, g:g + 1]
            u = _gelu(u_ref[rs, cs].astype(F32))
            o_ref[rs, cs] = (u * mixed).astype(BF16)


def _spatial_gating(proj, w_causal_b, ln_g, ln_b, bs_t):
    T = proj.shape[0]
    tm = TM_SGU
    ucol = 3 * D_ATTN // D_SGU
    return pl.pallas_call(
        _sgu_kernel,
        grid=(T // tm,),
        in_specs=[
            pl.BlockSpec((tm, D_SGU), lambda i: (i, ucol)),
            pl.BlockSpec((tm, D_SGU), lambda i: (i, ucol + 1)),
            pl.BlockSpec((N_SGU_GROUPS, SGU_CHUNK, SGU_CHUNK), lambda i: (0, 0, 0)),
            pl.BlockSpec((1, D_SGU), lambda i: (0, 0)),
            pl.BlockSpec((1, D_SGU), lambda i: (0, 0)),
            pl.BlockSpec((SGU_CHUNK, N_SGU_GROUPS), lambda i: (0, 0)),
        ],
        out_specs=pl.BlockSpec((tm, D_SGU), lambda i: (i, 0)),
        out_shape=jax.ShapeDtypeStruct((T, D_SGU), BF16),
        compiler_params=_cparams(("arbitrary",)),
        name="spatial_gating",
    )(proj, proj, w_causal_b, ln_g, ln_b, bs_t)


def _outproj_kernel(a_ref, s_ref, x_ref, mod_ref, g_ref, w_ref, x1_ref, h2_ref, h2p_ref):
    mixed = jnp.dot(a_ref[...], w_ref[0:D_ATTN, :], preferred_element_type=F32)
    mixed += jnp.dot(s_ref[...], w_ref[D_ATTN:, :], preferred_element_type=F32)
    x1 = x_ref[...] + mod_ref[0, 2:3, :] * mixed
    x1_ref[...] = x1
    r = lax.rsqrt(jnp.mean(x1 * x1, axis=-1, keepdims=True) + NORM_EPS)
    h2 = (x1 * r) * g_ref[...] * (1.0 + mod_ref[0, 4:5, :]) + mod_ref[0, 3:4, :]
    h2_ref[...] = h2.astype(BF16)
    _store_row_tiles(h2p_ref, 0, TM_OUT, _pack_halves(h2))


def _out_projection(attn, sgu, x2, mod3, norm_g, w_out_b):
    T, D = x2.shape
    tm = TM_OUT
    per_batch = SEQ // tm
    return pl.pallas_call(
        _outproj_kernel,
        grid=(T // tm,),
        in_specs=[
            pl.BlockSpec((tm, D_ATTN), lambda i: (i, 0)),
            pl.BlockSpec((tm, D_SGU), lambda i: (i, 0)),
            pl.BlockSpec((tm, D), lambda i: (i, 0)),
            pl.BlockSpec((1, 6, D), lambda i: (i // per_batch, 0, 0)),
            pl.BlockSpec((1, D), lambda i: (0, 0)),
            pl.BlockSpec((D_ATTN + D_SGU, D), lambda i: (0, 0)),
        ],
        out_specs=[pl.BlockSpec((tm, D), lambda i: (i, 0)), pl.BlockSpec((tm, D), lambda i: (i, 0)),
                   pl.BlockSpec((tm * ROW_TILE, LANES), lambda i: (i, 0))],
        out_shape=[jax.ShapeDtypeStruct((T, D), F32), jax.ShapeDtypeStruct((T, D), BF16),
                   jax.ShapeDtypeStruct((T * ROW_TILE, LANES), U32)],
        compiler_params=_cparams(("arbitrary",)),
        name="out_projection",
    )(attn, sgu, x2, mod3, norm_g, w_out_b)


def _router_kernel(h_ref, w_ref, b_ref, idx_ref, wt_ref, rank_ref, cnt_ref, run_ref):
    i = pl.program_id(0)
    tr = TM_ROUTE
    E = N_EXPERTS

    @pl.when(i == 0)
    def _():
        run_ref[...] = jnp.zeros_like(run_ref)

    logits = lax.dot_general(w_ref[...], h_ref[...], (((1,), (1,)), ((), ())), preferred_element_type=F32)
    scores = jax.nn.sigmoid(logits)
    sel = scores + b_ref[...]

    slabs = [sel[g * GROUP_SIZE:(g + 1) * GROUP_SIZE, :] for g in range(N_EXPERT_GROUPS)]
    si = lax.broadcasted_iota(I32, (GROUP_SIZE, tr), 0).astype(F32)
    gs = []
    for slab in slabs:
        m1 = jnp.max(slab, axis=0, keepdims=True)
        first = jnp.min(jnp.where(slab == m1, si, float(GROUP_SIZE)), axis=0, keepdims=True)
        m2 = jnp.max(jnp.where(si == first, -jnp.inf, slab), axis=0, keepdims=True)
        gs.append(m1 + m2)

    kept = []
    for g in range(N_EXPERT_GROUPS):
        beaten = jnp.zeros((1, tr), F32)
        for o in range(N_EXPERT_GROUPS):
            if o < g:
                beaten += (gs[o] >= gs[g]).astype(F32)
            elif o > g:
                beaten += (gs[o] > gs[g]).astype(F32)
        kept.append(jnp.where(beaten < TOPK_GROUPS, slabs[g], -jnp.inf))
    masked = jnp.concatenate(kept, axis=0)

    ei = lax.broadcasted_iota(I32, (E, tr), 0).astype(F32)
    picks, pick_scores = [], []
    onehot_sum = jnp.zeros((E, tr), F32)
    for k in range(TOP_K):
        m = jnp.max(masked, axis=0, keepdims=True)
        ik = jnp.min(jnp.where(masked == m, ei, float(E)), axis=0, keepdims=True)
        oh = ei == ik
        pick_scores.append(jnp.sum(jnp.where(oh, scores, 0.0), axis=0, keepdims=True))
        masked = jnp.where(oh, -jnp.inf, masked)
        onehot_sum += oh.astype(F32)
        picks.append(ik)

    ti = lax.broadcasted_iota(I32, (tr, tr), 0)
    tj = lax.broadcasted_iota(I32, (tr, tr), 1)
    before = (ti < tj).astype(BF16)
    prior = jnp.dot(onehot_sum.astype(BF16), before, preferred_element_type=F32) + run_ref[...]

    total = pick_scores[0]
    for k in range(1, TOP_K):
        total += pick_scores[k]
    for k in range(TOP_K):
        idx_ref[k:k + 1, :] = picks[k].astype(I32)
        wt_ref[k:k + 1, :] = pick_scores[k] / total * ROUTED_SCALE
        rk = jnp.sum(jnp.where(ei == picks[k], prior, 0.0), axis=0, keepdims=True)
        rank_ref[k:k + 1, :] = rk.astype(I32)

    run_ref[...] += jnp.sum(onehot_sum, axis=1, keepdims=True)
    cnt_ref[...] = run_ref[...].astype(I32)


def _route(h2, router_wt_b, router_bias):
    T, D = h2.shape
    tr = TM_ROUTE
    E = N_EXPERTS
    row_spec = pl.BlockSpec((TOP_K, tr), lambda i: (0, i))
    return pl.pallas_call(
        _router_kernel,
        grid=(T // tr,),
        in_specs=[
            pl.BlockSpec((tr, D), lambda i: (i, 0)),
            pl.BlockSpec((E, D), lambda i: (0, 0)),
            pl.BlockSpec((E, 1), lambda i: (0, 0)),
        ],
        out_specs=[row_spec, row_spec, row_spec, pl.BlockSpec((E, 1), lambda i: (0, 0))],
        out_shape=[
            jax.ShapeDtypeStruct((TOP_K, T), I32),
            jax.ShapeDtypeStruct((TOP_K, T), F32),
            jax.ShapeDtypeStruct((TOP_K, T), I32),
            jax.ShapeDtypeStruct((E, 1), I32),
        ],
        scratch_shapes=[pltpu.VMEM((E, 1), F32)],
        compiler_params=_cparams(("arbitrary",)),
        name="router",
    )(h2, router_wt_b, router_bias.reshape(E, 1))


def _dispatch_kernel(pend_ref, h_ref, dest_ref, xg_ref, dest_s, zero_ref, sem_s, sem):
    i = pl.program_id(0)
    td = TM_DISPATCH
    blk = EXPERT_BLOCK

    def zero_copy(e):
        start = pl.multiple_of(pend_ref[e + 1] - blk, blk)
        return pltpu.make_async_copy(zero_ref, xg_ref.at[pl.ds(start, blk)], sem)

    @pl.when(i == 0)
    def _():
        zero_ref[...] = jnp.zeros_like(zero_ref)

        def start_zero(e, c):
            @pl.when(pend_ref[e + 1] > pend_ref[e])
            def _():
                zero_copy(e).start()
            return c

        def wait_zero(e, c):
            @pl.when(pend_ref[e + 1] > pend_ref[e])
            def _():
                zero_copy(e).wait()
            return c

        lax.fori_loop(0, N_EXPERTS, start_zero, 0)
        lax.fori_loop(0, N_EXPERTS, wait_zero, 0)

        def tail_copy(b):
            return pltpu.make_async_copy(zero_ref, xg_ref.at[pl.ds(pl.multiple_of(b * blk, blk), blk)], sem)

        first_unused = pend_ref[N_EXPERTS] // blk
        n_blocks = xg_ref.shape[0] // blk
        lax.fori_loop(first_unused, n_blocks, lambda b, c: (tail_copy(b).start(), c)[1], 0)
        lax.fori_loop(first_unused, n_blocks, lambda b, c: (tail_copy(b).wait(), c)[1], 0)

    cp = pltpu.make_async_copy(dest_ref, dest_s, sem_s)
    cp.start()
    cp.wait()

    def row_copy(t, k):
        return pltpu.make_async_copy(h_ref.at[t], xg_ref.at[dest_s[k, t]], sem)

    def start_rows(t8, c):
        base = pl.multiple_of(t8 * 8, 8)
        for j in range(8):
            for k in range(TOP_K):
                row_copy(base + j, k).start(priority=k % DMA_QUEUES)
        return c

    lax.fori_loop(0, td // 8, start_rows, 0)
    for k in range(TOP_K):
        pltpu.make_async_copy(h_ref, h_ref, sem).wait()


def _dispatch(h2p, dest, pend0, rows):
    T = h2p.shape[0]
    td = TM_DISPATCH
    tile = (ROW_TILE, LANES)
    return pl.pallas_call(
        _dispatch_kernel,
        grid_spec=pltpu.PrefetchScalarGridSpec(
            num_scalar_prefetch=1,
            grid=(T // td,),
            in_specs=[
                pl.BlockSpec((td,) + tile, lambda i, p: (i, 0, 0)),
                pl.BlockSpec((TOP_K, td), lambda i, p: (0, i)),
            ],
            out_specs=pl.BlockSpec(memory_space=pl.ANY),
            scratch_shapes=[
                pltpu.SMEM((TOP_K, td), I32),
                pltpu.VMEM((EXPERT_BLOCK,) + tile, U32),
                pltpu.SemaphoreType.DMA,
                pltpu.SemaphoreType.DMA,
            ],
        ),
        out_shape=jax.ShapeDtypeStruct((rows,) + tile, U32),
        compiler_params=_cparams(("arbitrary",)),
        name="dispatch",
    )(pend0, h2p, dest)


def _experts_kernel(first_ref, gidx_ref, used_ref, nused_ref, ngroups_ref,
                    x_ref, wg_hbm, wu_hbm, wd_hbm, y_ref,
                    wgu_f, wd_f, wgu_b, wd_b, issued, sem):
    i = pl.program_id(0)
    F = D_EXPERT

    def tensor_copy(e, s, t):
        if t == 0:
            return pltpu.make_async_copy(wg_hbm.at[e], wgu_f.at[s, 0], sem.at[s, 0])
        if t == 1:
            return pltpu.make_async_copy(wu_hbm.at[e], wgu_f.at[s, 1], sem.at[s, 1])
        return pltpu.make_async_copy(wd_hbm.at[e], wd_f.at[s], sem.at[s, 2])

    def issue_until(n):
        def body(q, carry):
            h = q // 3
            for t in range(3):
                @pl.when(q % 3 == t)
                def _():
                    tensor_copy(used_ref[h], h % 2, t).start(priority=DMA_QUEUES - 1)
            return carry

        lax.fori_loop(issued[0], n, body, 0)
        issued[0] = jnp.maximum(issued[0], n)

    @pl.when(i == 0)
    def _():
        issued[0] = 0

    @pl.when(i < nused_ref[0])
    def _():
        g = gidx_ref[i]

        @pl.when(first_ref[i] == 1)
        def _():
            issue_until(3 * (g + 1))
            s = g % 2
            for t in range(3):
                tensor_copy(used_ref[g], s, t).wait()
            wgu_b[:, :F] = wgu_f[s, 0].astype(BF16)
            wgu_b[:, F:] = wgu_f[s, 1].astype(BF16)
            wd_b[...] = wd_f[s].astype(BF16)

        cap = 3 * jnp.minimum(g + 3, ngroups_ref[0])
        issue_until(jnp.minimum(issued[0] + 3, cap))

        lo, hi = _unpack_halves(_load_row_tiles(x_ref, 0, EXPERT_BLOCK))
        x = jnp.concatenate([lo.astype(BF16), hi.astype(BF16)], axis=1)
        gu = jnp.dot(x, wgu_b[...], preferred_element_type=F32)
        a = (_silu(gu[:, :F]) * gu[:, F:]).astype(BF16)
        y = jnp.dot(a, wd_b[...], preferred_element_type=F32)
        _store_row_tiles(y_ref, 0, EXPERT_BLOCK, _pack_halves(y))


def _experts(xg, first, gidx, used_list, n_used, n_groups, w_gate, w_up, w_down):
    D = D_MODEL
    bm = EXPERT_BLOCK
    F = D_EXPERT
    nb = xg.shape[0] // (bm * ROW_TILE)
    row_map = lambda i, fi, gi, ul, nu, ng: (jnp.minimum(i, nu[0] - 1), 0)
    return pl.pallas_call(
        _experts_kernel,
        grid_spec=pltpu.PrefetchScalarGridSpec(
            num_scalar_prefetch=5,
            grid=(nb,),
            in_specs=[
                pl.BlockSpec((bm * ROW_TILE, LANES), row_map),
                pl.BlockSpec(memory_space=pl.ANY),
                pl.BlockSpec(memory_space=pl.ANY),
                pl.BlockSpec(memory_space=pl.ANY),
            ],
            out_specs=pl.BlockSpec((bm * ROW_TILE, LANES), row_map),
            scratch_shapes=[
                pltpu.VMEM((2, 2, D, F), F32), pltpu.VMEM((2, F, D), F32),
                pltpu.VMEM((D, 2 * F), BF16), pltpu.VMEM((F, D), BF16),
                pltpu.SMEM((1,), I32),
                pltpu.SemaphoreType.DMA((2, 3)),
            ],
        ),
        out_shape=jax.ShapeDtypeStruct(xg.shape, U32),
        input_output_aliases={5: 0},
        compiler_params=_cparams(("arbitrary",)),
        name="experts",
    )(first, gidx, used_list, n_used, n_groups, xg, w_gate, w_up, w_down)


def _combine_kernel(dest_ref, dnext_ref, wt_ref, x1_ref, h_ref, mod_ref, sg_ref, su_ref, sd_ref, yg_ref, o_ref,
                    dest_s, buf, sem_s, sem):
    i = pl.program_id(0)
    last = pl.num_programs(0) - 1
    tc = TM_COMBINE
    cur = i % 2
    nxt = 1 - cur

    def plane(slot, k):
        off = (slot * TOP_K + k) * (tc * ROW_TILE)
        return off if isinstance(off, int) else pl.multiple_of(off, tc * ROW_TILE)

    def row_copy(slot, t, k):
        return pltpu.make_async_copy(yg_ref.at[dest_s[slot, k, t]],
                                     buf.at[pl.ds(plane(slot, k) + t * ROW_TILE, ROW_TILE), :], sem.at[slot])

    def load_dest(src_ref, slot):
        cp = pltpu.make_async_copy(src_ref, dest_s.at[slot], sem_s)
        cp.start()
        cp.wait()

    def wait_rows(slot):
        whole = buf.at[pl.ds(plane(slot, 0), TOP_K * tc * ROW_TILE), :]
        pltpu.make_async_copy(whole, whole, sem.at[slot]).wait()

    @pl.when(i == 0)
    def _():
        load_dest(dest_ref, 0)

        def body(t, c):
            for k in range(TOP_K):
                row_copy(0, t, k).start(priority=k % DMA_QUEUES)
            return c

        lax.fori_loop(0, tc, body, 0)

    load_dest(dnext_ref, nxt)
    for t in range(tc):
        for k in range(TOP_K):
            row_copy(nxt, t, k).start(priority=k % DMA_QUEUES)

    hb = h_ref[...]
    g = jnp.dot(hb, sg_ref[...], preferred_element_type=F32)
    u = jnp.dot(hb, su_ref[...], preferred_element_type=F32)
    ffn = jnp.dot((_silu(g) * u).astype(BF16), sd_ref[...], preferred_element_type=F32)

    wait_rows(cur)
    half = D_MODEL // 2
    ffn_lo, ffn_hi = ffn[:, :half], ffn[:, half:]
    for k in range(TOP_K):
        lo, hi = _unpack_halves(_load_row_tiles(buf, plane(cur, k), tc))
        w = wt_ref[:, k:k + 1]
        ffn_lo += lo * w
        ffn_hi += hi * w
    o_ref[:, :half] = x1_ref[:, :half] + mod_ref[0, 5:6, :half] * ffn_lo
    o_ref[:, half:] = x1_ref[:, half:] + mod_ref[0, 5:6, half:] * ffn_hi

    @pl.when(i == last)
    def _():
        wait_rows(nxt)


def _combine(dest, wt_t, x1, h2, mod3, sg_b, su_b, sd_b, yg):
    T, D = x1.shape
    tc = TM_COMBINE
    per_batch = SEQ // tc
    F = D_EXPERT
    return pl.pallas_call(
        _combine_kernel,
        grid=(T // tc,),
        in_specs=[
            pl.BlockSpec((TOP_K, tc), lambda i: (0, i)),
            pl.BlockSpec((TOP_K, tc), lambda i: (0, jnp.minimum(i + 1, T // tc - 1))),
            pl.BlockSpec((tc, TOP_K), lambda i: (i, 0)),
            pl.BlockSpec((tc, D), lambda i: (i, 0)),
            pl.BlockSpec((tc, D), lambda i: (i, 0)),
            pl.BlockSpec((1, 6, D), lambda i: (i // per_batch, 0, 0)),
            pl.BlockSpec((D, F), lambda i: (0, 0)),
            pl.BlockSpec((D, F), lambda i: (0, 0)),
            pl.BlockSpec((F, D), lambda i: (0, 0)),
            pl.BlockSpec(memory_space=pl.ANY),
        ],
        out_specs=pl.BlockSpec((tc, D), lambda i: (i, 0)),
        out_shape=jax.ShapeDtypeStruct((T, D), F32),
        scratch_shapes=[
            pltpu.SMEM((2, TOP_K, tc), I32),
            pltpu.VMEM((2 * TOP_K * tc * ROW_TILE, LANES), U32),
            pltpu.SemaphoreType.DMA,
            pltpu.SemaphoreType.DMA((2,)),
        ],
        compiler_params=_cparams(("arbitrary",)),
        name="combine",
    )(dest, dest, wt_t, x1, h2, mod3, sg_b, su_b, sd_b, yg)


def kernel(x, c, ada_w, ada_b, mix_norm_g, ffn_norm_g, w_in, q_norm_g, k_norm_g, rel_bias, sgu_ln_g, sgu_ln_b, sgu_w, sgu_b, w_out, router_w, router_bias, shared_w_gate, shared_w_up, shared_w_down, expert_w_gate, expert_w_up, expert_w_down):
    B, S, D = x.shape
    assert S == SEQ and D == D_MODEL and ada_w.shape[0] == 1
    T = B * S
    x2 = x.reshape(T, D)

    mod3 = _modulation(c, ada_w[0], ada_b[0]).reshape(B, 6, D)

    proj = _in_projection(x2, mod3, mix_norm_g, w_in[0].astype(BF16), q_norm_g, k_norm_g)
    attn = _attention(proj.reshape(B, S, D_IN_PROJ), _bias_tables(rel_bias)).reshape(T, D_ATTN)
    causal = jnp.tril(jnp.ones((SGU_CHUNK, SGU_CHUNK), F32))
    sgu = _spatial_gating(proj, (sgu_w[0] * causal).astype(BF16), sgu_ln_g, sgu_ln_b, sgu_b[0].T)
    x1, h2, h2p = _out_projection(attn, sgu, x2, mod3, ffn_norm_g, w_out[0].astype(BF16))

    idx, wts, rank, counts = _route(h2, router_w[0].T.astype(BF16), router_bias[0])

    bm = EXPERT_BLOCK
    counts = counts.reshape(N_EXPERTS)
    padded = (counts + bm - 1) // bm * bm
    pends = jnp.cumsum(padded)
    pstarts = pends - padded
    n_blocks = T * TOP_K // bm + N_EXPERTS
    block_start = jnp.arange(n_blocks, dtype=I32) * bm
    block_exp = jnp.minimum(jnp.sum((pends[None, :] <= block_start[:, None]).astype(I32), axis=1), N_EXPERTS - 1)
    n_used = (pends[-1] // bm).astype(I32).reshape(1)
    eids = jnp.arange(N_EXPERTS, dtype=I32)
    dest = jnp.sum(jnp.where(idx[:, :, None] == eids, pstarts.astype(I32), 0), axis=-1) + rank
    pend0 = jnp.concatenate([jnp.zeros((1,), I32), pends.astype(I32)])
    first = jnp.concatenate([jnp.ones((1,), I32), (block_exp[1:] != block_exp[:-1]).astype(I32)])
    used = counts > 0
    ordinal = jnp.cumsum(used.astype(I32)) - 1
    n_groups = jnp.sum(used.astype(I32)).reshape(1)
    used_list = jnp.sum(jnp.where(used[None, :] & (ordinal[None, :] == eids[:, None]), eids[None, :], 0), axis=-1)
    gidx = jnp.sum(jnp.where(block_exp[:, None] == eids, ordinal, 0), axis=-1)

    n_rows = n_blocks * bm
    xg = _dispatch(h2p.reshape(T, ROW_TILE, LANES), dest, pend0, n_rows)
    yg = _experts(xg.reshape(n_rows * ROW_TILE, LANES), first, gidx.astype(I32), used_list.astype(I32),
                  n_used, n_groups, expert_w_gate[0], expert_w_up[0], expert_w_down[0])
    out = _combine(dest, wts.T, x1, h2, mod3, shared_w_gate[0].astype(BF16), shared_w_up[0].astype(BF16),
                   shared_w_down[0].astype(BF16), yg.reshape(n_rows, ROW_TILE, LANES))
    return out.reshape(B, S, D)
```

```python
import functools
import math

import numpy as np
import jax
import jax.numpy as jnp
from jax import lax
from jax.experimental import pallas as pl
from jax.experimental.pallas import tpu as pltpu

F32 = jnp.float32
BF16 = jnp.bfloat16
I32 = jnp.int32
U32 = jnp.uint32

D_MODEL = 2048
SEQ = 2048
HEAD_DIM = 128
N_HEADS = 8
D_ATTN = N_HEADS * HEAD_DIM
D_SGU = 1024
N_SGU_GROUPS = 8
SGU_CHUNK = 128
D_IN_PROJ = 3 * D_ATTN + 2 * D_SGU
DILATED_GROUPS = ((128, 1), (512, 4), (2048, 16))
BAND = 128
NUM_REL_BUCKETS = 32
REL_MAX_DISTANCE = 2048
N_EXPERTS = 256
TOP_K = 8
N_EXPERT_GROUPS = 8
GROUP_SIZE = N_EXPERTS // N_EXPERT_GROUPS
TOPK_GROUPS = 4
D_EXPERT = 512
ROUTED_SCALE = 2.5
NORM_EPS = 1e-6
MASK_VALUE = -1e30

LANES = 128
ROW_TILE = D_MODEL // 2 // LANES
VMEM_LIMIT = 56 * 1024 * 1024
DMA_QUEUES = 2

TM_PROJ = 512
TM_SGU = 512
TM_OUT = 256
TM_ROUTE = 512
TM_DISPATCH = 256
TM_COMBINE = 128
EXPERT_BLOCK = 128
ATTN_PIECES_IN_FLIGHT = 4


def _cparams(sem):
    return pltpu.CompilerParams(dimension_semantics=sem, vmem_limit_bytes=VMEM_LIMIT)


def _silu(v):
    return v * jax.nn.sigmoid(v)


def _gelu(v):
    return 0.5 * v * (1.0 + lax.erf(v * (1.0 / math.sqrt(2.0))))


def _pack_halves(v):
    h = v.shape[1] // 2
    lo = lax.bitcast_convert_type(v[:, :h].astype(BF16).astype(F32), U32)
    hi = lax.bitcast_convert_type(v[:, h:].astype(BF16).astype(F32), U32)
    return (hi & jnp.uint32(0xFFFF0000)) | (lo >> 16)


def _unpack_halves(w):
    lo = lax.bitcast_convert_type(w << 16, F32)
    hi = lax.bitcast_convert_type(w & jnp.uint32(0xFFFF0000), F32)
    return lo, hi


def _store_row_tiles(ref, base, rows, packed):
    for s in range(ROW_TILE):
        ref[pl.ds(base + s, rows, stride=ROW_TILE), :] = packed[:, s * LANES:(s + 1) * LANES]


def _load_row_tiles(ref, base, rows):
    return jnp.concatenate([ref[pl.ds(base + s, rows, stride=ROW_TILE), :] for s in range(ROW_TILE)], axis=1)


def _mod_kernel(c_ref, w_ref, b_ref, o_ref):
    ca = _silu(c_ref[...]).astype(BF16)
    o_ref[...] = jnp.dot(ca, w_ref[...].astype(BF16), preferred_element_type=F32) + b_ref[...]


def _modulation(c, ada_w, ada_b):
    B, D = c.shape
    N = ada_w.shape[1]
    tn = 1024
    return pl.pallas_call(
        _mod_kernel,
        grid=(N // tn,),
        in_specs=[
            pl.BlockSpec((B, D), lambda j: (0, 0)),
            pl.BlockSpec((D, tn), lambda j: (0, j)),
            pl.BlockSpec((1, tn), lambda j: (0, j)),
        ],
        out_specs=pl.BlockSpec((B, tn), lambda j: (0, j)),
        out_shape=jax.ShapeDtypeStruct((B, N), F32),
        compiler_params=_cparams(("arbitrary",)),
        name="modulation",
    )(c, ada_w, ada_b.reshape(1, N))


def _inproj_kernel(x_ref, mod_ref, g_ref, w_ref, qg_ref, kg_ref, o_ref, h_ref):
    j = pl.program_id(1)

    @pl.when(j == 0)
    def _():
        x = x_ref[...]
        r = lax.rsqrt(jnp.mean(x * x, axis=-1, keepdims=True) + NORM_EPS)
        h = (x * r) * g_ref[...] * (1.0 + mod_ref[0, 1:2, :]) + mod_ref[0, 0:1, :]
        h_ref[...] = h.astype(BF16)

    acc = jnp.dot(h_ref[...], w_ref[...], preferred_element_type=F32)

    def head_norm(gain_ref, scale):
        for hd in range(N_HEADS):
            a = acc[:, hd * HEAD_DIM:(hd + 1) * HEAD_DIM]
            r = lax.rsqrt(jnp.mean(a * a, axis=-1, keepdims=True) + NORM_EPS)
            o_ref[:, hd * HEAD_DIM:(hd + 1) * HEAD_DIM] = ((a * r) * gain_ref[...] * scale).astype(BF16)

    @pl.when(j == 0)
    def _():
        head_norm(qg_ref, HEAD_DIM ** -0.5)

    @pl.when(j == 1)
    def _():
        head_norm(kg_ref, 1.0)

    @pl.when(j >= 2)
    def _():
        o_ref[...] = acc.astype(BF16)


def _in_projection(x2, mod3, norm_g, w_in_b, q_g, k_g):
    T, D = x2.shape
    tm, tn = TM_PROJ, D_ATTN
    per_batch = SEQ // tm
    return pl.pallas_call(
        _inproj_kernel,
        grid=(T // tm, D_IN_PROJ // tn),
        in_specs=[
            pl.BlockSpec((tm, D), lambda i, j: (i, 0)),
            pl.BlockSpec((1, 6, D), lambda i, j: (i // per_batch, 0, 0)),
            pl.BlockSpec((1, D), lambda i, j: (0, 0)),
            pl.BlockSpec((D, tn), lambda i, j: (0, j)),
            pl.BlockSpec((1, HEAD_DIM), lambda i, j: (0, 0)),
            pl.BlockSpec((1, HEAD_DIM), lambda i, j: (0, 0)),
        ],
        out_specs=pl.BlockSpec((tm, tn), lambda i, j: (i, j)),
        out_shape=jax.ShapeDtypeStruct((T, D_IN_PROJ), BF16),
        scratch_shapes=[pltpu.VMEM((tm, D), BF16)],
        compiler_params=_cparams(("arbitrary", "arbitrary")),
        name="in_projection",
    )(x2, mod3, norm_g, w_in_b, q_g, k_g)


def _bias_tables(rel_bias):
    n = BAND
    qi = jnp.arange(n)[:, None]
    ki = jnp.arange(2 * n)[None, :]
    steps = n + qi - ki
    in_band = (steps >= 0) & (steps <= n)
    max_exact = NUM_REL_BUCKETS // 2
    tabs = []
    for _, dilation in DILATED_GROUPS:
        dist = jnp.clip(steps, 0, n) * dilation
        nf = jnp.maximum(dist, 1).astype(F32)
        large = max_exact + (jnp.log(nf / max_exact) / math.log(REL_MAX_DISTANCE / max_exact)
                             * (NUM_REL_BUCKETS - max_exact)).astype(I32)
        large = jnp.minimum(large, NUM_REL_BUCKETS - 1)
        bucket = jnp.where(dist < max_exact, dist, large)
        onehot = jax.nn.one_hot(bucket, NUM_REL_BUCKETS, dtype=F32)
        b = jnp.einsum("qkb,bh->hqk", onehot, rel_bias.astype(F32), precision=lax.Precision.HIGHEST)
        tabs.append(jnp.where(in_band[None], b, MASK_VALUE))
    return jnp.stack(tabs)


def _attn_kernel(q_ref, k_ref, v_ref, bias_ref, o_ref, qf, kf, vf, q4, k4, v4,
                 o0, l0, o1, l1, o2, l2, stage):
    n = BAND
    quarter = SEQ // 4

    for src, nat, res in ((q_ref, qf, q4), (k_ref, kf, k4), (v_ref, vf, v4)):
        nat[...] = src[...].astype(F32)
        for r in range(4):
            res[r * quarter:(r + 1) * quarter, :] = nat[pl.ds(r, quarter, stride=4), :]

    def logits(qb, kb, bias):
        return lax.dot_general(qb, kb, (((1,), (1,)), ((), ())), preferred_element_type=F32) + bias

    def finish(l, vb, o_out, l_out, rows):
        m = jnp.max(l, axis=-1, keepdims=True)
        p = jnp.exp(l - m)
        s = jnp.sum(p, axis=-1, keepdims=True)
        o = jnp.dot(p.astype(BF16), vb, preferred_element_type=F32)
        o_out[rows, :] = o / s
        l_out[rows, :] = jnp.broadcast_to(m + jnp.log(s), (n, LANES))

    pieces = []

    for i in range(SEQ // n):
        rows = slice(i * n, (i + 1) * n)
        krows = rows if i == 0 else slice((i - 1) * n, (i + 1) * n)
        bias = (0, slice(None), slice(n, None)) if i == 0 else (0,)
        pieces.append((q_ref, k_ref, v_ref, rows, krows, bias, o0, l0))

    for r in range(4):
        for blk in range(quarter // n):
            base = r * quarter + blk * n
            rows = slice(base, base + n)
            krows = rows if blk == 0 else slice(base - n, base + n)
            bias = (1, slice(None), slice(n, None)) if blk == 0 else (1,)
            pieces.append((q4, k4, v4, rows, krows, bias, o1, l1))

    for r in range(4):
        for a in range(4):
            rows = pl.ds(r * quarter + a, n, stride=4)
            pieces.append((q4, k4, v4, rows, rows, (2, slice(None), slice(n, None)), o2, l2))

    pending = []
    for qs, ks, vs, rows, krows, bias, o_out, l_out in pieces:
        l = logits(qs[rows, :].astype(BF16), ks[krows, :].astype(BF16), bias_ref[bias])
        if len(pending) == ATTN_PIECES_IN_FLIGHT:
            finish(*pending.pop(0))
        pending.append((l, vs[krows, :].astype(BF16), o_out, l_out, rows))
    for args in pending:
        finish(*args)

    for r in range(4):
        for c in range(quarter // n):
            rows = slice(r * quarter + c * n, r * quarter + (c + 1) * n)
            nat = pl.ds(r + 4 * c * n, n, stride=4)
            a0, a1, a2 = l0[nat, :], l1[rows, :], l2[rows, :]
            m = jnp.maximum(jnp.maximum(a0, a1), a2)
            e0, e1, e2 = jnp.exp(a0 - m), jnp.exp(a1 - m), jnp.exp(a2 - m)
            mix = (e0 * o0[nat, :] + e1 * o1[rows, :] + e2 * o2[rows, :]) / (e0 + e1 + e2)
            stage[nat, :] = mix
    o_ref[...] = stage[...].astype(BF16)


def _attention(proj3, bias_tabs):
    B = proj3.shape[0]
    blk = lambda off: pl.BlockSpec((None, SEQ, HEAD_DIM), lambda b, h: (b, 0, off + h))
    return pl.pallas_call(
        _attn_kernel,
        grid=(B, N_HEADS),
        in_specs=[
            blk(0), blk(N_HEADS), blk(2 * N_HEADS),
            pl.BlockSpec((3, None, BAND, 2 * BAND), lambda b, h: (0, h, 0, 0)),
        ],
        out_specs=pl.BlockSpec((None, SEQ, HEAD_DIM), lambda b, h: (b, 0, h)),
        out_shape=jax.ShapeDtypeStruct((B, SEQ, D_ATTN), BF16),
        scratch_shapes=[pltpu.VMEM((SEQ, HEAD_DIM), F32) for _ in range(13)],
        compiler_params=_cparams(("arbitrary", "arbitrary")),
        name="dilated_attention",
    )(proj3, proj3, proj3, bias_tabs)


def _sgu_kernel(u_ref, z_ref, w_ref, g_ref, b_ref, bs_ref, o_ref):
    n = SGU_CHUNK
    for c in range(TM_SGU // n):
        rs = slice(c * n, (c + 1) * n)
        for g in range(N_SGU_GROUPS):
            cs = slice(g * n, (g + 1) * n)
            z = _gelu(z_ref[rs, cs].astype(F32))
            mu = jnp.mean(z, axis=-1, keepdims=True)
            zc = z - mu
            var = jnp.mean(zc * zc, axis=-1, keepdims=True)
            zn = (zc * lax.rsqrt(var + NORM_EPS)) * g_ref[:, cs] + b_ref[:, cs]
            mixed = jnp.dot(w_ref[g], zn.astype(BF16), preferred_element_type=F32) + bs_ref[:, g:g + 1]
            u = _gelu(u_ref[rs, cs].astype(F32))
            o_ref[rs, cs] = (u * mixed).astype(BF16)


def _spatial_gating(proj, w_causal_b, ln_g, ln_b, bs_t):
    T = proj.shape[0]
    tm = TM_SGU
    ucol = 3 * D_ATTN // D_SGU
    return pl.pallas_call(
        _sgu_kernel,
        grid=(T // tm,),
        in_specs=[
            pl.BlockSpec((tm, D_SGU), lambda i: (i, ucol)),
            pl.BlockSpec((tm, D_SGU), lambda i: (i, ucol + 1)),
            pl.BlockSpec((N_SGU_GROUPS, SGU_CHUNK, SGU_CHUNK), lambda i: (0, 0, 0)),
            pl.BlockSpec((1, D_SGU), lambda i: (0, 0)),
            pl.BlockSpec((1, D_SGU), lambda i: (0, 0)),
            pl.BlockSpec((SGU_CHUNK, N_SGU_GROUPS), lambda i: (0, 0)),
        ],
        out_specs=pl.BlockSpec((tm, D_SGU), lambda i: (i, 0)),
        out_shape=jax.ShapeDtypeStruct((T, D_SGU), BF16),
        compiler_params=_cparams(("arbitrary",)),
        name="spatial_gating",
    )(proj, proj, w_causal_b, ln_g, ln_b, bs_t)


def _outproj_kernel(a_ref, s_ref, x_ref, mod_ref, g_ref, w_ref, x1_ref, h2_ref, h2p_ref):
    mixed = jnp.dot(a_ref[...], w_ref[0:D_ATTN, :], preferred_element_type=F32)
    mixed += jnp.dot(s_ref[...], w_ref[D_ATTN:, :], preferred_element_type=F32)
    x1 = x_ref[...] + mod_ref[0, 2:3, :] * mixed
    x1_ref[...] = x1
    r = lax.rsqrt(jnp.mean(x1 * x1, axis=-1, keepdims=True) + NORM_EPS)
    h2 = (x1 * r) * g_ref[...] * (1.0 + mod_ref[0, 4:5, :]) + mod_ref[0, 3:4, :]
    h2_ref[...] = h2.astype(BF16)
    _store_row_tiles(h2p_ref, 0, TM_OUT, _pack_halves(h2))


def _out_projection(attn, sgu, x2, mod3, norm_g, w_out_b):
    T, D = x2.shape
    tm = TM_OUT
    per_batch = SEQ // tm
    return pl.pallas_call(
        _outproj_kernel,
        grid=(T // tm,),
        in_specs=[
            pl.BlockSpec((tm, D_ATTN), lambda i: (i, 0)),
            pl.BlockSpec((tm, D_SGU), lambda i: (i, 0)),
            pl.BlockSpec((tm, D), lambda i: (i, 0)),
            pl.BlockSpec((1, 6, D), lambda i: (i // per_batch, 0, 0)),
            pl.BlockSpec((1, D), lambda i: (0, 0)),
            pl.BlockSpec((D_ATTN + D_SGU, D), lambda i: (0, 0)),
        ],
        out_specs=[pl.BlockSpec((tm, D), lambda i: (i, 0)), pl.BlockSpec((tm, D), lambda i: (i, 0)),
                   pl.BlockSpec((tm * ROW_TILE, LANES), lambda i: (i, 0))],
        out_shape=[jax.ShapeDtypeStruct((T, D), F32), jax.ShapeDtypeStruct((T, D), BF16),
                   jax.ShapeDtypeStruct((T * ROW_TILE, LANES), U32)],
        compiler_params=_cparams(("arbitrary",)),
        name="out_projection",
    )(attn, sgu, x2, mod3, norm_g, w_out_b)


def _router_kernel(h_ref, w_ref, b_ref, idx_ref, wt_ref, rank_ref, cnt_ref, run_ref):
    i = pl.program_id(0)
    tr = TM_ROUTE
    E = N_EXPERTS

    @pl.when(i == 0)
    def _():
        run_ref[...] = jnp.zeros_like(run_ref)

    logits = lax.dot_general(w_ref[...], h_ref[...], (((1,), (1,)), ((), ())), preferred_element_type=F32)
    scores = jax.nn.sigmoid(logits)
    sel = scores + b_ref[...]

    slabs = [sel[g * GROUP_SIZE:(g + 1) * GROUP_SIZE, :] for g in range(N_EXPERT_GROUPS)]
    si = lax.broadcasted_iota(I32, (GROUP_SIZE, tr), 0).astype(F32)
    gs = []
    for slab in slabs:
        m1 = jnp.max(slab, axis=0, keepdims=True)
        first = jnp.min(jnp.where(slab == m1, si, float(GROUP_SIZE)), axis=0, keepdims=True)
        m2 = jnp.max(jnp.where(si == first, -jnp.inf, slab), axis=0, keepdims=True)
        gs.append(m1 + m2)

    kept = []
    for g in range(N_EXPERT_GROUPS):
        beaten = jnp.zeros((1, tr), F32)
        for o in range(N_EXPERT_GROUPS):
            if o < g:
                beaten += (gs[o] >= gs[g]).astype(F32)
            elif o > g:
                beaten += (gs[o] > gs[g]).astype(F32)
        kept.append(jnp.where(beaten < TOPK_GROUPS, slabs[g], -jnp.inf))
    masked = jnp.concatenate(kept, axis=0)

    ei = lax.broadcasted_iota(I32, (E, tr), 0).astype(F32)
    picks, pick_scores = [], []
    onehot_sum = jnp.zeros((E, tr), F32)
    for k in range(TOP_K):
        m = jnp.max(masked, axis=0, keepdims=True)
        ik = jnp.min(jnp.where(masked == m, ei, float(E)), axis=0, keepdims=True)
        oh = ei == ik
        pick_scores.append(jnp.sum(jnp.where(oh, scores, 0.0), axis=0, keepdims=True))
        masked = jnp.where(oh, -jnp.inf, masked)
        onehot_sum += oh.astype(F32)
        picks.append(ik)

    ti = lax.broadcasted_iota(I32, (tr, tr), 0)
    tj = lax.broadcasted_iota(I32, (tr, tr), 1)
    before = (ti < tj).astype(BF16)
    prior = jnp.dot(onehot_sum.astype(BF16), before, preferred_element_type=F32) + run_ref[...]

    total = pick_scores[0]
    for k in range(1, TOP_K):
        total += pick_scores[k]
    for k in range(TOP_K):
        idx_ref[k:k + 1, :] = picks[k].astype(I32)
        wt_ref[k:k + 1, :] = pick_scores[k] / total * ROUTED_SCALE
        rk = jnp.sum(jnp.where(ei == picks[k], prior, 0.0), axis=0, keepdims=True)
        rank_ref[k:k + 1, :] = rk.astype(I32)

    run_ref[...] += jnp.sum(onehot_sum, axis=1, keepdims=True)
    cnt_ref[...] = run_ref[...].astype(I32)


def _route(h2, router_wt_b, router_bias):
    T, D = h2.shape
    tr = TM_ROUTE
    E = N_EXPERTS
    row_spec = pl.BlockSpec((TOP_K, tr), lambda i: (0, i))
    return pl.pallas_call(
        _router_kernel,
        grid=(T // tr,),
        in_specs=[
            pl.BlockSpec((tr, D), lambda i: (i, 0)),
            pl.BlockSpec((E, D), lambda i: (0, 0)),
            pl.BlockSpec((E, 1), lambda i: (0, 0)),
        ],
        out_specs=[row_spec, row_spec, row_spec, pl.BlockSpec((E, 1), lambda i: (0, 0))],
        out_shape=[
            jax.ShapeDtypeStruct((TOP_K, T), I32),
            jax.ShapeDtypeStruct((TOP_K, T), F32),
            jax.ShapeDtypeStruct((TOP_K, T), I32),
            jax.ShapeDtypeStruct((E, 1), I32),
        ],
        scratch_shapes=[pltpu.VMEM((E, 1), F32)],
        compiler_params=_cparams(("arbitrary",)),
        name="router",
    )(h2, router_wt_b, router_bias.reshape(E, 1))


def _dispatch_kernel(pend_ref, h_ref, dest_ref, xg_ref, dest_s, zero_ref, sem_s, sem):
    i = pl.program_id(0)
    td = TM_DISPATCH
    blk = EXPERT_BLOCK

    def zero_copy(e):
        start = pl.multiple_of(pend_ref[e + 1] - blk, blk)
        return pltpu.make_async_copy(zero_ref, xg_ref.at[pl.ds(start, blk)], sem)

    @pl.when(i == 0)
    def _():
        zero_ref[...] = jnp.zeros_like(zero_ref)

        def start_zero(e, c):
            @pl.when(pend_ref[e + 1] > pend_ref[e])
            def _():
                zero_copy(e).start()
            return c

        def wait_zero(e, c):
            @pl.when(pend_ref[e + 1] > pend_ref[e])
            def _():
                zero_copy(e).wait()
            return c

        lax.fori_loop(0, N_EXPERTS, start_zero, 0)
        lax.fori_loop(0, N_EXPERTS, wait_zero, 0)

        def tail_copy(b):
            return pltpu.make_async_copy(zero_ref, xg_ref.at[pl.ds(pl.multiple_of(b * blk, blk), blk)], sem)

        first_unused = pend_ref[N_EXPERTS] // blk
        n_blocks = xg_ref.shape[0] // blk
        lax.fori_loop(first_unused, n_blocks, lambda b, c: (tail_copy(b).start(), c)[1], 0)
        lax.fori_loop(first_unused, n_blocks, lambda b, c: (tail_copy(b).wait(), c)[1], 0)

    cp = pltpu.make_async_copy(dest_ref, dest_s, sem_s)
    cp.start()
    cp.wait()

    def row_copy(t, k):
        return pltpu.make_async_copy(h_ref.at[t], xg_ref.at[dest_s[k, t]], sem)

    def start_rows(t8, c):
        base = pl.multiple_of(t8 * 8, 8)
        for j in range(8):
            for k in range(TOP_K):
                row_copy(base + j, k).start(priority=k % DMA_QUEUES)
        return c

    lax.fori_loop(0, td // 8, start_rows, 0)
    for k in range(TOP_K):
        pltpu.make_async_copy(h_ref, h_ref, sem).wait()


def _dispatch(h2p, dest, pend0, rows):
    T = h2p.shape[0]
    td = TM_DISPATCH
    tile = (ROW_TILE, LANES)
    return pl.pallas_call(
        _dispatch_kernel,
        grid_spec=pltpu.PrefetchScalarGridSpec(
            num_scalar_prefetch=1,
            grid=(T // td,),
            in_specs=[
                pl.BlockSpec((td,) + tile, lambda i, p: (i, 0, 0)),
                pl.BlockSpec((TOP_K, td), lambda i, p: (0, i)),
            ],
            out_specs=pl.BlockSpec(memory_space=pl.ANY),
            scratch_shapes=[
                pltpu.SMEM((TOP_K, td), I32),
                pltpu.VMEM((EXPERT_BLOCK,) + tile, U32),
                pltpu.SemaphoreType.DMA,
                pltpu.SemaphoreType.DMA,
            ],
        ),
        out_shape=jax.ShapeDtypeStruct((rows,) + tile, U32),
        compiler_params=_cparams(("arbitrary",)),
        name="dispatch",
    )(pend0, h2p, dest)


def _experts_kernel(first_ref, gidx_ref, used_ref, nused_ref, ngroups_ref,
                    x_ref, wg_hbm, wu_hbm, wd_hbm, y_ref,
                    wgu_f, wd_f, wgu_b, wd_b, issued, sem):
    i = pl.program_id(0)
    F = D_EXPERT

    def tensor_copy(e, s, t):
        if t == 0:
            return pltpu.make_async_copy(wg_hbm.at[e], wgu_f.at[s, 0], sem.at[s, 0])
        if t == 1:
            return pltpu.make_async_copy(wu_hbm.at[e], wgu_f.at[s, 1], sem.at[s, 1])
        return pltpu.make_async_copy(wd_hbm.at[e], wd_f.at[s], sem.at[s, 2])

    def issue_until(n):
        def body(q, carry):
            h = q // 3
            for t in range(3):
                @pl.when(q % 3 == t)
                def _():
                    tensor_copy(used_ref[h], h % 2, t).start(priority=DMA_QUEUES - 1)
            return carry

        lax.fori_loop(issued[0], n, body, 0)
        issued[0] = jnp.maximum(issued[0], n)

    @pl.when(i == 0)
    def _():
        issued[0] = 0

    @pl.when(i < nused_ref[0])
    def _():
        g = gidx_ref[i]

        @pl.when(first_ref[i] == 1)
        def _():
            issue_until(3 * (g + 1))
            s = g % 2
            for t in range(3):
                tensor_copy(used_ref[g], s, t).wait()
            wgu_b[:, :F] = wgu_f[s, 0].astype(BF16)
            wgu_b[:, F:] = wgu_f[s, 1].astype(BF16)
            wd_b[...] = wd_f[s].astype(BF16)

        cap = 3 * jnp.minimum(g + 3, ngroups_ref[0])
        issue_until(jnp.minimum(issued[0] + 1, cap))

        lo, hi = _unpack_halves(_load_row_tiles(x_ref, 0, EXPERT_BLOCK))
        x = jnp.concatenate([lo.astype(BF16), hi.astype(BF16)], axis=1)
        gu = jnp.dot(x, wgu_b[...], preferred_element_type=F32)
        a = (_silu(gu[:, :F]) * gu[:, F:]).astype(BF16)
        y = jnp.dot(a, wd_b[...], preferred_element_type=F32)
        _store_row_tiles(y_ref, 0, EXPERT_BLOCK, _pack_halves(y))


def _experts(xg, first, gidx, used_list, n_used, n_groups, w_gate, w_up, w_down):
    D = D_MODEL
    bm = EXPERT_BLOCK
    F = D_EXPERT
    nb = xg.shape[0] // (bm * ROW_TILE)
    row_map = lambda i, fi, gi, ul, nu, ng: (jnp.minimum(i, nu[0] - 1), 0)
    return pl.pallas_call(
        _experts_kernel,
        grid_spec=pltpu.PrefetchScalarGridSpec(
            num_scalar_prefetch=5,
            grid=(nb,),
            in_specs=[
                pl.BlockSpec((bm * ROW_TILE, LANES), row_map),
                pl.BlockSpec(memory_space=pl.ANY),
                pl.BlockSpec(memory_space=pl.ANY),
                pl.BlockSpec(memory_space=pl.ANY),
            ],
            out_specs=pl.BlockSpec((bm * ROW_TILE, LANES), row_map),
            scratch_shapes=[
                pltpu.VMEM((2, 2, D, F), F32), pltpu.VMEM((2, F, D), F32),
                pltpu.VMEM((D, 2 * F), BF16), pltpu.VMEM((F, D), BF16),
                pltpu.SMEM((1,), I32),
                pltpu.SemaphoreType.DMA((2, 3)),
            ],
        ),
        out_shape=jax.ShapeDtypeStruct(xg.shape, U32),
        input_output_aliases={5: 0},
        compiler_params=_cparams(("arbitrary",)),
        name="experts",
    )(first, gidx, used_list, n_used, n_groups, xg, w_gate, w_up, w_down)


def _combine_kernel(dest_ref, dnext_ref, wt_ref, x1_ref, h_ref, mod_ref, sg_ref, su_ref, sd_ref, yg_ref, o_ref,
                    dest_s, buf, sem_s, sem):
    i = pl.program_id(0)
    last = pl.num_programs(0) - 1
    tc = TM_COMBINE
    cur = i % 2
    nxt = 1 - cur

    def plane(slot, k):
        off = (slot * TOP_K + k) * (tc * ROW_TILE)
        return off if isinstance(off, int) else pl.multiple_of(off, tc * ROW_TILE)

    def row_copy(slot, t, k):
        return pltpu.make_async_copy(yg_ref.at[dest_s[slot, k, t]],
                                     buf.at[pl.ds(plane(slot, k) + t * ROW_TILE, ROW_TILE), :], sem.at[slot])

    def load_dest(src_ref, slot):
        cp = pltpu.make_async_copy(src_ref, dest_s.at[slot], sem_s)
        cp.start()
        cp.wait()

    def wait_rows(slot):
        whole = buf.at[pl.ds(plane(slot, 0), TOP_K * tc * ROW_TILE), :]
        pltpu.make_async_copy(whole, whole, sem.at[slot]).wait()

    @pl.when(i == 0)
    def _():
        load_dest(dest_ref, 0)

        def body(t, c):
            for k in range(TOP_K):
                row_copy(0, t, k).start(priority=k % DMA_QUEUES)
            return c

        lax.fori_loop(0, tc, body, 0)

    load_dest(dnext_ref, nxt)
    for t in range(tc):
        for k in range(TOP_K):
            row_copy(nxt, t, k).start(priority=k % DMA_QUEUES)

    hb = h_ref[...]
    g = jnp.dot(hb, sg_ref[...], preferred_element_type=F32)
    u = jnp.dot(hb, su_ref[...], preferred_element_type=F32)
    ffn = jnp.dot((_silu(g) * u).astype(BF16), sd_ref[...], preferred_element_type=F32)

    wait_rows(cur)
    half = D_MODEL // 2
    ffn_lo, ffn_hi = ffn[:, :half], ffn[:, half:]
    for k in range(TOP_K):
        lo, hi = _unpack_halves(_load_row_tiles(buf, plane(cur, k), tc))
        w = wt_ref[:, k:k + 1]
        ffn_lo += lo * w
        ffn_hi += hi * w
    o_ref[:, :half] = x1_ref[:, :half] + mod_ref[0, 5:6, :half] * ffn_lo
    o_ref[:, half:] = x1_ref[:, half:] + mod_ref[0, 5:6, half:] * ffn_hi

    @pl.when(i == last)
    def _():
        wait_rows(nxt)


def _combine(dest, wt_t, x1, h2, mod3, sg_b, su_b, sd_b, yg):
    T, D = x1.shape
    tc = TM_COMBINE
    per_batch = SEQ // tc
    F = D_EXPERT
    return pl.pallas_call(
        _combine_kernel,
        grid=(T // tc,),
        in_specs=[
            pl.BlockSpec((TOP_K, tc), lambda i: (0, i)),
            pl.BlockSpec((TOP_K, tc), lambda i: (0, jnp.minimum(i + 1, T // tc - 1))),
            pl.BlockSpec((tc, TOP_K), lambda i: (i, 0)),
            pl.BlockSpec((tc, D), lambda i: (i, 0)),
            pl.BlockSpec((tc, D), lambda i: (i, 0)),
            pl.BlockSpec((1, 6, D), lambda i: (i // per_batch, 0, 0)),
            pl.BlockSpec((D, F), lambda i: (0, 0)),
            pl.BlockSpec((D, F), lambda i: (0, 0)),
            pl.BlockSpec((F, D), lambda i: (0, 0)),
            pl.BlockSpec(memory_space=pl.ANY),
        ],
        out_specs=pl.BlockSpec((tc, D), lambda i: (i, 0)),
        out_shape=jax.ShapeDtypeStruct((T, D), F32),
        scratch_shapes=[
            pltpu.SMEM((2, TOP_K, tc), I32),
            pltpu.VMEM((2 * TOP_K * tc * ROW_TILE, LANES), U32),
            pltpu.SemaphoreType.DMA,
            pltpu.SemaphoreType.DMA((2,)),
        ],
        compiler_params=_cparams(("arbitrary",)),
        name="combine",
    )(dest, dest, wt_t, x1, h2, mod3, sg_b, su_b, sd_b, yg)


def kernel(x, c, ada_w, ada_b, mix_norm_g, ffn_norm_g, w_in, q_norm_g, k_norm_g, rel_bias, sgu_ln_g, sgu_ln_b, sgu_w, sgu_b, w_out, router_w, router_bias, shared_w_gate, shared_w_up, shared_w_down, expert_w_gate, expert_w_up, expert_w_down):
    B, S, D = x.shape
    assert S == SEQ and D == D_MODEL and ada_w.shape[0] == 1
    T = B * S
    x2 = x.reshape(T, D)

    mod3 = _modulation(c, ada_w[0], ada_b[0]).reshape(B, 6, D)

    proj = _in_projection(x2, mod3, mix_norm_g, w_in[0].astype(BF16), q_norm_g, k_norm_g)
    attn = _attention(proj.reshape(B, S, D_IN_PROJ), _bias_tables(rel_bias)).reshape(T, D_ATTN)
    causal = jnp.tril(jnp.ones((SGU_CHUNK, SGU_CHUNK), F32))
    sgu = _spatial_gating(proj, (sgu_w[0] * causal).astype(BF16), sgu_ln_g, sgu_ln_b, sgu_b[0].T)
    x1, h2, h2p = _out_projection(attn, sgu, x2, mod3, ffn_norm_g, w_out[0].astype(BF16))

    idx, wts, rank, counts = _route(h2, router_w[0].T.astype(BF16), router_bias[0])

    bm = EXPERT_BLOCK
    counts = counts.reshape(N_EXPERTS)
    padded = (counts + bm - 1) // bm * bm
    pends = jnp.cumsum(padded)
    pstarts = pends - padded
    n_blocks = T * TOP_K // bm + N_EXPERTS
    block_start = jnp.arange(n_blocks, dtype=I32) * bm
    block_exp = jnp.minimum(jnp.sum((pends[None, :] <= block_start[:, None]).astype(I32), axis=1), N_EXPERTS - 1)
    n_used = (pends[-1] // bm).astype(I32).reshape(1)
    eids = jnp.arange(N_EXPERTS, dtype=I32)
    dest = jnp.sum(jnp.where(idx[:, :, None] == eids, pstarts.astype(I32), 0), axis=-1) + rank
    pend0 = jnp.concatenate([jnp.zeros((1,), I32), pends.astype(I32)])
    first = jnp.concatenate([jnp.ones((1,), I32), (block_exp[1:] != block_exp[:-1]).astype(I32)])
    used = counts > 0
    ordinal = jnp.cumsum(used.astype(I32)) - 1
    n_groups = jnp.sum(used.astype(I32)).reshape(1)
    used_list = jnp.sum(jnp.where(used[None, :] & (ordinal[None, :] == eids[:, None]), eids[None, :], 0), axis=-1)
    gidx = jnp.sum(jnp.where(block_exp[:, None] == eids, ordinal, 0), axis=-1)

    n_rows = n_blocks * bm
    xg = _dispatch(h2p.reshape(T, ROW_TILE, LANES), dest, pend0, n_rows)
    yg = _experts(xg.reshape(n_rows * ROW_TILE, LANES), first, gidx.astype(I32), used_list.astype(I32),
                  n_used, n_groups, expert_w_gate[0], expert_w_up[0], expert_w_down[0])
    out = _combine(dest, wts.T, x1, h2, mod3, shared_w_gate[0].astype(BF16), shared_w_up[0].astype(BF16),
                   shared_w_down[0].astype(BF16), yg.reshape(n_rows, ROW_TILE, LANES))
    return out.reshape(B, S, D)
```

```python
import functools
import math

import numpy as np
import jax
import jax.numpy as jnp
from jax import lax
from jax.experimental import pallas as pl
from jax.experimental.pallas import tpu as pltpu

F32 = jnp.float32
BF16 = jnp.bfloat16
I32 = jnp.int32
U32 = jnp.uint32

D_MODEL = 2048
SEQ = 2048
HEAD_DIM = 128
N_HEADS = 8
D_ATTN = N_HEADS * HEAD_DIM
D_SGU = 1024
N_SGU_GROUPS = 8
SGU_CHUNK = 128
D_IN_PROJ = 3 * D_ATTN + 2 * D_SGU
DILATED_GROUPS = ((128, 1), (512, 4), (2048, 16))
BAND = 128
NUM_REL_BUCKETS = 32
REL_MAX_DISTANCE = 2048
N_EXPERTS = 256
TOP_K = 8
N_EXPERT_GROUPS = 8
GROUP_SIZE = N_EXPERTS // N_EXPERT_GROUPS
TOPK_GROUPS = 4
D_EXPERT = 512
ROUTED_SCALE = 2.5
NORM_EPS = 1e-6
MASK_VALUE = -1e30

LANES = 128
ROW_TILE = D_MODEL // 2 // LANES
VMEM_LIMIT = 56 * 1024 * 1024
DMA_QUEUES = 2

TM_PROJ = 512
TM_SGU = 512
TM_OUT = 256
TM_ROUTE = 512
TM_DISPATCH = 256
TM_COMBINE = 128
EXPERT_BLOCK = 128
ATTN_PIECES_IN_FLIGHT = 4


def _cparams(sem):
    return pltpu.CompilerParams(dimension_semantics=sem, vmem_limit_bytes=VMEM_LIMIT)


def _silu(v):
    return v * jax.nn.sigmoid(v)


def _gelu(v):
    return 0.5 * v * (1.0 + lax.erf(v * (1.0 / math.sqrt(2.0))))


def _pack_halves(v):
    h = v.shape[1] // 2
    lo = lax.bitcast_convert_type(v[:, :h].astype(BF16).astype(F32), U32)
    hi = lax.bitcast_convert_type(v[:, h:].astype(BF16).astype(F32), U32)
    return (hi & jnp.uint32(0xFFFF0000)) | (lo >> 16)


def _unpack_halves(w):
    lo = lax.bitcast_convert_type(w << 16, F32)
    hi = lax.bitcast_convert_type(w & jnp.uint32(0xFFFF0000), F32)
    return lo, hi


def _store_row_tiles(ref, base, rows, packed):
    for s in range(ROW_TILE):
        ref[pl.ds(base + s, rows, stride=ROW_TILE), :] = packed[:, s * LANES:(s + 1) * LANES]


def _load_row_tiles(ref, base, rows):
    return jnp.concatenate([ref[pl.ds(base + s, rows, stride=ROW_TILE), :] for s in range(ROW_TILE)], axis=1)


def _mod_kernel(c_ref, w_ref, b_ref, o_ref):
    ca = _silu(c_ref[...]).astype(BF16)
    o_ref[...] = jnp.dot(ca, w_ref[...].astype(BF16), preferred_element_type=F32) + b_ref[...]


def _modulation(c, ada_w, ada_b):
    B, D = c.shape
    N = ada_w.shape[1]
    tn = 1024
    return pl.pallas_call(
        _mod_kernel,
        grid=(N // tn,),
        in_specs=[
            pl.BlockSpec((B, D), lambda j: (0, 0)),
            pl.BlockSpec((D, tn), lambda j: (0, j)),
            pl.BlockSpec((1, tn), lambda j: (0, j)),
        ],
        out_specs=pl.BlockSpec((B, tn), lambda j: (0, j)),
        out_shape=jax.ShapeDtypeStruct((B, N), F32),
        compiler_params=_cparams(("arbitrary",)),
        name="modulation",
    )(c, ada_w, ada_b.reshape(1, N))


def _inproj_kernel(x_ref, mod_ref, g_ref, w_ref, qg_ref, kg_ref, o_ref, h_ref):
    j = pl.program_id(1)

    @pl.when(j == 0)
    def _():
        x = x_ref[...]
        r = lax.rsqrt(jnp.mean(x * x, axis=-1, keepdims=True) + NORM_EPS)
        h = (x * r) * g_ref[...] * (1.0 + mod_ref[0, 1:2, :]) + mod_ref[0, 0:1, :]
        h_ref[...] = h.astype(BF16)

    acc = jnp.dot(h_ref[...], w_ref[...], preferred_element_type=F32)

    def head_norm(gain_ref, scale):
        for hd in range(N_HEADS):
            a = acc[:, hd * HEAD_DIM:(hd + 1) * HEAD_DIM]
            r = lax.rsqrt(jnp.mean(a * a, axis=-1, keepdims=True) + NORM_EPS)
            o_ref[:, hd * HEAD_DIM:(hd + 1) * HEAD_DIM] = ((a * r) * gain_ref[...] * scale).astype(BF16)

    @pl.when(j == 0)
    def _():
        head_norm(qg_ref, HEAD_DIM ** -0.5)

    @pl.when(j == 1)
    def _():
        head_norm(kg_ref, 1.0)

    @pl.when(j >= 2)
    def _():
        o_ref[...] = acc.astype(BF16)


def _in_projection(x2, mod3, norm_g, w_in_b, q_g, k_g):
    T, D = x2.shape
    tm, tn = TM_PROJ, D_ATTN
    per_batch = SEQ // tm
    return pl.pallas_call(
        _inproj_kernel,
        grid=(T // tm, D_IN_PROJ // tn),
        in_specs=[
            pl.BlockSpec((tm, D), lambda i, j: (i, 0)),
            pl.BlockSpec((1, 6, D), lambda i, j: (i // per_batch, 0, 0)),
            pl.BlockSpec((1, D), lambda i, j: (0, 0)),
            pl.BlockSpec((D, tn), lambda i, j: (0, j)),
            pl.BlockSpec((1, HEAD_DIM), lambda i, j: (0, 0)),
            pl.BlockSpec((1, HEAD_DIM), lambda i, j: (0, 0)),
        ],
        out_specs=pl.BlockSpec((tm, tn), lambda i, j: (i, j)),
        out_shape=jax.ShapeDtypeStruct((T, D_IN_PROJ), BF16),
        scratch_shapes=[pltpu.VMEM((tm, D), BF16)],
        compiler_params=_cparams(("arbitrary", "arbitrary")),
        name="in_projection",
    )(x2, mod3, norm_g, w_in_b, q_g, k_g)


def _bias_tables(rel_bias):
    n = BAND
    qi = jnp.arange(n)[:, None]
    ki = jnp.arange(2 * n)[None, :]
    steps = n + qi - ki
    in_band = (steps >= 0) & (steps <= n)
    max_exact = NUM_REL_BUCKETS // 2
    tabs = []
    for _, dilation in DILATED_GROUPS:
        dist = jnp.clip(steps, 0, n) * dilation
        nf = jnp.maximum(dist, 1).astype(F32)
        large = max_exact + (jnp.log(nf / max_exact) / math.log(REL_MAX_DISTANCE / max_exact)
                             * (NUM_REL_BUCKETS - max_exact)).astype(I32)
        large = jnp.minimum(large, NUM_REL_BUCKETS - 1)
        bucket = jnp.where(dist < max_exact, dist, large)
        onehot = jax.nn.one_hot(bucket, NUM_REL_BUCKETS, dtype=F32)
        b = jnp.einsum("qkb,bh->hqk", onehot, rel_bias.astype(F32), precision=lax.Precision.HIGHEST)
        tabs.append(jnp.where(in_band[None], b, MASK_VALUE))
    return jnp.stack(tabs)


def _attn_kernel(q_ref, k_ref, v_ref, bias_ref, o_ref, qf, kf, vf, q4, k4, v4,
                 o0, l0, o1, l1, o2, l2, stage):
    n = BAND
    quarter = SEQ // 4

    for src, nat, res in ((q_ref, qf, q4), (k_ref, kf, k4), (v_ref, vf, v4)):
        nat[...] = src[...].astype(F32)
        for r in range(4):
            res[r * quarter:(r + 1) * quarter, :] = nat[pl.ds(r, quarter, stride=4), :]

    def logits(qb, kb, bias):
        return lax.dot_general(qb, kb, (((1,), (1,)), ((), ())), preferred_element_type=F32) + bias

    def finish(l, vb, o_out, l_out, rows):
        m = jnp.max(l, axis=-1, keepdims=True)
        p = jnp.exp(l - m)
        s = jnp.sum(p, axis=-1, keepdims=True)
        o = jnp.dot(p.astype(BF16), vb, preferred_element_type=F32)
        o_out[rows, :] = o / s
        l_out[rows, :] = jnp.broadcast_to(m + jnp.log(s), (n, LANES))

    pieces = []

    for i in range(SEQ // n):
        rows = slice(i * n, (i + 1) * n)
        krows = rows if i == 0 else slice((i - 1) * n, (i + 1) * n)
        bias = (0, slice(None), slice(n, None)) if i == 0 else (0,)
        pieces.append((q_ref, k_ref, v_ref, rows, krows, bias, o0, l0))

    for r in range(4):
        for blk in range(quarter // n):
            base = r * quarter + blk * n
            rows = slice(base, base + n)
            krows = rows if blk == 0 else slice(base - n, base + n)
            bias = (1, slice(None), slice(n, None)) if blk == 0 else (1,)
            pieces.append((q4, k4, v4, rows, krows, bias, o1, l1))

    for r in range(4):
        for a in range(4):
            rows = pl.ds(r * quarter + a, n, stride=4)
            pieces.append((q4, k4, v4, rows, rows, (2, slice(None), slice(n, None)), o2, l2))

    pending = []
    for qs, ks, vs, rows, krows, bias, o_out, l_out in pieces:
        l = logits(qs[rows, :].astype(BF16), ks[krows, :].astype(BF16), bias_ref[bias])
        if len(pending) == ATTN_PIECES_IN_FLIGHT:
            finish(*pending.pop(0))
        pending.append((l, vs[krows, :].astype(BF16), o_out, l_out, rows))
    for args in pending:
        finish(*args)

    for r in range(4):
        for c in range(quarter // n):
            rows = slice(r * quarter + c * n, r * quarter + (c + 1) * n)
            nat = pl.ds(r + 4 * c * n, n, stride=4)
            a0, a1, a2 = l0[nat, :], l1[rows, :], l2[rows, :]
            m = jnp.maximum(jnp.maximum(a0, a1), a2)
            e0, e1, e2 = jnp.exp(a0 - m), jnp.exp(a1 - m), jnp.exp(a2 - m)
            mix = (e0 * o0[nat, :] + e1 * o1[rows, :] + e2 * o2[rows, :]) / (e0 + e1 + e2)
            stage[nat, :] = mix
    o_ref[...] = stage[...].astype(BF16)


def _attention(proj3, bias_tabs):
    B = proj3.shape[0]
    blk = lambda off: pl.BlockSpec((None, SEQ, HEAD_DIM), lambda b, h: (b, 0, off + h))
    return pl.pallas_call(
        _attn_kernel,
        grid=(B, N_HEADS),
        in_specs=[
            blk(0), blk(N_HEADS), blk(2 * N_HEADS),
            pl.BlockSpec((3, None, BAND, 2 * BAND), lambda b, h: (0, h, 0, 0)),
        ],
        out_specs=pl.BlockSpec((None, SEQ, HEAD_DIM), lambda b, h: (b, 0, h)),
        out_shape=jax.ShapeDtypeStruct((B, SEQ, D_ATTN), BF16),
        scratch_shapes=[pltpu.VMEM((SEQ, HEAD_DIM), F32) for _ in range(13)],
        compiler_params=_cparams(("arbitrary", "arbitrary")),
        name="dilated_attention",
    )(proj3, proj3, proj3, bias_tabs)


def _sgu_kernel(u_ref, z_ref, w_ref, g_ref, b_ref, bs_ref, o_ref):
    n = SGU_CHUNK
    for c in range(TM_SGU // n):
        rs = slice(c * n, (c + 1) * n)
        for g in range(N_SGU_GROUPS):
            cs = slice(g * n, (g + 1) * n)
            z = _gelu(z_ref[rs, cs].astype(F32))
            mu = jnp.mean(z, axis=-1, keepdims=True)
            zc = z - mu
            var = jnp.mean(zc * zc, axis=-1, keepdims=True)
            zn = (zc * lax.rsqrt(var + NORM_EPS)) * g_ref[:, cs] + b_ref[:, cs]
            mixed = jnp.dot(w_ref[g], zn.astype(BF16), preferred_element_type=F32) + bs_ref[:, g:g + 1]
            u = _gelu(u_ref[rs, cs].astype(F32))
            o_ref[rs, cs] = (u * mixed).astype(BF16)


def _spatial_gating(proj, w_causal_b, ln_g, ln_b, bs_t):
    T = proj.shape[0]
    tm = TM_SGU
    ucol = 3 * D_ATTN // D_SGU
    return pl.pallas_call(
        _sgu_kernel,
        grid=(T // tm,),
        in_specs=[
            pl.BlockSpec((tm, D_SGU), lambda i: (i, ucol)),
            pl.BlockSpec((tm, D_SGU), lambda i: (i, ucol + 1)),
            pl.BlockSpec((N_SGU_GROUPS, SGU_CHUNK, SGU_CHUNK), lambda i: (0, 0, 0)),
            pl.BlockSpec((1, D_SGU), lambda i: (0, 0)),
            pl.BlockSpec((1, D_SGU), lambda i: (0, 0)),
            pl.BlockSpec((SGU_CHUNK, N_SGU_GROUPS), lambda i: (0, 0)),
        ],
        out_specs=pl.BlockSpec((tm, D_SGU), lambda i: (i, 0)),
        out_shape=jax.ShapeDtypeStruct((T, D_SGU), BF16),
        compiler_params=_cparams(("arbitrary",)),
        name="spatial_gating",
    )(proj, proj, w_causal_b, ln_g, ln_b, bs_t)


def _outproj_kernel(a_ref, s_ref, x_ref, mod_ref, g_ref, w_ref, x1_ref, h2_ref, h2p_ref):
    mixed = jnp.dot(a_ref[...], w_ref[0:D_ATTN, :], preferred_element_type=F32)
    mixed += jnp.dot(s_ref[...], w_ref[D_ATTN:, :], preferred_element_type=F32)
    x1 = x_ref[...] + mod_ref[0, 2:3, :] * mixed
    x1_ref[...] = x1
    r = lax.rsqrt(jnp.mean(x1 * x1, axis=-1, keepdims=True) + NORM_EPS)
    h2 = (x1 * r) * g_ref[...] * (1.0 + mod_ref[0, 4:5, :]) + mod_ref[0, 3:4, :]
    h2_ref[...] = h2.astype(BF16)
    _store_row_tiles(h2p_ref, 0, TM_OUT, _pack_halves(h2))


def _out_projection(attn, sgu, x2, mod3, norm_g, w_out_b):
    T, D = x2.shape
    tm = TM_OUT
    per_batch = SEQ // tm
    return pl.pallas_call(
        _outproj_kernel,
        grid=(T // tm,),
        in_specs=[
            pl.BlockSpec((tm, D_ATTN), lambda i: (i, 0)),
            pl.BlockSpec((tm, D_SGU), lambda i: (i, 0)),
            pl.BlockSpec((tm, D), lambda i: (i, 0)),
            pl.BlockSpec((1, 6, D), lambda i: (i // per_batch, 0, 0)),
            pl.BlockSpec((1, D), lambda i: (0, 0)),
            pl.BlockSpec((D_ATTN + D_SGU, D), lambda i: (0, 0)),
        ],
        out_specs=[pl.BlockSpec((tm, D), lambda i: (i, 0)), pl.BlockSpec((tm, D), lambda i: (i, 0)),
                   pl.BlockSpec((tm * ROW_TILE, LANES), lambda i: (i, 0))],
        out_shape=[jax.ShapeDtypeStruct((T, D), F32), jax.ShapeDtypeStruct((T, D), BF16),
                   jax.ShapeDtypeStruct((T * ROW_TILE, LANES), U32)],
        compiler_params=_cparams(("arbitrary",)),
        name="out_projection",
    )(attn, sgu, x2, mod3, norm_g, w_out_b)


def _router_kernel(h_ref, w_ref, b_ref, idx_ref, wt_ref, rank_ref, cnt_ref, run_ref):
    i = pl.program_id(0)
    tr = TM_ROUTE
    E = N_EXPERTS

    @pl.when(i == 0)
    def _():
        run_ref[...] = jnp.zeros_like(run_ref)

    logits = lax.dot_general(w_ref[...], h_ref[...], (((1,), (1,)), ((), ())), preferred_element_type=F32)
    scores = jax.nn.sigmoid(logits)
    sel = scores + b_ref[...]

    slabs = [sel[g * GROUP_SIZE:(g + 1) * GROUP_SIZE, :] for g in range(N_EXPERT_GROUPS)]
    si = lax.broadcasted_iota(I32, (GROUP_SIZE, tr), 0).astype(F32)
    gs = []
    for slab in slabs:
        m1 = jnp.max(slab, axis=0, keepdims=True)
        first = jnp.min(jnp.where(slab == m1, si, float(GROUP_SIZE)), axis=0, keepdims=True)
        m2 = jnp.max(jnp.where(si == first, -jnp.inf, slab), axis=0, keepdims=True)
        gs.append(m1 + m2)

    kept = []
    for g in range(N_EXPERT_GROUPS):
        beaten = jnp.zeros((1, tr), F32)
        for o in range(N_EXPERT_GROUPS):
            if o < g:
                beaten += (gs[o] >= gs[g]).astype(F32)
            elif o > g:
                beaten += (gs[o] > gs[g]).astype(F32)
        kept.append(jnp.where(beaten < TOPK_GROUPS, slabs[g], -jnp.inf))
    masked = jnp.concatenate(kept, axis=0)

    ei = lax.broadcasted_iota(I32, (E, tr), 0).astype(F32)
    picks, pick_scores = [], []
    onehot_sum = jnp.zeros((E, tr), F32)
    for k in range(TOP_K):
        m = jnp.max(masked, axis=0, keepdims=True)
        ik = jnp.min(jnp.where(masked == m, ei, float(E)), axis=0, keepdims=True)
        oh = ei == ik
        pick_scores.append(jnp.sum(jnp.where(oh, scores, 0.0), axis=0, keepdims=True))
        masked = jnp.where(oh, -jnp.inf, masked)
        onehot_sum += oh.astype(F32)
        picks.append(ik)

    ti = lax.broadcasted_iota(I32, (tr, tr), 0)
    tj = lax.broadcasted_iota(I32, (tr, tr), 1)
    before = (ti < tj).astype(BF16)
    prior = jnp.dot(onehot_sum.astype(BF16), before, preferred_element_type=F32) + run_ref[...]

    total = pick_scores[0]
    for k in range(1, TOP_K):
        total += pick_scores[k]
    for k in range(TOP_K):
        idx_ref[k:k + 1, :] = picks[k].astype(I32)
        wt_ref[k:k + 1, :] = pick_scores[k] / total * ROUTED_SCALE
        rk = jnp.sum(jnp.where(ei == picks[k], prior, 0.0), axis=0, keepdims=True)
        rank_ref[k:k + 1, :] = rk.astype(I32)

    run_ref[...] += jnp.sum(onehot_sum, axis=1, keepdims=True)
    cnt_ref[...] = run_ref[...].astype(I32)


def _route(h2, router_wt_b, router_bias):
    T, D = h2.shape
    tr = TM_ROUTE
    E = N_EXPERTS
    row_spec = pl.BlockSpec((TOP_K, tr), lambda i: (0, i))
    return pl.pallas_call(
        _router_kernel,
        grid=(T // tr,),
        in_specs=[
            pl.BlockSpec((tr, D), lambda i: (i, 0)),
            pl.BlockSpec((E, D), lambda i: (0, 0)),
            pl.BlockSpec((E, 1), lambda i: (0, 0)),
        ],
        out_specs=[row_spec, row_spec, row_spec, pl.BlockSpec((E, 1), lambda i: (0, 0))],
        out_shape=[
            jax.ShapeDtypeStruct((TOP_K, T), I32),
            jax.ShapeDtypeStruct((TOP_K, T), F32),
            jax.ShapeDtypeStruct((TOP_K, T), I32),
            jax.ShapeDtypeStruct((E, 1), I32),
        ],
        scratch_shapes=[pltpu.VMEM((E, 1), F32)],
        compiler_params=_cparams(("arbitrary",)),
        name="router",
    )(h2, router_wt_b, router_bias.reshape(E, 1))


def _dispatch_kernel(pend_ref, h_ref, dest_ref, xg_ref, dest_s, zero_ref, sem_s, sem):
    i = pl.program_id(0)
    td = TM_DISPATCH
    blk = EXPERT_BLOCK

    def zero_copy(e):
        start = pl.multiple_of(pend_ref[e + 1] - blk, blk)
        return pltpu.make_async_copy(zero_ref, xg_ref.at[pl.ds(start, blk)], sem)

    @pl.when(i == 0)
    def _():
        zero_ref[...] = jnp.zeros_like(zero_ref)

        def start_zero(e, c):
            @pl.when(pend_ref[e + 1] > pend_ref[e])
            def _():
                zero_copy(e).start()
            return c

        def wait_zero(e, c):
            @pl.when(pend_ref[e + 1] > pend_ref[e])
            def _():
                zero_copy(e).wait()
            return c

        lax.fori_loop(0, N_EXPERTS, start_zero, 0)
        lax.fori_loop(0, N_EXPERTS, wait_zero, 0)

        def tail_copy(b):
            return pltpu.make_async_copy(zero_ref, xg_ref.at[pl.ds(pl.multiple_of(b * blk, blk), blk)], sem)

        first_unused = pend_ref[N_EXPERTS] // blk
        n_blocks = xg_ref.shape[0] // blk
        lax.fori_loop(first_unused, n_blocks, lambda b, c: (tail_copy(b).start(), c)[1], 0)
        lax.fori_loop(first_unused, n_blocks, lambda b, c: (tail_copy(b).wait(), c)[1], 0)

    cp = pltpu.make_async_copy(dest_ref, dest_s, sem_s)
    cp.start()
    cp.wait()

    def row_copy(t, k):
        return pltpu.make_async_copy(h_ref.at[t], xg_ref.at[dest_s[k, t]], sem)

    def start_rows(t8, c):
        base = pl.multiple_of(t8 * 8, 8)
        for j in range(8):
            for k in range(TOP_K):
                row_copy(base + j, k).start(priority=k % DMA_QUEUES)
        return c

    lax.fori_loop(0, td // 8, start_rows, 0)
    for k in range(TOP_K):
        pltpu.make_async_copy(h_ref, h_ref, sem).wait()


def _dispatch(h2p, dest, pend0, rows):
    T = h2p.shape[0]
    td = TM_DISPATCH
    tile = (ROW_TILE, LANES)
    return pl.pallas_call(
        _dispatch_kernel,
        grid_spec=pltpu.PrefetchScalarGridSpec(
            num_scalar_prefetch=1,
            grid=(T // td,),
            in_specs=[
                pl.BlockSpec((td,) + tile, lambda i, p: (i, 0, 0)),
                pl.BlockSpec((TOP_K, td), lambda i, p: (0, i)),
            ],
            out_specs=pl.BlockSpec(memory_space=pl.ANY),
            scratch_shapes=[
                pltpu.SMEM((TOP_K, td), I32),
                pltpu.VMEM((EXPERT_BLOCK,) + tile, U32),
                pltpu.SemaphoreType.DMA,
                pltpu.SemaphoreType.DMA,
            ],
        ),
        out_shape=jax.ShapeDtypeStruct((rows,) + tile, U32),
        compiler_params=_cparams(("arbitrary",)),
        name="dispatch",
    )(pend0, h2p, dest)


def _experts_kernel(first_ref, gidx_ref, used_ref, nused_ref, ngroups_ref,
                    x_ref, wg_hbm, wu_hbm, wd_hbm, y_ref,
                    wgu_f, wd_f, wgu_b, wd_b, issued, sem):
    i = pl.program_id(0)
    F = D_EXPERT

    def tensor_copy(e, s, t):
        if t == 0:
            return pltpu.make_async_copy(wg_hbm.at[e], wgu_f.at[s, 0], sem.at[s, 0])
        if t == 1:
            return pltpu.make_async_copy(wu_hbm.at[e], wgu_f.at[s, 1], sem.at[s, 1])
        return pltpu.make_async_copy(wd_hbm.at[e], wd_f.at[s], sem.at[s, 2])

    def issue_until(n):
        def body(q, carry):
            h = q // 3
            for t in range(3):
                @pl.when(q % 3 == t)
                def _():
                    tensor_copy(used_ref[h], h % 2, t).start(priority=DMA_QUEUES - 1)
            return carry

        lax.fori_loop(issued[0], n, body, 0)
        issued[0] = jnp.maximum(issued[0], n)

    @pl.when(i == 0)
    def _():
        issued[0] = 0

    @pl.when(i < nused_ref[0])
    def _():
        g = gidx_ref[i]

        @pl.when(first_ref[i] == 1)
        def _():
            issue_until(3 * (g + 1))
            s = g % 2
            for t in range(3):
                tensor_copy(used_ref[g], s, t).wait()
            wgu_b[:, :F] = wgu_f[s, 0].astype(BF16)
            wgu_b[:, F:] = wgu_f[s, 1].astype(BF16)
            wd_b[...] = wd_f[s].astype(BF16)

        cap = 3 * jnp.minimum(g + 3, ngroups_ref[0])
        issue_until(jnp.minimum(issued[0] + 1, cap))

        lo, hi = _unpack_halves(_load_row_tiles(x_ref, 0, EXPERT_BLOCK))
        x = jnp.concatenate([lo.astype(BF16), hi.astype(BF16)], axis=1)
        gu = jnp.dot(x, wgu_b[...], preferred_element_type=F32)
        a = (_silu(gu[:, :F]) * gu[:, F:]).astype(BF16)
        y = jnp.dot(a, wd_b[...], preferred_element_type=F32)
        _store_row_tiles(y_ref, 0, EXPERT_BLOCK, _pack_halves(y))


def _experts(xg, first, gidx, used_list, n_used, n_groups, w_gate, w_up, w_down):
    D = D_MODEL
    bm = EXPERT_BLOCK
    F = D_EXPERT
    nb = xg.shape[0] // (bm * ROW_TILE)
    row_map = lambda i, fi, gi, ul, nu, ng: (jnp.minimum(i, nu[0] - 1), 0)
    return pl.pallas_call(
        _experts_kernel,
        grid_spec=pltpu.PrefetchScalarGridSpec(
            num_scalar_prefetch=5,
            grid=(nb,),
            in_specs=[
                pl.BlockSpec((bm * ROW_TILE, LANES), row_map),
                pl.BlockSpec(memory_space=pl.ANY),
                pl.BlockSpec(memory_space=pl.ANY),
                pl.BlockSpec(memory_space=pl.ANY),
            ],
            out_specs=pl.BlockSpec((bm * ROW_TILE, LANES), row_map),
            scratch_shapes=[
                pltpu.VMEM((2, 2, D, F), F32), pltpu.VMEM((2, F, D), F32),
                pltpu.VMEM((D, 2 * F), BF16), pltpu.VMEM((F, D), BF16),
                pltpu.SMEM((1,), I32),
                pltpu.SemaphoreType.DMA((2, 3)),
            ],
        ),
        out_shape=jax.ShapeDtypeStruct(xg.shape, U32),
        input_output_aliases={5: 0},
        compiler_params=_cparams(("arbitrary",)),
        name="experts",
    )(first, gidx, used_list, n_used, n_groups, xg, w_gate, w_up, w_down)


def _combine_kernel(dest_ref, dnext_ref, wt_ref, x1_ref, h_ref, mod_ref, sg_ref, su_ref, sd_ref, yg_ref, o_ref,
                    dest_s, dnext_s, buf_a, buf_b, sem_s, sem):
    i = pl.program_id(0)
    last = pl.num_programs(0) - 1
    tc = TM_COMBINE
    half = D_MODEL // 2

    for src, dst in ((dest_ref, dest_s), (dnext_ref, dnext_s)):
        cp = pltpu.make_async_copy(src, dst, sem_s)
        cp.start()
        cp.wait()

    def row_copy(idx_s, col, buf, which, t, k):
        return pltpu.make_async_copy(yg_ref.at[idx_s[k, col + t]],
                                     buf.at[pl.ds((k * tc + t) * ROW_TILE, ROW_TILE), :], sem.at[which])

    def start_rows(idx_s, col, buf, which):
        for t in range(tc):
            for k in range(TOP_K):
                row_copy(idx_s, col, buf, which, t, k).start(priority=k % DMA_QUEUES)

    def wait_rows(buf, which):
        pltpu.make_async_copy(buf, buf, sem.at[which]).wait()

    def finish_tile(buf, r0):
        rows = slice(r0, r0 + tc)
        hb = h_ref[rows, :]
        g = jnp.dot(hb, sg_ref[...], preferred_element_type=F32)
        u = jnp.dot(hb, su_ref[...], preferred_element_type=F32)
        ffn = jnp.dot((_silu(g) * u).astype(BF16), sd_ref[...], preferred_element_type=F32)
        ffn_lo, ffn_hi = ffn[:, :half], ffn[:, half:]
        for k in range(TOP_K):
            lo, hi = _unpack_halves(_load_row_tiles(buf, k * tc * ROW_TILE, tc))
            w = wt_ref[rows, k:k + 1]
            ffn_lo += lo * w
            ffn_hi += hi * w
        o_ref[rows, :half] = x1_ref[rows, :half] + mod_ref[0, 5:6, :half] * ffn_lo
        o_ref[rows, half:] = x1_ref[rows, half:] + mod_ref[0, 5:6, half:] * ffn_hi

    @pl.when(i == 0)
    def _():
        def body(t, c):
            for k in range(TOP_K):
                row_copy(dest_s, 0, buf_a, 0, t, k).start(priority=k % DMA_QUEUES)
            return c

        lax.fori_loop(0, tc, body, 0)

    wait_rows(buf_a, 0)
    start_rows(dest_s, tc, buf_b, 1)
    finish_tile(buf_a, 0)
    wait_rows(buf_b, 1)
    start_rows(dnext_s, 0, buf_a, 0)
    finish_tile(buf_b, tc)

    @pl.when(i == last)
    def _():
        wait_rows(buf_a, 0)


def _combine(dest, wt_t, x1, h2, mod3, sg_b, su_b, sd_b, yg):
    T, D = x1.shape
    tc = TM_COMBINE
    step = 2 * tc
    per_batch = SEQ // step
    F = D_EXPERT
    return pl.pallas_call(
        _combine_kernel,
        grid=(T // step,),
        in_specs=[
            pl.BlockSpec((TOP_K, step), lambda i: (0, i)),
            pl.BlockSpec((TOP_K, step), lambda i: (0, jnp.minimum(i + 1, T // step - 1))),
            pl.BlockSpec((step, TOP_K), lambda i: (i, 0)),
            pl.BlockSpec((step, D), lambda i: (i, 0)),
            pl.BlockSpec((step, D), lambda i: (i, 0)),
            pl.BlockSpec((1, 6, D), lambda i: (i // per_batch, 0, 0)),
            pl.BlockSpec((D, F), lambda i: (0, 0)),
            pl.BlockSpec((D, F), lambda i: (0, 0)),
            pl.BlockSpec((F, D), lambda i: (0, 0)),
            pl.BlockSpec(memory_space=pl.ANY),
        ],
        out_specs=pl.BlockSpec((step, D), lambda i: (i, 0)),
        out_shape=jax.ShapeDtypeStruct((T, D), F32),
        scratch_shapes=[
            pltpu.SMEM((TOP_K, step), I32),
            pltpu.SMEM((TOP_K, step), I32),
            pltpu.VMEM((TOP_K * tc * ROW_TILE, LANES), U32),
            pltpu.VMEM((TOP_K * tc * ROW_TILE, LANES), U32),
            pltpu.SemaphoreType.DMA,
            pltpu.SemaphoreType.DMA((2,)),
        ],
        compiler_params=_cparams(("arbitrary",)),
        name="combine",
    )(dest, dest, wt_t, x1, h2, mod3, sg_b, su_b, sd_b, yg)


def kernel(x, c, ada_w, ada_b, mix_norm_g, ffn_norm_g, w_in, q_norm_g, k_norm_g, rel_bias, sgu_ln_g, sgu_ln_b, sgu_w, sgu_b, w_out, router_w, router_bias, shared_w_gate, shared_w_up, shared_w_down, expert_w_gate, expert_w_up, expert_w_down):
    B, S, D = x.shape
    assert S == SEQ and D == D_MODEL and ada_w.shape[0] == 1
    T = B * S
    x2 = x.reshape(T, D)

    mod3 = _modulation(c, ada_w[0], ada_b[0]).reshape(B, 6, D)

    proj = _in_projection(x2, mod3, mix_norm_g, w_in[0].astype(BF16), q_norm_g, k_norm_g)
    attn = _attention(proj.reshape(B, S, D_IN_PROJ), _bias_tables(rel_bias)).reshape(T, D_ATTN)
    causal = jnp.tril(jnp.ones((SGU_CHUNK, SGU_CHUNK), F32))
    sgu = _spatial_gating(proj, (sgu_w[0] * causal).astype(BF16), sgu_ln_g, sgu_ln_b, sgu_b[0].T)
    x1, h2, h2p = _out_projection(attn, sgu, x2, mod3, ffn_norm_g, w_out[0].astype(BF16))

    idx, wts, rank, counts = _route(h2, router_w[0].T.astype(BF16), router_bias[0])

    bm = EXPERT_BLOCK
    counts = counts.reshape(N_EXPERTS)
    padded = (counts + bm - 1) // bm * bm
    pends = jnp.cumsum(padded)
    pstarts = pends - padded
    n_blocks = T * TOP_K // bm + N_EXPERTS
    block_start = jnp.arange(n_blocks, dtype=I32) * bm
    block_exp = jnp.minimum(jnp.sum((pends[None, :] <= block_start[:, None]).astype(I32), axis=1), N_EXPERTS - 1)
    n_used = (pends[-1] // bm).astype(I32).reshape(1)
    eids = jnp.arange(N_EXPERTS, dtype=I32)
    dest = jnp.sum(jnp.where(idx[:, :, None] == eids, pstarts.astype(I32), 0), axis=-1) + rank
    pend0 = jnp.concatenate([jnp.zeros((1,), I32), pends.astype(I32)])
    first = jnp.concatenate([jnp.ones((1,), I32), (block_exp[1:] != block_exp[:-1]).astype(I32)])
    used = counts > 0
    ordinal = jnp.cumsum(used.astype(I32)) - 1
    n_groups = jnp.sum(used.astype(I32)).reshape(1)
    used_list = jnp.sum(jnp.where(used[None, :] & (ordinal[None, :] == eids[:, None]), eids[None, :], 0), axis=-1)
    gidx = jnp.sum(jnp.where(block_exp[:, None] == eids, ordinal, 0), axis=-1)

    n_rows = n_blocks * bm
    xg = _dispatch(h2p.reshape(T, ROW_TILE, LANES), dest, pend0, n_rows)
    yg = _experts(xg.reshape(n_rows * ROW_TILE, LANES), first, gidx.astype(I32), used_list.astype(I32),
                  n_used, n_groups, expert_w_gate[0], expert_w_up[0], expert_w_down[0])
    out = _combine(dest, wts.T, x1, h2, mod3, shared_w_gate[0].astype(BF16), shared_w_up[0].astype(BF16),
                   shared_w_down[0].astype(BF16), yg.reshape(n_rows, ROW_TILE, LANES))
    return out.reshape(B, S, D)
```

```python
import functools
import math

import numpy as np
import jax
import jax.numpy as jnp
from jax import lax
from jax.experimental import pallas as pl
from jax.experimental.pallas import tpu as pltpu

F32 = jnp.float32
BF16 = jnp.bfloat16
I32 = jnp.int32
U32 = jnp.uint32

D_MODEL = 2048
SEQ = 2048
HEAD_DIM = 128
N_HEADS = 8
D_ATTN = N_HEADS * HEAD_DIM
D_SGU = 1024
N_SGU_GROUPS = 8
SGU_CHUNK = 128
D_IN_PROJ = 3 * D_ATTN + 2 * D_SGU
DILATED_GROUPS = ((128, 1), (512, 4), (2048, 16))
BAND = 128
NUM_REL_BUCKETS = 32
REL_MAX_DISTANCE = 2048
N_EXPERTS = 256
TOP_K = 8
N_EXPERT_GROUPS = 8
GROUP_SIZE = N_EXPERTS // N_EXPERT_GROUPS
TOPK_GROUPS = 4
D_EXPERT = 512
ROUTED_SCALE = 2.5
NORM_EPS = 1e-6
MASK_VALUE = -1e30

LANES = 128
ROW_TILE = D_MODEL // 2 // LANES
VMEM_LIMIT = 56 * 1024 * 1024
DMA_QUEUES = 2

TM_PROJ = 512
TM_SGU = 512
TM_OUT = 256
TM_ROUTE = 512
TM_DISPATCH = 256
TM_COMBINE = 128
EXPERT_BLOCK = 128
CAST_CHUNK = 256
ATTN_PIECES_IN_FLIGHT = 4


def _cparams(sem):
    return pltpu.CompilerParams(dimension_semantics=sem, vmem_limit_bytes=VMEM_LIMIT)


def _silu(v):
    return v * jax.nn.sigmoid(v)


def _gelu(v):
    return 0.5 * v * (1.0 + lax.erf(v * (1.0 / math.sqrt(2.0))))


def _pack_halves(v):
    h = v.shape[1] // 2
    lo = lax.bitcast_convert_type(v[:, :h].astype(BF16).astype(F32), U32)
    hi = lax.bitcast_convert_type(v[:, h:].astype(BF16).astype(F32), U32)
    return (hi & jnp.uint32(0xFFFF0000)) | (lo >> 16)


def _unpack_halves(w):
    lo = lax.bitcast_convert_type(w << 16, F32)
    hi = lax.bitcast_convert_type(w & jnp.uint32(0xFFFF0000), F32)
    return lo, hi


def _store_row_tiles(ref, base, rows, packed):
    for s in range(ROW_TILE):
        ref[pl.ds(base + s, rows, stride=ROW_TILE), :] = packed[:, s * LANES:(s + 1) * LANES]


def _load_row_tiles(ref, base, rows):
    return jnp.concatenate([ref[pl.ds(base + s, rows, stride=ROW_TILE), :] for s in range(ROW_TILE)], axis=1)


def _mod_kernel(c_ref, w_ref, b_ref, o_ref):
    ca = _silu(c_ref[...]).astype(BF16)
    o_ref[...] = jnp.dot(ca, w_ref[...].astype(BF16), preferred_element_type=F32) + b_ref[...]


def _modulation(c, ada_w, ada_b):
    B, D = c.shape
    N = ada_w.shape[1]
    tn = 1024
    return pl.pallas_call(
        _mod_kernel,
        grid=(N // tn,),
        in_specs=[
            pl.BlockSpec((B, D), lambda j: (0, 0)),
            pl.BlockSpec((D, tn), lambda j: (0, j)),
            pl.BlockSpec((1, tn), lambda j: (0, j)),
        ],
        out_specs=pl.BlockSpec((B, tn), lambda j: (0, j)),
        out_shape=jax.ShapeDtypeStruct((B, N), F32),
        compiler_params=_cparams(("arbitrary",)),
        name="modulation",
    )(c, ada_w, ada_b.reshape(1, N))


def _inproj_kernel(x_ref, mod_ref, g_ref, w_ref, qg_ref, kg_ref, o_ref, h_ref):
    j = pl.program_id(1)

    @pl.when(j == 0)
    def _():
        x = x_ref[...]
        r = lax.rsqrt(jnp.mean(x * x, axis=-1, keepdims=True) + NORM_EPS)
        h = (x * r) * g_ref[...] * (1.0 + mod_ref[0, 1:2, :]) + mod_ref[0, 0:1, :]
        h_ref[...] = h.astype(BF16)

    acc = jnp.dot(h_ref[...], w_ref[...], preferred_element_type=F32)

    def head_norm(gain_ref, scale):
        for hd in range(N_HEADS):
            a = acc[:, hd * HEAD_DIM:(hd + 1) * HEAD_DIM]
            r = lax.rsqrt(jnp.mean(a * a, axis=-1, keepdims=True) + NORM_EPS)
            o_ref[:, hd * HEAD_DIM:(hd + 1) * HEAD_DIM] = ((a * r) * gain_ref[...] * scale).astype(BF16)

    @pl.when(j == 0)
    def _():
        head_norm(qg_ref, HEAD_DIM ** -0.5)

    @pl.when(j == 1)
    def _():
        head_norm(kg_ref, 1.0)

    @pl.when(j >= 2)
    def _():
        o_ref[...] = acc.astype(BF16)


def _in_projection(x2, mod3, norm_g, w_in_b, q_g, k_g):
    T, D = x2.shape
    tm, tn = TM_PROJ, D_ATTN
    per_batch = SEQ // tm
    return pl.pallas_call(
        _inproj_kernel,
        grid=(T // tm, D_IN_PROJ // tn),
        in_specs=[
            pl.BlockSpec((tm, D), lambda i, j: (i, 0)),
            pl.BlockSpec((1, 6, D), lambda i, j: (i // per_batch, 0, 0)),
            pl.BlockSpec((1, D), lambda i, j: (0, 0)),
            pl.BlockSpec((D, tn), lambda i, j: (0, j)),
            pl.BlockSpec((1, HEAD_DIM), lambda i, j: (0, 0)),
            pl.BlockSpec((1, HEAD_DIM), lambda i, j: (0, 0)),
        ],
        out_specs=pl.BlockSpec((tm, tn), lambda i, j: (i, j)),
        out_shape=jax.ShapeDtypeStruct((T, D_IN_PROJ), BF16),
        scratch_shapes=[pltpu.VMEM((tm, D), BF16)],
        compiler_params=_cparams(("arbitrary", "arbitrary")),
        name="in_projection",
    )(x2, mod3, norm_g, w_in_b, q_g, k_g)


def _bias_tables(rel_bias):
    n = BAND
    qi = jnp.arange(n)[:, None]
    ki = jnp.arange(2 * n)[None, :]
    steps = n + qi - ki
    in_band = (steps >= 0) & (steps <= n)
    max_exact = NUM_REL_BUCKETS // 2
    tabs = []
    for _, dilation in DILATED_GROUPS:
        dist = jnp.clip(steps, 0, n) * dilation
        nf = jnp.maximum(dist, 1).astype(F32)
        large = max_exact + (jnp.log(nf / max_exact) / math.log(REL_MAX_DISTANCE / max_exact)
                             * (NUM_REL_BUCKETS - max_exact)).astype(I32)
        large = jnp.minimum(large, NUM_REL_BUCKETS - 1)
        bucket = jnp.where(dist < max_exact, dist, large)
        onehot = jax.nn.one_hot(bucket, NUM_REL_BUCKETS, dtype=F32)
        b = jnp.einsum("qkb,bh->hqk", onehot, rel_bias.astype(F32), precision=lax.Precision.HIGHEST)
        tabs.append(jnp.where(in_band[None], b, MASK_VALUE))
    return jnp.stack(tabs)


def _attn_kernel(q_ref, k_ref, v_ref, bias_ref, o_ref, qf, kf, vf, q4, k4, v4,
                 o0, l0, o1, l1, o2, l2, stage):
    n = BAND
    quarter = SEQ // 4

    for src, nat, res in ((q_ref, qf, q4), (k_ref, kf, k4), (v_ref, vf, v4)):
        nat[...] = src[...].astype(F32)
        for r in range(4):
            res[r * quarter:(r + 1) * quarter, :] = nat[pl.ds(r, quarter, stride=4), :]

    def logits(qb, kb, bias):
        return lax.dot_general(qb, kb, (((1,), (1,)), ((), ())), preferred_element_type=F32) + bias

    def finish(l, vb, o_out, l_out, rows):
        m = jnp.max(l, axis=-1, keepdims=True)
        p = jnp.exp(l - m)
        s = jnp.sum(p, axis=-1, keepdims=True)
        o = jnp.dot(p.astype(BF16), vb, preferred_element_type=F32)
        o_out[rows, :] = o / s
        l_out[rows, :] = jnp.broadcast_to(m + jnp.log(s), (n, LANES))

    pieces = []

    for i in range(SEQ // n):
        rows = slice(i * n, (i + 1) * n)
        krows = rows if i == 0 else slice((i - 1) * n, (i + 1) * n)
        bias = (0, slice(None), slice(n, None)) if i == 0 else (0,)
        pieces.append((q_ref, k_ref, v_ref, rows, krows, bias, o0, l0))

    for r in range(4):
        for blk in range(quarter // n):
            base = r * quarter + blk * n
            rows = slice(base, base + n)
            krows = rows if blk == 0 else slice(base - n, base + n)
            bias = (1, slice(None), slice(n, None)) if blk == 0 else (1,)
            pieces.append((q4, k4, v4, rows, krows, bias, o1, l1))

    for r in range(4):
        for a in range(4):
            rows = pl.ds(r * quarter + a, n, stride=4)
            pieces.append((q4, k4, v4, rows, rows, (2, slice(None), slice(n, None)), o2, l2))

    pending = []
    for qs, ks, vs, rows, krows, bias, o_out, l_out in pieces:
        l = logits(qs[rows, :].astype(BF16), ks[krows, :].astype(BF16), bias_ref[bias])
        if len(pending) == ATTN_PIECES_IN_FLIGHT:
            finish(*pending.pop(0))
        pending.append((l, vs[krows, :].astype(BF16), o_out, l_out, rows))
    for args in pending:
        finish(*args)

    for r in range(4):
        for c in range(quarter // n):
            rows = slice(r * quarter + c * n, r * quarter + (c + 1) * n)
            nat = pl.ds(r + 4 * c * n, n, stride=4)
            a0, a1, a2 = l0[nat, :], l1[rows, :], l2[rows, :]
            m = jnp.maximum(jnp.maximum(a0, a1), a2)
            e0, e1, e2 = jnp.exp(a0 - m), jnp.exp(a1 - m), jnp.exp(a2 - m)
            mix = (e0 * o0[nat, :] + e1 * o1[rows, :] + e2 * o2[rows, :]) / (e0 + e1 + e2)
            stage[nat, :] = mix
    o_ref[...] = stage[...].astype(BF16)


def _attention(proj3, bias_tabs):
    B = proj3.shape[0]
    blk = lambda off: pl.BlockSpec((None, SEQ, HEAD_DIM), lambda b, h: (b, 0, off + h))
    return pl.pallas_call(
        _attn_kernel,
        grid=(B, N_HEADS),
        in_specs=[
            blk(0), blk(N_HEADS), blk(2 * N_HEADS),
            pl.BlockSpec((3, None, BAND, 2 * BAND), lambda b, h: (0, h, 0, 0)),
        ],
        out_specs=pl.BlockSpec((None, SEQ, HEAD_DIM), lambda b, h: (b, 0, h)),
        out_shape=jax.ShapeDtypeStruct((B, SEQ, D_ATTN), BF16),
        scratch_shapes=[pltpu.VMEM((SEQ, HEAD_DIM), F32) for _ in range(13)],
        compiler_params=_cparams(("arbitrary", "arbitrary")),
        name="dilated_attention",
    )(proj3, proj3, proj3, bias_tabs)


def _sgu_kernel(u_ref, z_ref, w_ref, g_ref, b_ref, bs_ref, o_ref):
    n = SGU_CHUNK
    for c in range(TM_SGU // n):
        rs = slice(c * n, (c + 1) * n)
        for g in range(N_SGU_GROUPS):
            cs = slice(g * n, (g + 1) * n)
            z = _gelu(z_ref[rs, cs].astype(F32))
            mu = jnp.mean(z, axis=-1, keepdims=True)
            zc = z - mu
            var = jnp.mean(zc * zc, axis=-1, keepdims=True)
            zn = (zc * lax.rsqrt(var + NORM_EPS)) * g_ref[:, cs] + b_ref[:, cs]
            mixed = jnp.dot(w_ref[g], zn.astype(BF16), preferred_element_type=F32) + bs_ref[:, g:g + 1]
            u = _gelu(u_ref[rs, cs].astype(F32))
            o_ref[rs, cs] = (u * mixed).astype(BF16)


def _spatial_gating(proj, w_causal_b, ln_g, ln_b, bs_t):
    T = proj.shape[0]
    tm = TM_SGU
    ucol = 3 * D_ATTN // D_SGU
    return pl.pallas_call(
        _sgu_kernel,
        grid=(T // tm,),
        in_specs=[
            pl.BlockSpec((tm, D_SGU), lambda i: (i, ucol)),
            pl.BlockSpec((tm, D_SGU), lambda i: (i, ucol + 1)),
            pl.BlockSpec((N_SGU_GROUPS, SGU_CHUNK, SGU_CHUNK), lambda i: (0, 0, 0)),
            pl.BlockSpec((1, D_SGU), lambda i: (0, 0)),
            pl.BlockSpec((1, D_SGU), lambda i: (0, 0)),
            pl.BlockSpec((SGU_CHUNK, N_SGU_GROUPS), lambda i: (0, 0)),
        ],
        out_specs=pl.BlockSpec((tm, D_SGU), lambda i: (i, 0)),
        out_shape=jax.ShapeDtypeStruct((T, D_SGU), BF16),
        compiler_params=_cparams(("arbitrary",)),
        name="spatial_gating",
    )(proj, proj, w_causal_b, ln_g, ln_b, bs_t)


def _outproj_kernel(a_ref, s_ref, x_ref, mod_ref, g_ref, w_ref, x1_ref, h2_ref, h2p_ref):
    mixed = jnp.dot(a_ref[...], w_ref[0:D_ATTN, :], preferred_element_type=F32)
    mixed += jnp.dot(s_ref[...], w_ref[D_ATTN:, :], preferred_element_type=F32)
    x1 = x_ref[...] + mod_ref[0, 2:3, :] * mixed
    x1_ref[...] = x1
    r = lax.rsqrt(jnp.mean(x1 * x1, axis=-1, keepdims=True) + NORM_EPS)
    h2 = (x1 * r) * g_ref[...] * (1.0 + mod_ref[0, 4:5, :]) + mod_ref[0, 3:4, :]
    h2_ref[...] = h2.astype(BF16)
    _store_row_tiles(h2p_ref, 0, TM_OUT, _pack_halves(h2))


def _out_projection(attn, sgu, x2, mod3, norm_g, w_out_b):
    T, D = x2.shape
    tm = TM_OUT
    per_batch = SEQ // tm
    return pl.pallas_call(
        _outproj_kernel,
        grid=(T // tm,),
        in_specs=[
            pl.BlockSpec((tm, D_ATTN), lambda i: (i, 0)),
            pl.BlockSpec((tm, D_SGU), lambda i: (i, 0)),
            pl.BlockSpec((tm, D), lambda i: (i, 0)),
            pl.BlockSpec((1, 6, D), lambda i: (i // per_batch, 0, 0)),
            pl.BlockSpec((1, D), lambda i: (0, 0)),
            pl.BlockSpec((D_ATTN + D_SGU, D), lambda i: (0, 0)),
        ],
        out_specs=[pl.BlockSpec((tm, D), lambda i: (i, 0)), pl.BlockSpec((tm, D), lambda i: (i, 0)),
                   pl.BlockSpec((tm * ROW_TILE, LANES), lambda i: (i, 0))],
        out_shape=[jax.ShapeDtypeStruct((T, D), F32), jax.ShapeDtypeStruct((T, D), BF16),
                   jax.ShapeDtypeStruct((T * ROW_TILE, LANES), U32)],
        compiler_params=_cparams(("arbitrary",)),
        name="out_projection",
    )(attn, sgu, x2, mod3, norm_g, w_out_b)


def _router_kernel(h_ref, w_ref, b_ref, idx_ref, wt_ref, rank_ref, cnt_ref, run_ref):
    i = pl.program_id(0)
    tr = TM_ROUTE
    E = N_EXPERTS

    @pl.when(i == 0)
    def _():
        run_ref[...] = jnp.zeros_like(run_ref)

    logits = lax.dot_general(w_ref[...], h_ref[...], (((1,), (1,)), ((), ())), preferred_element_type=F32)
    scores = jax.nn.sigmoid(logits)
    sel = scores + b_ref[...]

    slabs = [sel[g * GROUP_SIZE:(g + 1) * GROUP_SIZE, :] for g in range(N_EXPERT_GROUPS)]
    si = lax.broadcasted_iota(I32, (GROUP_SIZE, tr), 0).astype(F32)
    gs = []
    for slab in slabs:
        m1 = jnp.max(slab, axis=0, keepdims=True)
        first = jnp.min(jnp.where(slab == m1, si, float(GROUP_SIZE)), axis=0, keepdims=True)
        m2 = jnp.max(jnp.where(si == first, -jnp.inf, slab), axis=0, keepdims=True)
        gs.append(m1 + m2)

    kept = []
    for g in range(N_EXPERT_GROUPS):
        beaten = jnp.zeros((1, tr), F32)
        for o in range(N_EXPERT_GROUPS):
            if o < g:
                beaten += (gs[o] >= gs[g]).astype(F32)
            elif o > g:
                beaten += (gs[o] > gs[g]).astype(F32)
        kept.append(jnp.where(beaten < TOPK_GROUPS, slabs[g], -jnp.inf))
    masked = jnp.concatenate(kept, axis=0)

    ei = lax.broadcasted_iota(I32, (E, tr), 0).astype(F32)
    picks, pick_scores = [], []
    onehot_sum = jnp.zeros((E, tr), F32)
    for k in range(TOP_K):
        m = jnp.max(masked, axis=0, keepdims=True)
        ik = jnp.min(jnp.where(masked == m, ei, float(E)), axis=0, keepdims=True)
        oh = ei == ik
        pick_scores.append(jnp.sum(jnp.where(oh, scores, 0.0), axis=0, keepdims=True))
        masked = jnp.where(oh, -jnp.inf, masked)
        onehot_sum += oh.astype(F32)
        picks.append(ik)

    ti = lax.broadcasted_iota(I32, (tr, tr), 0)
    tj = lax.broadcasted_iota(I32, (tr, tr), 1)
    before = (ti < tj).astype(BF16)
    prior = jnp.dot(onehot_sum.astype(BF16), before, preferred_element_type=F32) + run_ref[...]

    total = pick_scores[0]
    for k in range(1, TOP_K):
        total += pick_scores[k]
    for k in range(TOP_K):
        idx_ref[k:k + 1, :] = picks[k].astype(I32)
        wt_ref[k:k + 1, :] = pick_scores[k] / total * ROUTED_SCALE
        rk = jnp.sum(jnp.where(ei == picks[k], prior, 0.0), axis=0, keepdims=True)
        rank_ref[k:k + 1, :] = rk.astype(I32)

    run_ref[...] += jnp.sum(onehot_sum, axis=1, keepdims=True)
    cnt_ref[...] = run_ref[...].astype(I32)


def _route(h2, router_wt_b, router_bias):
    T, D = h2.shape
    tr = TM_ROUTE
    E = N_EXPERTS
    row_spec = pl.BlockSpec((TOP_K, tr), lambda i: (0, i))
    return pl.pallas_call(
        _router_kernel,
        grid=(T // tr,),
        in_specs=[
            pl.BlockSpec((tr, D), lambda i: (i, 0)),
            pl.BlockSpec((E, D), lambda i: (0, 0)),
            pl.BlockSpec((E, 1), lambda i: (0, 0)),
        ],
        out_specs=[row_spec, row_spec, row_spec, pl.BlockSpec((E, 1), lambda i: (0, 0))],
        out_shape=[
            jax.ShapeDtypeStruct((TOP_K, T), I32),
            jax.ShapeDtypeStruct((TOP_K, T), F32),
            jax.ShapeDtypeStruct((TOP_K, T), I32),
            jax.ShapeDtypeStruct((E, 1), I32),
        ],
        scratch_shapes=[pltpu.VMEM((E, 1), F32)],
        compiler_params=_cparams(("arbitrary",)),
        name="router",
    )(h2, router_wt_b, router_bias.reshape(E, 1))


def _dispatch_kernel(pend_ref, h_ref, dest_ref, xg_ref, dest_s, zero_ref, sem_s, sem):
    i = pl.program_id(0)
    td = TM_DISPATCH
    blk = EXPERT_BLOCK

    def zero_copy(e):
        start = pl.multiple_of(pend_ref[e + 1] - blk, blk)
        return pltpu.make_async_copy(zero_ref, xg_ref.at[pl.ds(start, blk)], sem)

    @pl.when(i == 0)
    def _():
        zero_ref[...] = jnp.zeros_like(zero_ref)

        def start_zero(e, c):
            @pl.when(pend_ref[e + 1] > pend_ref[e])
            def _():
                zero_copy(e).start()
            return c

        def wait_zero(e, c):
            @pl.when(pend_ref[e + 1] > pend_ref[e])
            def _():
                zero_copy(e).wait()
            return c

        lax.fori_loop(0, N_EXPERTS, start_zero, 0)
        lax.fori_loop(0, N_EXPERTS, wait_zero, 0)

        def tail_copy(b):
            return pltpu.make_async_copy(zero_ref, xg_ref.at[pl.ds(pl.multiple_of(b * blk, blk), blk)], sem)

        first_unused = pend_ref[N_EXPERTS] // blk
        n_blocks = xg_ref.shape[0] // blk
        lax.fori_loop(first_unused, n_blocks, lambda b, c: (tail_copy(b).start(), c)[1], 0)
        lax.fori_loop(first_unused, n_blocks, lambda b, c: (tail_copy(b).wait(), c)[1], 0)

    cp = pltpu.make_async_copy(dest_ref, dest_s, sem_s)
    cp.start()
    cp.wait()

    def row_copy(t, k):
        return pltpu.make_async_copy(h_ref.at[t], xg_ref.at[dest_s[k, t]], sem)

    def start_rows(t8, c):
        base = pl.multiple_of(t8 * 8, 8)
        for j in range(8):
            for k in range(TOP_K):
                row_copy(base + j, k).start(priority=k % DMA_QUEUES)
        return c

    lax.fori_loop(0, td // 8, start_rows, 0)
    for k in range(TOP_K):
        pltpu.make_async_copy(h_ref, h_ref, sem).wait()


def _dispatch(h2p, dest, pend0, rows):
    T = h2p.shape[0]
    td = TM_DISPATCH
    tile = (ROW_TILE, LANES)
    return pl.pallas_call(
        _dispatch_kernel,
        grid_spec=pltpu.PrefetchScalarGridSpec(
            num_scalar_prefetch=1,
            grid=(T // td,),
            in_specs=[
                pl.BlockSpec((td,) + tile, lambda i, p: (i, 0, 0)),
                pl.BlockSpec((TOP_K, td), lambda i, p: (0, i)),
            ],
            out_specs=pl.BlockSpec(memory_space=pl.ANY),
            scratch_shapes=[
                pltpu.SMEM((TOP_K, td), I32),
                pltpu.VMEM((EXPERT_BLOCK,) + tile, U32),
                pltpu.SemaphoreType.DMA,
                pltpu.SemaphoreType.DMA,
            ],
        ),
        out_shape=jax.ShapeDtypeStruct((rows,) + tile, U32),
        compiler_params=_cparams(("arbitrary",)),
        name="dispatch",
    )(pend0, h2p, dest)


def _experts_kernel(first_ref, gidx_ref, used_ref, nused_ref, ngroups_ref,
                    x_ref, wg_hbm, wu_hbm, wd_hbm, y_ref,
                    wgu_f, wd_f, wgu_b, wd_b, issued, sem):
    i = pl.program_id(0)
    F = D_EXPERT

    def tensor_copy(e, s, t):
        if t == 0:
            return pltpu.make_async_copy(wg_hbm.at[e], wgu_f.at[s, 0], sem.at[s, 0])
        if t == 1:
            return pltpu.make_async_copy(wu_hbm.at[e], wgu_f.at[s, 1], sem.at[s, 1])
        return pltpu.make_async_copy(wd_hbm.at[e], wd_f.at[s], sem.at[s, 2])

    def issue_until(n):
        def body(q, carry):
            h = q // 3
            for t in range(3):
                @pl.when(q % 3 == t)
                def _():
                    tensor_copy(used_ref[h], h % 2, t).start(priority=DMA_QUEUES - 1)
            return carry

        lax.fori_loop(issued[0], n, body, 0)
        issued[0] = jnp.maximum(issued[0], n)

    @pl.when(i == 0)
    def _():
        issued[0] = 0

    @pl.when(i < nused_ref[0])
    def _():
        g = gidx_ref[i]

        def block(cast_from):
            lo, hi = _unpack_halves(_load_row_tiles(x_ref, 0, EXPERT_BLOCK))
            x = jnp.concatenate([lo.astype(BF16), hi.astype(BF16)], axis=1)
            if cast_from is None:
                gu = jnp.dot(x, wgu_b[...], preferred_element_type=F32)
            else:
                parts = []
                for c in range(2 * F // CAST_CHUNK):
                    cols = slice(c * CAST_CHUNK, (c + 1) * CAST_CHUNK)
                    src = slice((c * CAST_CHUNK) % F, (c * CAST_CHUNK) % F + CAST_CHUNK)
                    wgu_b[:, cols] = wgu_f[cast_from, c * CAST_CHUNK // F, :, src].astype(BF16)
                    parts.append(jnp.dot(x, wgu_b[:, cols], preferred_element_type=F32))
                gu = jnp.concatenate(parts, axis=1)
                wd_b[...] = wd_f[cast_from].astype(BF16)
            a = (_silu(gu[:, :F]) * gu[:, F:]).astype(BF16)
            y = jnp.dot(a, wd_b[...], preferred_element_type=F32)
            _store_row_tiles(y_ref, 0, EXPERT_BLOCK, _pack_halves(y))

        @pl.when(first_ref[i] == 1)
        def _():
            issue_until(3 * (g + 1))
            s = g % 2
            for t in range(3):
                tensor_copy(used_ref[g], s, t).wait()
            block(s)

        @pl.when(first_ref[i] != 1)
        def _():
            block(None)

        cap = 3 * jnp.minimum(g + 3, ngroups_ref[0])
        issue_until(jnp.minimum(issued[0] + 1, cap))


def _experts(xg, first, gidx, used_list, n_used, n_groups, w_gate, w_up, w_down):
    D = D_MODEL
    bm = EXPERT_BLOCK
    F = D_EXPERT
    nb = xg.shape[0] // (bm * ROW_TILE)
    row_map = lambda i, fi, gi, ul, nu, ng: (jnp.minimum(i, nu[0] - 1), 0)
    return pl.pallas_call(
        _experts_kernel,
        grid_spec=pltpu.PrefetchScalarGridSpec(
            num_scalar_prefetch=5,
            grid=(nb,),
            in_specs=[
                pl.BlockSpec((bm * ROW_TILE, LANES), row_map),
                pl.BlockSpec(memory_space=pl.ANY),
                pl.BlockSpec(memory_space=pl.ANY),
                pl.BlockSpec(memory_space=pl.ANY),
            ],
            out_specs=pl.BlockSpec((bm * ROW_TILE, LANES), row_map),
            scratch_shapes=[
                pltpu.VMEM((2, 2, D, F), F32), pltpu.VMEM((2, F, D), F32),
                pltpu.VMEM((D, 2 * F), BF16), pltpu.VMEM((F, D), BF16),
                pltpu.SMEM((1,), I32),
                pltpu.SemaphoreType.DMA((2, 3)),
            ],
        ),
        out_shape=jax.ShapeDtypeStruct(xg.shape, U32),
        input_output_aliases={5: 0},
        compiler_params=_cparams(("arbitrary",)),
        name="experts",
    )(first, gidx, used_list, n_used, n_groups, xg, w_gate, w_up, w_down)


def _combine_kernel(dest_ref, dnext_ref, wt_ref, x1_ref, h_ref, mod_ref, sg_ref, su_ref, sd_ref, yg_ref, o_ref,
                    dest_s, dnext_s, buf_a, buf_b, sem_s, sem):
    i = pl.program_id(0)
    last = pl.num_programs(0) - 1
    tc = TM_COMBINE
    half = D_MODEL // 2

    for src, dst in ((dest_ref, dest_s), (dnext_ref, dnext_s)):
        cp = pltpu.make_async_copy(src, dst, sem_s)
        cp.start()
        cp.wait()

    def row_copy(idx_s, col, buf, which, t, k):
        return pltpu.make_async_copy(yg_ref.at[idx_s[k, col + t]],
                                     buf.at[pl.ds((k * tc + t) * ROW_TILE, ROW_TILE), :], sem.at[which])

    def start_rows(idx_s, col, buf, which):
        for t in range(tc):
            for k in range(TOP_K):
                row_copy(idx_s, col, buf, which, t, k).start(priority=k % DMA_QUEUES)

    def wait_rows(buf, which):
        pltpu.make_async_copy(buf, buf, sem.at[which]).wait()

    def finish_tile(buf, r0):
        rows = slice(r0, r0 + tc)
        hb = h_ref[rows, :]
        g = jnp.dot(hb, sg_ref[...], preferred_element_type=F32)
        u = jnp.dot(hb, su_ref[...], preferred_element_type=F32)
        ffn = jnp.dot((_silu(g) * u).astype(BF16), sd_ref[...], preferred_element_type=F32)
        ffn_lo, ffn_hi = ffn[:, :half], ffn[:, half:]
        for k in range(TOP_K):
            lo, hi = _unpack_halves(_load_row_tiles(buf, k * tc * ROW_TILE, tc))
            w = wt_ref[rows, k:k + 1]
            ffn_lo += lo * w
            ffn_hi += hi * w
        o_ref[rows, :half] = x1_ref[rows, :half] + mod_ref[0, 5:6, :half] * ffn_lo
        o_ref[rows, half:] = x1_ref[rows, half:] + mod_ref[0, 5:6, half:] * ffn_hi

    @pl.when(i == 0)
    def _():
        def body(t, c):
            for k in range(TOP_K):
                row_copy(dest_s, 0, buf_a, 0, t, k).start(priority=k % DMA_QUEUES)
            return c

        lax.fori_loop(0, tc, body, 0)

    wait_rows(buf_a, 0)
    start_rows(dest_s, tc, buf_b, 1)
    finish_tile(buf_a, 0)
    wait_rows(buf_b, 1)
    start_rows(dnext_s, 0, buf_a, 0)
    finish_tile(buf_b, tc)

    @pl.when(i == last)
    def _():
        wait_rows(buf_a, 0)


def _combine(dest, wt_t, x1, h2, mod3, sg_b, su_b, sd_b, yg):
    T, D = x1.shape
    tc = TM_COMBINE
    step = 2 * tc
    per_batch = SEQ // step
    F = D_EXPERT
    return pl.pallas_call(
        _combine_kernel,
        grid=(T // step,),
        in_specs=[
            pl.BlockSpec((TOP_K, step), lambda i: (0, i)),
            pl.BlockSpec((TOP_K, step), lambda i: (0, jnp.minimum(i + 1, T // step - 1))),
            pl.BlockSpec((step, TOP_K), lambda i: (i, 0)),
            pl.BlockSpec((step, D), lambda i: (i, 0)),
            pl.BlockSpec((step, D), lambda i: (i, 0)),
            pl.BlockSpec((1, 6, D), lambda i: (i // per_batch, 0, 0)),
            pl.BlockSpec((D, F), lambda i: (0, 0)),
            pl.BlockSpec((D, F), lambda i: (0, 0)),
            pl.BlockSpec((F, D), lambda i: (0, 0)),
            pl.BlockSpec(memory_space=pl.ANY),
        ],
        out_specs=pl.BlockSpec((step, D), lambda i: (i, 0)),
        out_shape=jax.ShapeDtypeStruct((T, D), F32),
        scratch_shapes=[
            pltpu.SMEM((TOP_K, step), I32),
            pltpu.SMEM((TOP_K, step), I32),
            pltpu.VMEM((TOP_K * tc * ROW_TILE, LANES), U32),
            pltpu.VMEM((TOP_K * tc * ROW_TILE, LANES), U32),
            pltpu.SemaphoreType.DMA,
            pltpu.SemaphoreType.DMA((2,)),
        ],
        compiler_params=_cparams(("arbitrary",)),
        name="combine",
    )(dest, dest, wt_t, x1, h2, mod3, sg_b, su_b, sd_b, yg)


def kernel(x, c, ada_w, ada_b, mix_norm_g, ffn_norm_g, w_in, q_norm_g, k_norm_g, rel_bias, sgu_ln_g, sgu_ln_b, sgu_w, sgu_b, w_out, router_w, router_bias, shared_w_gate, shared_w_up, shared_w_down, expert_w_gate, expert_w_up, expert_w_down):
    B, S, D = x.shape
    assert S == SEQ and D == D_MODEL and ada_w.shape[0] == 1
    T = B * S
    x2 = x.reshape(T, D)

    mod3 = _modulation(c, ada_w[0], ada_b[0]).reshape(B, 6, D)

    proj = _in_projection(x2, mod3, mix_norm_g, w_in[0].astype(BF16), q_norm_g, k_norm_g)
    attn = _attention(proj.reshape(B, S, D_IN_PROJ), _bias_tables(rel_bias)).reshape(T, D_ATTN)
    causal = jnp.tril(jnp.ones((SGU_CHUNK, SGU_CHUNK), F32))
    sgu = _spatial_gating(proj, (sgu_w[0] * causal).astype(BF16), sgu_ln_g, sgu_ln_b, sgu_b[0].T)
    x1, h2, h2p = _out_projection(attn, sgu, x2, mod3, ffn_norm_g, w_out[0].astype(BF16))

    idx, wts, rank, counts = _route(h2, router_w[0].T.astype(BF16), router_bias[0])

    bm = EXPERT_BLOCK
    counts = counts.reshape(N_EXPERTS)
    padded = (counts + bm - 1) // bm * bm
    pends = jnp.cumsum(padded)
    pstarts = pends - padded
    n_blocks = T * TOP_K // bm + N_EXPERTS
    block_start = jnp.arange(n_blocks, dtype=I32) * bm
    block_exp = jnp.minimum(jnp.sum((pends[None, :] <= block_start[:, None]).astype(I32), axis=1), N_EXPERTS - 1)
    n_used = (pends[-1] // bm).astype(I32).reshape(1)
    eids = jnp.arange(N_EXPERTS, dtype=I32)
    dest = jnp.sum(jnp.where(idx[:, :, None] == eids, pstarts.astype(I32), 0), axis=-1) + rank
    pend0 = jnp.concatenate([jnp.zeros((1,), I32), pends.astype(I32)])
    first = jnp.concatenate([jnp.ones((1,), I32), (block_exp[1:] != block_exp[:-1]).astype(I32)])
    used = counts > 0
    ordinal = jnp.cumsum(used.astype(I32)) - 1
    n_groups = jnp.sum(used.astype(I32)).reshape(1)
    used_list = jnp.sum(jnp.where(used[None, :] & (ordinal[None, :] == eids[:, None]), eids[None, :], 0), axis=-1)
    gidx = jnp.sum(jnp.where(block_exp[:, None] == eids, ordinal, 0), axis=-1)

    n_rows = n_blocks * bm
    xg = _dispatch(h2p.reshape(T, ROW_TILE, LANES), dest, pend0, n_rows)
    yg = _experts(xg.reshape(n_rows * ROW_TILE, LANES), first, gidx.astype(I32), used_list.astype(I32),
                  n_used, n_groups, expert_w_gate[0], expert_w_up[0], expert_w_down[0])
    out = _combine(dest, wts.T, x1, h2, mod3, shared_w_gate[0].astype(BF16), shared_w_up[0].astype(BF16),
                   shared_w_down[0].astype(BF16), yg.reshape(n_rows, ROW_TILE, LANES))
    return out.reshape(B, S, D)
```

```python
import functools
import math

import numpy as np
import jax
import jax.numpy as jnp
from jax import lax
from jax.experimental import pallas as pl
from jax.experimental.pallas import tpu as pltpu

F32 = jnp.float32
BF16 = jnp.bfloat16
I32 = jnp.int32
U32 = jnp.uint32

D_MODEL = 2048
SEQ = 2048
HEAD_DIM = 128
N_HEADS = 8
D_ATTN = N_HEADS * HEAD_DIM
D_SGU = 1024
N_SGU_GROUPS = 8
SGU_CHUNK = 128
D_IN_PROJ = 3 * D_ATTN + 2 * D_SGU
DILATED_GROUPS = ((128, 1), (512, 4), (2048, 16))
BAND = 128
NUM_REL_BUCKETS = 32
REL_MAX_DISTANCE = 2048
N_EXPERTS = 256
TOP_K = 8
N_EXPERT_GROUPS = 8
GROUP_SIZE = N_EXPERTS // N_EXPERT_GROUPS
TOPK_GROUPS = 4
D_EXPERT = 512
ROUTED_SCALE = 2.5
NORM_EPS = 1e-6
MASK_VALUE = -1e30

LANES = 128
ROW_TILE = D_MODEL // 2 // LANES
VMEM_LIMIT = 56 * 1024 * 1024
DMA_QUEUES = 2

TM_PROJ = 512
TM_SGU = 512
TM_OUT = 256
TM_ROUTE = 512
TM_DISPATCH = 256
TM_COMBINE = 128
EXPERT_BLOCK = 128
WEIGHT_SLOTS = 3
CAST_CHUNK = 256
ATTN_PIECES_IN_FLIGHT = 4


def _cparams(sem):
    return pltpu.CompilerParams(dimension_semantics=sem, vmem_limit_bytes=VMEM_LIMIT)


def _silu(v):
    return v * jax.nn.sigmoid(v)


def _gelu(v):
    return 0.5 * v * (1.0 + lax.erf(v * (1.0 / math.sqrt(2.0))))


def _pack_halves(v):
    h = v.shape[1] // 2
    lo = lax.bitcast_convert_type(v[:, :h].astype(BF16).astype(F32), U32)
    hi = lax.bitcast_convert_type(v[:, h:].astype(BF16).astype(F32), U32)
    return (hi & jnp.uint32(0xFFFF0000)) | (lo >> 16)


def _unpack_halves(w):
    lo = lax.bitcast_convert_type(w << 16, F32)
    hi = lax.bitcast_convert_type(w & jnp.uint32(0xFFFF0000), F32)
    return lo, hi


def _store_row_tiles(ref, base, rows, packed):
    for s in range(ROW_TILE):
        ref[pl.ds(base + s, rows, stride=ROW_TILE), :] = packed[:, s * LANES:(s + 1) * LANES]


def _load_row_tiles(ref, base, rows):
    return jnp.concatenate([ref[pl.ds(base + s, rows, stride=ROW_TILE), :] for s in range(ROW_TILE)], axis=1)


def _mod_kernel(c_ref, w_ref, b_ref, o_ref):
    ca = _silu(c_ref[...]).astype(BF16)
    o_ref[...] = jnp.dot(ca, w_ref[...].astype(BF16), preferred_element_type=F32) + b_ref[...]


def _modulation(c, ada_w, ada_b):
    B, D = c.shape
    N = ada_w.shape[1]
    tn = 1024
    return pl.pallas_call(
        _mod_kernel,
        grid=(N // tn,),
        in_specs=[
            pl.BlockSpec((B, D), lambda j: (0, 0)),
            pl.BlockSpec((D, tn), lambda j: (0, j)),
            pl.BlockSpec((1, tn), lambda j: (0, j)),
        ],
        out_specs=pl.BlockSpec((B, tn), lambda j: (0, j)),
        out_shape=jax.ShapeDtypeStruct((B, N), F32),
        compiler_params=_cparams(("arbitrary",)),
        name="modulation",
    )(c, ada_w, ada_b.reshape(1, N))


def _inproj_kernel(x_ref, mod_ref, g_ref, w_ref, qg_ref, kg_ref, o_ref, h_ref):
    j = pl.program_id(1)

    @pl.when(j == 0)
    def _():
        x = x_ref[...]
        r = lax.rsqrt(jnp.mean(x * x, axis=-1, keepdims=True) + NORM_EPS)
        h = (x * r) * g_ref[...] * (1.0 + mod_ref[0, 1:2, :]) + mod_ref[0, 0:1, :]
        h_ref[...] = h.astype(BF16)

    acc = jnp.dot(h_ref[...], w_ref[...], preferred_element_type=F32)

    def head_norm(gain_ref, scale):
        for hd in range(N_HEADS):
            a = acc[:, hd * HEAD_DIM:(hd + 1) * HEAD_DIM]
            r = lax.rsqrt(jnp.mean(a * a, axis=-1, keepdims=True) + NORM_EPS)
            o_ref[:, hd * HEAD_DIM:(hd + 1) * HEAD_DIM] = ((a * r) * gain_ref[...] * scale).astype(BF16)

    @pl.when(j == 0)
    def _():
        head_norm(qg_ref, HEAD_DIM ** -0.5)

    @pl.when(j == 1)
    def _():
        head_norm(kg_ref, 1.0)

    @pl.when(j >= 2)
    def _():
        o_ref[...] = acc.astype(BF16)


def _in_projection(x2, mod3, norm_g, w_in_b, q_g, k_g):
    T, D = x2.shape
    tm, tn = TM_PROJ, D_ATTN
    per_batch = SEQ // tm
    return pl.pallas_call(
        _inproj_kernel,
        grid=(T // tm, D_IN_PROJ // tn),
        in_specs=[
            pl.BlockSpec((tm, D), lambda i, j: (i, 0)),
            pl.BlockSpec((1, 6, D), lambda i, j: (i // per_batch, 0, 0)),
            pl.BlockSpec((1, D), lambda i, j: (0, 0)),
            pl.BlockSpec((D, tn), lambda i, j: (0, j)),
            pl.BlockSpec((1, HEAD_DIM), lambda i, j: (0, 0)),
            pl.BlockSpec((1, HEAD_DIM), lambda i, j: (0, 0)),
        ],
        out_specs=pl.BlockSpec((tm, tn), lambda i, j: (i, j)),
        out_shape=jax.ShapeDtypeStruct((T, D_IN_PROJ), BF16),
        scratch_shapes=[pltpu.VMEM((tm, D), BF16)],
        compiler_params=_cparams(("arbitrary", "arbitrary")),
        name="in_projection",
    )(x2, mod3, norm_g, w_in_b, q_g, k_g)


def _bias_tables(rel_bias):
    n = BAND
    qi = jnp.arange(n)[:, None]
    ki = jnp.arange(2 * n)[None, :]
    steps = n + qi - ki
    in_band = (steps >= 0) & (steps <= n)
    max_exact = NUM_REL_BUCKETS // 2
    tabs = []
    for _, dilation in DILATED_GROUPS:
        dist = jnp.clip(steps, 0, n) * dilation
        nf = jnp.maximum(dist, 1).astype(F32)
        large = max_exact + (jnp.log(nf / max_exact) / math.log(REL_MAX_DISTANCE / max_exact)
                             * (NUM_REL_BUCKETS - max_exact)).astype(I32)
        large = jnp.minimum(large, NUM_REL_BUCKETS - 1)
        bucket = jnp.where(dist < max_exact, dist, large)
        onehot = jax.nn.one_hot(bucket, NUM_REL_BUCKETS, dtype=F32)
        b = jnp.einsum("qkb,bh->hqk", onehot, rel_bias.astype(F32), precision=lax.Precision.HIGHEST)
        tabs.append(jnp.where(in_band[None], b, MASK_VALUE))
    return jnp.stack(tabs)


def _attn_kernel(q_ref, k_ref, v_ref, bias_ref, o_ref, qf, kf, vf, q4, k4, v4,
                 o0, l0, o1, l1, o2, l2, stage):
    n = BAND
    quarter = SEQ // 4

    for src, nat, res in ((q_ref, qf, q4), (k_ref, kf, k4), (v_ref, vf, v4)):
        nat[...] = src[...].astype(F32)
        for r in range(4):
            res[r * quarter:(r + 1) * quarter, :] = nat[pl.ds(r, quarter, stride=4), :]

    def logits(qb, kb, bias):
        return lax.dot_general(qb, kb, (((1,), (1,)), ((), ())), preferred_element_type=F32) + bias

    def finish(l, vb, o_out, l_out, rows):
        m = jnp.max(l, axis=-1, keepdims=True)
        p = jnp.exp(l - m)
        s = jnp.sum(p, axis=-1, keepdims=True)
        o = jnp.dot(p.astype(BF16), vb, preferred_element_type=F32)
        o_out[rows, :] = o / s
        l_out[rows, :] = jnp.broadcast_to(m + jnp.log(s), (n, LANES))

    pieces = []

    for i in range(SEQ // n):
        rows = slice(i * n, (i + 1) * n)
        krows = rows if i == 0 else slice((i - 1) * n, (i + 1) * n)
        bias = (0, slice(None), slice(n, None)) if i == 0 else (0,)
        pieces.append((q_ref, k_ref, v_ref, rows, krows, bias, o0, l0))

    for r in range(4):
        for blk in range(quarter // n):
            base = r * quarter + blk * n
            rows = slice(base, base + n)
            krows = rows if blk == 0 else slice(base - n, base + n)
            bias = (1, slice(None), slice(n, None)) if blk == 0 else (1,)
            pieces.append((q4, k4, v4, rows, krows, bias, o1, l1))

    for r in range(4):
        for a in range(4):
            rows = pl.ds(r * quarter + a, n, stride=4)
            pieces.append((q4, k4, v4, rows, rows, (2, slice(None), slice(n, None)), o2, l2))

    pending = []
    for qs, ks, vs, rows, krows, bias, o_out, l_out in pieces:
        l = logits(qs[rows, :].astype(BF16), ks[krows, :].astype(BF16), bias_ref[bias])
        if len(pending) == ATTN_PIECES_IN_FLIGHT:
            finish(*pending.pop(0))
        pending.append((l, vs[krows, :].astype(BF16), o_out, l_out, rows))
    for args in pending:
        finish(*args)

    for r in range(4):
        for c in range(quarter // n):
            rows = slice(r * quarter + c * n, r * quarter + (c + 1) * n)
            nat = pl.ds(r + 4 * c * n, n, stride=4)
            a0, a1, a2 = l0[nat, :], l1[rows, :], l2[rows, :]
            m = jnp.maximum(jnp.maximum(a0, a1), a2)
            e0, e1, e2 = jnp.exp(a0 - m), jnp.exp(a1 - m), jnp.exp(a2 - m)
            mix = (e0 * o0[nat, :] + e1 * o1[rows, :] + e2 * o2[rows, :]) / (e0 + e1 + e2)
            stage[nat, :] = mix
    o_ref[...] = stage[...].astype(BF16)


def _attention(proj3, bias_tabs):
    B = proj3.shape[0]
    blk = lambda off: pl.BlockSpec((None, SEQ, HEAD_DIM), lambda b, h: (b, 0, off + h))
    return pl.pallas_call(
        _attn_kernel,
        grid=(B, N_HEADS),
        in_specs=[
            blk(0), blk(N_HEADS), blk(2 * N_HEADS),
            pl.BlockSpec((3, None, BAND, 2 * BAND), lambda b, h: (0, h, 0, 0)),
        ],
        out_specs=pl.BlockSpec((None, SEQ, HEAD_DIM), lambda b, h: (b, 0, h)),
        out_shape=jax.ShapeDtypeStruct((B, SEQ, D_ATTN), BF16),
        scratch_shapes=[pltpu.VMEM((SEQ, HEAD_DIM), F32) for _ in range(13)],
        compiler_params=_cparams(("arbitrary", "arbitrary")),
        name="dilated_attention",
    )(proj3, proj3, proj3, bias_tabs)


def _sgu_kernel(u_ref, z_ref, w_ref, g_ref, b_ref, bs_ref, o_ref):
    n = SGU_CHUNK
    for c in range(TM_SGU // n):
        rs = slice(c * n, (c + 1) * n)
        for g in range(N_SGU_GROUPS):
            cs = slice(g * n, (g + 1) * n)
            z = _gelu(z_ref[rs, cs].astype(F32))
            mu = jnp.mean(z, axis=-1, keepdims=True)
            zc = z - mu
            var = jnp.mean(zc * zc, axis=-1, keepdims=True)
            zn = (zc * lax.rsqrt(var + NORM_EPS)) * g_ref[:, cs] + b_ref[:, cs]
            mixed = jnp.dot(w_ref[g], zn.astype(BF16), preferred_element_type=F32) + bs_ref[:, g:g + 1]
            u = _gelu(u_ref[rs, cs].astype(F32))
            o_ref[rs, cs] = (u * mixed).astype(BF16)


def _spatial_gating(proj, w_causal_b, ln_g, ln_b, bs_t):
    T = proj.shape[0]
    tm = TM_SGU
    ucol = 3 * D_ATTN // D_SGU
    return pl.pallas_call(
        _sgu_kernel,
        grid=(T // tm,),
        in_specs=[
            pl.BlockSpec((tm, D_SGU), lambda i: (i, ucol)),
            pl.BlockSpec((tm, D_SGU), lambda i: (i, ucol + 1)),
            pl.BlockSpec((N_SGU_GROUPS, SGU_CHUNK, SGU_CHUNK), lambda i: (0, 0, 0)),
            pl.BlockSpec((1, D_SGU), lambda i: (0, 0)),
            pl.BlockSpec((1, D_SGU), lambda i: (0, 0)),
            pl.BlockSpec((SGU_CHUNK, N_SGU_GROUPS), lambda i: (0, 0)),
        ],
        out_specs=pl.BlockSpec((tm, D_SGU), lambda i: (i, 0)),
        out_shape=jax.ShapeDtypeStruct((T, D_SGU), BF16),
        compiler_params=_cparams(("arbitrary",)),
        name="spatial_gating",
    )(proj, proj, w_causal_b, ln_g, ln_b, bs_t)


def _outproj_kernel(a_ref, s_ref, x_ref, mod_ref, g_ref, w_ref, x1_ref, h2_ref, h2p_ref):
    mixed = jnp.dot(a_ref[...], w_ref[0:D_ATTN, :], preferred_element_type=F32)
    mixed += jnp.dot(s_ref[...], w_ref[D_ATTN:, :], preferred_element_type=F32)
    x1 = x_ref[...] + mod_ref[0, 2:3, :] * mixed
    x1_ref[...] = x1
    r = lax.rsqrt(jnp.mean(x1 * x1, axis=-1, keepdims=True) + NORM_EPS)
    h2 = (x1 * r) * g_ref[...] * (1.0 + mod_ref[0, 4:5, :]) + mod_ref[0, 3:4, :]
    h2_ref[...] = h2.astype(BF16)
    _store_row_tiles(h2p_ref, 0, TM_OUT, _pack_halves(h2))


def _out_projection(attn, sgu, x2, mod3, norm_g, w_out_b):
    T, D = x2.shape
    tm = TM_OUT
    per_batch = SEQ // tm
    return pl.pallas_call(
        _outproj_kernel,
        grid=(T // tm,),
        in_specs=[
            pl.BlockSpec((tm, D_ATTN), lambda i: (i, 0)),
            pl.BlockSpec((tm, D_SGU), lambda i: (i, 0)),
            pl.BlockSpec((tm, D), lambda i: (i, 0)),
            pl.BlockSpec((1, 6, D), lambda i: (i // per_batch, 0, 0)),
            pl.BlockSpec((1, D), lambda i: (0, 0)),
            pl.BlockSpec((D_ATTN + D_SGU, D), lambda i: (0, 0)),
        ],
        out_specs=[pl.BlockSpec((tm, D), lambda i: (i, 0)), pl.BlockSpec((tm, D), lambda i: (i, 0)),
                   pl.BlockSpec((tm * ROW_TILE, LANES), lambda i: (i, 0))],
        out_shape=[jax.ShapeDtypeStruct((T, D), F32), jax.ShapeDtypeStruct((T, D), BF16),
                   jax.ShapeDtypeStruct((T * ROW_TILE, LANES), U32)],
        compiler_params=_cparams(("arbitrary",)),
        name="out_projection",
    )(attn, sgu, x2, mod3, norm_g, w_out_b)


def _router_kernel(h_ref, w_ref, b_ref, idx_ref, wt_ref, rank_ref, cnt_ref, run_ref):
    i = pl.program_id(0)
    tr = TM_ROUTE
    E = N_EXPERTS

    @pl.when(i == 0)
    def _():
        run_ref[...] = jnp.zeros_like(run_ref)

    logits = lax.dot_general(w_ref[...], h_ref[...], (((1,), (1,)), ((), ())), preferred_element_type=F32)
    scores = jax.nn.sigmoid(logits)
    sel = scores + b_ref[...]

    slabs = [sel[g * GROUP_SIZE:(g + 1) * GROUP_SIZE, :] for g in range(N_EXPERT_GROUPS)]
    si = lax.broadcasted_iota(I32, (GROUP_SIZE, tr), 0).astype(F32)
    gs = []
    for slab in slabs:
        m1 = jnp.max(slab, axis=0, keepdims=True)
        first = jnp.min(jnp.where(slab == m1, si, float(GROUP_SIZE)), axis=0, keepdims=True)
        m2 = jnp.max(jnp.where(si == first, -jnp.inf, slab), axis=0, keepdims=True)
        gs.append(m1 + m2)

    kept = []
    for g in range(N_EXPERT_GROUPS):
        beaten = jnp.zeros((1, tr), F32)
        for o in range(N_EXPERT_GROUPS):
            if o < g:
                beaten += (gs[o] >= gs[g]).astype(F32)
            elif o > g:
                beaten += (gs[o] > gs[g]).astype(F32)
        kept.append(jnp.where(beaten < TOPK_GROUPS, slabs[g], -jnp.inf))
    masked = jnp.concatenate(kept, axis=0)

    ei = lax.broadcasted_iota(I32, (E, tr), 0).astype(F32)
    picks, pick_scores = [], []
    onehot_sum = jnp.zeros((E, tr), F32)
    for k in range(TOP_K):
        m = jnp.max(masked, axis=0, keepdims=True)
        ik = jnp.min(jnp.where(masked == m, ei, float(E)), axis=0, keepdims=True)
        oh = ei == ik
        pick_scores.append(jnp.sum(jnp.where(oh, scores, 0.0), axis=0, keepdims=True))
        masked = jnp.where(oh, -jnp.inf, masked)
        onehot_sum += oh.astype(F32)
        picks.append(ik)

    ti = lax.broadcasted_iota(I32, (tr, tr), 0)
    tj = lax.broadcasted_iota(I32, (tr, tr), 1)
    before = (ti < tj).astype(BF16)
    prior = jnp.dot(onehot_sum.astype(BF16), before, preferred_element_type=F32) + run_ref[...]

    total = pick_scores[0]
    for k in range(1, TOP_K):
        total += pick_scores[k]
    for k in range(TOP_K):
        idx_ref[k:k + 1, :] = picks[k].astype(I32)
        wt_ref[k:k + 1, :] = pick_scores[k] / total * ROUTED_SCALE
        rk = jnp.sum(jnp.where(ei == picks[k], prior, 0.0), axis=0, keepdims=True)
        rank_ref[k:k + 1, :] = rk.astype(I32)

    run_ref[...] += jnp.sum(onehot_sum, axis=1, keepdims=True)
    cnt_ref[...] = run_ref[...].astype(I32)


def _route(h2, router_wt_b, router_bias):
    T, D = h2.shape
    tr = TM_ROUTE
    E = N_EXPERTS
    row_spec = pl.BlockSpec((TOP_K, tr), lambda i: (0, i))
    return pl.pallas_call(
        _router_kernel,
        grid=(T // tr,),
        in_specs=[
            pl.BlockSpec((tr, D), lambda i: (i, 0)),
            pl.BlockSpec((E, D), lambda i: (0, 0)),
            pl.BlockSpec((E, 1), lambda i: (0, 0)),
        ],
        out_specs=[row_spec, row_spec, row_spec, pl.BlockSpec((E, 1), lambda i: (0, 0))],
        out_shape=[
            jax.ShapeDtypeStruct((TOP_K, T), I32),
            jax.ShapeDtypeStruct((TOP_K, T), F32),
            jax.ShapeDtypeStruct((TOP_K, T), I32),
            jax.ShapeDtypeStruct((E, 1), I32),
        ],
        scratch_shapes=[pltpu.VMEM((E, 1), F32)],
        compiler_params=_cparams(("arbitrary",)),
        name="router",
    )(h2, router_wt_b, router_bias.reshape(E, 1))


def _dispatch_kernel(pend_ref, h_ref, dest_ref, xg_ref, dest_s, zero_ref, sem_s, sem):
    i = pl.program_id(0)
    td = TM_DISPATCH
    blk = EXPERT_BLOCK

    def zero_copy(e):
        start = pl.multiple_of(pend_ref[e + 1] - blk, blk)
        return pltpu.make_async_copy(zero_ref, xg_ref.at[pl.ds(start, blk)], sem)

    @pl.when(i == 0)
    def _():
        zero_ref[...] = jnp.zeros_like(zero_ref)

        def start_zero(e, c):
            @pl.when(pend_ref[e + 1] > pend_ref[e])
            def _():
                zero_copy(e).start()
            return c

        def wait_zero(e, c):
            @pl.when(pend_ref[e + 1] > pend_ref[e])
            def _():
                zero_copy(e).wait()
            return c

        lax.fori_loop(0, N_EXPERTS, start_zero, 0)
        lax.fori_loop(0, N_EXPERTS, wait_zero, 0)

        def tail_copy(b):
            return pltpu.make_async_copy(zero_ref, xg_ref.at[pl.ds(pl.multiple_of(b * blk, blk), blk)], sem)

        first_unused = pend_ref[N_EXPERTS] // blk
        n_blocks = xg_ref.shape[0] // blk
        lax.fori_loop(first_unused, n_blocks, lambda b, c: (tail_copy(b).start(), c)[1], 0)
        lax.fori_loop(first_unused, n_blocks, lambda b, c: (tail_copy(b).wait(), c)[1], 0)

    cp = pltpu.make_async_copy(dest_ref, dest_s, sem_s)
    cp.start()
    cp.wait()

    def row_copy(t, k):
        return pltpu.make_async_copy(h_ref.at[t], xg_ref.at[dest_s[k, t]], sem)

    def start_rows(t8, c):
        base = pl.multiple_of(t8 * 8, 8)
        for j in range(8):
            for k in range(TOP_K):
                row_copy(base + j, k).start(priority=k % DMA_QUEUES)
        return c

    lax.fori_loop(0, td // 8, start_rows, 0)
    for k in range(TOP_K):
        pltpu.make_async_copy(h_ref, h_ref, sem).wait()


def _dispatch(h2p, dest, pend0, rows):
    T = h2p.shape[0]
    td = TM_DISPATCH
    tile = (ROW_TILE, LANES)
    return pl.pallas_call(
        _dispatch_kernel,
        grid_spec=pltpu.PrefetchScalarGridSpec(
            num_scalar_prefetch=1,
            grid=(T // td,),
            in_specs=[
                pl.BlockSpec((td,) + tile, lambda i, p: (i, 0, 0)),
                pl.BlockSpec((TOP_K, td), lambda i, p: (0, i)),
            ],
            out_specs=pl.BlockSpec(memory_space=pl.ANY),
            scratch_shapes=[
                pltpu.SMEM((TOP_K, td), I32),
                pltpu.VMEM((EXPERT_BLOCK,) + tile, U32),
                pltpu.SemaphoreType.DMA,
                pltpu.SemaphoreType.DMA,
            ],
        ),
        out_shape=jax.ShapeDtypeStruct((rows,) + tile, U32),
        compiler_params=_cparams(("arbitrary",)),
        name="dispatch",
    )(pend0, h2p, dest)


def _experts_kernel(first_ref, gidx_ref, used_ref, nused_ref, ngroups_ref,
                    x_ref, wg_hbm, wu_hbm, wd_hbm, y_ref,
                    wgu_f, wd_f, wgu_b, wd_b, issued, sem):
    i = pl.program_id(0)
    F = D_EXPERT

    def tensor_copy(e, s, t):
        if t == 0:
            return pltpu.make_async_copy(wg_hbm.at[e], wgu_f.at[s, 0], sem.at[s, 0])
        if t == 1:
            return pltpu.make_async_copy(wu_hbm.at[e], wgu_f.at[s, 1], sem.at[s, 1])
        return pltpu.make_async_copy(wd_hbm.at[e], wd_f.at[s], sem.at[s, 2])

    def issue_until(n):
        def body(q, carry):
            h = q // 3
            for t in range(3):
                @pl.when(q % 3 == t)
                def _():
                    tensor_copy(used_ref[h], h % WEIGHT_SLOTS, t).start(priority=DMA_QUEUES - 1)
            return carry

        lax.fori_loop(issued[0], n, body, 0)
        issued[0] = jnp.maximum(issued[0], n)

    @pl.when(i == 0)
    def _():
        issued[0] = 0

    @pl.when(i < nused_ref[0])
    def _():
        g = gidx_ref[i]

        def block(cast_from):
            lo, hi = _unpack_halves(_load_row_tiles(x_ref, 0, EXPERT_BLOCK))
            x = jnp.concatenate([lo.astype(BF16), hi.astype(BF16)], axis=1)
            if cast_from is None:
                gu = jnp.dot(x, wgu_b[...], preferred_element_type=F32)
            else:
                parts = []
                for c in range(2 * F // CAST_CHUNK):
                    cols = slice(c * CAST_CHUNK, (c + 1) * CAST_CHUNK)
                    src = slice((c * CAST_CHUNK) % F, (c * CAST_CHUNK) % F + CAST_CHUNK)
                    wgu_b[:, cols] = wgu_f[cast_from, c * CAST_CHUNK // F, :, src].astype(BF16)
                    parts.append(jnp.dot(x, wgu_b[:, cols], preferred_element_type=F32))
                gu = jnp.concatenate(parts, axis=1)
                wd_b[...] = wd_f[cast_from].astype(BF16)
            a = (_silu(gu[:, :F]) * gu[:, F:]).astype(BF16)
            y = jnp.dot(a, wd_b[...], preferred_element_type=F32)
            _store_row_tiles(y_ref, 0, EXPERT_BLOCK, _pack_halves(y))

        @pl.when(first_ref[i] == 1)
        def _():
            issue_until(3 * (g + 1))
            s = g % WEIGHT_SLOTS
            for t in range(3):
                tensor_copy(used_ref[g], s, t).wait()
            block(s)

        @pl.when(first_ref[i] != 1)
        def _():
            block(None)

        cap = 3 * jnp.minimum(g + WEIGHT_SLOTS + 1, ngroups_ref[0])
        issue_until(jnp.minimum(issued[0] + 1, cap))


def _experts(xg, first, gidx, used_list, n_used, n_groups, w_gate, w_up, w_down):
    D = D_MODEL
    bm = EXPERT_BLOCK
    F = D_EXPERT
    nb = xg.shape[0] // (bm * ROW_TILE)
    row_map = lambda i, fi, gi, ul, nu, ng: (jnp.minimum(i, nu[0] - 1), 0)
    return pl.pallas_call(
        _experts_kernel,
        grid_spec=pltpu.PrefetchScalarGridSpec(
            num_scalar_prefetch=5,
            grid=(nb,),
            in_specs=[
                pl.BlockSpec((bm * ROW_TILE, LANES), row_map),
                pl.BlockSpec(memory_space=pl.ANY),
                pl.BlockSpec(memory_space=pl.ANY),
                pl.BlockSpec(memory_space=pl.ANY),
            ],
            out_specs=pl.BlockSpec((bm * ROW_TILE, LANES), row_map),
            scratch_shapes=[
                pltpu.VMEM((WEIGHT_SLOTS, 2, D, F), F32), pltpu.VMEM((WEIGHT_SLOTS, F, D), F32),
                pltpu.VMEM((D, 2 * F), BF16), pltpu.VMEM((F, D), BF16),
                pltpu.SMEM((1,), I32),
                pltpu.SemaphoreType.DMA((WEIGHT_SLOTS, 3)),
            ],
        ),
        out_shape=jax.ShapeDtypeStruct(xg.shape, U32),
        input_output_aliases={5: 0},
        compiler_params=_cparams(("arbitrary",)),
        name="experts",
    )(first, gidx, used_list, n_used, n_groups, xg, w_gate, w_up, w_down)


def _combine_kernel(dest_ref, dnext_ref, wt_ref, x1_ref, h_ref, mod_ref, sg_ref, su_ref, sd_ref, yg_ref, o_ref,
                    dest_s, dnext_s, buf_a, buf_b, sem_s, sem):
    i = pl.program_id(0)
    last = pl.num_programs(0) - 1
    tc = TM_COMBINE
    half = D_MODEL // 2

    for src, dst in ((dest_ref, dest_s), (dnext_ref, dnext_s)):
        cp = pltpu.make_async_copy(src, dst, sem_s)
        cp.start()
        cp.wait()

    def row_copy(idx_s, col, buf, which, t, k):
        return pltpu.make_async_copy(yg_ref.at[idx_s[k, col + t]],
                                     buf.at[pl.ds((k * tc + t) * ROW_TILE, ROW_TILE), :], sem.at[which])

    def start_rows(idx_s, col, buf, which):
        for t in range(tc):
            for k in range(TOP_K):
                row_copy(idx_s, col, buf, which, t, k).start(priority=k % DMA_QUEUES)

    def wait_rows(buf, which):
        pltpu.make_async_copy(buf, buf, sem.at[which]).wait()

    def finish_tile(buf, r0):
        rows = slice(r0, r0 + tc)
        hb = h_ref[rows, :]
        g = jnp.dot(hb, sg_ref[...], preferred_element_type=F32)
        u = jnp.dot(hb, su_ref[...], preferred_element_type=F32)
        ffn = jnp.dot((_silu(g) * u).astype(BF16), sd_ref[...], preferred_element_type=F32)
        ffn_lo, ffn_hi = ffn[:, :half], ffn[:, half:]
        for k in range(TOP_K):
            lo, hi = _unpack_halves(_load_row_tiles(buf, k * tc * ROW_TILE, tc))
            w = wt_ref[rows, k:k + 1]
            ffn_lo += lo * w
            ffn_hi += hi * w
        o_ref[rows, :half] = x1_ref[rows, :half] + mod_ref[0, 5:6, :half] * ffn_lo
        o_ref[rows, half:] = x1_ref[rows, half:] + mod_ref[0, 5:6, half:] * ffn_hi

    @pl.when(i == 0)
    def _():
        def body(t, c):
            for k in range(TOP_K):
                row_copy(dest_s, 0, buf_a, 0, t, k).start(priority=k % DMA_QUEUES)
            return c

        lax.fori_loop(0, tc, body, 0)

    wait_rows(buf_a, 0)
    start_rows(dest_s, tc, buf_b, 1)
    finish_tile(buf_a, 0)
    wait_rows(buf_b, 1)
    start_rows(dnext_s, 0, buf_a, 0)
    finish_tile(buf_b, tc)

    @pl.when(i == last)
    def _():
        wait_rows(buf_a, 0)


def _combine(dest, wt_t, x1, h2, mod3, sg_b, su_b, sd_b, yg):
    T, D = x1.shape
    tc = TM_COMBINE
    step = 2 * tc
    per_batch = SEQ // step
    F = D_EXPERT
    return pl.pallas_call(
        _combine_kernel,
        grid=(T // step,),
        in_specs=[
            pl.BlockSpec((TOP_K, step), lambda i: (0, i)),
            pl.BlockSpec((TOP_K, step), lambda i: (0, jnp.minimum(i + 1, T // step - 1))),
            pl.BlockSpec((step, TOP_K), lambda i: (i, 0)),
            pl.BlockSpec((step, D), lambda i: (i, 0)),
            pl.BlockSpec((step, D), lambda i: (i, 0)),
            pl.BlockSpec((1, 6, D), lambda i: (i // per_batch, 0, 0)),
            pl.BlockSpec((D, F), lambda i: (0, 0)),
            pl.BlockSpec((D, F), lambda i: (0, 0)),
            pl.BlockSpec((F, D), lambda i: (0, 0)),
            pl.BlockSpec(memory_space=pl.ANY),
        ],
        out_specs=pl.BlockSpec((step, D), lambda i: (i, 0)),
        out_shape=jax.ShapeDtypeStruct((T, D), F32),
        scratch_shapes=[
            pltpu.SMEM((TOP_K, step), I32),
            pltpu.SMEM((TOP_K, step), I32),
            pltpu.VMEM((TOP_K * tc * ROW_TILE, LANES), U32),
            pltpu.VMEM((TOP_K * tc * ROW_TILE, LANES), U32),
            pltpu.SemaphoreType.DMA,
            pltpu.SemaphoreType.DMA((2,)),
        ],
        compiler_params=_cparams(("arbitrary",)),
        name="combine",
    )(dest, dest, wt_t, x1, h2, mod3, sg_b, su_b, sd_b, yg)


def kernel(x, c, ada_w, ada_b, mix_norm_g, ffn_norm_g, w_in, q_norm_g, k_norm_g, rel_bias, sgu_ln_g, sgu_ln_b, sgu_w, sgu_b, w_out, router_w, router_bias, shared_w_gate, shared_w_up, shared_w_down, expert_w_gate, expert_w_up, expert_w_down):
    B, S, D = x.shape
    assert S == SEQ and D == D_MODEL and ada_w.shape[0] == 1
    T = B * S
    x2 = x.reshape(T, D)

    mod3 = _modulation(c, ada_w[0], ada_b[0]).reshape(B, 6, D)

    proj = _in_projection(x2, mod3, mix_norm_g, w_in[0].astype(BF16), q_norm_g, k_norm_g)
    attn = _attention(proj.reshape(B, S, D_IN_PROJ), _bias_tables(rel_bias)).reshape(T, D_ATTN)
    causal = jnp.tril(jnp.ones((SGU_CHUNK, SGU_CHUNK), F32))
    sgu = _spatial_gating(proj, (sgu_w[0] * causal).astype(BF16), sgu_ln_g, sgu_ln_b, sgu_b[0].T)
    x1, h2, h2p = _out_projection(attn, sgu, x2, mod3, ffn_norm_g, w_out[0].astype(BF16))

    idx, wts, rank, counts = _route(h2, router_w[0].T.astype(BF16), router_bias[0])

    bm = EXPERT_BLOCK
    counts = counts.reshape(N_EXPERTS)
    padded = (counts + bm - 1) // bm * bm
    pends = jnp.cumsum(padded)
    pstarts = pends - padded
    n_blocks = T * TOP_K // bm + N_EXPERTS
    block_start = jnp.arange(n_blocks, dtype=I32) * bm
    block_exp = jnp.minimum(jnp.sum((pends[None, :] <= block_start[:, None]).astype(I32), axis=1), N_EXPERTS - 1)
    n_used = (pends[-1] // bm).astype(I32).reshape(1)
    eids = jnp.arange(N_EXPERTS, dtype=I32)
    dest = jnp.sum(jnp.where(idx[:, :, None] == eids, pstarts.astype(I32), 0), axis=-1) + rank
    pend0 = jnp.concatenate([jnp.zeros((1,), I32), pends.astype(I32)])
    first = jnp.concatenate([jnp.ones((1,), I32), (block_exp[1:] != block_exp[:-1]).astype(I32)])
    used = counts > 0
    ordinal = jnp.cumsum(used.astype(I32)) - 1
    n_groups = jnp.sum(used.astype(I32)).reshape(1)
    used_list = jnp.sum(jnp.where(used[None, :] & (ordinal[None, :] == eids[:, None]), eids[None, :], 0), axis=-1)
    gidx = jnp.sum(jnp.where(block_exp[:, None] == eids, ordinal, 0), axis=-1)

    n_rows = n_blocks * bm
    xg = _dispatch(h2p.reshape(T, ROW_TILE, LANES), dest, pend0, n_rows)
    yg = _experts(xg.reshape(n_rows * ROW_TILE, LANES), first, gidx.astype(I32), used_list.astype(I32),
                  n_used, n_groups, expert_w_gate[0], expert_w_up[0], expert_w_down[0])
    out = _combine(dest, wts.T, x1, h2, mod3, shared_w_gate[0].astype(BF16), shared_w_up[0].astype(BF16),
                   shared_w_down[0].astype(BF16), yg.reshape(n_rows, ROW_TILE, LANES))
    return out.reshape(B, S, D)
```

```python
import functools
import math

import numpy as np
import jax
import jax.numpy as jnp
from jax import lax
from jax.experimental import pallas as pl
from jax.experimental.pallas import tpu as pltpu

F32 = jnp.float32
BF16 = jnp.bfloat16
I32 = jnp.int32
U32 = jnp.uint32

D_MODEL = 2048
SEQ = 2048
HEAD_DIM = 128
N_HEADS = 8
D_ATTN = N_HEADS * HEAD_DIM
D_SGU = 1024
N_SGU_GROUPS = 8
SGU_CHUNK = 128
D_IN_PROJ = 3 * D_ATTN + 2 * D_SGU
DILATED_GROUPS = ((128, 1), (512, 4), (2048, 16))
BAND = 128
NUM_REL_BUCKETS = 32
REL_MAX_DISTANCE = 2048
N_EXPERTS = 256
TOP_K = 8
N_EXPERT_GROUPS = 8
GROUP_SIZE = N_EXPERTS // N_EXPERT_GROUPS
TOPK_GROUPS = 4
D_EXPERT = 512
ROUTED_SCALE = 2.5
NORM_EPS = 1e-6
MASK_VALUE = -1e30

LANES = 128
ROW_TILE = D_MODEL // 2 // LANES
VMEM_LIMIT = 56 * 1024 * 1024
DMA_QUEUES = 2

TM_PROJ = 256
TM_SGU = 512
TM_OUT = 256
TM_ROUTE = 512
TM_DISPATCH = 256
TM_COMBINE = 128
EXPERT_BLOCK = 128
WEIGHT_SLOTS = 2
ATTN_PIECES_IN_FLIGHT = 4


def _cparams(sem):
    return pltpu.CompilerParams(dimension_semantics=sem, vmem_limit_bytes=VMEM_LIMIT)


def _silu(v):
    return v * jax.nn.sigmoid(v)


def _gelu(v):
    return 0.5 * v * (1.0 + lax.erf(v * (1.0 / math.sqrt(2.0))))


def _pack_halves(v):
    h = v.shape[1] // 2
    lo = lax.bitcast_convert_type(v[:, :h].astype(BF16).astype(F32), U32)
    hi = lax.bitcast_convert_type(v[:, h:].astype(BF16).astype(F32), U32)
    return (hi & jnp.uint32(0xFFFF0000)) | (lo >> 16)


def _unpack_halves(w):
    lo = lax.bitcast_convert_type(w << 16, F32)
    hi = lax.bitcast_convert_type(w & jnp.uint32(0xFFFF0000), F32)
    return lo, hi


def _store_row_tiles(ref, base, rows, packed):
    for s in range(ROW_TILE):
        ref[pl.ds(base + s, rows, stride=ROW_TILE), :] = packed[:, s * LANES:(s + 1) * LANES]


def _load_row_tiles(ref, base, rows):
    return jnp.concatenate([ref[pl.ds(base + s, rows, stride=ROW_TILE), :] for s in range(ROW_TILE)], axis=1)


def _mod_kernel(c_ref, w_ref, b_ref, o_ref):
    ca = _silu(c_ref[...]).astype(BF16)
    o_ref[...] = jnp.dot(ca, w_ref[...].astype(BF16), preferred_element_type=F32) + b_ref[...]


def _modulation(c, ada_w, ada_b):
    B, D = c.shape
    N = ada_w.shape[1]
    tn = 1024
    return pl.pallas_call(
        _mod_kernel,
        grid=(N // tn,),
        in_specs=[
            pl.BlockSpec((B, D), lambda j: (0, 0)),
            pl.BlockSpec((D, tn), lambda j: (0, j)),
            pl.BlockSpec((1, tn), lambda j: (0, j)),
        ],
        out_specs=pl.BlockSpec((B, tn), lambda j: (0, j)),
        out_shape=jax.ShapeDtypeStruct((B, N), F32),
        compiler_params=_cparams(("arbitrary",)),
        name="modulation",
    )(c, ada_w, ada_b.reshape(1, N))


def _inproj_kernel(x_ref, mod_ref, g_ref, w_ref, qg_ref, kg_ref, o_ref):
    x = x_ref[...]
    r = lax.rsqrt(jnp.mean(x * x, axis=-1, keepdims=True) + NORM_EPS)
    h = ((x * r) * g_ref[...] * (1.0 + mod_ref[0, 1:2, :]) + mod_ref[0, 0:1, :]).astype(BF16)

    def head_norm(acc, col0, gain_ref, scale):
        for hd in range(N_HEADS):
            a = acc[:, hd * HEAD_DIM:(hd + 1) * HEAD_DIM]
            r = lax.rsqrt(jnp.mean(a * a, axis=-1, keepdims=True) + NORM_EPS)
            cols = slice(col0 + hd * HEAD_DIM, col0 + (hd + 1) * HEAD_DIM)
            o_ref[:, cols] = ((a * r) * gain_ref[...] * scale).astype(BF16)

    def epilogue(j, acc):
        col0 = j * D_ATTN
        if j == 0:
            head_norm(acc, col0, qg_ref, HEAD_DIM ** -0.5)
        elif j == 1:
            head_norm(acc, col0, kg_ref, 1.0)
        else:
            o_ref[:, col0:col0 + D_ATTN] = acc.astype(BF16)

    pending = None
    for j in range(D_IN_PROJ // D_ATTN):
        acc = jnp.dot(h, w_ref[:, j * D_ATTN:(j + 1) * D_ATTN], preferred_element_type=F32)
        if pending is not None:
            epilogue(*pending)
        pending = (j, acc)
    epilogue(*pending)


def _in_projection(x2, mod3, norm_g, w_in_b, q_g, k_g):
    T, D = x2.shape
    tm = TM_PROJ
    per_batch = SEQ // tm
    return pl.pallas_call(
        _inproj_kernel,
        grid=(T // tm,),
        in_specs=[
            pl.BlockSpec((tm, D), lambda i: (i, 0)),
            pl.BlockSpec((1, 6, D), lambda i: (i // per_batch, 0, 0)),
            pl.BlockSpec((1, D), lambda i: (0, 0)),
            pl.BlockSpec((D, D_IN_PROJ), lambda i: (0, 0), pipeline_mode=pl.Buffered(1)),
            pl.BlockSpec((1, HEAD_DIM), lambda i: (0, 0)),
            pl.BlockSpec((1, HEAD_DIM), lambda i: (0, 0)),
        ],
        out_specs=pl.BlockSpec((tm, D_IN_PROJ), lambda i: (i, 0)),
        out_shape=jax.ShapeDtypeStruct((T, D_IN_PROJ), BF16),
        compiler_params=_cparams(("arbitrary",)),
        name="in_projection",
    )(x2, mod3, norm_g, w_in_b, q_g, k_g)


def _bias_tables(rel_bias):
    n = BAND
    qi = jnp.arange(n)[:, None]
    ki = jnp.arange(2 * n)[None, :]
    steps = n + qi - ki
    in_band = (steps >= 0) & (steps <= n)
    max_exact = NUM_REL_BUCKETS // 2
    tabs = []
    for _, dilation in DILATED_GROUPS:
        dist = jnp.clip(steps, 0, n) * dilation
        nf = jnp.maximum(dist, 1).astype(F32)
        large = max_exact + (jnp.log(nf / max_exact) / math.log(REL_MAX_DISTANCE / max_exact)
                             * (NUM_REL_BUCKETS - max_exact)).astype(I32)
        large = jnp.minimum(large, NUM_REL_BUCKETS - 1)
        bucket = jnp.where(dist < max_exact, dist, large)
        onehot = jax.nn.one_hot(bucket, NUM_REL_BUCKETS, dtype=F32)
        b = jnp.einsum("qkb,bh->hqk", onehot, rel_bias.astype(F32), precision=lax.Precision.HIGHEST)
        tabs.append(jnp.where(in_band[None], b, MASK_VALUE))
    return jnp.stack(tabs)


def _attn_kernel(q_ref, k_ref, v_ref, bias_ref, o_ref, qf, kf, vf, q4, k4, v4,
                 o0, l0, o1, l1, o2, l2, stage):
    n = BAND
    quarter = SEQ // 4

    for src, nat, res in ((q_ref, qf, q4), (k_ref, kf, k4), (v_ref, vf, v4)):
        nat[...] = src[...].astype(F32)
        for r in range(4):
            res[r * quarter:(r + 1) * quarter, :] = nat[pl.ds(r, quarter, stride=4), :]

    def logits(qb, kb, bias):
        return lax.dot_general(qb, kb, (((1,), (1,)), ((), ())), preferred_element_type=F32) + bias

    def finish(l, vb, o_out, l_out, rows):
        m = jnp.max(l, axis=-1, keepdims=True)
        p = jnp.exp(l - m)
        s = jnp.sum(p, axis=-1, keepdims=True)
        o = jnp.dot(p.astype(BF16), vb, preferred_element_type=F32)
        o_out[rows, :] = o / s
        l_out[rows, :] = jnp.broadcast_to(m + jnp.log(s), (n, LANES))

    pieces = []

    for i in range(SEQ // n):
        rows = slice(i * n, (i + 1) * n)
        krows = rows if i == 0 else slice((i - 1) * n, (i + 1) * n)
        bias = (0, slice(None), slice(n, None)) if i == 0 else (0,)
        pieces.append((q_ref, k_ref, v_ref, rows, krows, bias, o0, l0))

    for r in range(4):
        for blk in range(quarter // n):
            base = r * quarter + blk * n
            rows = slice(base, base + n)
            krows = rows if blk == 0 else slice(base - n, base + n)
            bias = (1, slice(None), slice(n, None)) if blk == 0 else (1,)
            pieces.append((q4, k4, v4, rows, krows, bias, o1, l1))

    for r in range(4):
        for a in range(4):
            rows = pl.ds(r * quarter + a, n, stride=4)
            pieces.append((q4, k4, v4, rows, rows, (2, slice(None), slice(n, None)), o2, l2))

    pending = []
    for qs, ks, vs, rows, krows, bias, o_out, l_out in pieces:
        l = logits(qs[rows, :].astype(BF16), ks[krows, :].astype(BF16), bias_ref[bias])
        if len(pending) == ATTN_PIECES_IN_FLIGHT:
            finish(*pending.pop(0))
        pending.append((l, vs[krows, :].astype(BF16), o_out, l_out, rows))
    for args in pending:
        finish(*args)

    for r in range(4):
        for c in range(quarter // n):
            rows = slice(r * quarter + c * n, r * quarter + (c + 1) * n)
            nat = pl.ds(r + 4 * c * n, n, stride=4)
            a0, a1, a2 = l0[nat, :], l1[rows, :], l2[rows, :]
            m = jnp.maximum(jnp.maximum(a0, a1), a2)
            e0, e1, e2 = jnp.exp(a0 - m), jnp.exp(a1 - m), jnp.exp(a2 - m)
            mix = (e0 * o0[nat, :] + e1 * o1[rows, :] + e2 * o2[rows, :]) / (e0 + e1 + e2)
            stage[nat, :] = mix
    o_ref[...] = stage[...].astype(BF16)


def _attention(proj3, bias_tabs):
    B = proj3.shape[0]
    blk = lambda off: pl.BlockSpec((None, SEQ, HEAD_DIM), lambda b, h: (b, 0, off + h))
    return pl.pallas_call(
        _attn_kernel,
        grid=(B, N_HEADS),
        in_specs=[
            blk(0), blk(N_HEADS), blk(2 * N_HEADS),
            pl.BlockSpec((3, None, BAND, 2 * BAND), lambda b, h: (0, h, 0, 0)),
        ],
        out_specs=pl.BlockSpec((None, SEQ, HEAD_DIM), lambda b, h: (b, 0, h)),
        out_shape=jax.ShapeDtypeStruct((B, SEQ, D_ATTN), BF16),
        scratch_shapes=[pltpu.VMEM((SEQ, HEAD_DIM), F32) for _ in range(13)],
        compiler_params=_cparams(("arbitrary", "arbitrary")),
        name="dilated_attention",
    )(proj3, proj3, proj3, bias_tabs)


def _sgu_kernel(u_ref, z_ref, w_ref, g_ref, b_ref, bs_ref, o_ref):
    n = SGU_CHUNK
    for c in range(TM_SGU // n):
        rs = slice(c * n, (c + 1) * n)
        for g in range(N_SGU_GROUPS):
            cs = slice(g * n, (g + 1) * n)
            z = _gelu(z_ref[rs, cs].astype(F32))
            mu = jnp.mean(z, axis=-1, keepdims=True)
            zc = z - mu
            var = jnp.mean(zc * zc, axis=-1, keepdims=True)
            zn = (zc * lax.rsqrt(var + NORM_EPS)) * g_ref[:, cs] + b_ref[:, cs]
            mixed = jnp.dot(w_ref[g], zn.astype(BF16), preferred_element_type=F32) + bs_ref[:, g:g + 1]
            u = _gelu(u_ref[rs, cs].astype(F32))
            o_ref[rs, cs] = (u * mixed).astype(BF16)


def _spatial_gating(proj, w_causal_b, ln_g, ln_b, bs_t):
    T = proj.shape[0]
    tm = TM_SGU
    ucol = 3 * D_ATTN // D_SGU
    return pl.pallas_call(
        _sgu_kernel,
        grid=(T // tm,),
        in_specs=[
            pl.BlockSpec((tm, D_SGU), lambda i: (i, ucol)),
            pl.BlockSpec((tm, D_SGU), lambda i: (i, ucol + 1)),
            pl.BlockSpec((N_SGU_GROUPS, SGU_CHUNK, SGU_CHUNK), lambda i: (0, 0, 0)),
            pl.BlockSpec((1, D_SGU), lambda i: (0, 0)),
            pl.BlockSpec((1, D_SGU), lambda i: (0, 0)),
            pl.BlockSpec((SGU_CHUNK, N_SGU_GROUPS), lambda i: (0, 0)),
        ],
        out_specs=pl.BlockSpec((tm, D_SGU), lambda i: (i, 0)),
        out_shape=jax.ShapeDtypeStruct((T, D_SGU), BF16),
        compiler_params=_cparams(("arbitrary",)),
        name="spatial_gating",
    )(proj, proj, w_causal_b, ln_g, ln_b, bs_t)


def _outproj_kernel(a_ref, s_ref, x_ref, mod_ref, g_ref, w_ref, x1_ref, h2_ref, h2p_ref):
    mixed = jnp.dot(a_ref[...], w_ref[0:D_ATTN, :], preferred_element_type=F32)
    mixed += jnp.dot(s_ref[...], w_ref[D_ATTN:, :], preferred_element_type=F32)
    x1 = x_ref[...] + mod_ref[0, 2:3, :] * mixed
    x1_ref[...] = x1
    r = lax.rsqrt(jnp.mean(x1 * x1, axis=-1, keepdims=True) + NORM_EPS)
    h2 = (x1 * r) * g_ref[...] * (1.0 + mod_ref[0, 4:5, :]) + mod_ref[0, 3:4, :]
    h2_ref[...] = h2.astype(BF16)
    _store_row_tiles(h2p_ref, 0, TM_OUT, _pack_halves(h2))


def _out_projection(attn, sgu, x2, mod3, norm_g, w_out_b):
    T, D = x2.shape
    tm = TM_OUT
    per_batch = SEQ // tm
    return pl.pallas_call(
        _outproj_kernel,
        grid=(T // tm,),
        in_specs=[
            pl.BlockSpec((tm, D_ATTN), lambda i: (i, 0)),
            pl.BlockSpec((tm, D_SGU), lambda i: (i, 0)),
            pl.BlockSpec((tm, D), lambda i: (i, 0)),
            pl.BlockSpec((1, 6, D), lambda i: (i // per_batch, 0, 0)),
            pl.BlockSpec((1, D), lambda i: (0, 0)),
            pl.BlockSpec((D_ATTN + D_SGU, D), lambda i: (0, 0), pipeline_mode=pl.Buffered(1)),
        ],
        out_specs=[pl.BlockSpec((tm, D), lambda i: (i, 0)), pl.BlockSpec((tm, D), lambda i: (i, 0)),
                   pl.BlockSpec((tm * ROW_TILE, LANES), lambda i: (i, 0))],
        out_shape=[jax.ShapeDtypeStruct((T, D), F32), jax.ShapeDtypeStruct((T, D), BF16),
                   jax.ShapeDtypeStruct((T * ROW_TILE, LANES), U32)],
        compiler_params=_cparams(("arbitrary",)),
        name="out_projection",
    )(attn, sgu, x2, mod3, norm_g, w_out_b)


def _router_kernel(h_ref, w_ref, b_ref, idx_ref, wt_ref, rank_ref, cnt_ref, run_ref):
    i = pl.program_id(0)
    tr = TM_ROUTE
    E = N_EXPERTS

    @pl.when(i == 0)
    def _():
        run_ref[...] = jnp.zeros_like(run_ref)

    logits = lax.dot_general(w_ref[...], h_ref[...], (((1,), (1,)), ((), ())), preferred_element_type=F32)
    scores = jax.nn.sigmoid(logits)
    sel = scores + b_ref[...]

    slabs = [sel[g * GROUP_SIZE:(g + 1) * GROUP_SIZE, :] for g in range(N_EXPERT_GROUPS)]
    si = lax.broadcasted_iota(I32, (GROUP_SIZE, tr), 0).astype(F32)
    gs = []
    for slab in slabs:
        m1 = jnp.max(slab, axis=0, keepdims=True)
        first = jnp.min(jnp.where(slab == m1, si, float(GROUP_SIZE)), axis=0, keepdims=True)
        m2 = jnp.max(jnp.where(si == first, -jnp.inf, slab), axis=0, keepdims=True)
        gs.append(m1 + m2)

    kept = []
    for g in range(N_EXPERT_GROUPS):
        beaten = jnp.zeros((1, tr), F32)
        for o in range(N_EXPERT_GROUPS):
            if o < g:
                beaten += (gs[o] >= gs[g]).astype(F32)
            elif o > g:
                beaten += (gs[o] > gs[g]).astype(F32)
        kept.append(jnp.where(beaten < TOPK_GROUPS, slabs[g], -jnp.inf))
    masked = jnp.concatenate(kept, axis=0)

    ei = lax.broadcasted_iota(I32, (E, tr), 0).astype(F32)
    picks, pick_scores = [], []
    onehot_sum = jnp.zeros((E, tr), F32)
    for k in range(TOP_K):
        m = jnp.max(masked, axis=0, keepdims=True)
        ik = jnp.min(jnp.where(masked == m, ei, float(E)), axis=0, keepdims=True)
        oh = ei == ik
        pick_scores.append(jnp.sum(jnp.where(oh, scores, 0.0), axis=0, keepdims=True))
        masked = jnp.where(oh, -jnp.inf, masked)
        onehot_sum += oh.astype(F32)
        picks.append(ik)

    ti = lax.broadcasted_iota(I32, (tr, tr), 0)
    tj = lax.broadcasted_iota(I32, (tr, tr), 1)
    before = (ti < tj).astype(BF16)
    prior = jnp.dot(onehot_sum.astype(BF16), before, preferred_element_type=F32) + run_ref[...]

    total = pick_scores[0]
    for k in range(1, TOP_K):
        total += pick_scores[k]
    for k in range(TOP_K):
        idx_ref[k:k + 1, :] = picks[k].astype(I32)
        wt_ref[k:k + 1, :] = pick_scores[k] / total * ROUTED_SCALE
        rk = jnp.sum(jnp.where(ei == picks[k], prior, 0.0), axis=0, keepdims=True)
        rank_ref[k:k + 1, :] = rk.astype(I32)

    run_ref[...] += jnp.sum(onehot_sum, axis=1, keepdims=True)
    cnt_ref[...] = run_ref[...].astype(I32)


def _route(h2, router_wt_b, router_bias):
    T, D = h2.shape
    tr = TM_ROUTE
    E = N_EXPERTS
    row_spec = pl.BlockSpec((TOP_K, tr), lambda i: (0, i))
    return pl.pallas_call(
        _router_kernel,
        grid=(T // tr,),
        in_specs=[
            pl.BlockSpec((tr, D), lambda i: (i, 0)),
            pl.BlockSpec((E, D), lambda i: (0, 0)),
            pl.BlockSpec((E, 1), lambda i: (0, 0)),
        ],
        out_specs=[row_spec, row_spec, row_spec, pl.BlockSpec((E, 1), lambda i: (0, 0))],
        out_shape=[
            jax.ShapeDtypeStruct((TOP_K, T), I32),
            jax.ShapeDtypeStruct((TOP_K, T), F32),
            jax.ShapeDtypeStruct((TOP_K, T), I32),
            jax.ShapeDtypeStruct((E, 1), I32),
        ],
        scratch_shapes=[pltpu.VMEM((E, 1), F32)],
        compiler_params=_cparams(("arbitrary",)),
        name="router",
    )(h2, router_wt_b, router_bias.reshape(E, 1))


def _dispatch_kernel(pend_ref, h_ref, dest_ref, xg_ref, dest_s, zero_ref, sem_s, sem):
    i = pl.program_id(0)
    td = TM_DISPATCH
    blk = EXPERT_BLOCK

    def zero_copy(e):
        start = pl.multiple_of(pend_ref[e + 1] - blk, blk)
        return pltpu.make_async_copy(zero_ref, xg_ref.at[pl.ds(start, blk)], sem)

    @pl.when(i == 0)
    def _():
        zero_ref[...] = jnp.zeros_like(zero_ref)

        def start_zero(e, c):
            @pl.when(pend_ref[e + 1] > pend_ref[e])
            def _():
                zero_copy(e).start()
            return c

        def wait_zero(e, c):
            @pl.when(pend_ref[e + 1] > pend_ref[e])
            def _():
                zero_copy(e).wait()
            return c

        lax.fori_loop(0, N_EXPERTS, start_zero, 0)
        lax.fori_loop(0, N_EXPERTS, wait_zero, 0)

        def tail_copy(b):
            return pltpu.make_async_copy(zero_ref, xg_ref.at[pl.ds(pl.multiple_of(b * blk, blk), blk)], sem)

        first_unused = pend_ref[N_EXPERTS] // blk
        n_blocks = xg_ref.shape[0] // blk
        lax.fori_loop(first_unused, n_blocks, lambda b, c: (tail_copy(b).start(), c)[1], 0)
        lax.fori_loop(first_unused, n_blocks, lambda b, c: (tail_copy(b).wait(), c)[1], 0)

    cp = pltpu.make_async_copy(dest_ref, dest_s, sem_s)
    cp.start()
    cp.wait()

    def row_copy(t, k):
        return pltpu.make_async_copy(h_ref.at[t], xg_ref.at[dest_s[k, t]], sem)

    def start_rows(t8, c):
        base = pl.multiple_of(t8 * 8, 8)
        for j in range(8):
            for k in range(TOP_K):
                row_copy(base + j, k).start(priority=k % DMA_QUEUES)
        return c

    lax.fori_loop(0, td // 8, start_rows, 0)
    for k in range(TOP_K):
        pltpu.make_async_copy(h_ref, h_ref, sem).wait()


def _dispatch(h2p, dest, pend0, rows):
    T = h2p.shape[0]
    td = TM_DISPATCH
    tile = (ROW_TILE, LANES)
    return pl.pallas_call(
        _dispatch_kernel,
        grid_spec=pltpu.PrefetchScalarGridSpec(
            num_scalar_prefetch=1,
            grid=(T // td,),
            in_specs=[
                pl.BlockSpec((td,) + tile, lambda i, p: (i, 0, 0)),
                pl.BlockSpec((TOP_K, td), lambda i, p: (0, i)),
            ],
            out_specs=pl.BlockSpec(memory_space=pl.ANY),
            scratch_shapes=[
                pltpu.SMEM((TOP_K, td), I32),
                pltpu.VMEM((EXPERT_BLOCK,) + tile, U32),
                pltpu.SemaphoreType.DMA,
                pltpu.SemaphoreType.DMA,
            ],
        ),
        out_shape=jax.ShapeDtypeStruct((rows,) + tile, U32),
        compiler_params=_cparams(("arbitrary",)),
        name="dispatch",
    )(pend0, h2p, dest)


def _experts_kernel(first_ref, gidx_ref, used_ref, nused_ref, ngroups_ref,
                    x_ref, wg_hbm, wu_hbm, wd_hbm, y_ref,
                    wgu_f, wd_f, wgu_b, wd_b, issued, sem):
    i = pl.program_id(0)
    F = D_EXPERT

    def tensor_copy(e, s, t):
        if t == 0:
            return pltpu.make_async_copy(wg_hbm.at[e], wgu_f.at[s, 0], sem.at[s, 0])
        if t == 1:
            return pltpu.make_async_copy(wu_hbm.at[e], wgu_f.at[s, 1], sem.at[s, 1])
        return pltpu.make_async_copy(wd_hbm.at[e], wd_f.at[s], sem.at[s, 2])

    def issue_until(n):
        def body(q, carry):
            h = q // 3
            for t in range(3):
                @pl.when(q % 3 == t)
                def _():
                    tensor_copy(used_ref[h], h % WEIGHT_SLOTS, t).start(priority=DMA_QUEUES - 1)
            return carry

        lax.fori_loop(issued[0], n, body, 0)
        issued[0] = jnp.maximum(issued[0], n)

    @pl.when(i == 0)
    def _():
        issued[0] = 0

    @pl.when(i < nused_ref[0])
    def _():
        g = gidx_ref[i]

        @pl.when(first_ref[i] == 1)
        def _():
            issue_until(3 * (g + 1))
            s = g % WEIGHT_SLOTS
            for t in range(3):
                tensor_copy(used_ref[g], s, t).wait()
            wgu_b[:, :F] = wgu_f[s, 0].astype(BF16)
            wgu_b[:, F:] = wgu_f[s, 1].astype(BF16)
            wd_b[...] = wd_f[s].astype(BF16)

        cap = 3 * jnp.minimum(g + WEIGHT_SLOTS + 1, ngroups_ref[0])
        issue_until(jnp.minimum(issued[0] + 1, cap))

        lo, hi = _unpack_halves(_load_row_tiles(x_ref, 0, EXPERT_BLOCK))
        x = jnp.concatenate([lo.astype(BF16), hi.astype(BF16)], axis=1)
        gu = jnp.dot(x, wgu_b[...], preferred_element_type=F32)
        a = (_silu(gu[:, :F]) * gu[:, F:]).astype(BF16)
        y = jnp.dot(a, wd_b[...], preferred_element_type=F32)
        _store_row_tiles(y_ref, 0, EXPERT_BLOCK, _pack_halves(y))


def _experts(xg, first, gidx, used_list, n_used, n_groups, w_gate, w_up, w_down):
    D = D_MODEL
    bm = EXPERT_BLOCK
    F = D_EXPERT
    nb = xg.shape[0] // (bm * ROW_TILE)
    row_map = lambda i, fi, gi, ul, nu, ng: (jnp.minimum(i, nu[0] - 1), 0)
    return pl.pallas_call(
        _experts_kernel,
        grid_spec=pltpu.PrefetchScalarGridSpec(
            num_scalar_prefetch=5,
            grid=(nb,),
            in_specs=[
                pl.BlockSpec((bm * ROW_TILE, LANES), row_map),
                pl.BlockSpec(memory_space=pl.ANY),
                pl.BlockSpec(memory_space=pl.ANY),
                pl.BlockSpec(memory_space=pl.ANY),
            ],
            out_specs=pl.BlockSpec((bm * ROW_TILE, LANES), row_map),
            scratch_shapes=[
                pltpu.VMEM((WEIGHT_SLOTS, 2, D, F), F32), pltpu.VMEM((WEIGHT_SLOTS, F, D), F32),
                pltpu.VMEM((D, 2 * F), BF16), pltpu.VMEM((F, D), BF16),
                pltpu.SMEM((1,), I32),
                pltpu.SemaphoreType.DMA((WEIGHT_SLOTS, 3)),
            ],
        ),
        out_shape=jax.ShapeDtypeStruct(xg.shape, U32),
        input_output_aliases={5: 0},
        compiler_params=_cparams(("arbitrary",)),
        name="experts",
    )(first, gidx, used_list, n_used, n_groups, xg, w_gate, w_up, w_down)


def _combine_kernel(dest_ref, dnext_ref, wt_ref, x1_ref, h_ref, mod_ref, sg_ref, su_ref, sd_ref, yg_ref, o_ref,
                    dest_s, dnext_s, buf_a, buf_b, sem_s, sem):
    i = pl.program_id(0)
    last = pl.num_programs(0) - 1
    tc = TM_COMBINE
    half = D_MODEL // 2

    for src, dst in ((dest_ref, dest_s), (dnext_ref, dnext_s)):
        cp = pltpu.make_async_copy(src, dst, sem_s)
        cp.start()
        cp.wait()

    def row_copy(idx_s, col, buf, which, t, k):
        return pltpu.make_async_copy(yg_ref.at[idx_s[k, col + t]],
                                     buf.at[pl.ds((k * tc + t) * ROW_TILE, ROW_TILE), :], sem.at[which])

    def start_rows(idx_s, col, buf, which):
        for t in range(tc):
            for k in range(TOP_K):
                row_copy(idx_s, col, buf, which, t, k).start(priority=k % DMA_QUEUES)

    def wait_rows(buf, which):
        pltpu.make_async_copy(buf, buf, sem.at[which]).wait()

    def finish_tile(buf, r0):
        rows = slice(r0, r0 + tc)
        hb = h_ref[rows, :]
        g = jnp.dot(hb, sg_ref[...], preferred_element_type=F32)
        u = jnp.dot(hb, su_ref[...], preferred_element_type=F32)
        ffn = jnp.dot((_silu(g) * u).astype(BF16), sd_ref[...], preferred_element_type=F32)
        ffn_lo, ffn_hi = ffn[:, :half], ffn[:, half:]
        for k in range(TOP_K):
            lo, hi = _unpack_halves(_load_row_tiles(buf, k * tc * ROW_TILE, tc))
            w = wt_ref[rows, k:k + 1]
            ffn_lo += lo * w
            ffn_hi += hi * w
        o_ref[rows, :half] = x1_ref[rows, :half] + mod_ref[0, 5:6, :half] * ffn_lo
        o_ref[rows, half:] = x1_ref[rows, half:] + mod_ref[0, 5:6, half:] * ffn_hi

    @pl.when(i == 0)
    def _():
        def body(t, c):
            for k in range(TOP_K):
                row_copy(dest_s, 0, buf_a, 0, t, k).start(priority=k % DMA_QUEUES)
            return c

        lax.fori_loop(0, tc, body, 0)

    wait_rows(buf_a, 0)
    start_rows(dest_s, tc, buf_b, 1)
    finish_tile(buf_a, 0)
    wait_rows(buf_b, 1)
    start_rows(dnext_s, 0, buf_a, 0)
    finish_tile(buf_b, tc)

    @pl.when(i == last)
    def _():
        wait_rows(buf_a, 0)


def _combine(dest, wt_t, x1, h2, mod3, sg_b, su_b, sd_b, yg):
    T, D = x1.shape
    tc = TM_COMBINE
    step = 2 * tc
    per_batch = SEQ // step
    F = D_EXPERT
    return pl.pallas_call(
        _combine_kernel,
        grid=(T // step,),
        in_specs=[
            pl.BlockSpec((TOP_K, step), lambda i: (0, i)),
            pl.BlockSpec((TOP_K, step), lambda i: (0, jnp.minimum(i + 1, T // step - 1))),
            pl.BlockSpec((step, TOP_K), lambda i: (i, 0)),
            pl.BlockSpec((step, D), lambda i: (i, 0)),
            pl.BlockSpec((step, D), lambda i: (i, 0)),
            pl.BlockSpec((1, 6, D), lambda i: (i // per_batch, 0, 0)),
            pl.BlockSpec((D, F), lambda i: (0, 0)),
            pl.BlockSpec((D, F), lambda i: (0, 0)),
            pl.BlockSpec((F, D), lambda i: (0, 0)),
            pl.BlockSpec(memory_space=pl.ANY),
        ],
        out_specs=pl.BlockSpec((step, D), lambda i: (i, 0)),
        out_shape=jax.ShapeDtypeStruct((T, D), F32),
        scratch_shapes=[
            pltpu.SMEM((TOP_K, step), I32),
            pltpu.SMEM((TOP_K, step), I32),
            pltpu.VMEM((TOP_K * tc * ROW_TILE, LANES), U32),
            pltpu.VMEM((TOP_K * tc * ROW_TILE, LANES), U32),
            pltpu.SemaphoreType.DMA,
            pltpu.SemaphoreType.DMA((2,)),
        ],
        compiler_params=_cparams(("arbitrary",)),
        name="combine",
    )(dest, dest, wt_t, x1, h2, mod3, sg_b, su_b, sd_b, yg)


def kernel(x, c, ada_w, ada_b, mix_norm_g, ffn_norm_g, w_in, q_norm_g, k_norm_g, rel_bias, sgu_ln_g, sgu_ln_b, sgu_w, sgu_b, w_out, router_w, router_bias, shared_w_gate, shared_w_up, shared_w_down, expert_w_gate, expert_w_up, expert_w_down):
    B, S, D = x.shape
    assert S == SEQ and D == D_MODEL and ada_w.shape[0] == 1
    T = B * S
    x2 = x.reshape(T, D)

    mod3 = _modulation(c, ada_w[0], ada_b[0]).reshape(B, 6, D)

    proj = _in_projection(x2, mod3, mix_norm_g, w_in[0].astype(BF16), q_norm_g, k_norm_g)
    attn = _attention(proj.reshape(B, S, D_IN_PROJ), _bias_tables(rel_bias)).reshape(T, D_ATTN)
    causal = jnp.tril(jnp.ones((SGU_CHUNK, SGU_CHUNK), F32))
    sgu = _spatial_gating(proj, (sgu_w[0] * causal).astype(BF16), sgu_ln_g, sgu_ln_b, sgu_b[0].T)
    x1, h2, h2p = _out_projection(attn, sgu, x2, mod3, ffn_norm_g, w_out[0].astype(BF16))

    idx, wts, rank, counts = _route(h2, router_w[0].T.astype(BF16), router_bias[0])

    bm = EXPERT_BLOCK
    counts = counts.reshape(N_EXPERTS)
    padded = (counts + bm - 1) // bm * bm
    pends = jnp.cumsum(padded)
    pstarts = pends - padded
    n_blocks = T * TOP_K // bm + N_EXPERTS
    block_start = jnp.arange(n_blocks, dtype=I32) * bm
    block_exp = jnp.minimum(jnp.sum((pends[None, :] <= block_start[:, None]).astype(I32), axis=1), N_EXPERTS - 1)
    n_used = (pends[-1] // bm).astype(I32).reshape(1)
    eids = jnp.arange(N_EXPERTS, dtype=I32)
    dest = jnp.sum(jnp.where(idx[:, :, None] == eids, pstarts.astype(I32), 0), axis=-1) + rank
    pend0 = jnp.concatenate([jnp.zeros((1,), I32), pends.astype(I32)])
    first = jnp.concatenate([jnp.ones((1,), I32), (block_exp[1:] != block_exp[:-1]).astype(I32)])
    used = counts > 0
    ordinal = jnp.cumsum(used.astype(I32)) - 1
    n_groups = jnp.sum(used.astype(I32)).reshape(1)
    used_list = jnp.sum(jnp.where(used[None, :] & (ordinal[None, :] == eids[:, None]), eids[None, :], 0), axis=-1)
    gidx = jnp.sum(jnp.where(block_exp[:, None] == eids, ordinal, 0), axis=-1)

    n_rows = n_blocks * bm
    xg = _dispatch(h2p.reshape(T, ROW_TILE, LANES), dest, pend0, n_rows)
    yg = _experts(xg.reshape(n_rows * ROW_TILE, LANES), first, gidx.astype(I32), used_list.astype(I32),
                  n_used, n_groups, expert_w_gate[0], expert_w_up[0], expert_w_down[0])
    out = _combine(dest, wts.T, x1, h2, mod3, shared_w_gate[0].astype(BF16), shared_w_up[0].astype(BF16),
                   shared_w_down[0].astype(BF16), yg.reshape(n_rows, ROW_TILE, LANES))
    return out.reshape(B, S, D)
```

```python
import math

import jax
import jax.numpy as jnp
from jax import lax
from jax.experimental import pallas as pl
from jax.experimental.pallas import tpu as pltpu

F32 = jnp.float32
BF16 = jnp.bfloat16
I32 = jnp.int32
U32 = jnp.uint32

D_MODEL = 2048
SEQ = 2048
HEAD_DIM = 128
N_HEADS = 8
D_ATTN = N_HEADS * HEAD_DIM
D_SGU = 1024
N_SGU_GROUPS = 8
SGU_CHUNK = 128
D_IN_PROJ = 3 * D_ATTN + 2 * D_SGU
DILATED_GROUPS = ((128, 1), (512, 4), (2048, 16))
BAND = 128
NUM_REL_BUCKETS = 32
REL_MAX_DISTANCE = 2048
N_EXPERTS = 256
TOP_K = 8
N_EXPERT_GROUPS = 8
GROUP_SIZE = N_EXPERTS // N_EXPERT_GROUPS
TOPK_GROUPS = 4
D_EXPERT = 512
ROUTED_SCALE = 2.5
NORM_EPS = 1e-6
MASK_VALUE = -1e30

LANES = 128
ROW_TILE = D_MODEL // 2 // LANES
VMEM_LIMIT = 56 * 1024 * 1024
DMA_QUEUES = 2

TM_PROJ = 256
TM_SGU = 512
TM_OUT = 256
TM_ROUTE = 512
TM_DISPATCH = 256
TN_MOD = 1024
TM_COMBINE = 128
EXPERT_BLOCK = 128
WEIGHT_SLOTS = 2
ATTN_PIECES_IN_FLIGHT = 6


def _cparams(sem):
    return pltpu.CompilerParams(dimension_semantics=sem, vmem_limit_bytes=VMEM_LIMIT)


def _silu(v):
    return v * jax.nn.sigmoid(v)


def _gelu(v):
    return 0.5 * v * (1.0 + lax.erf(v * (1.0 / math.sqrt(2.0))))


def _pack_halves(v):
    h = v.shape[1] // 2
    lo = lax.bitcast_convert_type(v[:, :h].astype(BF16).astype(F32), U32)
    hi = lax.bitcast_convert_type(v[:, h:].astype(BF16).astype(F32), U32)
    return (hi & jnp.uint32(0xFFFF0000)) | (lo >> 16)


def _unpack_halves(w):
    lo = lax.bitcast_convert_type(w << 16, F32)
    hi = lax.bitcast_convert_type(w & jnp.uint32(0xFFFF0000), F32)
    return lo, hi


def _store_row_tiles(ref, base, rows, packed):
    for s in range(ROW_TILE):
        ref[pl.ds(base + s, rows, stride=ROW_TILE), :] = packed[:, s * LANES:(s + 1) * LANES]


def _load_row_tiles(ref, base, rows):
    return jnp.concatenate([ref[pl.ds(base + s, rows, stride=ROW_TILE), :] for s in range(ROW_TILE)], axis=1)


def _mod_kernel(c_ref, w_ref, b_ref, o_ref):
    ca = _silu(c_ref[...]).astype(BF16)
    o_ref[...] = jnp.dot(ca, w_ref[...].astype(BF16), preferred_element_type=F32) + b_ref[...]


def _modulation(c, ada_w, ada_b):
    B, D = c.shape
    N = ada_w.shape[1]
    tn = TN_MOD
    return pl.pallas_call(
        _mod_kernel,
        grid=(N // tn,),
        in_specs=[
            pl.BlockSpec((B, D), lambda j: (0, 0)),
            pl.BlockSpec((D, tn), lambda j: (0, j)),
            pl.BlockSpec((1, tn), lambda j: (0, j)),
        ],
        out_specs=pl.BlockSpec((B, tn), lambda j: (0, j)),
        out_shape=jax.ShapeDtypeStruct((B, N), F32),
        compiler_params=_cparams(("arbitrary",)),
        name="modulation",
    )(c, ada_w, ada_b.reshape(1, N))


def _inproj_kernel(x_ref, mod_ref, g_ref, w_ref, qg_ref, kg_ref, o_ref):
    x = x_ref[...]
    r = lax.rsqrt(jnp.mean(x * x, axis=-1, keepdims=True) + NORM_EPS)
    h = ((x * r) * g_ref[...] * (1.0 + mod_ref[0, 1:2, :]) + mod_ref[0, 0:1, :]).astype(BF16)

    def head_norm(acc, col0, gain_ref, scale):
        for hd in range(N_HEADS):
            a = acc[:, hd * HEAD_DIM:(hd + 1) * HEAD_DIM]
            r = lax.rsqrt(jnp.mean(a * a, axis=-1, keepdims=True) + NORM_EPS)
            cols = slice(col0 + hd * HEAD_DIM, col0 + (hd + 1) * HEAD_DIM)
            o_ref[:, cols] = ((a * r) * gain_ref[...] * scale).astype(BF16)

    def epilogue(j, acc):
        col0 = j * D_ATTN
        if j == 0:
            head_norm(acc, col0, qg_ref, HEAD_DIM ** -0.5)
        elif j == 1:
            head_norm(acc, col0, kg_ref, 1.0)
        else:
            o_ref[:, col0:col0 + D_ATTN] = acc.astype(BF16)

    pending = None
    for j in range(D_IN_PROJ // D_ATTN):
        acc = jnp.dot(h, w_ref[:, j * D_ATTN:(j + 1) * D_ATTN], preferred_element_type=F32)
        if pending is not None:
            epilogue(*pending)
        pending = (j, acc)
    epilogue(*pending)


def _in_projection(x2, mod3, norm_g, w_in_b, q_g, k_g):
    T, D = x2.shape
    tm = TM_PROJ
    per_batch = SEQ // tm
    return pl.pallas_call(
        _inproj_kernel,
        grid=(T // tm,),
        in_specs=[
            pl.BlockSpec((tm, D), lambda i: (i, 0)),
            pl.BlockSpec((1, 6, D), lambda i: (i // per_batch, 0, 0)),
            pl.BlockSpec((1, D), lambda i: (0, 0)),
            pl.BlockSpec((D, D_IN_PROJ), lambda i: (0, 0), pipeline_mode=pl.Buffered(1)),
            pl.BlockSpec((1, HEAD_DIM), lambda i: (0, 0)),
            pl.BlockSpec((1, HEAD_DIM), lambda i: (0, 0)),
        ],
        out_specs=pl.BlockSpec((tm, D_IN_PROJ), lambda i: (i, 0)),
        out_shape=jax.ShapeDtypeStruct((T, D_IN_PROJ), BF16),
        compiler_params=_cparams(("arbitrary",)),
        name="in_projection",
    )(x2, mod3, norm_g, w_in_b, q_g, k_g)


def _bias_tables(rel_bias):
    n = BAND
    qi = jnp.arange(n)[:, None]
    ki = jnp.arange(2 * n)[None, :]
    steps = n + qi - ki
    in_band = (steps >= 0) & (steps <= n)
    max_exact = NUM_REL_BUCKETS // 2
    tabs = []
    for _, dilation in DILATED_GROUPS:
        dist = jnp.clip(steps, 0, n) * dilation
        nf = jnp.maximum(dist, 1).astype(F32)
        large = max_exact + (jnp.log(nf / max_exact) / math.log(REL_MAX_DISTANCE / max_exact)
                             * (NUM_REL_BUCKETS - max_exact)).astype(I32)
        large = jnp.minimum(large, NUM_REL_BUCKETS - 1)
        bucket = jnp.where(dist < max_exact, dist, large)
        onehot = jax.nn.one_hot(bucket, NUM_REL_BUCKETS, dtype=F32)
        b = jnp.einsum("qkb,bh->hqk", onehot, rel_bias.astype(F32), precision=lax.Precision.HIGHEST)
        tabs.append(jnp.where(in_band[None], b, MASK_VALUE))
    return jnp.stack(tabs)


def _attn_kernel(q_ref, k_ref, v_ref, bias_ref, o_ref, qf, kf, vf, q4, k4, v4,
                 o0, l0, o1, l1, o2, l2, stage):
    n = BAND
    quarter = SEQ // 4

    for src, nat, res in ((q_ref, qf, q4), (k_ref, kf, k4), (v_ref, vf, v4)):
        nat[...] = src[...].astype(F32)
        for r in range(4):
            res[r * quarter:(r + 1) * quarter, :] = nat[pl.ds(r, quarter, stride=4), :]

    def logits(qb, kb, bias):
        return lax.dot_general(qb, kb, (((1,), (1,)), ((), ())), preferred_element_type=F32) + bias

    def finish(l, vb, o_out, l_out, rows):
        m = jnp.max(l, axis=-1, keepdims=True)
        p = jnp.exp(l - m)
        s = jnp.sum(p, axis=-1, keepdims=True)
        o = jnp.dot(p.astype(BF16), vb, preferred_element_type=F32)
        o_out[rows, :] = o / s
        l_out[rows, :] = jnp.broadcast_to(m + jnp.log(s), (n, LANES))

    pieces = []

    for i in range(SEQ // n):
        rows = slice(i * n, (i + 1) * n)
        krows = rows if i == 0 else slice((i - 1) * n, (i + 1) * n)
        bias = (0, slice(None), slice(n, None)) if i == 0 else (0,)
        pieces.append((q_ref, k_ref, v_ref, rows, krows, bias, o0, l0))

    for r in range(4):
        for blk in range(quarter // n):
            base = r * quarter + blk * n
            rows = slice(base, base + n)
            krows = rows if blk == 0 else slice(base - n, base + n)
            bias = (1, slice(None), slice(n, None)) if blk == 0 else (1,)
            pieces.append((q4, k4, v4, rows, krows, bias, o1, l1))

    for r in range(4):
        for a in range(4):
            rows = pl.ds(r * quarter + a, n, stride=4)
            pieces.append((q4, k4, v4, rows, rows, (2, slice(None), slice(n, None)), o2, l2))

    pending = []
    for qs, ks, vs, rows, krows, bias, o_out, l_out in pieces:
        l = logits(qs[rows, :].astype(BF16), ks[krows, :].astype(BF16), bias_ref[bias])
        if len(pending) == ATTN_PIECES_IN_FLIGHT:
            finish(*pending.pop(0))
        pending.append((l, vs[krows, :].astype(BF16), o_out, l_out, rows))
    for args in pending:
        finish(*args)

    for r in range(4):
        for c in range(quarter // n):
            rows = slice(r * quarter + c * n, r * quarter + (c + 1) * n)
            nat = pl.ds(r + 4 * c * n, n, stride=4)
            a0, a1, a2 = l0[nat, :], l1[rows, :], l2[rows, :]
            m = jnp.maximum(jnp.maximum(a0, a1), a2)
            e0, e1, e2 = jnp.exp(a0 - m), jnp.exp(a1 - m), jnp.exp(a2 - m)
            mix = (e0 * o0[nat, :] + e1 * o1[rows, :] + e2 * o2[rows, :]) / (e0 + e1 + e2)
            stage[nat, :] = mix
    o_ref[...] = stage[...].astype(BF16)


def _attention(proj3, bias_tabs):
    B = proj3.shape[0]
    blk = lambda off: pl.BlockSpec((None, SEQ, HEAD_DIM), lambda b, h: (b, 0, off + h))
    return pl.pallas_call(
        _attn_kernel,
        grid=(B, N_HEADS),
        in_specs=[
            blk(0), blk(N_HEADS), blk(2 * N_HEADS),
            pl.BlockSpec((3, None, BAND, 2 * BAND), lambda b, h: (0, h, 0, 0)),
        ],
        out_specs=pl.BlockSpec((None, SEQ, HEAD_DIM), lambda b, h: (b, 0, h)),
        out_shape=jax.ShapeDtypeStruct((B, SEQ, D_ATTN), BF16),
        scratch_shapes=[pltpu.VMEM((SEQ, HEAD_DIM), F32) for _ in range(13)],
        compiler_params=_cparams(("arbitrary", "arbitrary")),
        name="dilated_attention",
    )(proj3, proj3, proj3, bias_tabs)


def _sgu_kernel(u_ref, z_ref, w_ref, g_ref, b_ref, bs_ref, o_ref):
    n = SGU_CHUNK
    for c in range(TM_SGU // n):
        rs = slice(c * n, (c + 1) * n)
        for g in range(N_SGU_GROUPS):
            cs = slice(g * n, (g + 1) * n)
            z = _gelu(z_ref[rs, cs].astype(F32))
            mu = jnp.mean(z, axis=-1, keepdims=True)
            var = jnp.maximum(jnp.mean(z * z, axis=-1, keepdims=True) - mu * mu, 0.0)
            zn = ((z - mu) * lax.rsqrt(var + NORM_EPS)) * g_ref[:, cs] + b_ref[:, cs]
            mixed = jnp.dot(w_ref[g], zn.astype(BF16), preferred_element_type=F32) + bs_ref[:, g:g + 1]
            u = _gelu(u_ref[rs, cs].astype(F32))
            o_ref[rs, cs] = (u * mixed).astype(BF16)


def _spatial_gating(proj, w_causal_b, ln_g, ln_b, bs_t):
    T = proj.shape[0]
    tm = TM_SGU
    ucol = 3 * D_ATTN // D_SGU
    return pl.pallas_call(
        _sgu_kernel,
        grid=(T // tm,),
        in_specs=[
            pl.BlockSpec((tm, D_SGU), lambda i: (i, ucol)),
            pl.BlockSpec((tm, D_SGU), lambda i: (i, ucol + 1)),
            pl.BlockSpec((N_SGU_GROUPS, SGU_CHUNK, SGU_CHUNK), lambda i: (0, 0, 0)),
            pl.BlockSpec((1, D_SGU), lambda i: (0, 0)),
            pl.BlockSpec((1, D_SGU), lambda i: (0, 0)),
            pl.BlockSpec((SGU_CHUNK, N_SGU_GROUPS), lambda i: (0, 0)),
        ],
        out_specs=pl.BlockSpec((tm, D_SGU), lambda i: (i, 0)),
        out_shape=jax.ShapeDtypeStruct((T, D_SGU), BF16),
        compiler_params=_cparams(("arbitrary",)),
        name="spatial_gating",
    )(proj, proj, w_causal_b, ln_g, ln_b, bs_t)


def _outproj_kernel(a_ref, s_ref, x_ref, mod_ref, g_ref, w_ref, x1_ref, h2_ref, h2p_ref):
    mixed = jnp.dot(a_ref[...], w_ref[0:D_ATTN, :], preferred_element_type=F32)
    mixed += jnp.dot(s_ref[...], w_ref[D_ATTN:, :], preferred_element_type=F32)
    x1 = x_ref[...] + mod_ref[0, 2:3, :] * mixed
    x1_ref[...] = x1
    r = lax.rsqrt(jnp.mean(x1 * x1, axis=-1, keepdims=True) + NORM_EPS)
    h2 = (x1 * r) * g_ref[...] * (1.0 + mod_ref[0, 4:5, :]) + mod_ref[0, 3:4, :]
    h2_ref[...] = h2.astype(BF16)
    _store_row_tiles(h2p_ref, 0, TM_OUT, _pack_halves(h2))


def _out_projection(attn, sgu, x2, mod3, norm_g, w_out_b):
    T, D = x2.shape
    tm = TM_OUT
    per_batch = SEQ // tm
    return pl.pallas_call(
        _outproj_kernel,
        grid=(T // tm,),
        in_specs=[
            pl.BlockSpec((tm, D_ATTN), lambda i: (i, 0)),
            pl.BlockSpec((tm, D_SGU), lambda i: (i, 0)),
            pl.BlockSpec((tm, D), lambda i: (i, 0)),
            pl.BlockSpec((1, 6, D), lambda i: (i // per_batch, 0, 0)),
            pl.BlockSpec((1, D), lambda i: (0, 0)),
            pl.BlockSpec((D_ATTN + D_SGU, D), lambda i: (0, 0), pipeline_mode=pl.Buffered(1)),
        ],
        out_specs=[pl.BlockSpec((tm, D), lambda i: (i, 0)), pl.BlockSpec((tm, D), lambda i: (i, 0)),
                   pl.BlockSpec((tm * ROW_TILE, LANES), lambda i: (i, 0))],
        out_shape=[jax.ShapeDtypeStruct((T, D), F32), jax.ShapeDtypeStruct((T, D), BF16),
                   jax.ShapeDtypeStruct((T * ROW_TILE, LANES), U32)],
        compiler_params=_cparams(("arbitrary",)),
        name="out_projection",
    )(attn, sgu, x2, mod3, norm_g, w_out_b)


def _router_kernel(h_ref, w_ref, b_ref, idx_ref, wt_ref, rank_ref, cnt_ref, run_ref):
    i = pl.program_id(0)
    tr = TM_ROUTE
    E = N_EXPERTS

    @pl.when(i == 0)
    def _():
        run_ref[...] = jnp.zeros_like(run_ref)

    logits = lax.dot_general(w_ref[...], h_ref[...], (((1,), (1,)), ((), ())), preferred_element_type=F32)
    scores = jax.nn.sigmoid(logits)
    sel = scores + b_ref[...]

    slabs = [sel[g * GROUP_SIZE:(g + 1) * GROUP_SIZE, :] for g in range(N_EXPERT_GROUPS)]
    si = lax.broadcasted_iota(I32, (GROUP_SIZE, tr), 0).astype(F32)
    gs = []
    for slab in slabs:
        m1 = jnp.max(slab, axis=0, keepdims=True)
        first = jnp.min(jnp.where(slab == m1, si, float(GROUP_SIZE)), axis=0, keepdims=True)
        m2 = jnp.max(jnp.where(si == first, -jnp.inf, slab), axis=0, keepdims=True)
        gs.append(m1 + m2)

    kept = []
    for g in range(N_EXPERT_GROUPS):
        beaten = jnp.zeros((1, tr), F32)
        for o in range(N_EXPERT_GROUPS):
            if o < g:
                beaten += (gs[o] >= gs[g]).astype(F32)
            elif o > g:
                beaten += (gs[o] > gs[g]).astype(F32)
        kept.append(jnp.where(beaten < TOPK_GROUPS, slabs[g], -jnp.inf))
    masked = jnp.concatenate(kept, axis=0)

    ei = lax.broadcasted_iota(I32, (E, tr), 0).astype(F32)
    picks, pick_scores = [], []
    onehot_sum = jnp.zeros((E, tr), F32)
    for k in range(TOP_K):
        m = jnp.max(masked, axis=0, keepdims=True)
        ik = jnp.min(jnp.where(masked == m, ei, float(E)), axis=0, keepdims=True)
        oh = ei == ik
        pick_scores.append(jnp.sum(jnp.where(oh, scores, 0.0), axis=0, keepdims=True))
        masked = jnp.where(oh, -jnp.inf, masked)
        onehot_sum += oh.astype(F32)
        picks.append(ik)

    ti = lax.broadcasted_iota(I32, (tr, tr), 0)
    tj = lax.broadcasted_iota(I32, (tr, tr), 1)
    before = (ti < tj).astype(BF16)
    prior = jnp.dot(onehot_sum.astype(BF16), before, preferred_element_type=F32) + run_ref[...]

    total = pick_scores[0]
    for k in range(1, TOP_K):
        total += pick_scores[k]
    for k in range(TOP_K):
        idx_ref[k:k + 1, :] = picks[k].astype(I32)
        wt_ref[k:k + 1, :] = pick_scores[k] / total * ROUTED_SCALE
        rk = jnp.sum(jnp.where(ei == picks[k], prior, 0.0), axis=0, keepdims=True)
        rank_ref[k:k + 1, :] = rk.astype(I32)

    run_ref[...] += jnp.sum(onehot_sum, axis=1, keepdims=True)
    cnt_ref[...] = run_ref[...].astype(I32)


def _route(h2, router_wt_b, router_bias):
    T, D = h2.shape
    tr = TM_ROUTE
    E = N_EXPERTS
    row_spec = pl.BlockSpec((TOP_K, tr), lambda i: (0, i))
    return pl.pallas_call(
        _router_kernel,
        grid=(T // tr,),
        in_specs=[
            pl.BlockSpec((tr, D), lambda i: (i, 0)),
            pl.BlockSpec((E, D), lambda i: (0, 0)),
            pl.BlockSpec((E, 1), lambda i: (0, 0)),
        ],
        out_specs=[row_spec, row_spec, row_spec, pl.BlockSpec((E, 1), lambda i: (0, 0))],
        out_shape=[
            jax.ShapeDtypeStruct((TOP_K, T), I32),
            jax.ShapeDtypeStruct((TOP_K, T), F32),
            jax.ShapeDtypeStruct((TOP_K, T), I32),
            jax.ShapeDtypeStruct((E, 1), I32),
        ],
        scratch_shapes=[pltpu.VMEM((E, 1), F32)],
        compiler_params=_cparams(("arbitrary",)),
        name="router",
    )(h2, router_wt_b, router_bias.reshape(E, 1))


def _dispatch_kernel(pend_ref, h_ref, dest_ref, xg_ref, dest_s, zero_ref, sem_s, sem):
    i = pl.program_id(0)
    td = TM_DISPATCH
    blk = EXPERT_BLOCK

    def zero_copy(e):
        start = pl.multiple_of(pend_ref[e + 1] - blk, blk)
        return pltpu.make_async_copy(zero_ref, xg_ref.at[pl.ds(start, blk)], sem)

    @pl.when(i == 0)
    def _():
        zero_ref[...] = jnp.zeros_like(zero_ref)

        def start_zero(e, c):
            @pl.when(pend_ref[e + 1] > pend_ref[e])
            def _():
                zero_copy(e).start()
            return c

        def wait_zero(e, c):
            @pl.when(pend_ref[e + 1] > pend_ref[e])
            def _():
                zero_copy(e).wait()
            return c

        lax.fori_loop(0, N_EXPERTS, start_zero, 0)
        lax.fori_loop(0, N_EXPERTS, wait_zero, 0)

        def tail_copy(b):
            return pltpu.make_async_copy(zero_ref, xg_ref.at[pl.ds(pl.multiple_of(b * blk, blk), blk)], sem)

        first_unused = pend_ref[N_EXPERTS] // blk
        n_blocks = xg_ref.shape[0] // blk
        lax.fori_loop(first_unused, n_blocks, lambda b, c: (tail_copy(b).start(), c)[1], 0)
        lax.fori_loop(first_unused, n_blocks, lambda b, c: (tail_copy(b).wait(), c)[1], 0)

    cp = pltpu.make_async_copy(dest_ref, dest_s, sem_s)
    cp.start()
    cp.wait()

    def row_copy(t, k):
        return pltpu.make_async_copy(h_ref.at[t], xg_ref.at[dest_s[k, t]], sem)

    def start_rows(t8, c):
        base = pl.multiple_of(t8 * 8, 8)
        for j in range(8):
            for k in range(TOP_K):
                row_copy(base + j, k).start(priority=k % DMA_QUEUES)
        return c

    lax.fori_loop(0, td // 8, start_rows, 0)
    for k in range(TOP_K):
        pltpu.make_async_copy(h_ref, h_ref, sem).wait()


def _dispatch(h2p, dest, pend0, rows):
    T = h2p.shape[0]
    td = TM_DISPATCH
    tile = (ROW_TILE, LANES)
    return pl.pallas_call(
        _dispatch_kernel,
        grid_spec=pltpu.PrefetchScalarGridSpec(
            num_scalar_prefetch=1,
            grid=(T // td,),
            in_specs=[
                pl.BlockSpec((td,) + tile, lambda i, p: (i, 0, 0)),
                pl.BlockSpec((TOP_K, td), lambda i, p: (0, i)),
            ],
            out_specs=pl.BlockSpec(memory_space=pl.ANY),
            scratch_shapes=[
                pltpu.SMEM((TOP_K, td), I32),
                pltpu.VMEM((EXPERT_BLOCK,) + tile, U32),
                pltpu.SemaphoreType.DMA,
                pltpu.SemaphoreType.DMA,
            ],
        ),
        out_shape=jax.ShapeDtypeStruct((rows,) + tile, U32),
        compiler_params=_cparams(("arbitrary",)),
        name="dispatch",
    )(pend0, h2p, dest)


def _experts_kernel(first_ref, gidx_ref, used_ref, nused_ref, ngroups_ref,
                    x_ref, wg_hbm, wu_hbm, wd_hbm, y_ref,
                    wgu_f, wd_f, wgu_b, wd_b, issued, sem):
    i = pl.program_id(0)
    F = D_EXPERT

    def tensor_copy(e, s, t):
        if t == 0:
            return pltpu.make_async_copy(wg_hbm.at[e], wgu_f.at[s, 0], sem.at[s, 0])
        if t == 1:
            return pltpu.make_async_copy(wu_hbm.at[e], wgu_f.at[s, 1], sem.at[s, 1])
        return pltpu.make_async_copy(wd_hbm.at[e], wd_f.at[s], sem.at[s, 2])

    def issue_until(n):
        def body(q, carry):
            h = q // 3
            for t in range(3):
                @pl.when(q % 3 == t)
                def _():
                    tensor_copy(used_ref[h], h % WEIGHT_SLOTS, t).start(priority=DMA_QUEUES - 1)
            return carry

        lax.fori_loop(issued[0], n, body, 0)
        issued[0] = jnp.maximum(issued[0], n)

    @pl.when(i == 0)
    def _():
        issued[0] = 0

    @pl.when(i < nused_ref[0])
    def _():
        g = gidx_ref[i]

        @pl.when(first_ref[i] == 1)
        def _():
            issue_until(3 * (g + 1))
            s = g % WEIGHT_SLOTS
            for t in range(3):
                tensor_copy(used_ref[g], s, t).wait()
            wgu_b[:, :F] = wgu_f[s, 0].astype(BF16)
            wgu_b[:, F:] = wgu_f[s, 1].astype(BF16)
            wd_b[...] = wd_f[s].astype(BF16)

        cap = 3 * jnp.minimum(g + WEIGHT_SLOTS + 1, ngroups_ref[0])
        issue_until(jnp.minimum(issued[0] + 1, cap))

        lo, hi = _unpack_halves(_load_row_tiles(x_ref, 0, EXPERT_BLOCK))
        x = jnp.concatenate([lo.astype(BF16), hi.astype(BF16)], axis=1)
        gu = jnp.dot(x, wgu_b[...], preferred_element_type=F32)
        a = (_silu(gu[:, :F]) * gu[:, F:]).astype(BF16)
        y = jnp.dot(a, wd_b[...], preferred_element_type=F32)
        _store_row_tiles(y_ref, 0, EXPERT_BLOCK, _pack_halves(y))


def _experts(xg, first, gidx, used_list, n_used, n_groups, w_gate, w_up, w_down):
    D = D_MODEL
    bm = EXPERT_BLOCK
    F = D_EXPERT
    nb = xg.shape[0] // (bm * ROW_TILE)
    row_map = lambda i, fi, gi, ul, nu, ng: (jnp.minimum(i, nu[0] - 1), 0)
    return pl.pallas_call(
        _experts_kernel,
        grid_spec=pltpu.PrefetchScalarGridSpec(
            num_scalar_prefetch=5,
            grid=(nb,),
            in_specs=[
                pl.BlockSpec((bm * ROW_TILE, LANES), row_map),
                pl.BlockSpec(memory_space=pl.ANY),
                pl.BlockSpec(memory_space=pl.ANY),
                pl.BlockSpec(memory_space=pl.ANY),
            ],
            out_specs=pl.BlockSpec((bm * ROW_TILE, LANES), row_map),
            scratch_shapes=[
                pltpu.VMEM((WEIGHT_SLOTS, 2, D, F), F32), pltpu.VMEM((WEIGHT_SLOTS, F, D), F32),
                pltpu.VMEM((D, 2 * F), BF16), pltpu.VMEM((F, D), BF16),
                pltpu.SMEM((1,), I32),
                pltpu.SemaphoreType.DMA((WEIGHT_SLOTS, 3)),
            ],
        ),
        out_shape=jax.ShapeDtypeStruct(xg.shape, U32),
        input_output_aliases={5: 0},
        compiler_params=_cparams(("arbitrary",)),
        name="experts",
    )(first, gidx, used_list, n_used, n_groups, xg, w_gate, w_up, w_down)


def _combine_kernel(dest_ref, dnext_ref, wt_ref, x1_ref, h_ref, mod_ref, sg_ref, su_ref, sd_ref, yg_ref, o_ref,
                    dest_s, dnext_s, buf_a, buf_b, sem_s, sem):
    i = pl.program_id(0)
    last = pl.num_programs(0) - 1
    tc = TM_COMBINE
    half = D_MODEL // 2

    for src, dst in ((dest_ref, dest_s), (dnext_ref, dnext_s)):
        cp = pltpu.make_async_copy(src, dst, sem_s)
        cp.start()
        cp.wait()

    def row_copy(idx_s, col, buf, which, t, k):
        return pltpu.make_async_copy(yg_ref.at[idx_s[k, col + t]],
                                     buf.at[pl.ds((k * tc + t) * ROW_TILE, ROW_TILE), :], sem.at[which])

    def start_rows(idx_s, col, buf, which):
        for t in range(tc):
            for k in range(TOP_K):
                row_copy(idx_s, col, buf, which, t, k).start(priority=k % DMA_QUEUES)

    def wait_rows(buf, which):
        pltpu.make_async_copy(buf, buf, sem.at[which]).wait()

    def finish_tile(buf, r0):
        rows = slice(r0, r0 + tc)
        hb = h_ref[rows, :]
        g = jnp.dot(hb, sg_ref[...], preferred_element_type=F32)
        u = jnp.dot(hb, su_ref[...], preferred_element_type=F32)
        ffn = jnp.dot((_silu(g) * u).astype(BF16), sd_ref[...], preferred_element_type=F32)
        ffn_lo, ffn_hi = ffn[:, :half], ffn[:, half:]
        for k in range(TOP_K):
            lo, hi = _unpack_halves(_load_row_tiles(buf, k * tc * ROW_TILE, tc))
            w = wt_ref[rows, k:k + 1]
            ffn_lo += lo * w
            ffn_hi += hi * w
        o_ref[rows, :half] = x1_ref[rows, :half] + mod_ref[0, 5:6, :half] * ffn_lo
        o_ref[rows, half:] = x1_ref[rows, half:] + mod_ref[0, 5:6, half:] * ffn_hi

    @pl.when(i == 0)
    def _():
        def body(t, c):
            for k in range(TOP_K):
                row_copy(dest_s, 0, buf_a, 0, t, k).start(priority=k % DMA_QUEUES)
            return c

        lax.fori_loop(0, tc, body, 0)

    wait_rows(buf_a, 0)
    start_rows(dest_s, tc, buf_b, 1)
    finish_tile(buf_a, 0)
    wait_rows(buf_b, 1)
    start_rows(dnext_s, 0, buf_a, 0)
    finish_tile(buf_b, tc)

    @pl.when(i == last)
    def _():
        wait_rows(buf_a, 0)


def _combine(dest, wt_t, x1, h2, mod3, sg_b, su_b, sd_b, yg):
    T, D = x1.shape
    tc = TM_COMBINE
    step = 2 * tc
    per_batch = SEQ // step
    F = D_EXPERT
    return pl.pallas_call(
        _combine_kernel,
        grid=(T // step,),
        in_specs=[
            pl.BlockSpec((TOP_K, step), lambda i: (0, i)),
            pl.BlockSpec((TOP_K, step), lambda i: (0, jnp.minimum(i + 1, T // step - 1))),
            pl.BlockSpec((step, TOP_K), lambda i: (i, 0)),
            pl.BlockSpec((step, D), lambda i: (i, 0)),
            pl.BlockSpec((step, D), lambda i: (i, 0)),
            pl.BlockSpec((1, 6, D), lambda i: (i // per_batch, 0, 0)),
            pl.BlockSpec((D, F), lambda i: (0, 0)),
            pl.BlockSpec((D, F), lambda i: (0, 0)),
            pl.BlockSpec((F, D), lambda i: (0, 0)),
            pl.BlockSpec(memory_space=pl.ANY),
        ],
        out_specs=pl.BlockSpec((step, D), lambda i: (i, 0)),
        out_shape=jax.ShapeDtypeStruct((T, D), F32),
        scratch_shapes=[
            pltpu.SMEM((TOP_K, step), I32),
            pltpu.SMEM((TOP_K, step), I32),
            pltpu.VMEM((TOP_K * tc * ROW_TILE, LANES), U32),
            pltpu.VMEM((TOP_K * tc * ROW_TILE, LANES), U32),
            pltpu.SemaphoreType.DMA,
            pltpu.SemaphoreType.DMA((2,)),
        ],
        compiler_params=_cparams(("arbitrary",)),
        name="combine",
    )(dest, dest, wt_t, x1, h2, mod3, sg_b, su_b, sd_b, yg)


def kernel(x, c, ada_w, ada_b, mix_norm_g, ffn_norm_g, w_in, q_norm_g, k_norm_g, rel_bias, sgu_ln_g, sgu_ln_b, sgu_w, sgu_b, w_out, router_w, router_bias, shared_w_gate, shared_w_up, shared_w_down, expert_w_gate, expert_w_up, expert_w_down):
    B, S, D = x.shape
    assert S == SEQ and D == D_MODEL and ada_w.shape[0] == 1
    T = B * S
    x2 = x.reshape(T, D)

    mod3 = _modulation(c, ada_w[0], ada_b[0]).reshape(B, 6, D)

    proj = _in_projection(x2, mod3, mix_norm_g, w_in[0].astype(BF16), q_norm_g, k_norm_g)
    attn = _attention(proj.reshape(B, S, D_IN_PROJ), _bias_tables(rel_bias)).reshape(T, D_ATTN)
    causal = jnp.tril(jnp.ones((SGU_CHUNK, SGU_CHUNK), F32))
    sgu = _spatial_gating(proj, (sgu_w[0] * causal).astype(BF16), sgu_ln_g, sgu_ln_b, sgu_b[0].T)
    x1, h2, h2p = _out_projection(attn, sgu, x2, mod3, ffn_norm_g, w_out[0].astype(BF16))

    idx, wts, rank, counts = _route(h2, router_w[0].T.astype(BF16), router_bias[0])

    bm = EXPERT_BLOCK
    counts = counts.reshape(N_EXPERTS)
    padded = (counts + bm - 1) // bm * bm
    pends = jnp.cumsum(padded)
    pstarts = pends - padded
    n_blocks = T * TOP_K // bm + N_EXPERTS
    block_start = jnp.arange(n_blocks, dtype=I32) * bm
    block_exp = jnp.minimum(jnp.sum((pends[None, :] <= block_start[:, None]).astype(I32), axis=1), N_EXPERTS - 1)
    n_used = (pends[-1] // bm).astype(I32).reshape(1)
    eids = jnp.arange(N_EXPERTS, dtype=I32)
    dest = jnp.sum(jnp.where(idx[:, :, None] == eids, pstarts.astype(I32), 0), axis=-1) + rank
    pend0 = jnp.concatenate([jnp.zeros((1,), I32), pends.astype(I32)])
    first = jnp.concatenate([jnp.ones((1,), I32), (block_exp[1:] != block_exp[:-1]).astype(I32)])
    used = counts > 0
    ordinal = jnp.cumsum(used.astype(I32)) - 1
    n_groups = jnp.sum(used.astype(I32)).reshape(1)
    used_list = jnp.sum(jnp.where(used[None, :] & (ordinal[None, :] == eids[:, None]), eids[None, :], 0), axis=-1)
    gidx = jnp.sum(jnp.where(block_exp[:, None] == eids, ordinal, 0), axis=-1)

    n_rows = n_blocks * bm
    xg = _dispatch(h2p.reshape(T, ROW_TILE, LANES), dest, pend0, n_rows)
    yg = _experts(xg.reshape(n_rows * ROW_TILE, LANES), first, gidx.astype(I32), used_list.astype(I32),
                  n_used, n_groups, expert_w_gate[0], expert_w_up[0], expert_w_down[0])
    out = _combine(dest, wts.T, x1, h2, mod3, shared_w_gate[0].astype(BF16), shared_w_up[0].astype(BF16),
                   shared_w_down[0].astype(BF16), yg.reshape(n_rows, ROW_TILE, LANES))
    return out.reshape(B, S, D)
```

```python
import math

import jax
import jax.numpy as jnp
from jax import lax
from jax.experimental import pallas as pl
from jax.experimental.pallas import tpu as pltpu

F32 = jnp.float32
BF16 = jnp.bfloat16
I32 = jnp.int32
U32 = jnp.uint32

D_MODEL = 2048
SEQ = 2048
HEAD_DIM = 128
N_HEADS = 8
D_ATTN = N_HEADS * HEAD_DIM
D_SGU = 1024
N_SGU_GROUPS = 8
SGU_CHUNK = 128
D_IN_PROJ = 3 * D_ATTN + 2 * D_SGU
DILATED_GROUPS = ((128, 1), (512, 4), (2048, 16))
BAND = 128
NUM_REL_BUCKETS = 32
REL_MAX_DISTANCE = 2048
N_EXPERTS = 256
TOP_K = 8
N_EXPERT_GROUPS = 8
GROUP_SIZE = N_EXPERTS // N_EXPERT_GROUPS
TOPK_GROUPS = 4
D_EXPERT = 512
ROUTED_SCALE = 2.5
NORM_EPS = 1e-6
MASK_VALUE = -1e30

LANES = 128
ROW_TILE = D_MODEL // 2 // LANES
VMEM_LIMIT = 56 * 1024 * 1024
DMA_QUEUES = 2

TM_PROJ = 256
TM_SGU = 512
TM_OUT = 256
TM_ROUTE = 512
TM_DISPATCH = 256
TN_MOD = 1024
TM_COMBINE = 128
EXPERT_BLOCK = 128
WEIGHT_SLOTS = 2
ATTN_PIECES_IN_FLIGHT = 6


def _cparams(sem):
    return pltpu.CompilerParams(dimension_semantics=sem, vmem_limit_bytes=VMEM_LIMIT)


def _silu(v):
    return v * jax.nn.sigmoid(v)


def _gelu(v):
    return 0.5 * v * (1.0 + lax.erf(v * (1.0 / math.sqrt(2.0))))


def _pack_halves(v):
    h = v.shape[1] // 2
    lo = lax.bitcast_convert_type(v[:, :h].astype(BF16).astype(F32), U32)
    hi = lax.bitcast_convert_type(v[:, h:].astype(BF16).astype(F32), U32)
    return (hi & jnp.uint32(0xFFFF0000)) | (lo >> 16)


def _unpack_halves(w):
    lo = lax.bitcast_convert_type(w << 16, F32)
    hi = lax.bitcast_convert_type(w & jnp.uint32(0xFFFF0000), F32)
    return lo, hi


def _store_row_tiles(ref, base, rows, packed):
    for s in range(ROW_TILE):
        ref[pl.ds(base + s, rows, stride=ROW_TILE), :] = packed[:, s * LANES:(s + 1) * LANES]


def _load_row_tiles(ref, base, rows):
    return jnp.concatenate([ref[pl.ds(base + s, rows, stride=ROW_TILE), :] for s in range(ROW_TILE)], axis=1)


def _mod_kernel(c_ref, w_ref, b_ref, o_ref):
    ca = _silu(c_ref[...]).astype(BF16)
    o_ref[...] = jnp.dot(ca, w_ref[...].astype(BF16), preferred_element_type=F32) + b_ref[...]


def _modulation(c, ada_w, ada_b):
    B, D = c.shape
    N = ada_w.shape[1]
    tn = TN_MOD
    return pl.pallas_call(
        _mod_kernel,
        grid=(N // tn,),
        in_specs=[
            pl.BlockSpec((B, D), lambda j: (0, 0)),
            pl.BlockSpec((D, tn), lambda j: (0, j)),
            pl.BlockSpec((1, tn), lambda j: (0, j)),
        ],
        out_specs=pl.BlockSpec((B, tn), lambda j: (0, j)),
        out_shape=jax.ShapeDtypeStruct((B, N), F32),
        compiler_params=_cparams(("arbitrary",)),
        name="modulation",
    )(c, ada_w, ada_b.reshape(1, N))


def _inproj_kernel(x_ref, mod_ref, g_ref, w_ref, qg_ref, kg_ref, o_ref):
    x = x_ref[...]
    r = lax.rsqrt(jnp.mean(x * x, axis=-1, keepdims=True) + NORM_EPS)
    h = ((x * r) * g_ref[...] * (1.0 + mod_ref[0, 1:2, :]) + mod_ref[0, 0:1, :]).astype(BF16)

    def head_norm(acc, col0, gain_ref, scale):
        for hd in range(N_HEADS):
            a = acc[:, hd * HEAD_DIM:(hd + 1) * HEAD_DIM]
            r = lax.rsqrt(jnp.mean(a * a, axis=-1, keepdims=True) + NORM_EPS)
            cols = slice(col0 + hd * HEAD_DIM, col0 + (hd + 1) * HEAD_DIM)
            o_ref[:, cols] = ((a * r) * gain_ref[...] * scale).astype(BF16)

    def epilogue(j, acc):
        col0 = j * D_ATTN
        if j == 0:
            head_norm(acc, col0, qg_ref, HEAD_DIM ** -0.5)
        elif j == 1:
            head_norm(acc, col0, kg_ref, 1.0)
        else:
            o_ref[:, col0:col0 + D_ATTN] = acc.astype(BF16)

    pending = None
    for j in range(D_IN_PROJ // D_ATTN):
        acc = jnp.dot(h, w_ref[:, j * D_ATTN:(j + 1) * D_ATTN], preferred_element_type=F32)
        if pending is not None:
            epilogue(*pending)
        pending = (j, acc)
    epilogue(*pending)


def _in_projection(x2, mod3, norm_g, w_in_b, q_g, k_g):
    T, D = x2.shape
    tm = TM_PROJ
    per_batch = SEQ // tm
    return pl.pallas_call(
        _inproj_kernel,
        grid=(T // tm,),
        in_specs=[
            pl.BlockSpec((tm, D), lambda i: (i, 0)),
            pl.BlockSpec((1, 6, D), lambda i: (i // per_batch, 0, 0)),
            pl.BlockSpec((1, D), lambda i: (0, 0)),
            pl.BlockSpec((D, D_IN_PROJ), lambda i: (0, 0), pipeline_mode=pl.Buffered(1)),
            pl.BlockSpec((1, HEAD_DIM), lambda i: (0, 0)),
            pl.BlockSpec((1, HEAD_DIM), lambda i: (0, 0)),
        ],
        out_specs=pl.BlockSpec((tm, D_IN_PROJ), lambda i: (i, 0)),
        out_shape=jax.ShapeDtypeStruct((T, D_IN_PROJ), BF16),
        compiler_params=_cparams(("arbitrary",)),
        name="in_projection",
    )(x2, mod3, norm_g, w_in_b, q_g, k_g)


def _bias_tables(rel_bias):
    n = BAND
    qi = jnp.arange(n)[:, None]
    ki = jnp.arange(2 * n)[None, :]
    steps = n + qi - ki
    in_band = (steps >= 0) & (steps <= n)
    max_exact = NUM_REL_BUCKETS // 2
    tabs = []
    for _, dilation in DILATED_GROUPS:
        dist = jnp.clip(steps, 0, n) * dilation
        nf = jnp.maximum(dist, 1).astype(F32)
        large = max_exact + (jnp.log(nf / max_exact) / math.log(REL_MAX_DISTANCE / max_exact)
                             * (NUM_REL_BUCKETS - max_exact)).astype(I32)
        large = jnp.minimum(large, NUM_REL_BUCKETS - 1)
        bucket = jnp.where(dist < max_exact, dist, large)
        onehot = jax.nn.one_hot(bucket, NUM_REL_BUCKETS, dtype=F32)
        b = jnp.einsum("qkb,bh->hqk", onehot, rel_bias.astype(F32), precision=lax.Precision.HIGHEST)
        tabs.append(jnp.where(in_band[None], b, MASK_VALUE))
    return jnp.stack(tabs)


def _attn_kernel(q_ref, k_ref, v_ref, bias_ref, o_ref, qf, kf, vf, q4, k4, v4,
                 o0, l0, o1, l1, o2, l2, stage):
    n = BAND
    quarter = SEQ // 4

    for src, nat, res in ((q_ref, qf, q4), (k_ref, kf, k4), (v_ref, vf, v4)):
        nat[...] = src[...].astype(F32)
        for r in range(4):
            res[r * quarter:(r + 1) * quarter, :] = nat[pl.ds(r, quarter, stride=4), :]

    def logits(qb, kb, bias):
        return lax.dot_general(qb, kb, (((1,), (1,)), ((), ())), preferred_element_type=F32) + bias

    def finish(l, vb, o_out, l_out, rows):
        m = jnp.max(l, axis=-1, keepdims=True)
        p = jnp.exp(l - m)
        s = jnp.sum(p, axis=-1, keepdims=True)
        o = jnp.dot(p.astype(BF16), vb, preferred_element_type=F32)
        o_out[rows, :] = o / s
        l_out[rows, :] = jnp.broadcast_to(m + jnp.log(s), (n, LANES))

    pieces = []

    for i in range(SEQ // n):
        rows = slice(i * n, (i + 1) * n)
        krows = rows if i == 0 else slice((i - 1) * n, (i + 1) * n)
        bias = (0, slice(None), slice(n, None)) if i == 0 else (0,)
        pieces.append((q_ref, k_ref, v_ref, rows, krows, bias, o0, l0))

    for r in range(4):
        for blk in range(quarter // n):
            base = r * quarter + blk * n
            rows = slice(base, base + n)
            krows = rows if blk == 0 else slice(base - n, base + n)
            bias = (1, slice(None), slice(n, None)) if blk == 0 else (1,)
            pieces.append((q4, k4, v4, rows, krows, bias, o1, l1))

    for r in range(4):
        for a in range(4):
            rows = pl.ds(r * quarter + a, n, stride=4)
            pieces.append((q4, k4, v4, rows, rows, (2, slice(None), slice(n, None)), o2, l2))

    pending = []
    for qs, ks, vs, rows, krows, bias, o_out, l_out in pieces:
        l = logits(qs[rows, :].astype(BF16), ks[krows, :].astype(BF16), bias_ref[bias])
        if len(pending) == ATTN_PIECES_IN_FLIGHT:
            finish(*pending.pop(0))
        pending.append((l, vs[krows, :].astype(BF16), o_out, l_out, rows))
    for args in pending:
        finish(*args)

    for r in range(4):
        for c in range(quarter // n):
            rows = slice(r * quarter + c * n, r * quarter + (c + 1) * n)
            nat = pl.ds(r + 4 * c * n, n, stride=4)
            a0, a1, a2 = l0[nat, :], l1[rows, :], l2[rows, :]
            m = jnp.maximum(jnp.maximum(a0, a1), a2)
            e0, e1, e2 = jnp.exp(a0 - m), jnp.exp(a1 - m), jnp.exp(a2 - m)
            mix = (e0 * o0[nat, :] + e1 * o1[rows, :] + e2 * o2[rows, :]) / (e0 + e1 + e2)
            stage[nat, :] = mix
    o_ref[...] = stage[...].astype(BF16)


def _attention(proj3, bias_tabs):
    B = proj3.shape[0]
    blk = lambda off: pl.BlockSpec((None, SEQ, HEAD_DIM), lambda b, h: (b, 0, off + h))
    return pl.pallas_call(
        _attn_kernel,
        grid=(B, N_HEADS),
        in_specs=[
            blk(0), blk(N_HEADS), blk(2 * N_HEADS),
            pl.BlockSpec((3, None, BAND, 2 * BAND), lambda b, h: (0, h, 0, 0)),
        ],
        out_specs=pl.BlockSpec((None, SEQ, HEAD_DIM), lambda b, h: (b, 0, h)),
        out_shape=jax.ShapeDtypeStruct((B, SEQ, D_ATTN), BF16),
        scratch_shapes=[pltpu.VMEM((SEQ, HEAD_DIM), F32) for _ in range(13)],
        compiler_params=_cparams(("arbitrary", "arbitrary")),
        name="dilated_attention",
    )(proj3, proj3, proj3, bias_tabs)


def _sgu_kernel(u_ref, z_ref, w_ref, g_ref, b_ref, bs_ref, o_ref):
    n = SGU_CHUNK
    for c in range(TM_SGU // n):
        rs = slice(c * n, (c + 1) * n)
        for g in range(N_SGU_GROUPS):
            cs = slice(g * n, (g + 1) * n)
            z = _gelu(z_ref[rs, cs].astype(F32))
            mu = jnp.mean(z, axis=-1, keepdims=True)
            var = jnp.maximum(jnp.mean(z * z, axis=-1, keepdims=True) - mu * mu, 0.0)
            zn = ((z - mu) * lax.rsqrt(var + NORM_EPS)) * g_ref[:, cs] + b_ref[:, cs]
            mixed = jnp.dot(w_ref[g], zn.astype(BF16), preferred_element_type=F32) + bs_ref[:, g:g + 1]
            u = _gelu(u_ref[rs, cs].astype(F32))
            o_ref[rs, cs] = (u * mixed).astype(BF16)


def _spatial_gating(proj, w_causal_b, ln_g, ln_b, bs_t):
    T = proj.shape[0]
    tm = TM_SGU
    ucol = 3 * D_ATTN // D_SGU
    return pl.pallas_call(
        _sgu_kernel,
        grid=(T // tm,),
        in_specs=[
            pl.BlockSpec((tm, D_SGU), lambda i: (i, ucol)),
            pl.BlockSpec((tm, D_SGU), lambda i: (i, ucol + 1)),
            pl.BlockSpec((N_SGU_GROUPS, SGU_CHUNK, SGU_CHUNK), lambda i: (0, 0, 0)),
            pl.BlockSpec((1, D_SGU), lambda i: (0, 0)),
            pl.BlockSpec((1, D_SGU), lambda i: (0, 0)),
            pl.BlockSpec((SGU_CHUNK, N_SGU_GROUPS), lambda i: (0, 0)),
        ],
        out_specs=pl.BlockSpec((tm, D_SGU), lambda i: (i, 0)),
        out_shape=jax.ShapeDtypeStruct((T, D_SGU), BF16),
        compiler_params=_cparams(("arbitrary",)),
        name="spatial_gating",
    )(proj, proj, w_causal_b, ln_g, ln_b, bs_t)


def _outproj_kernel(a_ref, s_ref, x_ref, mod_ref, g_ref, w_ref, x1_ref, h2_ref, h2p_ref):
    mixed = jnp.dot(a_ref[...], w_ref[0:D_ATTN, :], preferred_element_type=F32)
    mixed += jnp.dot(s_ref[...], w_ref[D_ATTN:, :], preferred_element_type=F32)
    x1 = x_ref[...] + mod_ref[0, 2:3, :] * mixed
    x1_ref[...] = x1
    r = lax.rsqrt(jnp.mean(x1 * x1, axis=-1, keepdims=True) + NORM_EPS)
    h2 = (x1 * r) * g_ref[...] * (1.0 + mod_ref[0, 4:5, :]) + mod_ref[0, 3:4, :]
    h2_ref[...] = h2.astype(BF16)
    _store_row_tiles(h2p_ref, 0, TM_OUT, _pack_halves(h2))


def _out_projection(attn, sgu, x2, mod3, norm_g, w_out_b):
    T, D = x2.shape
    tm = TM_OUT
    per_batch = SEQ // tm
    return pl.pallas_call(
        _outproj_kernel,
        grid=(T // tm,),
        in_specs=[
            pl.BlockSpec((tm, D_ATTN), lambda i: (i, 0)),
            pl.BlockSpec((tm, D_SGU), lambda i: (i, 0)),
            pl.BlockSpec((tm, D), lambda i: (i, 0)),
            pl.BlockSpec((1, 6, D), lambda i: (i // per_batch, 0, 0)),
            pl.BlockSpec((1, D), lambda i: (0, 0)),
            pl.BlockSpec((D_ATTN + D_SGU, D), lambda i: (0, 0), pipeline_mode=pl.Buffered(1)),
        ],
        out_specs=[pl.BlockSpec((tm, D), lambda i: (i, 0)), pl.BlockSpec((tm, D), lambda i: (i, 0)),
                   pl.BlockSpec((tm * ROW_TILE, LANES), lambda i: (i, 0))],
        out_shape=[jax.ShapeDtypeStruct((T, D), F32), jax.ShapeDtypeStruct((T, D), BF16),
                   jax.ShapeDtypeStruct((T * ROW_TILE, LANES), U32)],
        compiler_params=_cparams(("arbitrary",)),
        name="out_projection",
    )(attn, sgu, x2, mod3, norm_g, w_out_b)


def _router_kernel(h_ref, w_ref, b_ref, idx_ref, wt_ref, rank_ref, cnt_ref, run_ref):
    i = pl.program_id(0)
    tr = TM_ROUTE
    E = N_EXPERTS

    @pl.when(i == 0)
    def _():
        run_ref[...] = jnp.zeros_like(run_ref)

    logits = lax.dot_general(w_ref[...], h_ref[...], (((1,), (1,)), ((), ())), preferred_element_type=F32)
    scores = jax.nn.sigmoid(logits)
    sel = scores + b_ref[...]

    slabs = [sel[g * GROUP_SIZE:(g + 1) * GROUP_SIZE, :] for g in range(N_EXPERT_GROUPS)]
    si = lax.broadcasted_iota(I32, (GROUP_SIZE, tr), 0).astype(F32)
    gs = []
    for slab in slabs:
        m1 = jnp.max(slab, axis=0, keepdims=True)
        first = jnp.min(jnp.where(slab == m1, si, float(GROUP_SIZE)), axis=0, keepdims=True)
        m2 = jnp.max(jnp.where(si == first, -jnp.inf, slab), axis=0, keepdims=True)
        gs.append(m1 + m2)

    kept = []
    for g in range(N_EXPERT_GROUPS):
        beaten = jnp.zeros((1, tr), F32)
        for o in range(N_EXPERT_GROUPS):
            if o < g:
                beaten += (gs[o] >= gs[g]).astype(F32)
            elif o > g:
                beaten += (gs[o] > gs[g]).astype(F32)
        kept.append(jnp.where(beaten < TOPK_GROUPS, slabs[g], -jnp.inf))
    masked = jnp.concatenate(kept, axis=0)

    ei = lax.broadcasted_iota(I32, (E, tr), 0).astype(F32)
    picks, pick_scores = [], []
    onehot_sum = jnp.zeros((E, tr), F32)
    for k in range(TOP_K):
        m = jnp.max(masked, axis=0, keepdims=True)
        ik = jnp.min(jnp.where(masked == m, ei, float(E)), axis=0, keepdims=True)
        oh = ei == ik
        pick_scores.append(jnp.sum(jnp.where(oh, scores, 0.0), axis=0, keepdims=True))
        masked = jnp.where(oh, -jnp.inf, masked)
        onehot_sum += oh.astype(F32)
        picks.append(ik)

    ti = lax.broadcasted_iota(I32, (tr, tr), 0)
    tj = lax.broadcasted_iota(I32, (tr, tr), 1)
    before = (ti < tj).astype(BF16)
    prior = jnp.dot(onehot_sum.astype(BF16), before, preferred_element_type=F32) + run_ref[...]

    total = pick_scores[0]
    for k in range(1, TOP_K):
        total += pick_scores[k]
    for k in range(TOP_K):
        idx_ref[k:k + 1, :] = picks[k].astype(I32)
        wt_ref[k:k + 1, :] = pick_scores[k] / total * ROUTED_SCALE
        rk = jnp.sum(jnp.where(ei == picks[k], prior, 0.0), axis=0, keepdims=True)
        rank_ref[k:k + 1, :] = rk.astype(I32)

    run_ref[...] += jnp.sum(onehot_sum, axis=1, keepdims=True)
    cnt_ref[...] = run_ref[...].astype(I32)


def _route(h2, router_wt_b, router_bias):
    T, D = h2.shape
    tr = TM_ROUTE
    E = N_EXPERTS
    row_spec = pl.BlockSpec((TOP_K, tr), lambda i: (0, i))
    return pl.pallas_call(
        _router_kernel,
        grid=(T // tr,),
        in_specs=[
            pl.BlockSpec((tr, D), lambda i: (i, 0)),
            pl.BlockSpec((E, D), lambda i: (0, 0)),
            pl.BlockSpec((E, 1), lambda i: (0, 0)),
        ],
        out_specs=[row_spec, row_spec, row_spec, pl.BlockSpec((E, 1), lambda i: (0, 0))],
        out_shape=[
            jax.ShapeDtypeStruct((TOP_K, T), I32),
            jax.ShapeDtypeStruct((TOP_K, T), F32),
            jax.ShapeDtypeStruct((TOP_K, T), I32),
            jax.ShapeDtypeStruct((E, 1), I32),
        ],
        scratch_shapes=[pltpu.VMEM((E, 1), F32)],
        compiler_params=_cparams(("arbitrary",)),
        name="router",
    )(h2, router_wt_b, router_bias.reshape(E, 1))


def _dispatch_kernel(pend_ref, h_ref, dest_ref, xg_ref, dest_s, zero_ref, sem_s, sem):
    i = pl.program_id(0)
    td = TM_DISPATCH
    blk = EXPERT_BLOCK

    def zero_copy(e):
        start = pl.multiple_of(pend_ref[e + 1] - blk, blk)
        return pltpu.make_async_copy(zero_ref, xg_ref.at[pl.ds(start, blk)], sem)

    @pl.when(i == 0)
    def _():
        zero_ref[...] = jnp.zeros_like(zero_ref)

        def start_zero(e, c):
            @pl.when(pend_ref[e + 1] > pend_ref[e])
            def _():
                zero_copy(e).start()
            return c

        def wait_zero(e, c):
            @pl.when(pend_ref[e + 1] > pend_ref[e])
            def _():
                zero_copy(e).wait()
            return c

        lax.fori_loop(0, N_EXPERTS, start_zero, 0)
        lax.fori_loop(0, N_EXPERTS, wait_zero, 0)

        def tail_copy(b):
            return pltpu.make_async_copy(zero_ref, xg_ref.at[pl.ds(pl.multiple_of(b * blk, blk), blk)], sem)

        first_unused = pend_ref[N_EXPERTS] // blk
        n_blocks = xg_ref.shape[0] // blk
        lax.fori_loop(first_unused, n_blocks, lambda b, c: (tail_copy(b).start(), c)[1], 0)
        lax.fori_loop(first_unused, n_blocks, lambda b, c: (tail_copy(b).wait(), c)[1], 0)

    cp = pltpu.make_async_copy(dest_ref, dest_s, sem_s)
    cp.start()
    cp.wait()

    def row_copy(t, k):
        return pltpu.make_async_copy(h_ref.at[t], xg_ref.at[dest_s[k, t]], sem)

    def start_rows(t8, c):
        base = pl.multiple_of(t8 * 8, 8)
        for j in range(8):
            for k in range(TOP_K):
                row_copy(base + j, k).start(priority=k % DMA_QUEUES)
        return c

    lax.fori_loop(0, td // 8, start_rows, 0)
    for k in range(TOP_K):
        pltpu.make_async_copy(h_ref, h_ref, sem).wait()


def _dispatch(h2p, dest, pend0, rows):
    T = h2p.shape[0]
    td = TM_DISPATCH
    tile = (ROW_TILE, LANES)
    return pl.pallas_call(
        _dispatch_kernel,
        grid_spec=pltpu.PrefetchScalarGridSpec(
            num_scalar_prefetch=1,
            grid=(T // td,),
            in_specs=[
                pl.BlockSpec((td,) + tile, lambda i, p: (i, 0, 0)),
                pl.BlockSpec((TOP_K, td), lambda i, p: (0, i)),
            ],
            out_specs=pl.BlockSpec(memory_space=pl.ANY),
            scratch_shapes=[
                pltpu.SMEM((TOP_K, td), I32),
                pltpu.VMEM((EXPERT_BLOCK,) + tile, U32),
                pltpu.SemaphoreType.DMA,
                pltpu.SemaphoreType.DMA,
            ],
        ),
        out_shape=jax.ShapeDtypeStruct((rows,) + tile, U32),
        compiler_params=_cparams(("arbitrary",)),
        name="dispatch",
    )(pend0, h2p, dest)


def _experts_kernel(first_ref, gidx_ref, used_ref, nused_ref, ngroups_ref,
                    x_ref, wg_hbm, wu_hbm, wd_hbm, y_ref,
                    wgu_f, wd_f, wgu_b, wd_b, issued, sem):
    i = pl.program_id(0)
    F = D_EXPERT

    def tensor_copy(e, s, t):
        if t == 0:
            return pltpu.make_async_copy(wg_hbm.at[e], wgu_f.at[s, 0], sem.at[s, 0])
        if t == 1:
            return pltpu.make_async_copy(wu_hbm.at[e], wgu_f.at[s, 1], sem.at[s, 1])
        return pltpu.make_async_copy(wd_hbm.at[e], wd_f.at[s], sem.at[s, 2])

    def issue_until(n):
        def body(q, carry):
            h = q // 3
            for t in range(3):
                @pl.when(q % 3 == t)
                def _():
                    tensor_copy(used_ref[h], h % WEIGHT_SLOTS, t).start(priority=t % DMA_QUEUES)
            return carry

        lax.fori_loop(issued[0], n, body, 0)
        issued[0] = jnp.maximum(issued[0], n)

    @pl.when(i == 0)
    def _():
        issued[0] = 0

    @pl.when(i < nused_ref[0])
    def _():
        g = gidx_ref[i]

        @pl.when(first_ref[i] == 1)
        def _():
            issue_until(3 * (g + 1))
            s = g % WEIGHT_SLOTS
            for t in range(3):
                tensor_copy(used_ref[g], s, t).wait()
            wgu_b[:, :F] = wgu_f[s, 0].astype(BF16)
            wgu_b[:, F:] = wgu_f[s, 1].astype(BF16)
            wd_b[...] = wd_f[s].astype(BF16)

        cap = 3 * jnp.minimum(g + WEIGHT_SLOTS + 1, ngroups_ref[0])
        issue_until(jnp.minimum(issued[0] + 1, cap))

        lo, hi = _unpack_halves(_load_row_tiles(x_ref, 0, EXPERT_BLOCK))
        x = jnp.concatenate([lo.astype(BF16), hi.astype(BF16)], axis=1)
        gu = jnp.dot(x, wgu_b[...], preferred_element_type=F32)
        a = (_silu(gu[:, :F]) * gu[:, F:]).astype(BF16)
        y = jnp.dot(a, wd_b[...], preferred_element_type=F32)
        _store_row_tiles(y_ref, 0, EXPERT_BLOCK, _pack_halves(y))


def _experts(xg, first, gidx, used_list, n_used, n_groups, w_gate, w_up, w_down):
    D = D_MODEL
    bm = EXPERT_BLOCK
    F = D_EXPERT
    nb = xg.shape[0] // (bm * ROW_TILE)
    row_map = lambda i, fi, gi, ul, nu, ng: (jnp.minimum(i, nu[0] - 1), 0)
    return pl.pallas_call(
        _experts_kernel,
        grid_spec=pltpu.PrefetchScalarGridSpec(
            num_scalar_prefetch=5,
            grid=(nb,),
            in_specs=[
                pl.BlockSpec((bm * ROW_TILE, LANES), row_map),
                pl.BlockSpec(memory_space=pl.ANY),
                pl.BlockSpec(memory_space=pl.ANY),
                pl.BlockSpec(memory_space=pl.ANY),
            ],
            out_specs=pl.BlockSpec((bm * ROW_TILE, LANES), row_map),
            scratch_shapes=[
                pltpu.VMEM((WEIGHT_SLOTS, 2, D, F), F32), pltpu.VMEM((WEIGHT_SLOTS, F, D), F32),
                pltpu.VMEM((D, 2 * F), BF16), pltpu.VMEM((F, D), BF16),
                pltpu.SMEM((1,), I32),
                pltpu.SemaphoreType.DMA((WEIGHT_SLOTS, 3)),
            ],
        ),
        out_shape=jax.ShapeDtypeStruct(xg.shape, U32),
        input_output_aliases={5: 0},
        compiler_params=_cparams(("arbitrary",)),
        name="experts",
    )(first, gidx, used_list, n_used, n_groups, xg, w_gate, w_up, w_down)


def _combine_kernel(dest_ref, dnext_ref, wt_ref, x1_ref, h_ref, mod_ref, sg_ref, su_ref, sd_ref, yg_ref, o_ref,
                    dest_s, dnext_s, buf_a, buf_b, sem_s, sem):
    i = pl.program_id(0)
    last = pl.num_programs(0) - 1
    tc = TM_COMBINE
    half = D_MODEL // 2

    for src, dst in ((dest_ref, dest_s), (dnext_ref, dnext_s)):
        cp = pltpu.make_async_copy(src, dst, sem_s)
        cp.start()
        cp.wait()

    def row_copy(idx_s, col, buf, which, t, k):
        return pltpu.make_async_copy(yg_ref.at[idx_s[k, col + t]],
                                     buf.at[pl.ds((k * tc + t) * ROW_TILE, ROW_TILE), :], sem.at[which])

    def start_rows(idx_s, col, buf, which):
        for t in range(tc):
            for k in range(TOP_K):
                row_copy(idx_s, col, buf, which, t, k).start(priority=k % DMA_QUEUES)

    def wait_rows(buf, which):
        pltpu.make_async_copy(buf, buf, sem.at[which]).wait()

    def finish_tile(buf, r0):
        rows = slice(r0, r0 + tc)
        hb = h_ref[rows, :]
        g = jnp.dot(hb, sg_ref[...], preferred_element_type=F32)
        u = jnp.dot(hb, su_ref[...], preferred_element_type=F32)
        ffn = jnp.dot((_silu(g) * u).astype(BF16), sd_ref[...], preferred_element_type=F32)
        ffn_lo, ffn_hi = ffn[:, :half], ffn[:, half:]
        for k in range(TOP_K):
            lo, hi = _unpack_halves(_load_row_tiles(buf, k * tc * ROW_TILE, tc))
            w = wt_ref[rows, k:k + 1]
            ffn_lo += lo * w
            ffn_hi += hi * w
        o_ref[rows, :half] = x1_ref[rows, :half] + mod_ref[0, 5:6, :half] * ffn_lo
        o_ref[rows, half:] = x1_ref[rows, half:] + mod_ref[0, 5:6, half:] * ffn_hi

    @pl.when(i == 0)
    def _():
        def body(t, c):
            for k in range(TOP_K):
                row_copy(dest_s, 0, buf_a, 0, t, k).start(priority=k % DMA_QUEUES)
            return c

        lax.fori_loop(0, tc, body, 0)

    wait_rows(buf_a, 0)
    start_rows(dest_s, tc, buf_b, 1)
    finish_tile(buf_a, 0)
    wait_rows(buf_b, 1)
    start_rows(dnext_s, 0, buf_a, 0)
    finish_tile(buf_b, tc)

    @pl.when(i == last)
    def _():
        wait_rows(buf_a, 0)


def _combine(dest, wt_t, x1, h2, mod3, sg_b, su_b, sd_b, yg):
    T, D = x1.shape
    tc = TM_COMBINE
    step = 2 * tc
    per_batch = SEQ // step
    F = D_EXPERT
    return pl.pallas_call(
        _combine_kernel,
        grid=(T // step,),
        in_specs=[
            pl.BlockSpec((TOP_K, step), lambda i: (0, i)),
            pl.BlockSpec((TOP_K, step), lambda i: (0, jnp.minimum(i + 1, T // step - 1))),
            pl.BlockSpec((step, TOP_K), lambda i: (i, 0)),
            pl.BlockSpec((step, D), lambda i: (i, 0)),
            pl.BlockSpec((step, D), lambda i: (i, 0)),
            pl.BlockSpec((1, 6, D), lambda i: (i // per_batch, 0, 0)),
            pl.BlockSpec((D, F), lambda i: (0, 0)),
            pl.BlockSpec((D, F), lambda i: (0, 0)),
            pl.BlockSpec((F, D), lambda i: (0, 0)),
            pl.BlockSpec(memory_space=pl.ANY),
        ],
        out_specs=pl.BlockSpec((step, D), lambda i: (i, 0)),
        out_shape=jax.ShapeDtypeStruct((T, D), F32),
        scratch_shapes=[
            pltpu.SMEM((TOP_K, step), I32),
            pltpu.SMEM((TOP_K, step), I32),
            pltpu.VMEM((TOP_K * tc * ROW_TILE, LANES), U32),
            pltpu.VMEM((TOP_K * tc * ROW_TILE, LANES), U32),
            pltpu.SemaphoreType.DMA,
            pltpu.SemaphoreType.DMA((2,)),
        ],
        compiler_params=_cparams(("arbitrary",)),
        name="combine",
    )(dest, dest, wt_t, x1, h2, mod3, sg_b, su_b, sd_b, yg)


def kernel(x, c, ada_w, ada_b, mix_norm_g, ffn_norm_g, w_in, q_norm_g, k_norm_g, rel_bias, sgu_ln_g, sgu_ln_b, sgu_w, sgu_b, w_out, router_w, router_bias, shared_w_gate, shared_w_up, shared_w_down, expert_w_gate, expert_w_up, expert_w_down):
    B, S, D = x.shape
    assert S == SEQ and D == D_MODEL and ada_w.shape[0] == 1
    T = B * S
    x2 = x.reshape(T, D)

    mod3 = _modulation(c, ada_w[0], ada_b[0]).reshape(B, 6, D)

    proj = _in_projection(x2, mod3, mix_norm_g, w_in[0].astype(BF16), q_norm_g, k_norm_g)
    attn = _attention(proj.reshape(B, S, D_IN_PROJ), _bias_tables(rel_bias)).reshape(T, D_ATTN)
    causal = jnp.tril(jnp.ones((SGU_CHUNK, SGU_CHUNK), F32))
    sgu = _spatial_gating(proj, (sgu_w[0] * causal).astype(BF16), sgu_ln_g, sgu_ln_b, sgu_b[0].T)
    x1, h2, h2p = _out_projection(attn, sgu, x2, mod3, ffn_norm_g, w_out[0].astype(BF16))

    idx, wts, rank, counts = _route(h2, router_w[0].T.astype(BF16), router_bias[0])

    bm = EXPERT_BLOCK
    counts = counts.reshape(N_EXPERTS)
    padded = (counts + bm - 1) // bm * bm
    pends = jnp.cumsum(padded)
    pstarts = pends - padded
    n_blocks = T * TOP_K // bm + N_EXPERTS
    block_start = jnp.arange(n_blocks, dtype=I32) * bm
    block_exp = jnp.minimum(jnp.sum((pends[None, :] <= block_start[:, None]).astype(I32), axis=1), N_EXPERTS - 1)
    n_used = (pends[-1] // bm).astype(I32).reshape(1)
    eids = jnp.arange(N_EXPERTS, dtype=I32)
    dest = jnp.sum(jnp.where(idx[:, :, None] == eids, pstarts.astype(I32), 0), axis=-1) + rank
    pend0 = jnp.concatenate([jnp.zeros((1,), I32), pends.astype(I32)])
    first = jnp.concatenate([jnp.ones((1,), I32), (block_exp[1:] != block_exp[:-1]).astype(I32)])
    used = counts > 0
    ordinal = jnp.cumsum(used.astype(I32)) - 1
    n_groups = jnp.sum(used.astype(I32)).reshape(1)
    used_list = jnp.sum(jnp.where(used[None, :] & (ordinal[None, :] == eids[:, None]), eids[None, :], 0), axis=-1)
    gidx = jnp.sum(jnp.where(block_exp[:, None] == eids, ordinal, 0), axis=-1)

    n_rows = n_blocks * bm
    xg = _dispatch(h2p.reshape(T, ROW_TILE, LANES), dest, pend0, n_rows)
    yg = _experts(xg.reshape(n_rows * ROW_TILE, LANES), first, gidx.astype(I32), used_list.astype(I32),
                  n_used, n_groups, expert_w_gate[0], expert_w_up[0], expert_w_down[0])
    out = _combine(dest, wts.T, x1, h2, mod3, shared_w_gate[0].astype(BF16), shared_w_up[0].astype(BF16),
                   shared_w_down[0].astype(BF16), yg.reshape(n_rows, ROW_TILE, LANES))
    return out.reshape(B, S, D)
```

```python
import math

import jax
import jax.numpy as jnp
from jax import lax
from jax.experimental import pallas as pl
from jax.experimental.pallas import tpu as pltpu

F32 = jnp.float32
BF16 = jnp.bfloat16
I32 = jnp.int32
U32 = jnp.uint32

D_MODEL = 2048
SEQ = 2048
HEAD_DIM = 128
N_HEADS = 8
D_ATTN = N_HEADS * HEAD_DIM
D_SGU = 1024
N_SGU_GROUPS = 8
SGU_CHUNK = 128
D_IN_PROJ = 3 * D_ATTN + 2 * D_SGU
DILATED_GROUPS = ((128, 1), (512, 4), (2048, 16))
BAND = 128
NUM_REL_BUCKETS = 32
REL_MAX_DISTANCE = 2048
N_EXPERTS = 256
TOP_K = 8
N_EXPERT_GROUPS = 8
GROUP_SIZE = N_EXPERTS // N_EXPERT_GROUPS
TOPK_GROUPS = 4
D_EXPERT = 512
ROUTED_SCALE = 2.5
NORM_EPS = 1e-6
MASK_VALUE = -1e30

LANES = 128
ROW_TILE = D_MODEL // 2 // LANES
VMEM_LIMIT = 56 * 1024 * 1024
DMA_QUEUES = 2

TM_PROJ = 256
TM_SGU = 512
TM_OUT = 256
TM_ROUTE = 512
TM_DISPATCH = 256
TN_MOD = 1024
TM_COMBINE = 128
EXPERT_BLOCK = 128
WEIGHT_SLOTS = 2
ATTN_PIECES_IN_FLIGHT = 6


def _cparams(sem):
    return pltpu.CompilerParams(dimension_semantics=sem, vmem_limit_bytes=VMEM_LIMIT)


def _silu(v):
    return v * jax.nn.sigmoid(v)


def _gelu(v):
    return 0.5 * v * (1.0 + lax.erf(v * (1.0 / math.sqrt(2.0))))


def _pack_halves(v):
    h = v.shape[1] // 2
    lo = lax.bitcast_convert_type(v[:, :h].astype(BF16).astype(F32), U32)
    hi = lax.bitcast_convert_type(v[:, h:].astype(BF16).astype(F32), U32)
    return (hi & jnp.uint32(0xFFFF0000)) | (lo >> 16)


def _unpack_halves(w):
    lo = lax.bitcast_convert_type(w << 16, F32)
    hi = lax.bitcast_convert_type(w & jnp.uint32(0xFFFF0000), F32)
    return lo, hi


def _store_row_tiles(ref, base, rows, packed):
    for s in range(ROW_TILE):
        ref[pl.ds(base + s, rows, stride=ROW_TILE), :] = packed[:, s * LANES:(s + 1) * LANES]


def _load_row_tiles(ref, base, rows):
    return jnp.concatenate([ref[pl.ds(base + s, rows, stride=ROW_TILE), :] for s in range(ROW_TILE)], axis=1)


def _mod_kernel(c_ref, w_ref, b_ref, o_ref):
    ca = _silu(c_ref[...]).astype(BF16)
    o_ref[...] = jnp.dot(ca, w_ref[...].astype(BF16), preferred_element_type=F32) + b_ref[...]


def _modulation(c, ada_w, ada_b):
    B, D = c.shape
    N = ada_w.shape[1]
    tn = TN_MOD
    return pl.pallas_call(
        _mod_kernel,
        grid=(N // tn,),
        in_specs=[
            pl.BlockSpec((B, D), lambda j: (0, 0)),
            pl.BlockSpec((D, tn), lambda j: (0, j)),
            pl.BlockSpec((1, tn), lambda j: (0, j)),
        ],
        out_specs=pl.BlockSpec((B, tn), lambda j: (0, j)),
        out_shape=jax.ShapeDtypeStruct((B, N), F32),
        compiler_params=_cparams(("arbitrary",)),
        name="modulation",
    )(c, ada_w, ada_b.reshape(1, N))


def _inproj_kernel(x_ref, mod_ref, g_ref, w_ref, qg_ref, kg_ref, o_ref):
    x = x_ref[...]
    r = lax.rsqrt(jnp.mean(x * x, axis=-1, keepdims=True) + NORM_EPS)
    h = ((x * r) * g_ref[...] * (1.0 + mod_ref[0, 1:2, :]) + mod_ref[0, 0:1, :]).astype(BF16)

    def head_norm(acc, col0, gain_ref, scale):
        for hd in range(N_HEADS):
            a = acc[:, hd * HEAD_DIM:(hd + 1) * HEAD_DIM]
            r = lax.rsqrt(jnp.mean(a * a, axis=-1, keepdims=True) + NORM_EPS)
            cols = slice(col0 + hd * HEAD_DIM, col0 + (hd + 1) * HEAD_DIM)
            o_ref[:, cols] = ((a * r) * gain_ref[...] * scale).astype(BF16)

    def epilogue(j, acc):
        col0 = j * D_ATTN
        if j == 0:
            head_norm(acc, col0, qg_ref, HEAD_DIM ** -0.5)
        elif j == 1:
            head_norm(acc, col0, kg_ref, 1.0)
        else:
            o_ref[:, col0:col0 + D_ATTN] = acc.astype(BF16)

    pending = None
    for j in range(D_IN_PROJ // D_ATTN):
        acc = jnp.dot(h, w_ref[:, j * D_ATTN:(j + 1) * D_ATTN], preferred_element_type=F32)
        if pending is not None:
            epilogue(*pending)
        pending = (j, acc)
    epilogue(*pending)


def _in_projection(x2, mod3, norm_g, w_in_b, q_g, k_g):
    T, D = x2.shape
    tm = TM_PROJ
    per_batch = SEQ // tm
    return pl.pallas_call(
        _inproj_kernel,
        grid=(T // tm,),
        in_specs=[
            pl.BlockSpec((tm, D), lambda i: (i, 0)),
            pl.BlockSpec((1, 6, D), lambda i: (i // per_batch, 0, 0)),
            pl.BlockSpec((1, D), lambda i: (0, 0)),
            pl.BlockSpec((D, D_IN_PROJ), lambda i: (0, 0), pipeline_mode=pl.Buffered(1)),
            pl.BlockSpec((1, HEAD_DIM), lambda i: (0, 0)),
            pl.BlockSpec((1, HEAD_DIM), lambda i: (0, 0)),
        ],
        out_specs=pl.BlockSpec((tm, D_IN_PROJ), lambda i: (i, 0)),
        out_shape=jax.ShapeDtypeStruct((T, D_IN_PROJ), BF16),
        compiler_params=_cparams(("arbitrary",)),
        name="in_projection",
    )(x2, mod3, norm_g, w_in_b, q_g, k_g)


def _bias_tables(rel_bias):
    n = BAND
    qi = jnp.arange(n)[:, None]
    ki = jnp.arange(2 * n)[None, :]
    steps = n + qi - ki
    in_band = (steps >= 0) & (steps <= n)
    max_exact = NUM_REL_BUCKETS // 2
    tabs = []
    for _, dilation in DILATED_GROUPS:
        dist = jnp.clip(steps, 0, n) * dilation
        nf = jnp.maximum(dist, 1).astype(F32)
        large = max_exact + (jnp.log(nf / max_exact) / math.log(REL_MAX_DISTANCE / max_exact)
                             * (NUM_REL_BUCKETS - max_exact)).astype(I32)
        large = jnp.minimum(large, NUM_REL_BUCKETS - 1)
        bucket = jnp.where(dist < max_exact, dist, large)
        onehot = jax.nn.one_hot(bucket, NUM_REL_BUCKETS, dtype=F32)
        b = jnp.einsum("qkb,bh->hqk", onehot, rel_bias.astype(F32), precision=lax.Precision.HIGHEST)
        tabs.append(jnp.where(in_band[None], b, MASK_VALUE))
    return jnp.stack(tabs)


def _attn_kernel(q_ref, k_ref, v_ref, bias_ref, o_ref, qf, kf, vf, q4, k4, v4,
                 o0, l0, o1, l1, o2, l2, stage):
    n = BAND
    quarter = SEQ // 4

    for src, nat, res in ((q_ref, qf, q4), (k_ref, kf, k4), (v_ref, vf, v4)):
        nat[...] = src[...].astype(F32)
        for r in range(4):
            res[r * quarter:(r + 1) * quarter, :] = nat[pl.ds(r, quarter, stride=4), :]

    def logits(qb, kb, bias):
        return lax.dot_general(qb, kb, (((1,), (1,)), ((), ())), preferred_element_type=F32) + bias

    def finish(l, vb, o_out, l_out, rows):
        m = jnp.max(l, axis=-1, keepdims=True)
        p = jnp.exp(l - m)
        s = jnp.sum(p, axis=-1, keepdims=True)
        o = jnp.dot(p.astype(BF16), vb, preferred_element_type=F32)
        o_out[rows, :] = o / s
        l_out[rows, :] = jnp.broadcast_to(m + jnp.log(s), (n, LANES))

    pieces = []

    for i in range(SEQ // n):
        rows = slice(i * n, (i + 1) * n)
        krows = rows if i == 0 else slice((i - 1) * n, (i + 1) * n)
        bias = (0, slice(None), slice(n, None)) if i == 0 else (0,)
        pieces.append((q_ref, k_ref, v_ref, rows, krows, bias, o0, l0))

    for r in range(4):
        for blk in range(quarter // n):
            base = r * quarter + blk * n
            rows = slice(base, base + n)
            krows = rows if blk == 0 else slice(base - n, base + n)
            bias = (1, slice(None), slice(n, None)) if blk == 0 else (1,)
            pieces.append((q4, k4, v4, rows, krows, bias, o1, l1))

    for r in range(4):
        for a in range(4):
            rows = pl.ds(r * quarter + a, n, stride=4)
            pieces.append((q4, k4, v4, rows, rows, (2, slice(None), slice(n, None)), o2, l2))

    pending = []
    for qs, ks, vs, rows, krows, bias, o_out, l_out in pieces:
        l = logits(qs[rows, :].astype(BF16), ks[krows, :].astype(BF16), bias_ref[bias])
        if len(pending) == ATTN_PIECES_IN_FLIGHT:
            finish(*pending.pop(0))
        pending.append((l, vs[krows, :].astype(BF16), o_out, l_out, rows))
    for args in pending:
        finish(*args)

    for r in range(4):
        for c in range(quarter // n):
            rows = slice(r * quarter + c * n, r * quarter + (c + 1) * n)
            nat = pl.ds(r + 4 * c * n, n, stride=4)
            a0, a1, a2 = l0[nat, :], l1[rows, :], l2[rows, :]
            m = jnp.maximum(jnp.maximum(a0, a1), a2)
            e0, e1, e2 = jnp.exp(a0 - m), jnp.exp(a1 - m), jnp.exp(a2 - m)
            mix = (e0 * o0[nat, :] + e1 * o1[rows, :] + e2 * o2[rows, :]) / (e0 + e1 + e2)
            stage[nat, :] = mix
    o_ref[...] = stage[...].astype(BF16)


def _attention(proj3, bias_tabs):
    B = proj3.shape[0]
    blk = lambda off: pl.BlockSpec((None, SEQ, HEAD_DIM), lambda b, h: (b, 0, off + h))
    return pl.pallas_call(
        _attn_kernel,
        grid=(B, N_HEADS),
        in_specs=[
            blk(0), blk(N_HEADS), blk(2 * N_HEADS),
            pl.BlockSpec((3, None, BAND, 2 * BAND), lambda b, h: (0, h, 0, 0)),
        ],
        out_specs=pl.BlockSpec((None, SEQ, HEAD_DIM), lambda b, h: (b, 0, h)),
        out_shape=jax.ShapeDtypeStruct((B, SEQ, D_ATTN), BF16),
        scratch_shapes=[pltpu.VMEM((SEQ, HEAD_DIM), F32) for _ in range(13)],
        compiler_params=_cparams(("arbitrary", "arbitrary")),
        name="dilated_attention",
    )(proj3, proj3, proj3, bias_tabs)


def _sgu_kernel(u_ref, z_ref, w_ref, g_ref, b_ref, bs_ref, o_ref):
    n = SGU_CHUNK
    for c in range(TM_SGU // n):
        rs = slice(c * n, (c + 1) * n)
        for g in range(N_SGU_GROUPS):
            cs = slice(g * n, (g + 1) * n)
            z = _gelu(z_ref[rs, cs].astype(F32))
            mu = jnp.mean(z, axis=-1, keepdims=True)
            var = jnp.maximum(jnp.mean(z * z, axis=-1, keepdims=True) - mu * mu, 0.0)
            zn = ((z - mu) * lax.rsqrt(var + NORM_EPS)) * g_ref[:, cs] + b_ref[:, cs]
            mixed = jnp.dot(w_ref[g], zn.astype(BF16), preferred_element_type=F32) + bs_ref[:, g:g + 1]
            u = _gelu(u_ref[rs, cs].astype(F32))
            o_ref[rs, cs] = (u * mixed).astype(BF16)


def _spatial_gating(proj, w_causal_b, ln_g, ln_b, bs_t):
    T = proj.shape[0]
    tm = TM_SGU
    ucol = 3 * D_ATTN // D_SGU
    return pl.pallas_call(
        _sgu_kernel,
        grid=(T // tm,),
        in_specs=[
            pl.BlockSpec((tm, D_SGU), lambda i: (i, ucol)),
            pl.BlockSpec((tm, D_SGU), lambda i: (i, ucol + 1)),
            pl.BlockSpec((N_SGU_GROUPS, SGU_CHUNK, SGU_CHUNK), lambda i: (0, 0, 0)),
            pl.BlockSpec((1, D_SGU), lambda i: (0, 0)),
            pl.BlockSpec((1, D_SGU), lambda i: (0, 0)),
            pl.BlockSpec((SGU_CHUNK, N_SGU_GROUPS), lambda i: (0, 0)),
        ],
        out_specs=pl.BlockSpec((tm, D_SGU), lambda i: (i, 0)),
        out_shape=jax.ShapeDtypeStruct((T, D_SGU), BF16),
        compiler_params=_cparams(("arbitrary",)),
        name="spatial_gating",
    )(proj, proj, w_causal_b, ln_g, ln_b, bs_t)


def _outproj_kernel(a_ref, s_ref, x_ref, mod_ref, g_ref, w_ref, x1_ref, h2_ref, h2p_ref):
    mixed = jnp.dot(a_ref[...], w_ref[0:D_ATTN, :], preferred_element_type=F32)
    mixed += jnp.dot(s_ref[...], w_ref[D_ATTN:, :], preferred_element_type=F32)
    x1 = x_ref[...] + mod_ref[0, 2:3, :] * mixed
    x1_ref[...] = x1
    r = lax.rsqrt(jnp.mean(x1 * x1, axis=-1, keepdims=True) + NORM_EPS)
    h2 = (x1 * r) * g_ref[...] * (1.0 + mod_ref[0, 4:5, :]) + mod_ref[0, 3:4, :]
    h2_ref[...] = h2.astype(BF16)
    _store_row_tiles(h2p_ref, 0, TM_OUT, _pack_halves(h2))


def _out_projection(attn, sgu, x2, mod3, norm_g, w_out_b):
    T, D = x2.shape
    tm = TM_OUT
    per_batch = SEQ // tm
    return pl.pallas_call(
        _outproj_kernel,
        grid=(T // tm,),
        in_specs=[
            pl.BlockSpec((tm, D_ATTN), lambda i: (i, 0)),
            pl.BlockSpec((tm, D_SGU), lambda i: (i, 0)),
            pl.BlockSpec((tm, D), lambda i: (i, 0)),
            pl.BlockSpec((1, 6, D), lambda i: (i // per_batch, 0, 0)),
            pl.BlockSpec((1, D), lambda i: (0, 0)),
            pl.BlockSpec((D_ATTN + D_SGU, D), lambda i: (0, 0), pipeline_mode=pl.Buffered(1)),
        ],
        out_specs=[pl.BlockSpec((tm, D), lambda i: (i, 0)), pl.BlockSpec((tm, D), lambda i: (i, 0)),
                   pl.BlockSpec((tm * ROW_TILE, LANES), lambda i: (i, 0))],
        out_shape=[jax.ShapeDtypeStruct((T, D), F32), jax.ShapeDtypeStruct((T, D), BF16),
                   jax.ShapeDtypeStruct((T * ROW_TILE, LANES), U32)],
        compiler_params=_cparams(("arbitrary",)),
        name="out_projection",
    )(attn, sgu, x2, mod3, norm_g, w_out_b)


def _router_kernel(h_ref, w_ref, b_ref, idx_ref, wt_ref, rank_ref, cnt_ref, run_ref):
    i = pl.program_id(0)
    tr = TM_ROUTE
    E = N_EXPERTS

    @pl.when(i == 0)
    def _():
        run_ref[...] = jnp.zeros_like(run_ref)

    logits = lax.dot_general(w_ref[...], h_ref[...], (((1,), (1,)), ((), ())), preferred_element_type=F32)
    scores = jax.nn.sigmoid(logits)
    sel = scores + b_ref[...]

    slabs = [sel[g * GROUP_SIZE:(g + 1) * GROUP_SIZE, :] for g in range(N_EXPERT_GROUPS)]
    si = lax.broadcasted_iota(I32, (GROUP_SIZE, tr), 0).astype(F32)
    gs = []
    for slab in slabs:
        m1 = jnp.max(slab, axis=0, keepdims=True)
        first = jnp.min(jnp.where(slab == m1, si, float(GROUP_SIZE)), axis=0, keepdims=True)
        m2 = jnp.max(jnp.where(si == first, -jnp.inf, slab), axis=0, keepdims=True)
        gs.append(m1 + m2)

    kept = []
    for g in range(N_EXPERT_GROUPS):
        beaten = jnp.zeros((1, tr), F32)
        for o in range(N_EXPERT_GROUPS):
            if o < g:
                beaten += (gs[o] >= gs[g]).astype(F32)
            elif o > g:
                beaten += (gs[o] > gs[g]).astype(F32)
        kept.append(jnp.where(beaten < TOPK_GROUPS, slabs[g], -jnp.inf))
    masked = jnp.concatenate(kept, axis=0)

    ei = lax.broadcasted_iota(I32, (E, tr), 0).astype(F32)
    picks, pick_scores = [], []
    onehot_sum = jnp.zeros((E, tr), F32)
    for k in range(TOP_K):
        m = jnp.max(masked, axis=0, keepdims=True)
        ik = jnp.min(jnp.where(masked == m, ei, float(E)), axis=0, keepdims=True)
        oh = ei == ik
        pick_scores.append(jnp.sum(jnp.where(oh, scores, 0.0), axis=0, keepdims=True))
        masked = jnp.where(oh, -jnp.inf, masked)
        onehot_sum += oh.astype(F32)
        picks.append(ik)

    ti = lax.broadcasted_iota(I32, (tr, tr), 0)
    tj = lax.broadcasted_iota(I32, (tr, tr), 1)
    before = (ti < tj).astype(BF16)
    prior = jnp.dot(onehot_sum.astype(BF16), before, preferred_element_type=F32) + run_ref[...]

    total = pick_scores[0]
    for k in range(1, TOP_K):
        total += pick_scores[k]
    for k in range(TOP_K):
        idx_ref[k:k + 1, :] = picks[k].astype(I32)
        wt_ref[k:k + 1, :] = pick_scores[k] / total * ROUTED_SCALE
        rk = jnp.sum(jnp.where(ei == picks[k], prior, 0.0), axis=0, keepdims=True)
        rank_ref[k:k + 1, :] = rk.astype(I32)

    run_ref[...] += jnp.sum(onehot_sum, axis=1, keepdims=True)
    cnt_ref[...] = run_ref[...].astype(I32)


def _route(h2, router_wt_b, router_bias):
    T, D = h2.shape
    tr = TM_ROUTE
    E = N_EXPERTS
    row_spec = pl.BlockSpec((TOP_K, tr), lambda i: (0, i))
    return pl.pallas_call(
        _router_kernel,
        grid=(T // tr,),
        in_specs=[
            pl.BlockSpec((tr, D), lambda i: (i, 0)),
            pl.BlockSpec((E, D), lambda i: (0, 0)),
            pl.BlockSpec((E, 1), lambda i: (0, 0)),
        ],
        out_specs=[row_spec, row_spec, row_spec, pl.BlockSpec((E, 1), lambda i: (0, 0))],
        out_shape=[
            jax.ShapeDtypeStruct((TOP_K, T), I32),
            jax.ShapeDtypeStruct((TOP_K, T), F32),
            jax.ShapeDtypeStruct((TOP_K, T), I32),
            jax.ShapeDtypeStruct((E, 1), I32),
        ],
        scratch_shapes=[pltpu.VMEM((E, 1), F32)],
        compiler_params=_cparams(("arbitrary",)),
        name="router",
    )(h2, router_wt_b, router_bias.reshape(E, 1))


def _dispatch_kernel(pend_ref, h_ref, dest_ref, dnext_ref, xg_ref, dest_s, zero_ref, sem_s, sem):
    i = pl.program_id(0)
    td = TM_DISPATCH
    blk = EXPERT_BLOCK
    slot = i % 2

    def index_copy(src_ref, s):
        return pltpu.make_async_copy(src_ref, dest_s.at[s], sem_s.at[s])

    def zero_copy(e):
        start = pl.multiple_of(pend_ref[e + 1] - blk, blk)
        return pltpu.make_async_copy(zero_ref, xg_ref.at[pl.ds(start, blk)], sem)

    @pl.when(i == 0)
    def _():
        index_copy(dest_ref, 0).start()
        zero_ref[...] = jnp.zeros_like(zero_ref)

        def start_zero(e, c):
            @pl.when(pend_ref[e + 1] > pend_ref[e])
            def _():
                zero_copy(e).start()
            return c

        def wait_zero(e, c):
            @pl.when(pend_ref[e + 1] > pend_ref[e])
            def _():
                zero_copy(e).wait()
            return c

        lax.fori_loop(0, N_EXPERTS, start_zero, 0)
        lax.fori_loop(0, N_EXPERTS, wait_zero, 0)

        def tail_copy(b):
            return pltpu.make_async_copy(zero_ref, xg_ref.at[pl.ds(pl.multiple_of(b * blk, blk), blk)], sem)

        first_unused = pend_ref[N_EXPERTS] // blk
        n_blocks = xg_ref.shape[0] // blk
        lax.fori_loop(first_unused, n_blocks, lambda b, c: (tail_copy(b).start(), c)[1], 0)
        lax.fori_loop(first_unused, n_blocks, lambda b, c: (tail_copy(b).wait(), c)[1], 0)
        index_copy(dest_ref, 0).wait()

    index_copy(dnext_ref, 1 - slot).start()

    def row_copy(t, k):
        return pltpu.make_async_copy(h_ref.at[t], xg_ref.at[dest_s[slot, k, t]], sem)

    def start_rows(t8, c):
        base = pl.multiple_of(t8 * 8, 8)
        for j in range(8):
            for k in range(TOP_K):
                row_copy(base + j, k).start(priority=k % DMA_QUEUES)
        return c

    lax.fori_loop(0, td // 8, start_rows, 0)
    for k in range(TOP_K):
        pltpu.make_async_copy(h_ref, h_ref, sem).wait()
    index_copy(dnext_ref, 1 - slot).wait()


def _dispatch(h2p, dest, pend0, rows):
    T = h2p.shape[0]
    td = TM_DISPATCH
    tile = (ROW_TILE, LANES)
    return pl.pallas_call(
        _dispatch_kernel,
        grid_spec=pltpu.PrefetchScalarGridSpec(
            num_scalar_prefetch=1,
            grid=(T // td,),
            in_specs=[
                pl.BlockSpec((td,) + tile, lambda i, p: (i, 0, 0)),
                pl.BlockSpec((TOP_K, td), lambda i, p: (0, i)),
                pl.BlockSpec((TOP_K, td), lambda i, p: (0, jnp.minimum(i + 1, T // td - 1))),
            ],
            out_specs=pl.BlockSpec(memory_space=pl.ANY),
            scratch_shapes=[
                pltpu.SMEM((2, TOP_K, td), I32),
                pltpu.VMEM((EXPERT_BLOCK,) + tile, U32),
                pltpu.SemaphoreType.DMA((2,)),
                pltpu.SemaphoreType.DMA,
            ],
        ),
        out_shape=jax.ShapeDtypeStruct((rows,) + tile, U32),
        compiler_params=_cparams(("arbitrary",)),
        name="dispatch",
    )(pend0, h2p, dest, dest)


def _experts_kernel(first_ref, gidx_ref, used_ref, nused_ref, ngroups_ref,
                    x_ref, wg_hbm, wu_hbm, wd_hbm, y_ref,
                    wgu_f, wd_f, wgu_b, wd_b, issued, sem):
    i = pl.program_id(0)
    F = D_EXPERT

    def tensor_copy(e, s, t):
        if t == 0:
            return pltpu.make_async_copy(wg_hbm.at[e], wgu_f.at[s, 0], sem.at[s, 0])
        if t == 1:
            return pltpu.make_async_copy(wu_hbm.at[e], wgu_f.at[s, 1], sem.at[s, 1])
        return pltpu.make_async_copy(wd_hbm.at[e], wd_f.at[s], sem.at[s, 2])

    def issue_until(n):
        def body(q, carry):
            h = q // 3
            for t in range(3):
                @pl.when(q % 3 == t)
                def _():
                    tensor_copy(used_ref[h], h % WEIGHT_SLOTS, t).start(priority=DMA_QUEUES - 1)
            return carry

        lax.fori_loop(issued[0], n, body, 0)
        issued[0] = jnp.maximum(issued[0], n)

    @pl.when(i == 0)
    def _():
        issued[0] = 0

    @pl.when(i < nused_ref[0])
    def _():
        g = gidx_ref[i]

        @pl.when(first_ref[i] == 1)
        def _():
            issue_until(3 * (g + 1))
            s = g % WEIGHT_SLOTS
            for t in range(3):
                tensor_copy(used_ref[g], s, t).wait()
            wgu_b[:, :F] = wgu_f[s, 0].astype(BF16)
            wgu_b[:, F:] = wgu_f[s, 1].astype(BF16)
            wd_b[...] = wd_f[s].astype(BF16)

        cap = 3 * jnp.minimum(g + WEIGHT_SLOTS + 1, ngroups_ref[0])
        issue_until(jnp.minimum(issued[0] + 1, cap))

        lo, hi = _unpack_halves(_load_row_tiles(x_ref, 0, EXPERT_BLOCK))
        x = jnp.concatenate([lo.astype(BF16), hi.astype(BF16)], axis=1)
        gu = jnp.dot(x, wgu_b[...], preferred_element_type=F32)
        a = (_silu(gu[:, :F]) * gu[:, F:]).astype(BF16)
        y = jnp.dot(a, wd_b[...], preferred_element_type=F32)
        _store_row_tiles(y_ref, 0, EXPERT_BLOCK, _pack_halves(y))


def _experts(xg, first, gidx, used_list, n_used, n_groups, w_gate, w_up, w_down):
    D = D_MODEL
    bm = EXPERT_BLOCK
    F = D_EXPERT
    nb = xg.shape[0] // (bm * ROW_TILE)
    row_map = lambda i, fi, gi, ul, nu, ng: (jnp.minimum(i, nu[0] - 1), 0)
    return pl.pallas_call(
        _experts_kernel,
        grid_spec=pltpu.PrefetchScalarGridSpec(
            num_scalar_prefetch=5,
            grid=(nb,),
            in_specs=[
                pl.BlockSpec((bm * ROW_TILE, LANES), row_map),
                pl.BlockSpec(memory_space=pl.ANY),
                pl.BlockSpec(memory_space=pl.ANY),
                pl.BlockSpec(memory_space=pl.ANY),
            ],
            out_specs=pl.BlockSpec((bm * ROW_TILE, LANES), row_map),
            scratch_shapes=[
                pltpu.VMEM((WEIGHT_SLOTS, 2, D, F), F32), pltpu.VMEM((WEIGHT_SLOTS, F, D), F32),
                pltpu.VMEM((D, 2 * F), BF16), pltpu.VMEM((F, D), BF16),
                pltpu.SMEM((1,), I32),
                pltpu.SemaphoreType.DMA((WEIGHT_SLOTS, 3)),
            ],
        ),
        out_shape=jax.ShapeDtypeStruct(xg.shape, U32),
        input_output_aliases={5: 0},
        compiler_params=_cparams(("arbitrary",)),
        name="experts",
    )(first, gidx, used_list, n_used, n_groups, xg, w_gate, w_up, w_down)


def _combine_kernel(dest_ref, dnext_ref, wt_ref, x1_ref, h_ref, mod_ref, sg_ref, su_ref, sd_ref, yg_ref, o_ref,
                    dest_s, buf_a, buf_b, sem_s, sem):
    i = pl.program_id(0)
    last = pl.num_programs(0) - 1
    tc = TM_COMBINE
    half = D_MODEL // 2
    slot = i % 2

    def index_copy(src_ref, s):
        return pltpu.make_async_copy(src_ref, dest_s.at[s], sem_s.at[s])

    def row_copy(s, col, buf, which, t, k):
        return pltpu.make_async_copy(yg_ref.at[dest_s[s, k, col + t]],
                                     buf.at[pl.ds((k * tc + t) * ROW_TILE, ROW_TILE), :], sem.at[which])

    def start_rows(s, col, buf, which):
        for t in range(tc):
            for k in range(TOP_K):
                row_copy(s, col, buf, which, t, k).start(priority=k % DMA_QUEUES)

    def wait_rows(buf, which):
        pltpu.make_async_copy(buf, buf, sem.at[which]).wait()

    def finish_tile(buf, r0):
        rows = slice(r0, r0 + tc)
        hb = h_ref[rows, :]
        g = jnp.dot(hb, sg_ref[...], preferred_element_type=F32)
        u = jnp.dot(hb, su_ref[...], preferred_element_type=F32)
        ffn = jnp.dot((_silu(g) * u).astype(BF16), sd_ref[...], preferred_element_type=F32)
        ffn_lo, ffn_hi = ffn[:, :half], ffn[:, half:]
        for k in range(TOP_K):
            lo, hi = _unpack_halves(_load_row_tiles(buf, k * tc * ROW_TILE, tc))
            w = wt_ref[rows, k:k + 1]
            ffn_lo += lo * w
            ffn_hi += hi * w
        o_ref[rows, :half] = x1_ref[rows, :half] + mod_ref[0, 5:6, :half] * ffn_lo
        o_ref[rows, half:] = x1_ref[rows, half:] + mod_ref[0, 5:6, half:] * ffn_hi

    @pl.when(i == 0)
    def _():
        first = index_copy(dest_ref, 0)
        first.start()
        first.wait()

        def body(t, c):
            for k in range(TOP_K):
                row_copy(0, 0, buf_a, 0, t, k).start(priority=k % DMA_QUEUES)
            return c

        lax.fori_loop(0, tc, body, 0)

    index_copy(dnext_ref, 1 - slot).start()
    wait_rows(buf_a, 0)
    start_rows(slot, tc, buf_b, 1)
    finish_tile(buf_a, 0)
    wait_rows(buf_b, 1)
    index_copy(dnext_ref, 1 - slot).wait()
    start_rows(1 - slot, 0, buf_a, 0)
    finish_tile(buf_b, tc)

    @pl.when(i == last)
    def _():
        wait_rows(buf_a, 0)


def _combine(dest, wt_t, x1, h2, mod3, sg_b, su_b, sd_b, yg):
    T, D = x1.shape
    tc = TM_COMBINE
    step = 2 * tc
    per_batch = SEQ // step
    F = D_EXPERT
    return pl.pallas_call(
        _combine_kernel,
        grid=(T // step,),
        in_specs=[
            pl.BlockSpec((TOP_K, step), lambda i: (0, i)),
            pl.BlockSpec((TOP_K, step), lambda i: (0, jnp.minimum(i + 1, T // step - 1))),
            pl.BlockSpec((step, TOP_K), lambda i: (i, 0)),
            pl.BlockSpec((step, D), lambda i: (i, 0)),
            pl.BlockSpec((step, D), lambda i: (i, 0)),
            pl.BlockSpec((1, 6, D), lambda i: (i // per_batch, 0, 0)),
            pl.BlockSpec((D, F), lambda i: (0, 0)),
            pl.BlockSpec((D, F), lambda i: (0, 0)),
            pl.BlockSpec((F, D), lambda i: (0, 0)),
            pl.BlockSpec(memory_space=pl.ANY),
        ],
        out_specs=pl.BlockSpec((step, D), lambda i: (i, 0)),
        out_shape=jax.ShapeDtypeStruct((T, D), F32),
        scratch_shapes=[
            pltpu.SMEM((2, TOP_K, step), I32),
            pltpu.VMEM((TOP_K * tc * ROW_TILE, LANES), U32),
            pltpu.VMEM((TOP_K * tc * ROW_TILE, LANES), U32),
            pltpu.SemaphoreType.DMA((2,)),
            pltpu.SemaphoreType.DMA((2,)),
        ],
        compiler_params=_cparams(("arbitrary",)),
        name="combine",
    )(dest, dest, wt_t, x1, h2, mod3, sg_b, su_b, sd_b, yg)


def kernel(x, c, ada_w, ada_b, mix_norm_g, ffn_norm_g, w_in, q_norm_g, k_norm_g, rel_bias, sgu_ln_g, sgu_ln_b, sgu_w, sgu_b, w_out, router_w, router_bias, shared_w_gate, shared_w_up, shared_w_down, expert_w_gate, expert_w_up, expert_w_down):
    B, S, D = x.shape
    assert S == SEQ and D == D_MODEL and ada_w.shape[0] == 1
    T = B * S
    x2 = x.reshape(T, D)

    mod3 = _modulation(c, ada_w[0], ada_b[0]).reshape(B, 6, D)

    proj = _in_projection(x2, mod3, mix_norm_g, w_in[0].astype(BF16), q_norm_g, k_norm_g)
    attn = _attention(proj.reshape(B, S, D_IN_PROJ), _bias_tables(rel_bias)).reshape(T, D_ATTN)
    causal = jnp.tril(jnp.ones((SGU_CHUNK, SGU_CHUNK), F32))
    sgu = _spatial_gating(proj, (sgu_w[0] * causal).astype(BF16), sgu_ln_g, sgu_ln_b, sgu_b[0].T)
    x1, h2, h2p = _out_projection(attn, sgu, x2, mod3, ffn_norm_g, w_out[0].astype(BF16))

    idx, wts, rank, counts = _route(h2, router_w[0].T.astype(BF16), router_bias[0])

    bm = EXPERT_BLOCK
    counts = counts.reshape(N_EXPERTS)
    padded = (counts + bm - 1) // bm * bm
    pends = jnp.cumsum(padded)
    pstarts = pends - padded
    n_blocks = T * TOP_K // bm + N_EXPERTS
    block_start = jnp.arange(n_blocks, dtype=I32) * bm
    block_exp = jnp.minimum(jnp.sum((pends[None, :] <= block_start[:, None]).astype(I32), axis=1), N_EXPERTS - 1)
    n_used = (pends[-1] // bm).astype(I32).reshape(1)
    eids = jnp.arange(N_EXPERTS, dtype=I32)
    dest = jnp.sum(jnp.where(idx[:, :, None] == eids, pstarts.astype(I32), 0), axis=-1) + rank
    pend0 = jnp.concatenate([jnp.zeros((1,), I32), pends.astype(I32)])
    first = jnp.concatenate([jnp.ones((1,), I32), (block_exp[1:] != block_exp[:-1]).astype(I32)])
    used = counts > 0
    ordinal = jnp.cumsum(used.astype(I32)) - 1
    n_groups = jnp.sum(used.astype(I32)).reshape(1)
    used_list = jnp.sum(jnp.where(used[None, :] & (ordinal[None, :] == eids[:, None]), eids[None, :], 0), axis=-1)
    gidx = jnp.sum(jnp.where(block_exp[:, None] == eids, ordinal, 0), axis=-1)

    n_rows = n_blocks * bm
    xg = _dispatch(h2p.reshape(T, ROW_TILE, LANES), dest, pend0, n_rows)
    yg = _experts(xg.reshape(n_rows * ROW_TILE, LANES), first, gidx.astype(I32), used_list.astype(I32),
                  n_used, n_groups, expert_w_gate[0], expert_w_up[0], expert_w_down[0])
    out = _combine(dest, wts.T, x1, h2, mod3, shared_w_gate[0].astype(BF16), shared_w_up[0].astype(BF16),
                   shared_w_down[0].astype(BF16), yg.reshape(n_rows, ROW_TILE, LANES))
    return out.reshape(B, S, D)
```

```python
import math

import jax
import jax.numpy as jnp
from jax import lax
from jax.experimental import pallas as pl
from jax.experimental.pallas import tpu as pltpu

F32 = jnp.float32
BF16 = jnp.bfloat16
I32 = jnp.int32
U32 = jnp.uint32

D_MODEL = 2048
SEQ = 2048
HEAD_DIM = 128
N_HEADS = 8
D_ATTN = N_HEADS * HEAD_DIM
D_SGU = 1024
N_SGU_GROUPS = 8
SGU_CHUNK = 128
D_IN_PROJ = 3 * D_ATTN + 2 * D_SGU
DILATED_GROUPS = ((128, 1), (512, 4), (2048, 16))
BAND = 128
NUM_REL_BUCKETS = 32
REL_MAX_DISTANCE = 2048
N_EXPERTS = 256
TOP_K = 8
N_EXPERT_GROUPS = 8
GROUP_SIZE = N_EXPERTS // N_EXPERT_GROUPS
TOPK_GROUPS = 4
D_EXPERT = 512
ROUTED_SCALE = 2.5
NORM_EPS = 1e-6
MASK_VALUE = -1e30

LANES = 128
ROW_TILE = D_MODEL // 2 // LANES
VMEM_LIMIT = 56 * 1024 * 1024
DMA_QUEUES = 2

TM_PROJ = 256
TM_SGU = 512
TM_OUT = 256
TM_ROUTE = 512
TM_DISPATCH = 256
TN_MOD = 1024
TM_COMBINE = 128
EXPERT_BLOCK = 128
ZERO_CHUNK = 32
WEIGHT_SLOTS = 2
ATTN_PIECES_IN_FLIGHT = 6


def _cparams(sem):
    return pltpu.CompilerParams(dimension_semantics=sem, vmem_limit_bytes=VMEM_LIMIT)


def _silu(v):
    return v * jax.nn.sigmoid(v)


def _gelu(v):
    return 0.5 * v * (1.0 + lax.erf(v * (1.0 / math.sqrt(2.0))))


def _pack_halves(v):
    h = v.shape[1] // 2
    lo = lax.bitcast_convert_type(v[:, :h].astype(BF16).astype(F32), U32)
    hi = lax.bitcast_convert_type(v[:, h:].astype(BF16).astype(F32), U32)
    return (hi & jnp.uint32(0xFFFF0000)) | (lo >> 16)


def _unpack_halves(w):
    lo = lax.bitcast_convert_type(w << 16, F32)
    hi = lax.bitcast_convert_type(w & jnp.uint32(0xFFFF0000), F32)
    return lo, hi


def _store_row_tiles(ref, base, rows, packed):
    for s in range(ROW_TILE):
        ref[pl.ds(base + s, rows, stride=ROW_TILE), :] = packed[:, s * LANES:(s + 1) * LANES]


def _load_row_tiles(ref, base, rows):
    return jnp.concatenate([ref[pl.ds(base + s, rows, stride=ROW_TILE), :] for s in range(ROW_TILE)], axis=1)


def _mod_kernel(c_ref, w_ref, b_ref, o_ref):
    ca = _silu(c_ref[...]).astype(BF16)
    o_ref[...] = jnp.dot(ca, w_ref[...].astype(BF16), preferred_element_type=F32) + b_ref[...]


def _modulation(c, ada_w, ada_b):
    B, D = c.shape
    N = ada_w.shape[1]
    tn = TN_MOD
    return pl.pallas_call(
        _mod_kernel,
        grid=(N // tn,),
        in_specs=[
            pl.BlockSpec((B, D), lambda j: (0, 0)),
            pl.BlockSpec((D, tn), lambda j: (0, j)),
            pl.BlockSpec((1, tn), lambda j: (0, j)),
        ],
        out_specs=pl.BlockSpec((B, tn), lambda j: (0, j)),
        out_shape=jax.ShapeDtypeStruct((B, N), F32),
        compiler_params=_cparams(("arbitrary",)),
        name="modulation",
    )(c, ada_w, ada_b.reshape(1, N))


def _inproj_kernel(x_ref, mod_ref, g_ref, w_ref, qg_ref, kg_ref, o_ref):
    x = x_ref[...]
    r = lax.rsqrt(jnp.mean(x * x, axis=-1, keepdims=True) + NORM_EPS)
    h = ((x * r) * g_ref[...] * (1.0 + mod_ref[0, 1:2, :]) + mod_ref[0, 0:1, :]).astype(BF16)

    def head_norm(acc, col0, gain_ref, scale):
        for hd in range(N_HEADS):
            a = acc[:, hd * HEAD_DIM:(hd + 1) * HEAD_DIM]
            r = lax.rsqrt(jnp.mean(a * a, axis=-1, keepdims=True) + NORM_EPS)
            cols = slice(col0 + hd * HEAD_DIM, col0 + (hd + 1) * HEAD_DIM)
            o_ref[:, cols] = ((a * r) * gain_ref[...] * scale).astype(BF16)

    def epilogue(j, acc):
        col0 = j * D_ATTN
        if j == 0:
            head_norm(acc, col0, qg_ref, HEAD_DIM ** -0.5)
        elif j == 1:
            head_norm(acc, col0, kg_ref, 1.0)
        else:
            o_ref[:, col0:col0 + D_ATTN] = acc.astype(BF16)

    pending = None
    for j in range(D_IN_PROJ // D_ATTN):
        acc = jnp.dot(h, w_ref[:, j * D_ATTN:(j + 1) * D_ATTN], preferred_element_type=F32)
        if pending is not None:
            epilogue(*pending)
        pending = (j, acc)
    epilogue(*pending)


def _in_projection(x2, mod3, norm_g, w_in_b, q_g, k_g):
    T, D = x2.shape
    tm = TM_PROJ
    per_batch = SEQ // tm
    return pl.pallas_call(
        _inproj_kernel,
        grid=(T // tm,),
        in_specs=[
            pl.BlockSpec((tm, D), lambda i: (i, 0)),
            pl.BlockSpec((1, 6, D), lambda i: (i // per_batch, 0, 0)),
            pl.BlockSpec((1, D), lambda i: (0, 0)),
            pl.BlockSpec((D, D_IN_PROJ), lambda i: (0, 0), pipeline_mode=pl.Buffered(1)),
            pl.BlockSpec((1, HEAD_DIM), lambda i: (0, 0)),
            pl.BlockSpec((1, HEAD_DIM), lambda i: (0, 0)),
        ],
        out_specs=pl.BlockSpec((tm, D_IN_PROJ), lambda i: (i, 0)),
        out_shape=jax.ShapeDtypeStruct((T, D_IN_PROJ), BF16),
        compiler_params=_cparams(("arbitrary",)),
        name="in_projection",
    )(x2, mod3, norm_g, w_in_b, q_g, k_g)


def _bias_tables(rel_bias):
    n = BAND
    qi = jnp.arange(n)[:, None]
    ki = jnp.arange(2 * n)[None, :]
    steps = n + qi - ki
    in_band = (steps >= 0) & (steps <= n)
    max_exact = NUM_REL_BUCKETS // 2
    tabs = []
    for _, dilation in DILATED_GROUPS:
        dist = jnp.clip(steps, 0, n) * dilation
        nf = jnp.maximum(dist, 1).astype(F32)
        large = max_exact + (jnp.log(nf / max_exact) / math.log(REL_MAX_DISTANCE / max_exact)
                             * (NUM_REL_BUCKETS - max_exact)).astype(I32)
        large = jnp.minimum(large, NUM_REL_BUCKETS - 1)
        bucket = jnp.where(dist < max_exact, dist, large)
        onehot = jax.nn.one_hot(bucket, NUM_REL_BUCKETS, dtype=F32)
        b = jnp.einsum("qkb,bh->hqk", onehot, rel_bias.astype(F32), precision=lax.Precision.HIGHEST)
        tabs.append(jnp.where(in_band[None], b, MASK_VALUE))
    return jnp.stack(tabs)


def _attn_kernel(q_ref, k_ref, v_ref, bias_ref, o_ref, qf, kf, vf, q4, k4, v4,
                 o0, l0, o1, l1, o2, l2, stage):
    n = BAND
    quarter = SEQ // 4

    for src, nat, res in ((q_ref, qf, q4), (k_ref, kf, k4), (v_ref, vf, v4)):
        nat[...] = src[...].astype(F32)
        for r in range(4):
            res[r * quarter:(r + 1) * quarter, :] = nat[pl.ds(r, quarter, stride=4), :]

    def logits(qb, kb, bias):
        return lax.dot_general(qb, kb, (((1,), (1,)), ((), ())), preferred_element_type=F32) + bias

    def finish(l, vb, o_out, l_out, rows):
        m = jnp.max(l, axis=-1, keepdims=True)
        p = jnp.exp(l - m)
        s = jnp.sum(p, axis=-1, keepdims=True)
        o = jnp.dot(p.astype(BF16), vb, preferred_element_type=F32)
        o_out[rows, :] = o / s
        l_out[rows, :] = jnp.broadcast_to(m + jnp.log(s), (n, LANES))

    pieces = []

    for i in range(SEQ // n):
        rows = slice(i * n, (i + 1) * n)
        krows = rows if i == 0 else slice((i - 1) * n, (i + 1) * n)
        bias = (0, slice(None), slice(n, None)) if i == 0 else (0,)
        pieces.append((q_ref, k_ref, v_ref, rows, krows, bias, o0, l0))

    for r in range(4):
        for blk in range(quarter // n):
            base = r * quarter + blk * n
            rows = slice(base, base + n)
            krows = rows if blk == 0 else slice(base - n, base + n)
            bias = (1, slice(None), slice(n, None)) if blk == 0 else (1,)
            pieces.append((q4, k4, v4, rows, krows, bias, o1, l1))

    for r in range(4):
        for a in range(4):
            rows = pl.ds(r * quarter + a, n, stride=4)
            pieces.append((q4, k4, v4, rows, rows, (2, slice(None), slice(n, None)), o2, l2))

    pending = []
    for qs, ks, vs, rows, krows, bias, o_out, l_out in pieces:
        l = logits(qs[rows, :].astype(BF16), ks[krows, :].astype(BF16), bias_ref[bias])
        if len(pending) == ATTN_PIECES_IN_FLIGHT:
            finish(*pending.pop(0))
        pending.append((l, vs[krows, :].astype(BF16), o_out, l_out, rows))
    for args in pending:
        finish(*args)

    for r in range(4):
        for c in range(quarter // n):
            rows = slice(r * quarter + c * n, r * quarter + (c + 1) * n)
            nat = pl.ds(r + 4 * c * n, n, stride=4)
            a0, a1, a2 = l0[nat, :], l1[rows, :], l2[rows, :]
            m = jnp.maximum(jnp.maximum(a0, a1), a2)
            e0, e1, e2 = jnp.exp(a0 - m), jnp.exp(a1 - m), jnp.exp(a2 - m)
            mix = (e0 * o0[nat, :] + e1 * o1[rows, :] + e2 * o2[rows, :]) / (e0 + e1 + e2)
            stage[nat, :] = mix
    o_ref[...] = stage[...].astype(BF16)


def _attention(proj3, bias_tabs):
    B = proj3.shape[0]
    blk = lambda off: pl.BlockSpec((None, SEQ, HEAD_DIM), lambda b, h: (b, 0, off + h))
    return pl.pallas_call(
        _attn_kernel,
        grid=(B, N_HEADS),
        in_specs=[
            blk(0), blk(N_HEADS), blk(2 * N_HEADS),
            pl.BlockSpec((3, None, BAND, 2 * BAND), lambda b, h: (0, h, 0, 0)),
        ],
        out_specs=pl.BlockSpec((None, SEQ, HEAD_DIM), lambda b, h: (b, 0, h)),
        out_shape=jax.ShapeDtypeStruct((B, SEQ, D_ATTN), BF16),
        scratch_shapes=[pltpu.VMEM((SEQ, HEAD_DIM), F32) for _ in range(13)],
        compiler_params=_cparams(("arbitrary", "arbitrary")),
        name="dilated_attention",
    )(proj3, proj3, proj3, bias_tabs)


def _sgu_kernel(u_ref, z_ref, w_ref, g_ref, b_ref, bs_ref, o_ref):
    n = SGU_CHUNK
    for c in range(TM_SGU // n):
        rs = slice(c * n, (c + 1) * n)
        for g in range(N_SGU_GROUPS):
            cs = slice(g * n, (g + 1) * n)
            z = _gelu(z_ref[rs, cs].astype(F32))
            mu = jnp.mean(z, axis=-1, keepdims=True)
            var = jnp.maximum(jnp.mean(z * z, axis=-1, keepdims=True) - mu * mu, 0.0)
            zn = ((z - mu) * lax.rsqrt(var + NORM_EPS)) * g_ref[:, cs] + b_ref[:, cs]
            mixed = jnp.dot(w_ref[g], zn.astype(BF16), preferred_element_type=F32) + bs_ref[:, g:g + 1]
            u = _gelu(u_ref[rs, cs].astype(F32))
            o_ref[rs, cs] = (u * mixed).astype(BF16)


def _spatial_gating(proj, w_causal_b, ln_g, ln_b, bs_t):
    T = proj.shape[0]
    tm = TM_SGU
    ucol = 3 * D_ATTN // D_SGU
    return pl.pallas_call(
        _sgu_kernel,
        grid=(T // tm,),
        in_specs=[
            pl.BlockSpec((tm, D_SGU), lambda i: (i, ucol)),
            pl.BlockSpec((tm, D_SGU), lambda i: (i, ucol + 1)),
            pl.BlockSpec((N_SGU_GROUPS, SGU_CHUNK, SGU_CHUNK), lambda i: (0, 0, 0)),
            pl.BlockSpec((1, D_SGU), lambda i: (0, 0)),
            pl.BlockSpec((1, D_SGU), lambda i: (0, 0)),
            pl.BlockSpec((SGU_CHUNK, N_SGU_GROUPS), lambda i: (0, 0)),
        ],
        out_specs=pl.BlockSpec((tm, D_SGU), lambda i: (i, 0)),
        out_shape=jax.ShapeDtypeStruct((T, D_SGU), BF16),
        compiler_params=_cparams(("arbitrary",)),
        name="spatial_gating",
    )(proj, proj, w_causal_b, ln_g, ln_b, bs_t)


def _outproj_kernel(a_ref, s_ref, x_ref, mod_ref, g_ref, w_ref, x1_ref, h2_ref, h2p_ref):
    mixed = jnp.dot(a_ref[...], w_ref[0:D_ATTN, :], preferred_element_type=F32)
    mixed += jnp.dot(s_ref[...], w_ref[D_ATTN:, :], preferred_element_type=F32)
    x1 = x_ref[...] + mod_ref[0, 2:3, :] * mixed
    x1_ref[...] = x1
    r = lax.rsqrt(jnp.mean(x1 * x1, axis=-1, keepdims=True) + NORM_EPS)
    h2 = (x1 * r) * g_ref[...] * (1.0 + mod_ref[0, 4:5, :]) + mod_ref[0, 3:4, :]
    h2_ref[...] = h2.astype(BF16)
    _store_row_tiles(h2p_ref, 0, TM_OUT, _pack_halves(h2))


def _out_projection(attn, sgu, x2, mod3, norm_g, w_out_b):
    T, D = x2.shape
    tm = TM_OUT
    per_batch = SEQ // tm
    return pl.pallas_call(
        _outproj_kernel,
        grid=(T // tm,),
        in_specs=[
            pl.BlockSpec((tm, D_ATTN), lambda i: (i, 0)),
            pl.BlockSpec((tm, D_SGU), lambda i: (i, 0)),
            pl.BlockSpec((tm, D), lambda i: (i, 0)),
            pl.BlockSpec((1, 6, D), lambda i: (i // per_batch, 0, 0)),
            pl.BlockSpec((1, D), lambda i: (0, 0)),
            pl.BlockSpec((D_ATTN + D_SGU, D), lambda i: (0, 0), pipeline_mode=pl.Buffered(1)),
        ],
        out_specs=[pl.BlockSpec((tm, D), lambda i: (i, 0)), pl.BlockSpec((tm, D), lambda i: (i, 0)),
                   pl.BlockSpec((tm * ROW_TILE, LANES), lambda i: (i, 0))],
        out_shape=[jax.ShapeDtypeStruct((T, D), F32), jax.ShapeDtypeStruct((T, D), BF16),
                   jax.ShapeDtypeStruct((T * ROW_TILE, LANES), U32)],
        compiler_params=_cparams(("arbitrary",)),
        name="out_projection",
    )(attn, sgu, x2, mod3, norm_g, w_out_b)


def _router_kernel(h_ref, w_ref, b_ref, idx_ref, wt_ref, rank_ref, cnt_ref, run_ref):
    i = pl.program_id(0)
    tr = TM_ROUTE
    E = N_EXPERTS

    @pl.when(i == 0)
    def _():
        run_ref[...] = jnp.zeros_like(run_ref)

    logits = lax.dot_general(w_ref[...], h_ref[...], (((1,), (1,)), ((), ())), preferred_element_type=F32)
    scores = jax.nn.sigmoid(logits)
    sel = scores + b_ref[...]

    slabs = [sel[g * GROUP_SIZE:(g + 1) * GROUP_SIZE, :] for g in range(N_EXPERT_GROUPS)]
    si = lax.broadcasted_iota(I32, (GROUP_SIZE, tr), 0).astype(F32)
    gs = []
    for slab in slabs:
        m1 = jnp.max(slab, axis=0, keepdims=True)
        first = jnp.min(jnp.where(slab == m1, si, float(GROUP_SIZE)), axis=0, keepdims=True)
        m2 = jnp.max(jnp.where(si == first, -jnp.inf, slab), axis=0, keepdims=True)
        gs.append(m1 + m2)

    kept = []
    for g in range(N_EXPERT_GROUPS):
        beaten = jnp.zeros((1, tr), F32)
        for o in range(N_EXPERT_GROUPS):
            if o < g:
                beaten += (gs[o] >= gs[g]).astype(F32)
            elif o > g:
                beaten += (gs[o] > gs[g]).astype(F32)
        kept.append(jnp.where(beaten < TOPK_GROUPS, slabs[g], -jnp.inf))
    masked = jnp.concatenate(kept, axis=0)

    ei = lax.broadcasted_iota(I32, (E, tr), 0).astype(F32)
    picks, pick_scores = [], []
    onehot_sum = jnp.zeros((E, tr), F32)
    for k in range(TOP_K):
        m = jnp.max(masked, axis=0, keepdims=True)
        ik = jnp.min(jnp.where(masked == m, ei, float(E)), axis=0, keepdims=True)
        oh = ei == ik
        pick_scores.append(jnp.sum(jnp.where(oh, scores, 0.0), axis=0, keepdims=True))
        masked = jnp.where(oh, -jnp.inf, masked)
        onehot_sum += oh.astype(F32)
        picks.append(ik)

    ti = lax.broadcasted_iota(I32, (tr, tr), 0)
    tj = lax.broadcasted_iota(I32, (tr, tr), 1)
    before = (ti < tj).astype(BF16)
    prior = jnp.dot(onehot_sum.astype(BF16), before, preferred_element_type=F32) + run_ref[...]

    total = pick_scores[0]
    for k in range(1, TOP_K):
        total += pick_scores[k]
    for k in range(TOP_K):
        idx_ref[k:k + 1, :] = picks[k].astype(I32)
        wt_ref[k:k + 1, :] = pick_scores[k] / total * ROUTED_SCALE
        rk = jnp.sum(jnp.where(ei == picks[k], prior, 0.0), axis=0, keepdims=True)
        rank_ref[k:k + 1, :] = rk.astype(I32)

    run_ref[...] += jnp.sum(onehot_sum, axis=1, keepdims=True)
    cnt_ref[...] = run_ref[...].astype(I32)


def _route(h2, router_wt_b, router_bias):
    T, D = h2.shape
    tr = TM_ROUTE
    E = N_EXPERTS
    row_spec = pl.BlockSpec((TOP_K, tr), lambda i: (0, i))
    return pl.pallas_call(
        _router_kernel,
        grid=(T // tr,),
        in_specs=[
            pl.BlockSpec((tr, D), lambda i: (i, 0)),
            pl.BlockSpec((E, D), lambda i: (0, 0)),
            pl.BlockSpec((E, 1), lambda i: (0, 0)),
        ],
        out_specs=[row_spec, row_spec, row_spec, pl.BlockSpec((E, 1), lambda i: (0, 0))],
        out_shape=[
            jax.ShapeDtypeStruct((TOP_K, T), I32),
            jax.ShapeDtypeStruct((TOP_K, T), F32),
            jax.ShapeDtypeStruct((TOP_K, T), I32),
            jax.ShapeDtypeStruct((E, 1), I32),
        ],
        scratch_shapes=[pltpu.VMEM((E, 1), F32)],
        compiler_params=_cparams(("arbitrary",)),
        name="router",
    )(h2, router_wt_b, router_bias.reshape(E, 1))


def _dispatch_kernel(pend_ref, vend_ref, h_ref, dest_ref, dnext_ref, xg_ref, dest_s, zero_ref, sem_s, sem):
    i = pl.program_id(0)
    td = TM_DISPATCH
    blk = EXPERT_BLOCK
    slot = i % 2

    def index_copy(src_ref, s):
        return pltpu.make_async_copy(src_ref, dest_s.at[s], sem_s.at[s])

    def zero_copy(e, c):
        start = pl.multiple_of(pend_ref[e + 1] - ZERO_CHUNK * (c + 1), ZERO_CHUNK)
        return pltpu.make_async_copy(zero_ref.at[pl.ds(0, ZERO_CHUNK)], xg_ref.at[pl.ds(start, ZERO_CHUNK)], sem)

    def each_piece(e, act):
        for c in range(blk // ZERO_CHUNK):
            @pl.when(pend_ref[e + 1] - ZERO_CHUNK * c > vend_ref[e])
            def _():
                act(zero_copy(e, c))

    @pl.when(i == 0)
    def _():
        index_copy(dest_ref, 0).start()
        zero_ref[...] = jnp.zeros_like(zero_ref)

        def start_zero(e, c):
            each_piece(e, lambda cp: cp.start())
            return c

        def wait_zero(e, c):
            each_piece(e, lambda cp: cp.wait())
            return c

        lax.fori_loop(0, N_EXPERTS, start_zero, 0)
        lax.fori_loop(0, N_EXPERTS, wait_zero, 0)

        def tail_copy(b):
            return pltpu.make_async_copy(zero_ref, xg_ref.at[pl.ds(pl.multiple_of(b * blk, blk), blk)], sem)

        first_unused = pend_ref[N_EXPERTS] // blk
        n_blocks = xg_ref.shape[0] // blk
        lax.fori_loop(first_unused, n_blocks, lambda b, c: (tail_copy(b).start(), c)[1], 0)
        lax.fori_loop(first_unused, n_blocks, lambda b, c: (tail_copy(b).wait(), c)[1], 0)
        index_copy(dest_ref, 0).wait()

    index_copy(dnext_ref, 1 - slot).start()

    def row_copy(t, k):
        return pltpu.make_async_copy(h_ref.at[t], xg_ref.at[dest_s[slot, k, t]], sem)

    def start_rows(t8, c):
        base = pl.multiple_of(t8 * 8, 8)
        for j in range(8):
            for k in range(TOP_K):
                row_copy(base + j, k).start(priority=k % DMA_QUEUES)
        return c

    lax.fori_loop(0, td // 8, start_rows, 0)
    for k in range(TOP_K):
        pltpu.make_async_copy(h_ref, h_ref, sem).wait()
    index_copy(dnext_ref, 1 - slot).wait()


def _dispatch(h2p, dest, pend0, vend, rows):
    T = h2p.shape[0]
    td = TM_DISPATCH
    tile = (ROW_TILE, LANES)
    return pl.pallas_call(
        _dispatch_kernel,
        grid_spec=pltpu.PrefetchScalarGridSpec(
            num_scalar_prefetch=2,
            grid=(T // td,),
            in_specs=[
                pl.BlockSpec((td,) + tile, lambda i, p, v: (i, 0, 0)),
                pl.BlockSpec((TOP_K, td), lambda i, p, v: (0, i)),
                pl.BlockSpec((TOP_K, td), lambda i, p, v: (0, jnp.minimum(i + 1, T // td - 1))),
            ],
            out_specs=pl.BlockSpec(memory_space=pl.ANY),
            scratch_shapes=[
                pltpu.SMEM((2, TOP_K, td), I32),
                pltpu.VMEM((EXPERT_BLOCK,) + tile, U32),
                pltpu.SemaphoreType.DMA((2,)),
                pltpu.SemaphoreType.DMA,
            ],
        ),
        out_shape=jax.ShapeDtypeStruct((rows,) + tile, U32),
        compiler_params=_cparams(("arbitrary",)),
        name="dispatch",
    )(pend0, vend, h2p, dest, dest)


def _experts_kernel(first_ref, gidx_ref, used_ref, nused_ref, ngroups_ref,
                    x_ref, wg_hbm, wu_hbm, wd_hbm, y_ref,
                    wgu_f, wd_f, wgu_b, wd_b, issued, sem):
    i = pl.program_id(0)
    F = D_EXPERT

    def tensor_copy(e, s, t):
        if t == 0:
            return pltpu.make_async_copy(wg_hbm.at[e], wgu_f.at[s, 0], sem.at[s, 0])
        if t == 1:
            return pltpu.make_async_copy(wu_hbm.at[e], wgu_f.at[s, 1], sem.at[s, 1])
        return pltpu.make_async_copy(wd_hbm.at[e], wd_f.at[s], sem.at[s, 2])

    def issue_until(n):
        def body(q, carry):
            h = q // 3
            for t in range(3):
                @pl.when(q % 3 == t)
                def _():
                    tensor_copy(used_ref[h], h % WEIGHT_SLOTS, t).start(priority=DMA_QUEUES - 1)
            return carry

        lax.fori_loop(issued[0], n, body, 0)
        issued[0] = jnp.maximum(issued[0], n)

    @pl.when(i == 0)
    def _():
        issued[0] = 0

    @pl.when(i < nused_ref[0])
    def _():
        g = gidx_ref[i]

        @pl.when(first_ref[i] == 1)
        def _():
            issue_until(3 * (g + 1))
            s = g % WEIGHT_SLOTS
            for t in range(3):
                tensor_copy(used_ref[g], s, t).wait()
            wgu_b[:, :F] = wgu_f[s, 0].astype(BF16)
            wgu_b[:, F:] = wgu_f[s, 1].astype(BF16)
            wd_b[...] = wd_f[s].astype(BF16)

        cap = 3 * jnp.minimum(g + WEIGHT_SLOTS + 1, ngroups_ref[0])
        issue_until(jnp.minimum(issued[0] + 1, cap))

        lo, hi = _unpack_halves(_load_row_tiles(x_ref, 0, EXPERT_BLOCK))
        x = jnp.concatenate([lo.astype(BF16), hi.astype(BF16)], axis=1)
        gu = jnp.dot(x, wgu_b[...], preferred_element_type=F32)
        a = (_silu(gu[:, :F]) * gu[:, F:]).astype(BF16)
        y = jnp.dot(a, wd_b[...], preferred_element_type=F32)
        _store_row_tiles(y_ref, 0, EXPERT_BLOCK, _pack_halves(y))


def _experts(xg, first, gidx, used_list, n_used, n_groups, w_gate, w_up, w_down):
    D = D_MODEL
    bm = EXPERT_BLOCK
    F = D_EXPERT
    nb = xg.shape[0] // (bm * ROW_TILE)
    row_map = lambda i, fi, gi, ul, nu, ng: (jnp.minimum(i, nu[0] - 1), 0)
    return pl.pallas_call(
        _experts_kernel,
        grid_spec=pltpu.PrefetchScalarGridSpec(
            num_scalar_prefetch=5,
            grid=(nb,),
            in_specs=[
                pl.BlockSpec((bm * ROW_TILE, LANES), row_map),
                pl.BlockSpec(memory_space=pl.ANY),
                pl.BlockSpec(memory_space=pl.ANY),
                pl.BlockSpec(memory_space=pl.ANY),
            ],
            out_specs=pl.BlockSpec((bm * ROW_TILE, LANES), row_map),
            scratch_shapes=[
                pltpu.VMEM((WEIGHT_SLOTS, 2, D, F), F32), pltpu.VMEM((WEIGHT_SLOTS, F, D), F32),
                pltpu.VMEM((D, 2 * F), BF16), pltpu.VMEM((F, D), BF16),
                pltpu.SMEM((1,), I32),
                pltpu.SemaphoreType.DMA((WEIGHT_SLOTS, 3)),
            ],
        ),
        out_shape=jax.ShapeDtypeStruct(xg.shape, U32),
        input_output_aliases={5: 0},
        compiler_params=_cparams(("arbitrary",)),
        name="experts",
    )(first, gidx, used_list, n_used, n_groups, xg, w_gate, w_up, w_down)


def _combine_kernel(dest_ref, dnext_ref, wt_ref, x1_ref, h_ref, mod_ref, sg_ref, su_ref, sd_ref, yg_ref, o_ref,
                    dest_s, buf_a, buf_b, sem_s, sem):
    i = pl.program_id(0)
    last = pl.num_programs(0) - 1
    tc = TM_COMBINE
    half = D_MODEL // 2
    slot = i % 2

    def index_copy(src_ref, s):
        return pltpu.make_async_copy(src_ref, dest_s.at[s], sem_s.at[s])

    def row_copy(s, col, buf, which, t, k):
        return pltpu.make_async_copy(yg_ref.at[dest_s[s, k, col + t]],
                                     buf.at[pl.ds((k * tc + t) * ROW_TILE, ROW_TILE), :], sem.at[which])

    def start_rows(s, col, buf, which):
        for t in range(tc):
            for k in range(TOP_K):
                row_copy(s, col, buf, which, t, k).start(priority=k % DMA_QUEUES)

    def wait_rows(buf, which):
        pltpu.make_async_copy(buf, buf, sem.at[which]).wait()

    def finish_tile(buf, r0):
        rows = slice(r0, r0 + tc)
        hb = h_ref[rows, :]
        g = jnp.dot(hb, sg_ref[...], preferred_element_type=F32)
        u = jnp.dot(hb, su_ref[...], preferred_element_type=F32)
        ffn = jnp.dot((_silu(g) * u).astype(BF16), sd_ref[...], preferred_element_type=F32)
        ffn_lo, ffn_hi = ffn[:, :half], ffn[:, half:]
        for k in range(TOP_K):
            lo, hi = _unpack_halves(_load_row_tiles(buf, k * tc * ROW_TILE, tc))
            w = wt_ref[rows, k:k + 1]
            ffn_lo += lo * w
            ffn_hi += hi * w
        o_ref[rows, :half] = x1_ref[rows, :half] + mod_ref[0, 5:6, :half] * ffn_lo
        o_ref[rows, half:] = x1_ref[rows, half:] + mod_ref[0, 5:6, half:] * ffn_hi

    @pl.when(i == 0)
    def _():
        first = index_copy(dest_ref, 0)
        first.start()
        first.wait()

        def body(t, c):
            for k in range(TOP_K):
                row_copy(0, 0, buf_a, 0, t, k).start(priority=k % DMA_QUEUES)
            return c

        lax.fori_loop(0, tc, body, 0)

    index_copy(dnext_ref, 1 - slot).start()
    wait_rows(buf_a, 0)
    start_rows(slot, tc, buf_b, 1)
    finish_tile(buf_a, 0)
    wait_rows(buf_b, 1)
    index_copy(dnext_ref, 1 - slot).wait()
    start_rows(1 - slot, 0, buf_a, 0)
    finish_tile(buf_b, tc)

    @pl.when(i == last)
    def _():
        wait_rows(buf_a, 0)


def _combine(dest, wt_t, x1, h2, mod3, sg_b, su_b, sd_b, yg):
    T, D = x1.shape
    tc = TM_COMBINE
    step = 2 * tc
    per_batch = SEQ // step
    F = D_EXPERT
    return pl.pallas_call(
        _combine_kernel,
        grid=(T // step,),
        in_specs=[
            pl.BlockSpec((TOP_K, step), lambda i: (0, i)),
            pl.BlockSpec((TOP_K, step), lambda i: (0, jnp.minimum(i + 1, T // step - 1))),
            pl.BlockSpec((step, TOP_K), lambda i: (i, 0)),
            pl.BlockSpec((step, D), lambda i: (i, 0)),
            pl.BlockSpec((step, D), lambda i: (i, 0)),
            pl.BlockSpec((1, 6, D), lambda i: (i // per_batch, 0, 0)),
            pl.BlockSpec((D, F), lambda i: (0, 0)),
            pl.BlockSpec((D, F), lambda i: (0, 0)),
            pl.BlockSpec((F, D), lambda i: (0, 0)),
            pl.BlockSpec(memory_space=pl.ANY),
        ],
        out_specs=pl.BlockSpec((step, D), lambda i: (i, 0)),
        out_shape=jax.ShapeDtypeStruct((T, D), F32),
        scratch_shapes=[
            pltpu.SMEM((2, TOP_K, step), I32),
            pltpu.VMEM((TOP_K * tc * ROW_TILE, LANES), U32),
            pltpu.VMEM((TOP_K * tc * ROW_TILE, LANES), U32),
            pltpu.SemaphoreType.DMA((2,)),
            pltpu.SemaphoreType.DMA((2,)),
        ],
        compiler_params=_cparams(("arbitrary",)),
        name="combine",
    )(dest, dest, wt_t, x1, h2, mod3, sg_b, su_b, sd_b, yg)


def kernel(x, c, ada_w, ada_b, mix_norm_g, ffn_norm_g, w_in, q_norm_g, k_norm_g, rel_bias, sgu_ln_g, sgu_ln_b, sgu_w, sgu_b, w_out, router_w, router_bias, shared_w_gate, shared_w_up, shared_w_down, expert_w_gate, expert_w_up, expert_w_down):
    B, S, D = x.shape
    assert S == SEQ and D == D_MODEL and ada_w.shape[0] == 1
    T = B * S
    x2 = x.reshape(T, D)

    mod3 = _modulation(c, ada_w[0], ada_b[0]).reshape(B, 6, D)

    proj = _in_projection(x2, mod3, mix_norm_g, w_in[0].astype(BF16), q_norm_g, k_norm_g)
    attn = _attention(proj.reshape(B, S, D_IN_PROJ), _bias_tables(rel_bias)).reshape(T, D_ATTN)
    causal = jnp.tril(jnp.ones((SGU_CHUNK, SGU_CHUNK), F32))
    sgu = _spatial_gating(proj, (sgu_w[0] * causal).astype(BF16), sgu_ln_g, sgu_ln_b, sgu_b[0].T)
    x1, h2, h2p = _out_projection(attn, sgu, x2, mod3, ffn_norm_g, w_out[0].astype(BF16))

    idx, wts, rank, counts = _route(h2, router_w[0].T.astype(BF16), router_bias[0])

    bm = EXPERT_BLOCK
    counts = counts.reshape(N_EXPERTS)
    padded = (counts + bm - 1) // bm * bm
    pends = jnp.cumsum(padded)
    pstarts = pends - padded
    n_blocks = T * TOP_K // bm + N_EXPERTS
    block_start = jnp.arange(n_blocks, dtype=I32) * bm
    block_exp = jnp.minimum(jnp.sum((pends[None, :] <= block_start[:, None]).astype(I32), axis=1), N_EXPERTS - 1)
    n_used = (pends[-1] // bm).astype(I32).reshape(1)
    eids = jnp.arange(N_EXPERTS, dtype=I32)
    dest = jnp.sum(jnp.where(idx[:, :, None] == eids, pstarts.astype(I32), 0), axis=-1) + rank
    pend0 = jnp.concatenate([jnp.zeros((1,), I32), pends.astype(I32)])
    first = jnp.concatenate([jnp.ones((1,), I32), (block_exp[1:] != block_exp[:-1]).astype(I32)])
    used = counts > 0
    ordinal = jnp.cumsum(used.astype(I32)) - 1
    n_groups = jnp.sum(used.astype(I32)).reshape(1)
    used_list = jnp.sum(jnp.where(used[None, :] & (ordinal[None, :] == eids[:, None]), eids[None, :], 0), axis=-1)
    gidx = jnp.sum(jnp.where(block_exp[:, None] == eids, ordinal, 0), axis=-1)

    n_rows = n_blocks * bm
    xg = _dispatch(h2p.reshape(T, ROW_TILE, LANES), dest, pend0, (pstarts + counts).astype(I32), n_rows)
    yg = _experts(xg.reshape(n_rows * ROW_TILE, LANES), first, gidx.astype(I32), used_list.astype(I32),
                  n_used, n_groups, expert_w_gate[0], expert_w_up[0], expert_w_down[0])
    out = _combine(dest, wts.T, x1, h2, mod3, shared_w_gate[0].astype(BF16), shared_w_up[0].astype(BF16),
                   shared_w_down[0].astype(BF16), yg.reshape(n_rows, ROW_TILE, LANES))
    return out.reshape(B, S, D)
```

```python
import math

import jax
import jax.numpy as jnp
from jax import lax
from jax.experimental import pallas as pl
from jax.experimental.pallas import tpu as pltpu

F32 = jnp.float32
BF16 = jnp.bfloat16
I32 = jnp.int32
U32 = jnp.uint32

D_MODEL = 2048
SEQ = 2048
HEAD_DIM = 128
N_HEADS = 8
D_ATTN = N_HEADS * HEAD_DIM
D_SGU = 1024
N_SGU_GROUPS = 8
SGU_CHUNK = 128
D_IN_PROJ = 3 * D_ATTN + 2 * D_SGU
DILATED_GROUPS = ((128, 1), (512, 4), (2048, 16))
BAND = 128
NUM_REL_BUCKETS = 32
REL_MAX_DISTANCE = 2048
N_EXPERTS = 256
TOP_K = 8
N_EXPERT_GROUPS = 8
GROUP_SIZE = N_EXPERTS // N_EXPERT_GROUPS
TOPK_GROUPS = 4
D_EXPERT = 512
ROUTED_SCALE = 2.5
NORM_EPS = 1e-6
MASK_VALUE = -1e30

LANES = 128
ROW_TILE = D_MODEL // 2 // LANES
VMEM_LIMIT = 56 * 1024 * 1024
DMA_QUEUES = 2

TM_PROJ = 256
TM_SGU = 512
TM_OUT = 256
TM_ROUTE = 512
TM_DISPATCH = 256
TN_MOD = 1024
TM_COMBINE = 128
EXPERT_BLOCK = 128
ZERO_CHUNK = 32
WEIGHT_SLOTS = 2
ROW_SLOTS = 3
ATTN_PIECES_IN_FLIGHT = 6


def _cparams(sem):
    return pltpu.CompilerParams(dimension_semantics=sem, vmem_limit_bytes=VMEM_LIMIT)


def _silu(v):
    return v * jax.nn.sigmoid(v)


def _gelu(v):
    return 0.5 * v * (1.0 + lax.erf(v * (1.0 / math.sqrt(2.0))))


def _pack_halves(v):
    h = v.shape[1] // 2
    lo = lax.bitcast_convert_type(v[:, :h].astype(BF16).astype(F32), U32)
    hi = lax.bitcast_convert_type(v[:, h:].astype(BF16).astype(F32), U32)
    return (hi & jnp.uint32(0xFFFF0000)) | (lo >> 16)


def _unpack_halves(w):
    lo = lax.bitcast_convert_type(w << 16, F32)
    hi = lax.bitcast_convert_type(w & jnp.uint32(0xFFFF0000), F32)
    return lo, hi


def _store_row_tiles(ref, base, rows, packed):
    for s in range(ROW_TILE):
        ref[pl.ds(base + s, rows, stride=ROW_TILE), :] = packed[:, s * LANES:(s + 1) * LANES]


def _load_row_tiles(ref, base, rows):
    return jnp.concatenate([ref[pl.ds(base + s, rows, stride=ROW_TILE), :] for s in range(ROW_TILE)], axis=1)


def _mod_kernel(c_ref, w_ref, b_ref, o_ref):
    ca = _silu(c_ref[...]).astype(BF16)
    o_ref[...] = jnp.dot(ca, w_ref[...].astype(BF16), preferred_element_type=F32) + b_ref[...]


def _modulation(c, ada_w, ada_b):
    B, D = c.shape
    N = ada_w.shape[1]
    tn = TN_MOD
    return pl.pallas_call(
        _mod_kernel,
        grid=(N // tn,),
        in_specs=[
            pl.BlockSpec((B, D), lambda j: (0, 0)),
            pl.BlockSpec((D, tn), lambda j: (0, j)),
            pl.BlockSpec((1, tn), lambda j: (0, j)),
        ],
        out_specs=pl.BlockSpec((B, tn), lambda j: (0, j)),
        out_shape=jax.ShapeDtypeStruct((B, N), F32),
        compiler_params=_cparams(("arbitrary",)),
        name="modulation",
    )(c, ada_w, ada_b.reshape(1, N))


def _inproj_kernel(x_ref, mod_ref, g_ref, w_ref, qg_ref, kg_ref, o_ref):
    x = x_ref[...]
    r = lax.rsqrt(jnp.mean(x * x, axis=-1, keepdims=True) + NORM_EPS)
    h = ((x * r) * g_ref[...] * (1.0 + mod_ref[0, 1:2, :]) + mod_ref[0, 0:1, :]).astype(BF16)

    def head_norm(acc, col0, gain_ref, scale):
        for hd in range(N_HEADS):
            a = acc[:, hd * HEAD_DIM:(hd + 1) * HEAD_DIM]
            r = lax.rsqrt(jnp.mean(a * a, axis=-1, keepdims=True) + NORM_EPS)
            cols = slice(col0 + hd * HEAD_DIM, col0 + (hd + 1) * HEAD_DIM)
            o_ref[:, cols] = ((a * r) * gain_ref[...] * scale).astype(BF16)

    def epilogue(j, acc):
        col0 = j * D_ATTN
        if j == 0:
            head_norm(acc, col0, qg_ref, HEAD_DIM ** -0.5)
        elif j == 1:
            head_norm(acc, col0, kg_ref, 1.0)
        else:
            o_ref[:, col0:col0 + D_ATTN] = acc.astype(BF16)

    pending = None
    for j in range(D_IN_PROJ // D_ATTN):
        acc = jnp.dot(h, w_ref[:, j * D_ATTN:(j + 1) * D_ATTN], preferred_element_type=F32)
        if pending is not None:
            epilogue(*pending)
        pending = (j, acc)
    epilogue(*pending)


def _in_projection(x2, mod3, norm_g, w_in_b, q_g, k_g):
    T, D = x2.shape
    tm = TM_PROJ
    per_batch = SEQ // tm
    return pl.pallas_call(
        _inproj_kernel,
        grid=(T // tm,),
        in_specs=[
            pl.BlockSpec((tm, D), lambda i: (i, 0)),
            pl.BlockSpec((1, 6, D), lambda i: (i // per_batch, 0, 0)),
            pl.BlockSpec((1, D), lambda i: (0, 0)),
            pl.BlockSpec((D, D_IN_PROJ), lambda i: (0, 0), pipeline_mode=pl.Buffered(1)),
            pl.BlockSpec((1, HEAD_DIM), lambda i: (0, 0)),
            pl.BlockSpec((1, HEAD_DIM), lambda i: (0, 0)),
        ],
        out_specs=pl.BlockSpec((tm, D_IN_PROJ), lambda i: (i, 0)),
        out_shape=jax.ShapeDtypeStruct((T, D_IN_PROJ), BF16),
        compiler_params=_cparams(("arbitrary",)),
        name="in_projection",
    )(x2, mod3, norm_g, w_in_b, q_g, k_g)


def _bias_tables(rel_bias):
    n = BAND
    qi = jnp.arange(n)[:, None]
    ki = jnp.arange(2 * n)[None, :]
    steps = n + qi - ki
    in_band = (steps >= 0) & (steps <= n)
    max_exact = NUM_REL_BUCKETS // 2
    tabs = []
    for _, dilation in DILATED_GROUPS:
        dist = jnp.clip(steps, 0, n) * dilation
        nf = jnp.maximum(dist, 1).astype(F32)
        large = max_exact + (jnp.log(nf / max_exact) / math.log(REL_MAX_DISTANCE / max_exact)
                             * (NUM_REL_BUCKETS - max_exact)).astype(I32)
        large = jnp.minimum(large, NUM_REL_BUCKETS - 1)
        bucket = jnp.where(dist < max_exact, dist, large)
        onehot = jax.nn.one_hot(bucket, NUM_REL_BUCKETS, dtype=F32)
        b = jnp.einsum("qkb,bh->hqk", onehot, rel_bias.astype(F32), precision=lax.Precision.HIGHEST)
        tabs.append(jnp.where(in_band[None], b, MASK_VALUE))
    return jnp.stack(tabs)


def _attn_kernel(q_ref, k_ref, v_ref, bias_ref, o_ref, qf, kf, vf, q4, k4, v4,
                 o0, l0, o1, l1, o2, l2, stage):
    n = BAND
    quarter = SEQ // 4

    for src, nat, res in ((q_ref, qf, q4), (k_ref, kf, k4), (v_ref, vf, v4)):
        nat[...] = src[...].astype(F32)
        for r in range(4):
            res[r * quarter:(r + 1) * quarter, :] = nat[pl.ds(r, quarter, stride=4), :]

    def logits(qb, kb, bias):
        return lax.dot_general(qb, kb, (((1,), (1,)), ((), ())), preferred_element_type=F32) + bias

    def finish(l, vb, o_out, l_out, rows):
        m = jnp.max(l, axis=-1, keepdims=True)
        p = jnp.exp(l - m)
        s = jnp.sum(p, axis=-1, keepdims=True)
        o = jnp.dot(p.astype(BF16), vb, preferred_element_type=F32)
        o_out[rows, :] = o / s
        l_out[rows, :] = jnp.broadcast_to(m + jnp.log(s), (n, LANES))

    pieces = []

    for i in range(SEQ // n):
        rows = slice(i * n, (i + 1) * n)
        krows = rows if i == 0 else slice((i - 1) * n, (i + 1) * n)
        bias = (0, slice(None), slice(n, None)) if i == 0 else (0,)
        pieces.append((q_ref, k_ref, v_ref, rows, krows, bias, o0, l0))

    for r in range(4):
        for blk in range(quarter // n):
            base = r * quarter + blk * n
            rows = slice(base, base + n)
            krows = rows if blk == 0 else slice(base - n, base + n)
            bias = (1, slice(None), slice(n, None)) if blk == 0 else (1,)
            pieces.append((q4, k4, v4, rows, krows, bias, o1, l1))

    for r in range(4):
        for a in range(4):
            rows = pl.ds(r * quarter + a, n, stride=4)
            pieces.append((q4, k4, v4, rows, rows, (2, slice(None), slice(n, None)), o2, l2))

    pending = []
    for qs, ks, vs, rows, krows, bias, o_out, l_out in pieces:
        l = logits(qs[rows, :].astype(BF16), ks[krows, :].astype(BF16), bias_ref[bias])
        if len(pending) == ATTN_PIECES_IN_FLIGHT:
            finish(*pending.pop(0))
        pending.append((l, vs[krows, :].astype(BF16), o_out, l_out, rows))
    for args in pending:
        finish(*args)

    for r in range(4):
        for c in range(quarter // n):
            rows = slice(r * quarter + c * n, r * quarter + (c + 1) * n)
            nat = pl.ds(r + 4 * c * n, n, stride=4)
            a0, a1, a2 = l0[nat, :], l1[rows, :], l2[rows, :]
            m = jnp.maximum(jnp.maximum(a0, a1), a2)
            e0, e1, e2 = jnp.exp(a0 - m), jnp.exp(a1 - m), jnp.exp(a2 - m)
            mix = (e0 * o0[nat, :] + e1 * o1[rows, :] + e2 * o2[rows, :]) / (e0 + e1 + e2)
            stage[nat, :] = mix
    o_ref[...] = stage[...].astype(BF16)


def _attention(proj3, bias_tabs):
    B = proj3.shape[0]
    blk = lambda off: pl.BlockSpec((None, SEQ, HEAD_DIM), lambda b, h: (b, 0, off + h))
    return pl.pallas_call(
        _attn_kernel,
        grid=(B, N_HEADS),
        in_specs=[
            blk(0), blk(N_HEADS), blk(2 * N_HEADS),
            pl.BlockSpec((3, None, BAND, 2 * BAND), lambda b, h: (0, h, 0, 0)),
        ],
        out_specs=pl.BlockSpec((None, SEQ, HEAD_DIM), lambda b, h: (b, 0, h)),
        out_shape=jax.ShapeDtypeStruct((B, SEQ, D_ATTN), BF16),
        scratch_shapes=[pltpu.VMEM((SEQ, HEAD_DIM), F32) for _ in range(13)],
        compiler_params=_cparams(("arbitrary", "arbitrary")),
        name="dilated_attention",
    )(proj3, proj3, proj3, bias_tabs)


def _sgu_kernel(u_ref, z_ref, w_ref, g_ref, b_ref, bs_ref, o_ref):
    n = SGU_CHUNK
    for c in range(TM_SGU // n):
        rs = slice(c * n, (c + 1) * n)
        for g in range(N_SGU_GROUPS):
            cs = slice(g * n, (g + 1) * n)
            z = _gelu(z_ref[rs, cs].astype(F32))
            mu = jnp.mean(z, axis=-1, keepdims=True)
            var = jnp.maximum(jnp.mean(z * z, axis=-1, keepdims=True) - mu * mu, 0.0)
            zn = ((z - mu) * lax.rsqrt(var + NORM_EPS)) * g_ref[:, cs] + b_ref[:, cs]
            mixed = jnp.dot(w_ref[g], zn.astype(BF16), preferred_element_type=F32) + bs_ref[:, g:g + 1]
            u = _gelu(u_ref[rs, cs].astype(F32))
            o_ref[rs, cs] = (u * mixed).astype(BF16)


def _spatial_gating(proj, w_causal_b, ln_g, ln_b, bs_t):
    T = proj.shape[0]
    tm = TM_SGU
    ucol = 3 * D_ATTN // D_SGU
    return pl.pallas_call(
        _sgu_kernel,
        grid=(T // tm,),
        in_specs=[
            pl.BlockSpec((tm, D_SGU), lambda i: (i, ucol)),
            pl.BlockSpec((tm, D_SGU), lambda i: (i, ucol + 1)),
            pl.BlockSpec((N_SGU_GROUPS, SGU_CHUNK, SGU_CHUNK), lambda i: (0, 0, 0)),
            pl.BlockSpec((1, D_SGU), lambda i: (0, 0)),
            pl.BlockSpec((1, D_SGU), lambda i: (0, 0)),
            pl.BlockSpec((SGU_CHUNK, N_SGU_GROUPS), lambda i: (0, 0)),
        ],
        out_specs=pl.BlockSpec((tm, D_SGU), lambda i: (i, 0)),
        out_shape=jax.ShapeDtypeStruct((T, D_SGU), BF16),
        compiler_params=_cparams(("arbitrary",)),
        name="spatial_gating",
    )(proj, proj, w_causal_b, ln_g, ln_b, bs_t)


def _outproj_kernel(a_ref, s_ref, x_ref, mod_ref, g_ref, w_ref, x1_ref, h2_ref, h2p_ref):
    mixed = jnp.dot(a_ref[...], w_ref[0:D_ATTN, :], preferred_element_type=F32)
    mixed += jnp.dot(s_ref[...], w_ref[D_ATTN:, :], preferred_element_type=F32)
    x1 = x_ref[...] + mod_ref[0, 2:3, :] * mixed
    x1_ref[...] = x1
    r = lax.rsqrt(jnp.mean(x1 * x1, axis=-1, keepdims=True) + NORM_EPS)
    h2 = (x1 * r) * g_ref[...] * (1.0 + mod_ref[0, 4:5, :]) + mod_ref[0, 3:4, :]
    h2_ref[...] = h2.astype(BF16)
    _store_row_tiles(h2p_ref, 0, TM_OUT, _pack_halves(h2))


def _out_projection(attn, sgu, x2, mod3, norm_g, w_out_b):
    T, D = x2.shape
    tm = TM_OUT
    per_batch = SEQ // tm
    return pl.pallas_call(
        _outproj_kernel,
        grid=(T // tm,),
        in_specs=[
            pl.BlockSpec((tm, D_ATTN), lambda i: (i, 0)),
            pl.BlockSpec((tm, D_SGU), lambda i: (i, 0)),
            pl.BlockSpec((tm, D), lambda i: (i, 0)),
            pl.BlockSpec((1, 6, D), lambda i: (i // per_batch, 0, 0)),
            pl.BlockSpec((1, D), lambda i: (0, 0)),
            pl.BlockSpec((D_ATTN + D_SGU, D), lambda i: (0, 0), pipeline_mode=pl.Buffered(1)),
        ],
        out_specs=[pl.BlockSpec((tm, D), lambda i: (i, 0)), pl.BlockSpec((tm, D), lambda i: (i, 0)),
                   pl.BlockSpec((tm * ROW_TILE, LANES), lambda i: (i, 0))],
        out_shape=[jax.ShapeDtypeStruct((T, D), F32), jax.ShapeDtypeStruct((T, D), BF16),
                   jax.ShapeDtypeStruct((T * ROW_TILE, LANES), U32)],
        compiler_params=_cparams(("arbitrary",)),
        name="out_projection",
    )(attn, sgu, x2, mod3, norm_g, w_out_b)


def _router_kernel(h_ref, w_ref, b_ref, idx_ref, wt_ref, rank_ref, cnt_ref, run_ref):
    i = pl.program_id(0)
    tr = TM_ROUTE
    E = N_EXPERTS

    @pl.when(i == 0)
    def _():
        run_ref[...] = jnp.zeros_like(run_ref)

    logits = lax.dot_general(w_ref[...], h_ref[...], (((1,), (1,)), ((), ())), preferred_element_type=F32)
    scores = jax.nn.sigmoid(logits)
    sel = scores + b_ref[...]

    slabs = [sel[g * GROUP_SIZE:(g + 1) * GROUP_SIZE, :] for g in range(N_EXPERT_GROUPS)]
    si = lax.broadcasted_iota(I32, (GROUP_SIZE, tr), 0).astype(F32)
    gs = []
    for slab in slabs:
        m1 = jnp.max(slab, axis=0, keepdims=True)
        first = jnp.min(jnp.where(slab == m1, si, float(GROUP_SIZE)), axis=0, keepdims=True)
        m2 = jnp.max(jnp.where(si == first, -jnp.inf, slab), axis=0, keepdims=True)
        gs.append(m1 + m2)

    kept = []
    for g in range(N_EXPERT_GROUPS):
        beaten = jnp.zeros((1, tr), F32)
        for o in range(N_EXPERT_GROUPS):
            if o < g:
                beaten += (gs[o] >= gs[g]).astype(F32)
            elif o > g:
                beaten += (gs[o] > gs[g]).astype(F32)
        kept.append(jnp.where(beaten < TOPK_GROUPS, slabs[g], -jnp.inf))
    masked = jnp.concatenate(kept, axis=0)

    ei = lax.broadcasted_iota(I32, (E, tr), 0).astype(F32)
    picks, pick_scores = [], []
    onehot_sum = jnp.zeros((E, tr), F32)
    for k in range(TOP_K):
        m = jnp.max(masked, axis=0, keepdims=True)
        ik = jnp.min(jnp.where(masked == m, ei, float(E)), axis=0, keepdims=True)
        oh = ei == ik
        pick_scores.append(jnp.sum(jnp.where(oh, scores, 0.0), axis=0, keepdims=True))
        masked = jnp.where(oh, -jnp.inf, masked)
        onehot_sum += oh.astype(F32)
        picks.append(ik)

    ti = lax.broadcasted_iota(I32, (tr, tr), 0)
    tj = lax.broadcasted_iota(I32, (tr, tr), 1)
    before = (ti < tj).astype(BF16)
    prior = jnp.dot(onehot_sum.astype(BF16), before, preferred_element_type=F32) + run_ref[...]

    total = pick_scores[0]
    for k in range(1, TOP_K):
        total += pick_scores[k]
    for k in range(TOP_K):
        idx_ref[k:k + 1, :] = picks[k].astype(I32)
        wt_ref[k:k + 1, :] = pick_scores[k] / total * ROUTED_SCALE
        rk = jnp.sum(jnp.where(ei == picks[k], prior, 0.0), axis=0, keepdims=True)
        rank_ref[k:k + 1, :] = rk.astype(I32)

    run_ref[...] += jnp.sum(onehot_sum, axis=1, keepdims=True)
    cnt_ref[...] = run_ref[...].astype(I32)


def _route(h2, router_wt_b, router_bias):
    T, D = h2.shape
    tr = TM_ROUTE
    E = N_EXPERTS
    row_spec = pl.BlockSpec((TOP_K, tr), lambda i: (0, i))
    return pl.pallas_call(
        _router_kernel,
        grid=(T // tr,),
        in_specs=[
            pl.BlockSpec((tr, D), lambda i: (i, 0)),
            pl.BlockSpec((E, D), lambda i: (0, 0)),
            pl.BlockSpec((E, 1), lambda i: (0, 0)),
        ],
        out_specs=[row_spec, row_spec, row_spec, pl.BlockSpec((E, 1), lambda i: (0, 0))],
        out_shape=[
            jax.ShapeDtypeStruct((TOP_K, T), I32),
            jax.ShapeDtypeStruct((TOP_K, T), F32),
            jax.ShapeDtypeStruct((TOP_K, T), I32),
            jax.ShapeDtypeStruct((E, 1), I32),
        ],
        scratch_shapes=[pltpu.VMEM((E, 1), F32)],
        compiler_params=_cparams(("arbitrary",)),
        name="router",
    )(h2, router_wt_b, router_bias.reshape(E, 1))


def _dispatch_kernel(pend_ref, vend_ref, h_ref, dest_ref, dnext_ref, xg_ref, dest_s, zero_ref, sem_s, sem):
    i = pl.program_id(0)
    td = TM_DISPATCH
    blk = EXPERT_BLOCK
    slot = i % 2

    def index_copy(src_ref, s):
        return pltpu.make_async_copy(src_ref, dest_s.at[s], sem_s.at[s])

    def zero_copy(e, c):
        start = pl.multiple_of(pend_ref[e + 1] - ZERO_CHUNK * (c + 1), ZERO_CHUNK)
        return pltpu.make_async_copy(zero_ref.at[pl.ds(0, ZERO_CHUNK)], xg_ref.at[pl.ds(start, ZERO_CHUNK)], sem)

    def each_piece(e, act):
        for c in range(blk // ZERO_CHUNK):
            @pl.when(pend_ref[e + 1] - ZERO_CHUNK * c > vend_ref[e])
            def _():
                act(zero_copy(e, c))

    @pl.when(i == 0)
    def _():
        index_copy(dest_ref, 0).start()
        zero_ref[...] = jnp.zeros_like(zero_ref)

        def start_zero(e, c):
            each_piece(e, lambda cp: cp.start())
            return c

        def wait_zero(e, c):
            each_piece(e, lambda cp: cp.wait())
            return c

        lax.fori_loop(0, N_EXPERTS, start_zero, 0)
        lax.fori_loop(0, N_EXPERTS, wait_zero, 0)

        def tail_copy(b):
            return pltpu.make_async_copy(zero_ref, xg_ref.at[pl.ds(pl.multiple_of(b * blk, blk), blk)], sem)

        first_unused = pend_ref[N_EXPERTS] // blk
        n_blocks = xg_ref.shape[0] // blk
        lax.fori_loop(first_unused, n_blocks, lambda b, c: (tail_copy(b).start(), c)[1], 0)
        lax.fori_loop(first_unused, n_blocks, lambda b, c: (tail_copy(b).wait(), c)[1], 0)
        index_copy(dest_ref, 0).wait()

    index_copy(dnext_ref, 1 - slot).start()

    def row_copy(t, k):
        return pltpu.make_async_copy(h_ref.at[t], xg_ref.at[dest_s[slot, k, t]], sem)

    def start_rows(t8, c):
        base = pl.multiple_of(t8 * 8, 8)
        for j in range(8):
            for k in range(TOP_K):
                row_copy(base + j, k).start(priority=k % DMA_QUEUES)
        return c

    lax.fori_loop(0, td // 8, start_rows, 0)
    for k in range(TOP_K):
        pltpu.make_async_copy(h_ref, h_ref, sem).wait()
    index_copy(dnext_ref, 1 - slot).wait()


def _dispatch(h2p, dest, pend0, vend, rows):
    T = h2p.shape[0]
    td = TM_DISPATCH
    tile = (ROW_TILE, LANES)
    return pl.pallas_call(
        _dispatch_kernel,
        grid_spec=pltpu.PrefetchScalarGridSpec(
            num_scalar_prefetch=2,
            grid=(T // td,),
            in_specs=[
                pl.BlockSpec((td,) + tile, lambda i, p, v: (i, 0, 0)),
                pl.BlockSpec((TOP_K, td), lambda i, p, v: (0, i)),
                pl.BlockSpec((TOP_K, td), lambda i, p, v: (0, jnp.minimum(i + 1, T // td - 1))),
            ],
            out_specs=pl.BlockSpec(memory_space=pl.ANY),
            scratch_shapes=[
                pltpu.SMEM((2, TOP_K, td), I32),
                pltpu.VMEM((EXPERT_BLOCK,) + tile, U32),
                pltpu.SemaphoreType.DMA((2,)),
                pltpu.SemaphoreType.DMA,
            ],
        ),
        out_shape=jax.ShapeDtypeStruct((rows,) + tile, U32),
        compiler_params=_cparams(("arbitrary",)),
        name="dispatch",
    )(pend0, vend, h2p, dest, dest)


def _experts_kernel(first_ref, gidx_ref, used_ref, nused_ref, ngroups_ref,
                    x_hbm, wg_hbm, wu_hbm, wd_hbm, y_ref,
                    wgu_f, wd_f, wgu_b, wd_b, xbuf, issued, sem, xsem):
    i = pl.program_id(0)
    F = D_EXPERT
    block_rows = EXPERT_BLOCK * ROW_TILE

    def aligned(v):
        return v if isinstance(v, int) else pl.multiple_of(v, block_rows)

    def row_copy(b):
        src = x_hbm.at[pl.ds(aligned(b * block_rows), block_rows), :]
        dst = xbuf.at[pl.ds(aligned((b % ROW_SLOTS) * block_rows), block_rows), :]
        return pltpu.make_async_copy(src, dst, xsem.at[b % ROW_SLOTS])

    def tensor_copy(e, s, t):
        if t == 0:
            return pltpu.make_async_copy(wg_hbm.at[e], wgu_f.at[s, 0], sem.at[s, 0])
        if t == 1:
            return pltpu.make_async_copy(wu_hbm.at[e], wgu_f.at[s, 1], sem.at[s, 1])
        return pltpu.make_async_copy(wd_hbm.at[e], wd_f.at[s], sem.at[s, 2])

    def issue_until(n):
        def body(q, carry):
            h = q // 3
            for t in range(3):
                @pl.when(q % 3 == t)
                def _():
                    tensor_copy(used_ref[h], h % WEIGHT_SLOTS, t).start(priority=DMA_QUEUES - 1)
            return carry

        lax.fori_loop(issued[0], n, body, 0)
        issued[0] = jnp.maximum(issued[0], n)

    @pl.when(i == 0)
    def _():
        issued[0] = 0
        for b in range(ROW_SLOTS - 1):
            @pl.when(b < nused_ref[0])
            def _():
                row_copy(b).start()

    @pl.when(i >= nused_ref[0])
    def _():
        y_ref[...] = jnp.zeros_like(y_ref)

    @pl.when(i < nused_ref[0])
    def _():
        g = gidx_ref[i]

        @pl.when(i + ROW_SLOTS - 1 < nused_ref[0])
        def _():
            row_copy(i + ROW_SLOTS - 1).start()

        @pl.when(first_ref[i] == 1)
        def _():
            issue_until(3 * (g + 1))
            s = g % WEIGHT_SLOTS
            for t in range(3):
                tensor_copy(used_ref[g], s, t).wait()
            wgu_b[:, :F] = wgu_f[s, 0].astype(BF16)
            wgu_b[:, F:] = wgu_f[s, 1].astype(BF16)
            wd_b[...] = wd_f[s].astype(BF16)

        cap = 3 * jnp.minimum(g + WEIGHT_SLOTS + 1, ngroups_ref[0])
        issue_until(jnp.minimum(issued[0] + 1, cap))

        row_copy(i).wait()
        base = pl.multiple_of((i % ROW_SLOTS) * block_rows, block_rows)
        lo, hi = _unpack_halves(_load_row_tiles(xbuf, base, EXPERT_BLOCK))
        x = jnp.concatenate([lo.astype(BF16), hi.astype(BF16)], axis=1)
        gu = jnp.dot(x, wgu_b[...], preferred_element_type=F32)
        a = (_silu(gu[:, :F]) * gu[:, F:]).astype(BF16)
        y = jnp.dot(a, wd_b[...], preferred_element_type=F32)
        _store_row_tiles(y_ref, 0, EXPERT_BLOCK, _pack_halves(y))


def _experts(xg, first, gidx, used_list, n_used, n_groups, w_gate, w_up, w_down):
    D = D_MODEL
    bm = EXPERT_BLOCK
    F = D_EXPERT
    nb = xg.shape[0] // (bm * ROW_TILE)
    return pl.pallas_call(
        _experts_kernel,
        grid_spec=pltpu.PrefetchScalarGridSpec(
            num_scalar_prefetch=5,
            grid=(nb,),
            in_specs=[
                pl.BlockSpec(memory_space=pl.ANY),
                pl.BlockSpec(memory_space=pl.ANY),
                pl.BlockSpec(memory_space=pl.ANY),
                pl.BlockSpec(memory_space=pl.ANY),
            ],
            out_specs=pl.BlockSpec((bm * ROW_TILE, LANES), lambda i, fi, gi, ul, nu, ng: (i, 0)),
            scratch_shapes=[
                pltpu.VMEM((WEIGHT_SLOTS, 2, D, F), F32), pltpu.VMEM((WEIGHT_SLOTS, F, D), F32),
                pltpu.VMEM((D, 2 * F), BF16), pltpu.VMEM((F, D), BF16),
                pltpu.VMEM((ROW_SLOTS * bm * ROW_TILE, LANES), U32),
                pltpu.SMEM((1,), I32),
                pltpu.SemaphoreType.DMA((WEIGHT_SLOTS, 3)),
                pltpu.SemaphoreType.DMA((ROW_SLOTS,)),
            ],
        ),
        out_shape=jax.ShapeDtypeStruct(xg.shape, U32),
        compiler_params=_cparams(("arbitrary",)),
        name="experts",
    )(first, gidx, used_list, n_used, n_groups, xg, w_gate, w_up, w_down)


def _combine_kernel(dest_ref, dnext_ref, wt_ref, x1_ref, h_ref, mod_ref, sg_ref, su_ref, sd_ref, yg_ref, o_ref,
                    dest_s, buf_a, buf_b, sem_s, sem):
    i = pl.program_id(0)
    last = pl.num_programs(0) - 1
    tc = TM_COMBINE
    half = D_MODEL // 2
    slot = i % 2

    def index_copy(src_ref, s):
        return pltpu.make_async_copy(src_ref, dest_s.at[s], sem_s.at[s])

    def row_copy(s, col, buf, which, t, k):
        return pltpu.make_async_copy(yg_ref.at[dest_s[s, k, col + t]],
                                     buf.at[pl.ds((k * tc + t) * ROW_TILE, ROW_TILE), :], sem.at[which])

    def start_rows(s, col, buf, which):
        for t in range(tc):
            for k in range(TOP_K):
                row_copy(s, col, buf, which, t, k).start(priority=k % DMA_QUEUES)

    def wait_rows(buf, which):
        pltpu.make_async_copy(buf, buf, sem.at[which]).wait()

    def finish_tile(buf, r0):
        rows = slice(r0, r0 + tc)
        hb = h_ref[rows, :]
        g = jnp.dot(hb, sg_ref[...], preferred_element_type=F32)
        u = jnp.dot(hb, su_ref[...], preferred_element_type=F32)
        ffn = jnp.dot((_silu(g) * u).astype(BF16), sd_ref[...], preferred_element_type=F32)
        ffn_lo, ffn_hi = ffn[:, :half], ffn[:, half:]
        for k in range(TOP_K):
            lo, hi = _unpack_halves(_load_row_tiles(buf, k * tc * ROW_TILE, tc))
            w = wt_ref[rows, k:k + 1]
            ffn_lo += lo * w
            ffn_hi += hi * w
        o_ref[rows, :half] = x1_ref[rows, :half] + mod_ref[0, 5:6, :half] * ffn_lo
        o_ref[rows, half:] = x1_ref[rows, half:] + mod_ref[0, 5:6, half:] * ffn_hi

    @pl.when(i == 0)
    def _():
        first = index_copy(dest_ref, 0)
        first.start()
        first.wait()

        def body(t, c):
            for k in range(TOP_K):
                row_copy(0, 0, buf_a, 0, t, k).start(priority=k % DMA_QUEUES)
            return c

        lax.fori_loop(0, tc, body, 0)

    index_copy(dnext_ref, 1 - slot).start()
    wait_rows(buf_a, 0)
    start_rows(slot, tc, buf_b, 1)
    finish_tile(buf_a, 0)
    wait_rows(buf_b, 1)
    index_copy(dnext_ref, 1 - slot).wait()
    start_rows(1 - slot, 0, buf_a, 0)
    finish_tile(buf_b, tc)

    @pl.when(i == last)
    def _():
        wait_rows(buf_a, 0)


def _combine(dest, wt_t, x1, h2, mod3, sg_b, su_b, sd_b, yg):
    T, D = x1.shape
    tc = TM_COMBINE
    step = 2 * tc
    per_batch = SEQ // step
    F = D_EXPERT
    return pl.pallas_call(
        _combine_kernel,
        grid=(T // step,),
        in_specs=[
            pl.BlockSpec((TOP_K, step), lambda i: (0, i)),
            pl.BlockSpec((TOP_K, step), lambda i: (0, jnp.minimum(i + 1, T // step - 1))),
            pl.BlockSpec((step, TOP_K), lambda i: (i, 0)),
            pl.BlockSpec((step, D), lambda i: (i, 0)),
            pl.BlockSpec((step, D), lambda i: (i, 0)),
            pl.BlockSpec((1, 6, D), lambda i: (i // per_batch, 0, 0)),
            pl.BlockSpec((D, F), lambda i: (0, 0)),
            pl.BlockSpec((D, F), lambda i: (0, 0)),
            pl.BlockSpec((F, D), lambda i: (0, 0)),
            pl.BlockSpec(memory_space=pl.ANY),
        ],
        out_specs=pl.BlockSpec((step, D), lambda i: (i, 0)),
        out_shape=jax.ShapeDtypeStruct((T, D), F32),
        scratch_shapes=[
            pltpu.SMEM((2, TOP_K, step), I32),
            pltpu.VMEM((TOP_K * tc * ROW_TILE, LANES), U32),
            pltpu.VMEM((TOP_K * tc * ROW_TILE, LANES), U32),
            pltpu.SemaphoreType.DMA((2,)),
            pltpu.SemaphoreType.DMA((2,)),
        ],
        compiler_params=_cparams(("arbitrary",)),
        name="combine",
    )(dest, dest, wt_t, x1, h2, mod3, sg_b, su_b, sd_b, yg)


def kernel(x, c, ada_w, ada_b, mix_norm_g, ffn_norm_g, w_in, q_norm_g, k_norm_g, rel_bias, sgu_ln_g, sgu_ln_b, sgu_w, sgu_b, w_out, router_w, router_bias, shared_w_gate, shared_w_up, shared_w_down, expert_w_gate, expert_w_up, expert_w_down):
    B, S, D = x.shape
    assert S == SEQ and D == D_MODEL and ada_w.shape[0] == 1
    T = B * S
    x2 = x.reshape(T, D)

    mod3 = _modulation(c, ada_w[0], ada_b[0]).reshape(B, 6, D)

    proj = _in_projection(x2, mod3, mix_norm_g, w_in[0].astype(BF16), q_norm_g, k_norm_g)
    attn = _attention(proj.reshape(B, S, D_IN_PROJ), _bias_tables(rel_bias)).reshape(T, D_ATTN)
    causal = jnp.tril(jnp.ones((SGU_CHUNK, SGU_CHUNK), F32))
    sgu = _spatial_gating(proj, (sgu_w[0] * causal).astype(BF16), sgu_ln_g, sgu_ln_b, sgu_b[0].T)
    x1, h2, h2p = _out_projection(attn, sgu, x2, mod3, ffn_norm_g, w_out[0].astype(BF16))

    idx, wts, rank, counts = _route(h2, router_w[0].T.astype(BF16), router_bias[0])

    bm = EXPERT_BLOCK
    counts = counts.reshape(N_EXPERTS)
    padded = (counts + bm - 1) // bm * bm
    pends = jnp.cumsum(padded)
    pstarts = pends - padded
    n_blocks = T * TOP_K // bm + N_EXPERTS
    block_start = jnp.arange(n_blocks, dtype=I32) * bm
    block_exp = jnp.minimum(jnp.sum((pends[None, :] <= block_start[:, None]).astype(I32), axis=1), N_EXPERTS - 1)
    n_used = (pends[-1] // bm).astype(I32).reshape(1)
    eids = jnp.arange(N_EXPERTS, dtype=I32)
    dest = jnp.sum(jnp.where(idx[:, :, None] == eids, pstarts.astype(I32), 0), axis=-1) + rank
    pend0 = jnp.concatenate([jnp.zeros((1,), I32), pends.astype(I32)])
    first = jnp.concatenate([jnp.ones((1,), I32), (block_exp[1:] != block_exp[:-1]).astype(I32)])
    used = counts > 0
    ordinal = jnp.cumsum(used.astype(I32)) - 1
    n_groups = jnp.sum(used.astype(I32)).reshape(1)
    used_list = jnp.sum(jnp.where(used[None, :] & (ordinal[None, :] == eids[:, None]), eids[None, :], 0), axis=-1)
    gidx = jnp.sum(jnp.where(block_exp[:, None] == eids, ordinal, 0), axis=-1)

    n_rows = n_blocks * bm
    xg = _dispatch(h2p.reshape(T, ROW_TILE, LANES), dest, pend0, (pstarts + counts).astype(I32), n_rows)
    yg = _experts(xg.reshape(n_rows * ROW_TILE, LANES), first, gidx.astype(I32), used_list.astype(I32),
                  n_used, n_groups, expert_w_gate[0], expert_w_up[0], expert_w_down[0])
    out = _combine(dest, wts.T, x1, h2, mod3, shared_w_gate[0].astype(BF16), shared_w_up[0].astype(BF16),
                   shared_w_down[0].astype(BF16), yg.reshape(n_rows, ROW_TILE, LANES))
    return out.reshape(B, S, D)
```
